```python
import math
import jax
import jax.numpy as jnp
from jax import lax
import numpy as np

D_MODEL = 2048
BATCH = 8
SEQ = 2048
DEPTH = 2

FOX_HEADS = 8
FOX_HEAD_DIM = 128
FOX_WIDTH = FOX_HEADS * FOX_HEAD_DIM
SSM_WIDTH = D_MODEL - FOX_WIDTH
SSM_GROUP = 16
SSM_GROUPS = SSM_WIDTH // SSM_GROUP
SSM_STATE = 64
EVEN_IN = 3 * FOX_WIDTH + FOX_HEADS + SSM_WIDTH
FORGET_BIAS_INIT = 2.0
DT_MIN = 1e-3
DT_MAX = 1e-1
SWA_HEADS = 32
SWA_KV_HEADS = 4
SWA_HEAD_DIM = 64
SWA_GROUPS = SWA_HEADS // SWA_KV_HEADS
SWA_WINDOW = 128
ODD_IN = (SWA_HEADS + 2 * SWA_KV_HEADS) * SWA_HEAD_DIM
ROPE_DIM = SWA_HEAD_DIM // 4
ROPE_THETA = 500000.0
Q_BLOCK = 128
D_FF = 5504
CONV_WIDTH = 3
LN_EPS = 1e-5
DEEPNORM_ALPHA = (2.0 * DEPTH) ** 0.25
DEEPNORM_BETA = (8.0 * DEPTH) ** -0.25
N_EVEN = (DEPTH + 1) // 2
N_ODD = DEPTH // 2

kernel_name = 'hybrid_fox_s5_swa_deepnorm'


def _layer_norm(x, g, b):
    x32 = x.astype(jnp.float32)
    mu = jnp.mean(x32, axis=-1, keepdims=True)
    var = jnp.mean(jnp.square(x32 - mu), axis=-1, keepdims=True)
    y = (x32 - mu) * lax.rsqrt(var + LN_EPS)
    return (y * g.astype(jnp.float32) + b.astype(jnp.float32)).astype(x.dtype)


def _forgetting_attention(q, k, v, f_logit):
    s_len = q.shape[1]
    dh = q.shape[-1]
    scale = 1.0 / math.sqrt(dh)
    log_f = jax.nn.log_sigmoid(f_logit.astype(jnp.float32))
    c = jnp.cumsum(log_f, axis=1).transpose(0, 2, 1)
    outs = []
    for start in range(0, s_len, Q_BLOCK):
        end = start + Q_BLOCK
        s = jnp.einsum('bqhd,bkhd->bhqk', q[:, start:end], k[:, :end]).astype(jnp.float32) * scale
        s = s + c[:, :, start:end, None] - c[:, :, None, :end]
        causal = jnp.arange(start, end)[:, None] >= jnp.arange(end)[None, :]
        s = jnp.where(causal, s, -jnp.inf)
        p = jax.nn.softmax(s, axis=-1).astype(v.dtype)
        outs.append(jnp.einsum('bhqk,bkhd->bqhd', p, v[:, :end]))
    return jnp.concatenate(outs, axis=1)


def _s5_scan(u, lam_re, lam_im, log_step, b_re, b_im, c_re, c_im, d_skip):
    u32 = u.astype(jnp.float32)
    lr = lam_re.astype(jnp.float32)
    li = lam_im.astype(jnp.float32)
    dt = jnp.exp(log_step.astype(jnp.float32))[:, None]
    mag = jnp.exp(lr * dt)
    a_re = mag * jnp.cos(li * dt)
    a_im = mag * jnp.sin(li * dt)
    den = lr * lr + li * li
    xr = a_re - 1.0
    xi = a_im
    g_re = (xr * lr + xi * li) / den
    g_im = (xi * lr - xr * li) / den
    br = b_re.astype(jnp.float32)
    bi = b_im.astype(jnp.float32)
    bb_re = g_re[..., None] * br - g_im[..., None] * bi
    bb_im = g_re[..., None] * bi + g_im[..., None] * br
    bu_re = jnp.einsum('gpc,bsgc->bsgp', bb_re, u32)
    bu_im = jnp.einsum('gpc,bsgc->bsgp', bb_im, u32)

    def combine(e1, e2):
        a1r, a1i, b1r, b1i = e1
        a2r, a2i, b2r, b2i = e2
        return (a2r * a1r - a2i * a1i,
                a2r * a1i + a2i * a1r,
                a2r * b1r - a2i * b1i + b2r,
                a2r * b1i + a2i * b1r + b2i)

    s_len = u.shape[1]
    a_re_t = jnp.broadcast_to(a_re[None, None], (1, s_len) + a_re.shape)
    a_im_t = jnp.broadcast_to(a_im[None, None], (1, s_len) + a_im.shape)
    _, _, h_re, h_im = lax.associative_scan(combine, (a_re_t, a_im_t, bu_re, bu_im), axis=1)
    y = (jnp.einsum('gcp,bsgp->bsgc', c_re.astype(jnp.float32), h_re)
         - jnp.einsum('gcp,bsgp->bsgc', c_im.astype(jnp.float32), h_im)
         + d_skip.astype(jnp.float32) * u32)
    return y.astype(u.dtype)


def _partial_rope(x, positions):
    half = ROPE_DIM // 2
    inv_freq = ROPE_THETA ** (-jnp.arange(half, dtype=jnp.float32) / half)
    ang = positions.astype(jnp.float32)[..., None] * inv_freq
    cos = jnp.cos(ang)[:, :, None, :]
    sin = jnp.sin(ang)[:, :, None, :]
    xr = x[..., :ROPE_DIM].astype(jnp.float32)
    x1 = xr[..., :half]
    x2 = xr[..., half:]
    rot = jnp.concatenate([x1 * cos - x2 * sin, x2 * cos + x1 * sin], axis=-1).astype(x.dtype)
    return jnp.concatenate([rot, x[..., ROPE_DIM:]], axis=-1)


def _sliding_window_attention(q, k, v, sinks):
    bsz, s_len, _, dh = q.shape
    nb = s_len // Q_BLOCK
    scale = 1.0 / math.sqrt(dh)
    qb = q.reshape(bsz, nb, Q_BLOCK, SWA_KV_HEADS, SWA_GROUPS, dh)
    kb = k.reshape(bsz, nb, Q_BLOCK, SWA_KV_HEADS, dh)
    vb = v.reshape(bsz, nb, Q_BLOCK, SWA_KV_HEADS, dh)
    pad = ((0, 0), (1, 0), (0, 0), (0, 0), (0, 0))
    kk = jnp.concatenate([jnp.pad(kb, pad)[:, :-1], kb], axis=2)
    vv = jnp.concatenate([jnp.pad(vb, pad)[:, :-1], vb], axis=2)
    s = jnp.einsum('bnqhgd,bnkhd->bnhgqk', qb, kk).astype(jnp.float32) * scale
    qi = jnp.arange(Q_BLOCK)[:, None]
    kj = jnp.arange(2 * Q_BLOCK)[None, :]
    rel = Q_BLOCK + qi - kj
    band = (rel >= 0) & (rel < SWA_WINDOW)
    exists = (jnp.arange(nb)[:, None, None] > 0) | (kj[None] >= Q_BLOCK)
    valid = band[None] & exists
    s = jnp.where(valid[None, :, None, None], s, -jnp.inf)
    sink = jnp.broadcast_to(
        sinks.astype(jnp.float32).reshape(SWA_KV_HEADS, SWA_GROUPS)[None, None, :, :, None, None],
        s.shape[:-1] + (1,))
    p = jax.nn.softmax(jnp.concatenate([s, sink], axis=-1), axis=-1)[..., :-1]
    o = jnp.einsum('bnhgqk,bnkhd->bnqhgd', p.astype(v.dtype), vv)
    return o.reshape(bsz, s_len, SWA_HEADS * dh)


def _even_mixer(x, w_in, b_f, lam_re, lam_im, log_step, b_re, b_im, c_re, c_im, d_skip, w_glu, w_out):
    bsz, s_len, _ = x.shape
    proj = jnp.einsum('bsd,de->bse', x, w_in)
    q, k, v, f_logit, u = jnp.split(
        proj, [FOX_WIDTH, 2 * FOX_WIDTH, 3 * FOX_WIDTH, 3 * FOX_WIDTH + FOX_HEADS], axis=-1)
    hs = (bsz, s_len, FOX_HEADS, FOX_HEAD_DIM)
    fox = _forgetting_attention(q.reshape(hs), k.reshape(hs), v.reshape(hs), f_logit + b_f)
    fox = fox.reshape(bsz, s_len, FOX_WIDTH)
    y = _s5_scan(u.reshape(bsz, s_len, SSM_GROUPS, SSM_GROUP),
                 lam_re, lam_im, log_step, b_re, b_im, c_re, c_im, d_skip)
    z = jnp.einsum('bsc,ce->bse', jax.nn.gelu(y.reshape(bsz, s_len, SSM_WIDTH)), w_glu)
    ssm = z[..., :SSM_WIDTH] * jax.nn.sigmoid(z[..., SSM_WIDTH:])
    return jnp.einsum('bsc,cd->bsd', jnp.concatenate([fox, ssm], axis=-1), w_out)


def _odd_mixer(x, positions, w_in, sinks, w_out):
    bsz, s_len, _ = x.shape
    proj = jnp.einsum('bsd,de->bse', x, w_in)
    qw = SWA_HEADS * SWA_HEAD_DIM
    kw = SWA_KV_HEADS * SWA_HEAD_DIM
    q, k, v = jnp.split(proj, [qw, qw + kw], axis=-1)
    q = _partial_rope(q.reshape(bsz, s_len, SWA_HEADS, SWA_HEAD_DIM), positions)
    k = _partial_rope(k.reshape(bsz, s_len, SWA_KV_HEADS, SWA_HEAD_DIM), positions)
    v = v.reshape(bsz, s_len, SWA_KV_HEADS, SWA_HEAD_DIM)
    o = _sliding_window_attention(q, k, v, sinks)
    return jnp.einsum('bsc,cd->bsd', o, w_out)


def _conv_ffn(x, w_up, conv_w, conv_b, w_down):
    s_len = x.shape[1]
    h = jnp.einsum('bsd,df->bsf', x, w_up)
    hp = jnp.pad(h, ((0, 0), (CONV_WIDTH - 1, 0), (0, 0)))
    h = conv_b + sum(conv_w[t] * hp[:, t:t + s_len] for t in range(CONV_WIDTH))
    gate = h[..., :D_FF]
    val = h[..., D_FF:]
    return jnp.einsum('bsf,fd->bsd', jax.nn.silu(gate) * val, w_down)


def setup_inputs(seed: int = 0) -> dict:
    key = jax.random.key(seed)
    ks = jax.random.split(key, 26)
    f32 = jnp.float32

    def nrm(k, shape, scale):
        return jax.random.normal(k, shape, f32) * scale

    x = nrm(ks[0], (BATCH, SEQ, D_MODEL), 1.0)
    offs = jax.random.randint(ks[1], (BATCH, 1), 0, 1024, dtype=jnp.int32)
    positions = (offs + jnp.arange(SEQ, dtype=jnp.int32)[None, :]).astype(jnp.int32)

    ev_w_in = nrm(ks[2], (N_EVEN, D_MODEL, EVEN_IN), D_MODEL ** -0.5)
    ev_w_in = ev_w_in.at[:, :, 2 * FOX_WIDTH:3 * FOX_WIDTH].multiply(DEEPNORM_BETA)
    ev_b_f = FORGET_BIAS_INIT + nrm(ks[3], (N_EVEN, FOX_HEADS), 0.1)
    ev_lambda_re = -0.5 + nrm(ks[4], (N_EVEN, SSM_GROUPS, SSM_STATE), 0.01)
    ev_lambda_im = (math.pi * jnp.arange(SSM_STATE, dtype=f32))[None, None, :] + nrm(
        ks[5], (N_EVEN, SSM_GROUPS, SSM_STATE), 0.01)
    ev_log_step = jax.random.uniform(ks[6], (N_EVEN, SSM_GROUPS), f32,
                                     minval=math.log(DT_MIN), maxval=math.log(DT_MAX))
    ev_ssm_b_re = nrm(ks[7], (N_EVEN, SSM_GROUPS, SSM_STATE, SSM_GROUP), (2 * SSM_GROUP) ** -0.5)
    ev_ssm_b_im = nrm(ks[8], (N_EVEN, SSM_GROUPS, SSM_STATE, SSM_GROUP), (2 * SSM_GROUP) ** -0.5)
    ev_ssm_c_re = nrm(ks[9], (N_EVEN, SSM_GROUPS, SSM_GROUP, SSM_STATE), (2 * SSM_STATE) ** -0.5)
    ev_ssm_c_im = nrm(ks[10], (N_EVEN, SSM_GROUPS, SSM_GROUP, SSM_STATE), (2 * SSM_STATE) ** -0.5)
    ev_ssm_d = nrm(ks[11], (N_EVEN, SSM_GROUPS, SSM_GROUP), 1.0)
    ev_w_glu = nrm(ks[12], (N_EVEN, SSM_WIDTH, 2 * SSM_WIDTH), SSM_WIDTH ** -0.5)
    ev_w_out = nrm(ks[13], (N_EVEN, D_MODEL, D_MODEL), D_MODEL ** -0.5 * DEEPNORM_BETA)

    od_w_in = nrm(ks[14], (N_ODD, D_MODEL, ODD_IN), D_MODEL ** -0.5)
    v_start = (SWA_HEADS + SWA_KV_HEADS) * SWA_HEAD_DIM
    od_w_in = od_w_in.at[:, :, v_start:].multiply(DEEPNORM_BETA)
    od_sinks = nrm(ks[15], (N_ODD, SWA_HEADS), 0.1)
    od_w_out = nrm(ks[16], (N_ODD, SWA_HEADS * SWA_HEAD_DIM, D_MODEL),
                   (SWA_HEADS * SWA_HEAD_DIM) ** -0.5 * DEEPNORM_BETA)

    ln_mix_g = 1.0 + nrm(ks[17], (DEPTH, D_MODEL), 0.02)
    ln_mix_b = nrm(ks[18], (DEPTH, D_MODEL), 0.02)
    ffn_w_up = nrm(ks[19], (DEPTH, D_MODEL, 2 * D_FF), D_MODEL ** -0.5)
    ffn_conv_w = nrm(ks[20], (DEPTH, CONV_WIDTH, 2 * D_FF), CONV_WIDTH ** -0.5)
    ffn_conv_b = nrm(ks[21], (DEPTH, 2 * D_FF), 0.02)
    ffn_w_down = nrm(ks[22], (DEPTH, D_FF, D_MODEL), D_FF ** -0.5 * DEEPNORM_BETA)
    ln_ffn_g = 1.0 + nrm(ks[23], (DEPTH, D_MODEL), 0.02)
    ln_ffn_b = nrm(ks[24], (DEPTH, D_MODEL), 0.02)

    return {'x': x, 'positions': positions,
            'ev_w_in': ev_w_in, 'ev_b_f': ev_b_f,
            'ev_lambda_re': ev_lambda_re, 'ev_lambda_im': ev_lambda_im, 'ev_log_step': ev_log_step,
            'ev_ssm_b_re': ev_ssm_b_re, 'ev_ssm_b_im': ev_ssm_b_im,
            'ev_ssm_c_re': ev_ssm_c_re, 'ev_ssm_c_im': ev_ssm_c_im, 'ev_ssm_d': ev_ssm_d,
            'ev_w_glu': ev_w_glu, 'ev_w_out': ev_w_out,
            'od_w_in': od_w_in, 'od_sinks': od_sinks, 'od_w_out': od_w_out,
            'ln_mix_g': ln_mix_g, 'ln_mix_b': ln_mix_b,
            'ffn_w_up': ffn_w_up, 'ffn_conv_w': ffn_conv_w, 'ffn_conv_b': ffn_conv_b,
            'ffn_w_down': ffn_w_down, 'ln_ffn_g': ln_ffn_g, 'ln_ffn_b': ln_ffn_b}


def reference(x, positions, ev_w_in, ev_b_f, ev_lambda_re, ev_lambda_im, ev_log_step,
              ev_ssm_b_re, ev_ssm_b_im, ev_ssm_c_re, ev_ssm_c_im, ev_ssm_d, ev_w_glu, ev_w_out,
              od_w_in, od_sinks, od_w_out, ln_mix_g, ln_mix_b,
              ffn_w_up, ffn_conv_w, ffn_conv_b, ffn_w_down, ln_ffn_g, ln_ffn_b):
    for i in range(DEPTH):
        j = i // 2
        if i % 2 == 0:
            mix = _even_mixer(x, ev_w_in[j], ev_b_f[j], ev_lambda_re[j], ev_lambda_im[j],
                              ev_log_step[j], ev_ssm_b_re[j], ev_ssm_b_im[j], ev_ssm_c_re[j],
                              ev_ssm_c_im[j], ev_ssm_d[j], ev_w_glu[j], ev_w_out[j])
        else:
            mix = _odd_mixer(x, positions, od_w_in[j], od_sinks[j], od_w_out[j])
        x = _layer_norm(DEEPNORM_ALPHA * x + mix, ln_mix_g[i], ln_mix_b[i])
        ffn = _conv_ffn(x, ffn_w_up[i], ffn_conv_w[i], ffn_conv_b[i], ffn_w_down[i])
        x = _layer_norm(DEEPNORM_ALPHA * x + ffn, ln_ffn_g[i], ln_ffn_b[i])
    return x
```

```python
import functools
import math

import jax
import jax.numpy as jnp
from jax import lax
from jax.experimental import pallas as pl
from jax.experimental.pallas import tpu as pltpu

F32 = jnp.float32
BF16 = jnp.bfloat16

DEPTH = 2
FOX_HEADS = 8
FOX_HEAD_DIM = 128
FOX_WIDTH = FOX_HEADS * FOX_HEAD_DIM
SSM_GROUP = 16
SSM_STATE = 64
SSM_CHUNK = 16
SSM_ROW = SSM_CHUNK * SSM_GROUP
SWA_HEADS = 32
SWA_KV_HEADS = 4
SWA_HEAD_DIM = 64
SWA_GROUPS = SWA_HEADS // SWA_KV_HEADS
SWA_WINDOW = 128
ROPE_DIM = SWA_HEAD_DIM // 4
ROPE_THETA = 500000.0
CONV_WIDTH = 3
LN_EPS = 1e-5
DEEPNORM_ALPHA = (2.0 * DEPTH) ** 0.25

V7X_LANES = 128
V7X_BF16_SUBLANES = 16
V7X_VMEM_LIMIT_BYTES = 56 * 1024 * 1024
MASK_VALUE = -1e30

HIGHEST = lax.Precision.HIGHEST


def _params(*semantics):
    return pltpu.CompilerParams(dimension_semantics=semantics,
                                vmem_limit_bytes=V7X_VMEM_LIMIT_BYTES)


def _dot(a, b):
    return jnp.dot(a, b, preferred_element_type=F32)


def _dot_nt(a, b):
    return lax.dot_general(a, b, (((1,), (1,)), ((), ())), preferred_element_type=F32)


def _mm_kernel(a_ref, w_ref, o_ref):
    o_ref[...] = _dot(a_ref[...], w_ref[...]).astype(o_ref.dtype)


def _matmul(a, w, *, bm, bn, out_dtype):
    m, k = a.shape
    n = w.shape[1]
    return pl.pallas_call(
        _mm_kernel,
        grid=(m // bm, n // bn),
        in_specs=[pl.BlockSpec((bm, k), lambda i, j: (i, 0)),
                  pl.BlockSpec((k, bn), lambda i, j: (0, j))],
        out_specs=pl.BlockSpec((bm, bn), lambda i, j: (i, j)),
        out_shape=jax.ShapeDtypeStruct((m, n), out_dtype),
        compiler_params=_params("parallel", "arbitrary"),
        name="matmul",
    )(a, w)


def _mm_res_ln_kernel(*refs, n_pairs, bn):
    a_refs = refs[:n_pairs]
    w_refs = refs[n_pairs:2 * n_pairs]
    x_ref, g_ref, b_ref, o32_ref, o16_ref, pre_ref = refs[2 * n_pairs:]
    j = pl.program_id(1)
    acc = _dot(a_refs[0][...], w_refs[0][...])
    for a_ref, w_ref in zip(a_refs[1:], w_refs[1:]):
        acc = acc + _dot(a_ref[...], w_ref[...])
    col = pl.multiple_of(j * bn, bn)
    pre_ref[:, pl.ds(col, bn)] = acc + DEEPNORM_ALPHA * x_ref[:, pl.ds(col, bn)]

    @pl.when(j == pl.num_programs(1) - 1)
    def _():
        pre = pre_ref[...]
        mu = jnp.mean(pre, axis=-1, keepdims=True)
        cen = pre - mu
        var = jnp.mean(cen * cen, axis=-1, keepdims=True)
        y = cen * lax.rsqrt(var + LN_EPS) * g_ref[...] + b_ref[...]
        o32_ref[...] = y
        o16_ref[...] = y.astype(BF16)


def _matmul_residual_layernorm(pairs, x32, gain, bias, *, bm, bn):
    m, d = x32.shape
    n_pairs = len(pairs)
    a_list = [p[0] for p in pairs]
    w_list = [p[1] for p in pairs]
    in_specs = ([pl.BlockSpec((bm, a.shape[1]), lambda i, j: (i, 0)) for a in a_list]
                + [pl.BlockSpec((w.shape[0], bn), lambda i, j: (0, j)) for w in w_list]
                + [pl.BlockSpec((bm, d), lambda i, j: (i, 0)),
                   pl.BlockSpec((1, d), lambda i, j: (0, 0)),
                   pl.BlockSpec((1, d), lambda i, j: (0, 0))])
    return pl.pallas_call(
        functools.partial(_mm_res_ln_kernel, n_pairs=n_pairs, bn=bn),
        grid=(m // bm, d // bn),
        in_specs=in_specs,
        out_specs=[pl.BlockSpec((bm, d), lambda i, j: (i, 0)),
                   pl.BlockSpec((bm, d), lambda i, j: (i, 0))],
        out_shape=[jax.ShapeDtypeStruct((m, d), F32), jax.ShapeDtypeStruct((m, d), BF16)],
        scratch_shapes=[pltpu.VMEM((bm, d), F32)],
        compiler_params=_params("parallel", "arbitrary"),
        name="matmul_residual_layernorm",
    )(*a_list, *w_list, x32, gain.reshape(1, d).astype(F32), bias.reshape(1, d).astype(F32))


def _bf16_split3(v):
    hi = v.astype(BF16)
    r1 = v - hi.astype(F32)
    mid = r1.astype(BF16)
    lo = (r1 - mid.astype(F32)).astype(BF16)
    return hi, mid, lo


def _forget_gate_kernel(x_ref, wf_ref, bf_ref, c_ref, *, chunk):
    s_len = x_ref.shape[0]
    f = _dot(x_ref[...], wf_ref[...]) + bf_ref[...]
    log_f = jnp.minimum(f, 0.0) - jnp.log(1.0 + jnp.exp(-jnp.abs(f)))
    row = lax.broadcasted_iota(jnp.int32, (chunk, chunk), 0)
    col = lax.broadcasted_iota(jnp.int32, (chunk, chunk), 1)
    tri = (row >= col).astype(BF16)
    carry = jnp.zeros((1, log_f.shape[1]), F32)
    for start in range(0, s_len, chunk):
        hi, mid, lo = _bf16_split3(log_f[start:start + chunk])
        part = _dot(tri, hi) + _dot(tri, mid) + _dot(tri, lo) + carry
        c_ref[start:start + chunk, :] = part
        carry = part[chunk - 1:chunk, :]


def _forget_gate_cumsum(x16, w_f, b_f, *, chunk=256):
    bsz, s_len, d = x16.shape
    n = w_f.shape[1]
    return pl.pallas_call(
        functools.partial(_forget_gate_kernel, chunk=chunk),
        grid=(bsz,),
        in_specs=[pl.BlockSpec((None, s_len, d), lambda b: (b, 0, 0)),
                  pl.BlockSpec((d, n), lambda b: (0, 0)),
                  pl.BlockSpec((1, n), lambda b: (0, 0))],
        out_specs=pl.BlockSpec((None, s_len, n), lambda b: (b, 0, 0)),
        out_shape=jax.ShapeDtypeStruct((bsz, s_len, n), F32),
        compiler_params=_params("parallel"),
        name="forget_gate_cumsum",
    )(x16, w_f, b_f)


def _fox_kernel(q_ref, k_ref, v_ref, ccol_ref, crow_ref, o_ref, *, tile, scale):
    h = pl.program_id(1)
    qi = pl.program_id(2)
    q = q_ref[...]
    lane = lax.broadcasted_iota(jnp.int32, ccol_ref.shape, 1)
    cq = jnp.sum(jnp.where(lane == h, ccol_ref[...], 0.0), axis=1, keepdims=True)

    def step(j, carry, masked):
        m, l, acc = carry
        start = pl.multiple_of(j * tile, tile)
        k = k_ref[pl.ds(start, tile), :]
        v = v_ref[pl.ds(start, tile), :]
        ck = crow_ref[pl.ds(h, 1), pl.ds(start, tile)]
        s = _dot_nt(q, k) * scale + (cq - ck)
        if masked:
            row = lax.broadcasted_iota(jnp.int32, s.shape, 0)
            col = lax.broadcasted_iota(jnp.int32, s.shape, 1)
            s = jnp.where(row >= col, s, MASK_VALUE)
        m_new = jnp.maximum(m, jnp.max(s, axis=1, keepdims=True))
        alpha = jnp.exp(m - m_new)
        p = jnp.exp(s - m_new)
        l = alpha * l + jnp.sum(p, axis=1, keepdims=True)
        acc = alpha * acc + _dot(p.astype(BF16), v)
        return m_new, l, acc

    init = (jnp.full((tile, 1), MASK_VALUE, F32), jnp.zeros((tile, 1), F32),
            jnp.zeros((tile, q.shape[1]), F32))
    carry = lax.fori_loop(0, qi, functools.partial(step, masked=False), init)
    _, l, acc = step(qi, carry, masked=True)
    o_ref[...] = (acc / l).astype(o_ref.dtype)


def _forgetting_attention(qkvu, c_col, c_row, *, tile=256):
    bsz, s_len, _ = qkvu.shape
    dh = FOX_HEAD_DIM
    nh = FOX_HEADS
    return pl.pallas_call(
        functools.partial(_fox_kernel, tile=tile, scale=1.0 / math.sqrt(dh)),
        grid=(bsz, nh, s_len // tile),
        in_specs=[pl.BlockSpec((None, tile, dh), lambda b, h, i: (b, i, h)),
                  pl.BlockSpec((None, s_len, dh), lambda b, h, i: (b, 0, nh + h)),
                  pl.BlockSpec((None, s_len, dh), lambda b, h, i: (b, 0, 2 * nh + h)),
                  pl.BlockSpec((None, tile, c_col.shape[2]), lambda b, h, i: (b, i, 0)),
                  pl.BlockSpec((None, c_row.shape[1], s_len), lambda b, h, i: (b, 0, 0))],
        out_specs=pl.BlockSpec((None, tile, dh), lambda b, h, i: (b, i, h)),
        out_shape=jax.ShapeDtypeStruct((bsz, s_len, nh * dh), BF16),
        compiler_params=_params("parallel", "parallel", "arbitrary"),
        name="forgetting_attention",
    )(qkvu, qkvu, qkvu, c_col, c_row)


def _cmul(ar, ai, br, bi):
    return ar * br - ai * bi, ar * bi + ai * br


def _cpow(ar, ai, e, n_bits):
    rr = jnp.ones_like(ar)
    ri = jnp.zeros_like(ai)
    for bit in range(n_bits):
        nr, ni = _cmul(rr, ri, ar, ai)
        take = ((e >> bit) & 1) == 1
        rr = jnp.where(take, nr, rr)
        ri = jnp.where(take, ni, ri)
        if bit + 1 < n_bits:
            ar, ai = _cmul(ar, ai, ar, ai)
    return rr, ri


def _discretise(lr, li, dt):
    mag = jnp.exp(lr * dt)
    return mag * jnp.cos(li * dt), mag * jnp.sin(li * dt)


def _s5_prep_kernel(lam_row_ref, lam_col_ref, dt_ref, bt_re_ref, bt_im_ref, ct_re_ref, ct_im_ref,
                    d_ref, t0_ref, ws_re_ref, ws_im_ref, wc_re_ref, wc_im_ref, a_chunk_ref):
    chunk, grp, p = SSM_CHUNK, SSM_GROUP, SSM_STATE
    n_bits = chunk.bit_length()
    dt = jnp.exp(dt_ref[...])
    lr_r, li_r = lam_row_ref[0:1, :], lam_row_ref[1:2, :]
    ar_r, ai_r = _discretise(lr_r, li_r, dt)
    den = lr_r * lr_r + li_r * li_r
    xr, xi = ar_r - 1.0, ai_r
    g_re = (xr * lr_r + xi * li_r) / den
    g_im = (xi * lr_r - xr * li_r) / den
    bbt_re, bbt_im = _cmul(g_re, g_im, bt_re_ref[...], bt_im_ref[...])
    bx_re = jnp.concatenate([bbt_re] * chunk, axis=0)
    bx_im = jnp.concatenate([bbt_im] * chunk, axis=0)
    row_i = lax.broadcasted_iota(jnp.int32, (SSM_ROW, p), 0) // grp
    pr, pi = _cpow(jnp.broadcast_to(ar_r, (SSM_ROW, p)), jnp.broadcast_to(ai_r, (SSM_ROW, p)),
                   chunk - 1 - row_i, n_bits)
    ws_re, ws_im = _cmul(pr, pi, bx_re, bx_im)
    ws_re_ref[...] = ws_re.astype(BF16)
    ws_im_ref[...] = ws_im.astype(BF16)
    ac_r, ac_i = _cpow(ar_r, ai_r, jnp.full((1, p), chunk, jnp.int32), n_bits)
    a_chunk_ref[0:1, :] = ac_r
    a_chunk_ref[1:2, :] = ac_i
    lr_c, li_c = lam_col_ref[:, 0:1], lam_col_ref[:, 1:2]
    ar_c, ai_c = _discretise(lr_c, li_c, dt)
    rep = (lax.broadcasted_iota(jnp.int32, (grp, SSM_ROW), 0)
           == lax.broadcasted_iota(jnp.int32, (grp, SSM_ROW), 1) % grp).astype(F32)
    cx_re = jnp.dot(ct_re_ref[...], rep, precision=HIGHEST, preferred_element_type=F32)
    cx_im = jnp.dot(ct_im_ref[...], rep, precision=HIGHEST, preferred_element_type=F32)
    dx = jnp.dot(d_ref[...], rep, precision=HIGHEST, preferred_element_type=F32)
    col_j = lax.broadcasted_iota(jnp.int32, (p, SSM_ROW), 1) // grp
    ab_r = jnp.broadcast_to(ar_c, (p, SSM_ROW))
    ab_i = jnp.broadcast_to(ai_c, (p, SSM_ROW))
    qr, qi = _cpow(ab_r, ab_i, col_j, n_bits)
    f_re, f_im = _cmul(qr, qi, cx_re, cx_im)
    e_re, e_im = _cmul(f_re, f_im, ab_r, ab_i)
    wc_re_ref[...] = e_re.astype(BF16)
    wc_im_ref[...] = (-e_im).astype(BF16)
    r0 = (jnp.dot(bbt_re, f_re, precision=HIGHEST, preferred_element_type=F32)
          - jnp.dot(bbt_im, f_im, precision=HIGHEST, preferred_element_type=F32))
    lane = lax.broadcasted_iota(jnp.int32, (grp, SSM_ROW), 1)
    sub = lax.broadcasted_iota(jnp.int32, (grp, SSM_ROW), 0)
    for i in range(chunk):
        blk = r0 if i == 0 else pltpu.roll(r0, i * grp, 1)
        blk = jnp.where(lane >= i * grp, blk, 0.0)
        blk = blk + jnp.where(lane == i * grp + sub, dx, 0.0)
        t0_ref[i * grp:(i + 1) * grp, :] = blk.astype(BF16)


def _s5_prepare(lam_re, lam_im, log_step, b_re, b_im, c_re, c_im, d_skip):
    g, p = lam_re.shape
    grp = SSM_GROUP
    lam_row = jnp.stack([lam_re, lam_im], axis=1).astype(F32)
    lam_col = jnp.stack([lam_re, lam_im], axis=2).astype(F32)
    dt = log_step.reshape(g, 1, 1).astype(F32)
    bt_re = jnp.swapaxes(b_re, 1, 2).astype(F32)
    bt_im = jnp.swapaxes(b_im, 1, 2).astype(F32)
    ct_re = jnp.swapaxes(c_re, 1, 2).astype(F32)
    ct_im = jnp.swapaxes(c_im, 1, 2).astype(F32)
    d3 = d_skip.reshape(g, 1, grp).astype(F32)

    def spec(*shape):
        return pl.BlockSpec((None,) + shape, lambda i: (i,) + (0,) * len(shape))

    return pl.pallas_call(
        _s5_prep_kernel,
        grid=(g,),
        in_specs=[spec(2, p), spec(p, 2), spec(1, 1), spec(grp, p), spec(grp, p),
                  spec(p, grp), spec(p, grp), spec(1, grp)],
        out_specs=[spec(SSM_ROW, SSM_ROW), spec(SSM_ROW, p), spec(SSM_ROW, p),
                   spec(p, SSM_ROW), spec(p, SSM_ROW), spec(2, p)],
        out_shape=[jax.ShapeDtypeStruct((g, SSM_ROW, SSM_ROW), BF16),
                   jax.ShapeDtypeStruct((g, SSM_ROW, p), BF16),
                   jax.ShapeDtypeStruct((g, SSM_ROW, p), BF16),
                   jax.ShapeDtypeStruct((g, p, SSM_ROW), BF16),
                   jax.ShapeDtypeStruct((g, p, SSM_ROW), BF16),
                   jax.ShapeDtypeStruct((g, 2, p), F32)],
        compiler_params=_params("parallel"),
        name="s5_prepare",
    )(lam_row, lam_col, dt, bt_re, bt_im, ct_re, ct_im, d3)


def _gelu_tanh(y):
    return 0.5 * y * (1.0 + jnp.tanh(math.sqrt(2.0 / math.pi) * (y + 0.044715 * (y * y * y))))


def _s5_scan_kernel(u_ref, t0_ref, ws_re_ref, ws_im_ref, wc_re_ref, wc_im_ref, a_ref, o_ref,
                    sre_ref, sim_ref, hre_ref, him_ref, *, groups, bsz, n_chunks):
    for g in range(groups):
        u = u_ref[g]
        sre_ref[g] = _dot(u, ws_re_ref[g])
        sim_ref[g] = _dot(u, ws_im_ref[g])
    ar = a_ref[:, 0:1, :]
    ai = a_ref[:, 1:2, :]

    def step(k, carry):
        hr, hi = carry
        rows = pl.ds(pl.multiple_of(k * bsz, bsz), bsz)
        hre_ref[:, rows, :] = hr
        him_ref[:, rows, :] = hi
        sr = sre_ref[:, rows, :]
        si = sim_ref[:, rows, :]
        return ar * hr - ai * hi + sr, ar * hi + ai * hr + si

    zero = jnp.zeros((groups, bsz, SSM_STATE), F32)
    lax.fori_loop(0, n_chunks, step, (zero, zero))
    for g in range(groups):
        y = (_dot(u_ref[g], t0_ref[g]) + _dot(hre_ref[g].astype(BF16), wc_re_ref[g])
             + _dot(him_ref[g].astype(BF16), wc_im_ref[g]))
        o_ref[g] = _gelu_tanh(y).astype(o_ref.dtype)


def _s5_scan_gelu(u_rows, mats, *, bsz, groups_per_step=8):
    t0, ws_re, ws_im, wc_re, wc_im, a_chunk = mats
    g, rows, _ = u_rows.shape
    gs = groups_per_step
    p = SSM_STATE

    def spec(*shape):
        return pl.BlockSpec((gs,) + shape, lambda i: (i,) + (0,) * len(shape))

    return pl.pallas_call(
        functools.partial(_s5_scan_kernel, groups=gs, bsz=bsz, n_chunks=rows // bsz),
        grid=(g // gs,),
        in_specs=[spec(rows, SSM_ROW), spec(SSM_ROW, SSM_ROW), spec(SSM_ROW, p), spec(SSM_ROW, p),
                  spec(p, SSM_ROW), spec(p, SSM_ROW), spec(2, p)],
        out_specs=spec(rows, SSM_ROW),
        out_shape=jax.ShapeDtypeStruct((g, rows, SSM_ROW), BF16),
        scratch_shapes=[pltpu.VMEM((gs, rows, p), F32), pltpu.VMEM((gs, rows, p), F32),
                        pltpu.VMEM((gs, rows, p), F32), pltpu.VMEM((gs, rows, p), F32)],
        compiler_params=_params("parallel"),
        name="s5_scan_gelu",
    )(u_rows, t0, ws_re, ws_im, wc_re, wc_im, a_chunk)


def _sigmoid(x):
    return 1.0 / (1.0 + jnp.exp(-x))


def _glu_kernel(a_ref, wv_ref, wg_ref, o_ref):
    a = a_ref[...]
    o_ref[...] = (_dot(a, wv_ref[...]) * _sigmoid(_dot(a, wg_ref[...]))).astype(o_ref.dtype)


def _glu(a, w, *, bm, bn):
    m, k = a.shape
    n = w.shape[1] // 2
    nt = n // bn
    return pl.pallas_call(
        _glu_kernel,
        grid=(m // bm, nt),
        in_specs=[pl.BlockSpec((bm, k), lambda i, j: (i, 0)),
                  pl.BlockSpec((k, bn), lambda i, j: (0, j)),
                  pl.BlockSpec((k, bn), lambda i, j: (0, j + nt))],
        out_specs=pl.BlockSpec((bm, bn), lambda i, j: (i, j)),
        out_shape=jax.ShapeDtypeStruct((m, n), BF16),
        compiler_params=_params("parallel", "arbitrary"),
        name="glu",
    )(a, w, w)


def _ffn_up_kernel(x_ref, halo_ref, wg_ref, wv_ref, cwg_ref, cbg_ref, cwv_ref, cbv_ref, o_ref,
                   xe_ref, *, bm, halo, blocks_per_seq):
    i = pl.program_id(0)

    @pl.when(pl.program_id(1) == 0)
    def _():
        keep = (i % blocks_per_seq != 0).astype(xe_ref.dtype)
        xe_ref[0:halo, :] = halo_ref[...] * keep
        xe_ref[halo:halo + bm, :] = x_ref[...]

    xe = xe_ref[...]

    def conv(w_ref, cw_ref, cb_ref):
        hx = _dot(xe, w_ref[...])
        return (cb_ref[...] + cw_ref[0:1, :] * hx[halo - 2:halo - 2 + bm]
                + cw_ref[1:2, :] * hx[halo - 1:halo - 1 + bm] + cw_ref[2:3, :] * hx[halo:halo + bm])

    gate = conv(wg_ref, cwg_ref, cbg_ref)
    val = conv(wv_ref, cwv_ref, cbv_ref)
    o_ref[...] = (gate * _sigmoid(gate) * val).astype(o_ref.dtype)


def _ffn_up(x16, w_gate, w_val, cw_gate, cb_gate, cw_val, cb_val, *, seq_len, bm, bn):
    m, d = x16.shape
    f = w_gate.shape[1]
    halo = V7X_BF16_SUBLANES
    hb = bm // halo
    w_spec = pl.BlockSpec((d, bn), lambda i, j: (0, j))
    cw_spec = pl.BlockSpec((CONV_WIDTH, bn), lambda i, j: (0, j))
    cb_spec = pl.BlockSpec((1, bn), lambda i, j: (0, j))
    return pl.pallas_call(
        functools.partial(_ffn_up_kernel, bm=bm, halo=halo, blocks_per_seq=seq_len // bm),
        grid=(m // bm, pl.cdiv(f, bn)),
        in_specs=[pl.BlockSpec((bm, d), lambda i, j: (i, 0)),
                  pl.BlockSpec((halo, d), lambda i, j: (jnp.maximum(i * hb - 1, 0), 0)),
                  w_spec, w_spec, cw_spec, cb_spec, cw_spec, cb_spec],
        out_specs=pl.BlockSpec((bm, bn), lambda i, j: (i, j)),
        out_shape=jax.ShapeDtypeStruct((m, f), BF16),
        scratch_shapes=[pltpu.VMEM((halo + bm, d), BF16)],
        compiler_params=_params("parallel", "arbitrary"),
        name="ffn_up_conv_gate",
    )(x16, x16, w_gate, w_val, cw_gate, cb_gate, cw_val, cb_val)


def _conv_ffn(x32, x16, w_up, conv_w, conv_b, w_down, ln_g, ln_b, *, seq_len):
    f = w_down.shape[0]
    act = _ffn_up(x16, w_up[:, :f].astype(BF16), w_up[:, f:].astype(BF16),
                  conv_w[:, :f].astype(F32), conv_b[:f].reshape(1, f).astype(F32),
                  conv_w[:, f:].astype(F32), conv_b[f:].reshape(1, f).astype(F32),
                  seq_len=seq_len, bm=1024, bn=512)
    return _matmul_residual_layernorm([(act, w_down.astype(BF16))], x32, ln_g, ln_b, bm=512, bn=512)


def _rope_proj_kernel(a_ref, w_ref, pos_ref, o_ref, cos_ref, sin_ref, *, rope_tiles):
    j = pl.program_id(1)
    hd, half = SWA_HEAD_DIM, ROPE_DIM // 2

    @pl.when(j == 0)
    def _():
        lane = lax.broadcasted_iota(jnp.int32, (1, V7X_LANES), 1) % hd
        idx = (lane % half).astype(F32)
        inv_freq = jnp.where(lane < ROPE_DIM, jnp.exp(idx * (-math.log(ROPE_THETA) / half)), 0.0)
        ang = pos_ref[...].astype(F32) * inv_freq
        sign = jnp.where(lane < half, -1.0, 1.0)
        cos_ref[...] = jnp.cos(ang)
        sin_ref[...] = jnp.sin(ang) * sign

    acc = _dot(a_ref[...], w_ref[...])

    @pl.when(j < rope_tiles)
    def _():
        bm, bn = acc.shape
        reps = bn // V7X_LANES
        cos = jnp.concatenate([cos_ref[...]] * reps, axis=1)
        sin = jnp.concatenate([sin_ref[...]] * reps, axis=1)
        lane = lax.broadcasted_iota(jnp.int32, acc.shape, 1) % hd
        partner = jnp.where(lane < half, pltpu.roll(acc, bn - half, 1), pltpu.roll(acc, half, 1))
        o_ref[...] = (acc * cos + partner * sin).astype(o_ref.dtype)

    @pl.when(j >= rope_tiles)
    def _():
        o_ref[...] = acc.astype(o_ref.dtype)


def _rope_projection(x16, w, pos_col, *, rope_cols, bm, bn):
    m, k = x16.shape
    n = w.shape[1]
    return pl.pallas_call(
        functools.partial(_rope_proj_kernel, rope_tiles=rope_cols // bn),
        grid=(m // bm, n // bn),
        in_specs=[pl.BlockSpec((bm, k), lambda i, j: (i, 0)),
                  pl.BlockSpec((k, bn), lambda i, j: (0, j)),
                  pl.BlockSpec((bm, 1), lambda i, j: (i, 0))],
        out_specs=pl.BlockSpec((bm, bn), lambda i, j: (i, j)),
        out_shape=jax.ShapeDtypeStruct((m, n), BF16),
        scratch_shapes=[pltpu.VMEM((bm, V7X_LANES), F32), pltpu.VMEM((bm, V7X_LANES), F32)],
        compiler_params=_params("parallel", "arbitrary"),
        name="rope_projection",
    )(x16, w, pos_col)


def _swa_kernel(q_ref, kp_ref, kc_ref, vp_ref, vc_ref, sink_ref, o_ref, *, scale):
    n = pl.program_id(1)
    w, hd, grp = SWA_WINDOW, SWA_HEAD_DIM, SWA_GROUPS
    qi = lax.broadcasted_iota(jnp.int32, (w, 2 * w), 0)
    kj = lax.broadcasted_iota(jnp.int32, (w, 2 * w), 1)
    rel = w + qi - kj
    valid = (rel >= 0) & (rel < w) & ((n > 0) | (kj >= w))
    for hk in range(SWA_KV_HEADS):
        kcols = slice(hk * hd, (hk + 1) * hd)
        k = jnp.concatenate([kp_ref[:, kcols], kc_ref[:, kcols]], axis=0)
        v = jnp.concatenate([vp_ref[:, kcols], vc_ref[:, kcols]], axis=0)
        for g in range(grp):
            hq = hk * grp + g
            q = q_ref[:, hq * hd:(hq + 1) * hd]
            s = jnp.where(valid, _dot_nt(q, k) * scale, MASK_VALUE)
            sink = sink_ref[0:1, hq:hq + 1]
            m = jnp.maximum(jnp.max(s, axis=1, keepdims=True), sink)
            p = jnp.exp(s - m)
            den = jnp.sum(p, axis=1, keepdims=True) + jnp.exp(sink - m)
            o = _dot(p.astype(BF16), v) / den
            o_ref[:, hq * hd:(hq + 1) * hd] = o.astype(o_ref.dtype)


def _sliding_window_attention(qkv, sinks):
    bsz, s_len, _ = qkv.shape
    w, hd = SWA_WINDOW, SWA_HEAD_DIM
    qw = SWA_HEADS * hd
    kw = SWA_KV_HEADS * hd
    k_blk = qw // kw
    v_blk = k_blk + 1
    prev = lambda n: jnp.maximum(n - 1, 0)
    return pl.pallas_call(
        functools.partial(_swa_kernel, scale=1.0 / math.sqrt(hd)),
        grid=(bsz, s_len // w),
        in_specs=[pl.BlockSpec((None, w, qw), lambda b, n: (b, n, 0)),
                  pl.BlockSpec((None, w, kw), lambda b, n: (b, prev(n), k_blk)),
                  pl.BlockSpec((None, w, kw), lambda b, n: (b, n, k_blk)),
                  pl.BlockSpec((None, w, kw), lambda b, n: (b, prev(n), v_blk)),
                  pl.BlockSpec((None, w, kw), lambda b, n: (b, n, v_blk)),
                  pl.BlockSpec((1, SWA_HEADS), lambda b, n: (0, 0))],
        out_specs=pl.BlockSpec((None, w, qw), lambda b, n: (b, n, 0)),
        out_shape=jax.ShapeDtypeStruct((bsz, s_len, qw), BF16),
        compiler_params=_params("parallel", "arbitrary"),
        name="sliding_window_attention",
    )(qkv, qkv, qkv, qkv, qkv, sinks.reshape(1, SWA_HEADS).astype(F32))


def _even_mixer(x32, x16, bsz, s_len, w_in, b_f, lam_re, lam_im, log_step, b_re, b_im, c_re, c_im,
                d_skip, w_glu, w_out, ln_g, ln_b):
    m, d = x16.shape
    fw = FOX_WIDTH
    nh = FOX_HEADS
    ssm_w = w_glu.shape[0]
    n_groups = ssm_w // SSM_GROUP
    n_chunks = s_len // SSM_CHUNK
    w_main = jnp.concatenate([w_in[:, :3 * fw], w_in[:, 3 * fw + nh:]], axis=1).astype(BF16)
    w_f = jnp.pad(w_in[:, 3 * fw:3 * fw + nh], ((0, 0), (0, V7X_LANES - nh))).astype(BF16)
    bias_f = jnp.pad(b_f, (0, V7X_LANES - nh)).reshape(1, V7X_LANES).astype(F32)

    qkvu = _matmul(x16, w_main, bm=1024, bn=1024, out_dtype=BF16).reshape(bsz, s_len, -1)
    c_col = _forget_gate_cumsum(x16.reshape(bsz, s_len, d), w_f, bias_f)
    c_row = jnp.swapaxes(c_col[:, :, :nh], 1, 2)
    fox = _forgetting_attention(qkvu, c_col, c_row).reshape(m, fw)

    mats = _s5_prepare(lam_re, lam_im, log_step, b_re, b_im, c_re, c_im, d_skip)
    u = qkvu[:, :, 3 * fw:].reshape(bsz, n_chunks, SSM_CHUNK, n_groups, SSM_GROUP)
    u_rows = jnp.transpose(u, (3, 1, 0, 2, 4)).reshape(n_groups, n_chunks * bsz, SSM_ROW)
    gy_rows = _s5_scan_gelu(u_rows, mats, bsz=bsz)
    gy = jnp.transpose(gy_rows.reshape(n_groups, n_chunks, bsz, SSM_CHUNK, SSM_GROUP),
                       (2, 1, 3, 0, 4)).reshape(m, ssm_w)
    ssm = _glu(gy, w_glu.astype(BF16), bm=1024, bn=512)
    w_out16 = w_out.astype(BF16)
    return _matmul_residual_layernorm([(fox, w_out16[:fw]), (ssm, w_out16[fw:])], x32, ln_g, ln_b,
                                      bm=512, bn=512)


def _odd_mixer(x32, x16, bsz, s_len, positions, w_in, sinks, w_out, ln_g, ln_b):
    m = x16.shape[0]
    rope_cols = (SWA_HEADS + SWA_KV_HEADS) * SWA_HEAD_DIM
    qkv = _rope_projection(x16, w_in.astype(BF16), positions.reshape(m, 1).astype(jnp.int32),
                           rope_cols=rope_cols, bm=1024, bn=256)
    o = _sliding_window_attention(qkv.reshape(bsz, s_len, -1), sinks).reshape(m, -1)
    return _matmul_residual_layernorm([(o, w_out.astype(BF16))], x32, ln_g, ln_b, bm=512, bn=512)


def kernel(x, positions, ev_w_in, ev_b_f, ev_lambda_re, ev_lambda_im, ev_log_step, ev_ssm_b_re,
           ev_ssm_b_im, ev_ssm_c_re, ev_ssm_c_im, ev_ssm_d, ev_w_glu, ev_w_out, od_w_in, od_sinks,
           od_w_out, ln_mix_g, ln_mix_b, ffn_w_up, ffn_conv_w, ffn_conv_b, ffn_w_down, ln_ffn_g,
           ln_ffn_b):
    bsz, s_len, d = x.shape
    x32 = x.reshape(bsz * s_len, d).astype(F32)
    x16 = x32.astype(BF16)
    for i in range(DEPTH):
        j = i // 2
        if i % 2 == 0:
            x32, x16 = _even_mixer(x32, x16, bsz, s_len, ev_w_in[j], ev_b_f[j], ev_lambda_re[j],
                                   ev_lambda_im[j], ev_log_step[j], ev_ssm_b_re[j], ev_ssm_b_im[j],
                                   ev_ssm_c_re[j], ev_ssm_c_im[j], ev_ssm_d[j], ev_w_glu[j],
                                   ev_w_out[j], ln_mix_g[i], ln_mix_b[i])
        else:
            x32, x16 = _odd_mixer(x32, x16, bsz, s_len, positions, od_w_in[j], od_sinks[j],
                                  od_w_out[j], ln_mix_g[i], ln_mix_b[i])
        x32, x16 = _conv_ffn(x32, x16, ffn_w_up[i], ffn_conv_w[i], ffn_conv_b[i], ffn_w_down[i],
                             ln_ffn_g[i], ln_ffn_b[i], seq_len=s_len)
    return x32.reshape(bsz, s_len, d).astype(x.dtype)
```

```python
import functools
import math

import jax
import jax.numpy as jnp
from jax import lax
from jax.experimental import pallas as pl
from jax.experimental.pallas import tpu as pltpu

F32 = jnp.float32
BF16 = jnp.bfloat16

DEPTH = 2
FOX_HEADS = 8
FOX_HEAD_DIM = 128
FOX_WIDTH = FOX_HEADS * FOX_HEAD_DIM
SSM_GROUP = 16
SSM_STATE = 64
SSM_CHUNK = 16
SSM_ROW = SSM_CHUNK * SSM_GROUP
SWA_HEADS = 32
SWA_KV_HEADS = 4
SWA_HEAD_DIM = 64
SWA_GROUPS = SWA_HEADS // SWA_KV_HEADS
SWA_WINDOW = 128
ROPE_DIM = SWA_HEAD_DIM // 4
ROPE_THETA = 500000.0
CONV_WIDTH = 3
LN_EPS = 1e-5
DEEPNORM_ALPHA = (2.0 * DEPTH) ** 0.25

V7X_LANES = 128
V7X_BF16_SUBLANES = 16
V7X_VMEM_LIMIT_BYTES = 56 * 1024 * 1024
MASK_VALUE = -1e30

HIGHEST = lax.Precision.HIGHEST


def _params(*semantics):
    return pltpu.CompilerParams(dimension_semantics=semantics,
                                vmem_limit_bytes=V7X_VMEM_LIMIT_BYTES)


def _dot(a, b):
    return jnp.dot(a, b, preferred_element_type=F32)


def _dot_nt(a, b):
    return lax.dot_general(a, b, (((1,), (1,)), ((), ())), preferred_element_type=F32)


def _mm_kernel(a_ref, w_ref, o_ref):
    o_ref[...] = _dot(a_ref[...], w_ref[...]).astype(o_ref.dtype)


def _matmul(a, w, *, bm, bn, out_dtype):
    m, k = a.shape
    n = w.shape[1]
    return pl.pallas_call(
        _mm_kernel,
        grid=(m // bm, n // bn),
        in_specs=[pl.BlockSpec((bm, k), lambda i, j: (i, 0)),
                  pl.BlockSpec((k, bn), lambda i, j: (0, j))],
        out_specs=pl.BlockSpec((bm, bn), lambda i, j: (i, j)),
        out_shape=jax.ShapeDtypeStruct((m, n), out_dtype),
        compiler_params=_params("parallel", "arbitrary"),
        name="matmul",
    )(a, w)


def _mm_res_ln_kernel(*refs, n_pairs, bn):
    a_refs = refs[:n_pairs]
    w_refs = refs[n_pairs:2 * n_pairs]
    x_ref, g_ref, b_ref, o32_ref, o16_ref, pre_ref = refs[2 * n_pairs:]
    j = pl.program_id(1)
    acc = _dot(a_refs[0][...], w_refs[0][...])
    for a_ref, w_ref in zip(a_refs[1:], w_refs[1:]):
        acc = acc + _dot(a_ref[...], w_ref[...])
    col = pl.multiple_of(j * bn, bn)
    pre_ref[:, pl.ds(col, bn)] = acc + DEEPNORM_ALPHA * x_ref[:, pl.ds(col, bn)]

    @pl.when(j == pl.num_programs(1) - 1)
    def _():
        pre = pre_ref[...]
        mu = jnp.mean(pre, axis=-1, keepdims=True)
        cen = pre - mu
        var = jnp.mean(cen * cen, axis=-1, keepdims=True)
        y = cen * lax.rsqrt(var + LN_EPS) * g_ref[...] + b_ref[...]
        o32_ref[...] = y
        o16_ref[...] = y.astype(BF16)


def _matmul_residual_layernorm(pairs, x32, gain, bias, *, bm, bn):
    m, d = x32.shape
    n_pairs = len(pairs)
    a_list = [p[0] for p in pairs]
    w_list = [p[1] for p in pairs]
    in_specs = ([pl.BlockSpec((bm, a.shape[1]), lambda i, j: (i, 0)) for a in a_list]
                + [pl.BlockSpec((w.shape[0], bn), lambda i, j: (0, j)) for w in w_list]
                + [pl.BlockSpec((bm, d), lambda i, j: (i, 0)),
                   pl.BlockSpec((1, d), lambda i, j: (0, 0)),
                   pl.BlockSpec((1, d), lambda i, j: (0, 0))])
    return pl.pallas_call(
        functools.partial(_mm_res_ln_kernel, n_pairs=n_pairs, bn=bn),
        grid=(m // bm, d // bn),
        in_specs=in_specs,
        out_specs=[pl.BlockSpec((bm, d), lambda i, j: (i, 0)),
                   pl.BlockSpec((bm, d), lambda i, j: (i, 0))],
        out_shape=[jax.ShapeDtypeStruct((m, d), F32), jax.ShapeDtypeStruct((m, d), BF16)],
        scratch_shapes=[pltpu.VMEM((bm, d), F32)],
        compiler_params=_params("parallel", "arbitrary"),
        name="matmul_residual_layernorm",
    )(*a_list, *w_list, x32, gain.reshape(1, d).astype(F32), bias.reshape(1, d).astype(F32))


def _bf16_split3(v):
    hi = v.astype(BF16)
    r1 = v - hi.astype(F32)
    mid = r1.astype(BF16)
    lo = (r1 - mid.astype(F32)).astype(BF16)
    return hi, mid, lo


def _forget_gate_kernel(x_ref, wf_ref, bf_ref, c_ref, *, chunk):
    s_len = x_ref.shape[0]
    f = _dot(x_ref[...], wf_ref[...]) + bf_ref[...]
    log_f = jnp.minimum(f, 0.0) - jnp.log(1.0 + jnp.exp(-jnp.abs(f)))
    row = lax.broadcasted_iota(jnp.int32, (chunk, chunk), 0)
    col = lax.broadcasted_iota(jnp.int32, (chunk, chunk), 1)
    tri = (row >= col).astype(BF16)
    carry = jnp.zeros((1, log_f.shape[1]), F32)
    for start in range(0, s_len, chunk):
        hi, mid, lo = _bf16_split3(log_f[start:start + chunk])
        part = _dot(tri, hi) + _dot(tri, mid) + _dot(tri, lo) + carry
        c_ref[start:start + chunk, :] = part
        carry = part[chunk - 1:chunk, :]


def _forget_gate_cumsum(x16, w_f, b_f, *, chunk=256):
    bsz, s_len, d = x16.shape
    n = w_f.shape[1]
    return pl.pallas_call(
        functools.partial(_forget_gate_kernel, chunk=chunk),
        grid=(bsz,),
        in_specs=[pl.BlockSpec((None, s_len, d), lambda b: (b, 0, 0)),
                  pl.BlockSpec((d, n), lambda b: (0, 0)),
                  pl.BlockSpec((1, n), lambda b: (0, 0))],
        out_specs=pl.BlockSpec((None, s_len, n), lambda b: (b, 0, 0)),
        out_shape=jax.ShapeDtypeStruct((bsz, s_len, n), F32),
        compiler_params=_params("parallel"),
        name="forget_gate_cumsum",
    )(x16, w_f, b_f)


def _fox_kernel(q_ref, k_ref, v_ref, ccol_ref, crow_ref, o_ref, *, tile, scale, heads):
    qi = pl.program_id(2)
    dh = FOX_HEAD_DIM
    h0 = pl.program_id(1) * heads
    lane = lax.broadcasted_iota(jnp.int32, ccol_ref.shape, 1)
    ccol = ccol_ref[...]
    cq = [jnp.sum(jnp.where(lane == h0 + hh, ccol, 0.0), axis=1, keepdims=True) for hh in range(heads)]

    def step(j, carry, masked):
        start = pl.multiple_of(j * tile, tile)
        out = []
        for hh in range(heads):
            m, l, acc = carry[hh]
            cols = slice(hh * dh, (hh + 1) * dh)
            k = k_ref[pl.ds(start, tile), cols]
            v = v_ref[pl.ds(start, tile), cols]
            ck = crow_ref[pl.ds(h0 + hh, 1), pl.ds(start, tile)]
            s = _dot_nt(q_ref[:, cols], k) * scale + (cq[hh] - ck)
            if masked:
                row = lax.broadcasted_iota(jnp.int32, s.shape, 0)
                col = lax.broadcasted_iota(jnp.int32, s.shape, 1)
                s = jnp.where(row >= col, s, MASK_VALUE)
            m_new = jnp.maximum(m, jnp.max(s, axis=1, keepdims=True))
            alpha = jnp.exp(m - m_new)
            p = jnp.exp(s - m_new)
            l = alpha * l + jnp.sum(p, axis=1, keepdims=True)
            acc = alpha * acc + _dot(p.astype(BF16), v)
            out.append((m_new, l, acc))
        return tuple(out)

    init = tuple((jnp.full((tile, 1), MASK_VALUE, F32), jnp.zeros((tile, 1), F32),
                  jnp.zeros((tile, dh), F32)) for _ in range(heads))
    carry = lax.fori_loop(0, qi, functools.partial(step, masked=False), init)
    carry = step(qi, carry, masked=True)
    for hh in range(heads):
        _, l, acc = carry[hh]
        o_ref[:, hh * dh:(hh + 1) * dh] = (acc / l).astype(o_ref.dtype)


def _forgetting_attention(qkvu, c_col, c_row, *, tile=256, heads=4):
    bsz, s_len, _ = qkvu.shape
    width = heads * FOX_HEAD_DIM
    nblk = FOX_WIDTH // width
    return pl.pallas_call(
        functools.partial(_fox_kernel, tile=tile, scale=1.0 / math.sqrt(FOX_HEAD_DIM), heads=heads),
        grid=(bsz, nblk, s_len // tile),
        in_specs=[pl.BlockSpec((None, tile, width), lambda b, h, i: (b, i, h)),
                  pl.BlockSpec((None, s_len, width), lambda b, h, i: (b, 0, nblk + h)),
                  pl.BlockSpec((None, s_len, width), lambda b, h, i: (b, 0, 2 * nblk + h)),
                  pl.BlockSpec((None, tile, c_col.shape[2]), lambda b, h, i: (b, i, 0)),
                  pl.BlockSpec((None, c_row.shape[1], s_len), lambda b, h, i: (b, 0, 0))],
        out_specs=pl.BlockSpec((None, tile, width), lambda b, h, i: (b, i, h)),
        out_shape=jax.ShapeDtypeStruct((bsz, s_len, FOX_WIDTH), BF16),
        compiler_params=_params("parallel", "parallel", "arbitrary"),
        name="forgetting_attention",
    )(qkvu, qkvu, qkvu, c_col, c_row)


def _cmul(ar, ai, br, bi):
    return ar * br - ai * bi, ar * bi + ai * br


def _cpow(ar, ai, e, n_bits):
    rr = jnp.ones_like(ar)
    ri = jnp.zeros_like(ai)
    for bit in range(n_bits):
        nr, ni = _cmul(rr, ri, ar, ai)
        take = ((e >> bit) & 1) == 1
        rr = jnp.where(take, nr, rr)
        ri = jnp.where(take, ni, ri)
        if bit + 1 < n_bits:
            ar, ai = _cmul(ar, ai, ar, ai)
    return rr, ri


def _discretise(lr, li, dt):
    mag = jnp.exp(lr * dt)
    return mag * jnp.cos(li * dt), mag * jnp.sin(li * dt)


def _s5_prep_kernel(lam_row_ref, lam_col_ref, dt_ref, bt_re_ref, bt_im_ref, ct_re_ref, ct_im_ref,
                    d_ref, t0_ref, ws_re_ref, ws_im_ref, wc_re_ref, wc_im_ref, a_chunk_ref):
    chunk, grp, p = SSM_CHUNK, SSM_GROUP, SSM_STATE
    n_bits = chunk.bit_length()
    dt = jnp.exp(dt_ref[...])
    lr_r, li_r = lam_row_ref[0:1, :], lam_row_ref[1:2, :]
    ar_r, ai_r = _discretise(lr_r, li_r, dt)
    den = lr_r * lr_r + li_r * li_r
    xr, xi = ar_r - 1.0, ai_r
    g_re = (xr * lr_r + xi * li_r) / den
    g_im = (xi * lr_r - xr * li_r) / den
    bbt_re, bbt_im = _cmul(g_re, g_im, bt_re_ref[...], bt_im_ref[...])
    bx_re = jnp.concatenate([bbt_re] * chunk, axis=0)
    bx_im = jnp.concatenate([bbt_im] * chunk, axis=0)
    row_i = lax.broadcasted_iota(jnp.int32, (SSM_ROW, p), 0) // grp
    pr, pi = _cpow(jnp.broadcast_to(ar_r, (SSM_ROW, p)), jnp.broadcast_to(ai_r, (SSM_ROW, p)),
                   chunk - 1 - row_i, n_bits)
    ws_re, ws_im = _cmul(pr, pi, bx_re, bx_im)
    ws_re_ref[...] = ws_re.astype(BF16)
    ws_im_ref[...] = ws_im.astype(BF16)
    ac_r, ac_i = _cpow(ar_r, ai_r, jnp.full((1, p), chunk, jnp.int32), n_bits)
    a_chunk_ref[0:1, :] = ac_r
    a_chunk_ref[1:2, :] = ac_i
    lr_c, li_c = lam_col_ref[:, 0:1], lam_col_ref[:, 1:2]
    ar_c, ai_c = _discretise(lr_c, li_c, dt)
    rep = (lax.broadcasted_iota(jnp.int32, (grp, SSM_ROW), 0)
           == lax.broadcasted_iota(jnp.int32, (grp, SSM_ROW), 1) % grp).astype(F32)
    cx_re = jnp.dot(ct_re_ref[...], rep, precision=HIGHEST, preferred_element_type=F32)
    cx_im = jnp.dot(ct_im_ref[...], rep, precision=HIGHEST, preferred_element_type=F32)
    dx = jnp.dot(d_ref[...], rep, precision=HIGHEST, preferred_element_type=F32)
    col_j = lax.broadcasted_iota(jnp.int32, (p, SSM_ROW), 1) // grp
    ab_r = jnp.broadcast_to(ar_c, (p, SSM_ROW))
    ab_i = jnp.broadcast_to(ai_c, (p, SSM_ROW))
    qr, qi = _cpow(ab_r, ab_i, col_j, n_bits)
    f_re, f_im = _cmul(qr, qi, cx_re, cx_im)
    e_re, e_im = _cmul(f_re, f_im, ab_r, ab_i)
    wc_re_ref[...] = e_re.astype(BF16)
    wc_im_ref[...] = (-e_im).astype(BF16)
    r0 = (jnp.dot(bbt_re, f_re, precision=HIGHEST, preferred_element_type=F32)
          - jnp.dot(bbt_im, f_im, precision=HIGHEST, preferred_element_type=F32))
    lane = lax.broadcasted_iota(jnp.int32, (grp, SSM_ROW), 1)
    sub = lax.broadcasted_iota(jnp.int32, (grp, SSM_ROW), 0)
    for i in range(chunk):
        blk = r0 if i == 0 else pltpu.roll(r0, i * grp, 1)
        blk = jnp.where(lane >= i * grp, blk, 0.0)
        blk = blk + jnp.where(lane == i * grp + sub, dx, 0.0)
        t0_ref[i * grp:(i + 1) * grp, :] = blk.astype(BF16)


def _s5_prepare(lam_re, lam_im, log_step, b_re, b_im, c_re, c_im, d_skip):
    g, p = lam_re.shape
    grp = SSM_GROUP
    lam_row = jnp.stack([lam_re, lam_im], axis=1).astype(F32)
    lam_col = jnp.stack([lam_re, lam_im], axis=2).astype(F32)
    dt = log_step.reshape(g, 1, 1).astype(F32)
    bt_re = jnp.swapaxes(b_re, 1, 2).astype(F32)
    bt_im = jnp.swapaxes(b_im, 1, 2).astype(F32)
    ct_re = jnp.swapaxes(c_re, 1, 2).astype(F32)
    ct_im = jnp.swapaxes(c_im, 1, 2).astype(F32)
    d3 = d_skip.reshape(g, 1, grp).astype(F32)

    def spec(*shape):
        return pl.BlockSpec((None,) + shape, lambda i: (i,) + (0,) * len(shape))

    return pl.pallas_call(
        _s5_prep_kernel,
        grid=(g,),
        in_specs=[spec(2, p), spec(p, 2), spec(1, 1), spec(grp, p), spec(grp, p),
                  spec(p, grp), spec(p, grp), spec(1, grp)],
        out_specs=[spec(SSM_ROW, SSM_ROW), spec(SSM_ROW, p), spec(SSM_ROW, p),
                   spec(p, SSM_ROW), spec(p, SSM_ROW), spec(2, p)],
        out_shape=[jax.ShapeDtypeStruct((g, SSM_ROW, SSM_ROW), BF16),
                   jax.ShapeDtypeStruct((g, SSM_ROW, p), BF16),
                   jax.ShapeDtypeStruct((g, SSM_ROW, p), BF16),
                   jax.ShapeDtypeStruct((g, p, SSM_ROW), BF16),
                   jax.ShapeDtypeStruct((g, p, SSM_ROW), BF16),
                   jax.ShapeDtypeStruct((g, 2, p), F32)],
        compiler_params=_params("parallel"),
        name="s5_prepare",
    )(lam_row, lam_col, dt, bt_re, bt_im, ct_re, ct_im, d3)


def _gelu_tanh(y):
    return 0.5 * y * (1.0 + jnp.tanh(math.sqrt(2.0 / math.pi) * (y + 0.044715 * (y * y * y))))


def _s5_scan_kernel(u_ref, t0_ref, ws_re_ref, ws_im_ref, wc_re_ref, wc_im_ref, a_ref, o_ref,
                    sre_ref, sim_ref, hre_ref, him_ref, *, groups, bsz, n_chunks):
    for g in range(groups):
        u = u_ref[g]
        sre_ref[g] = _dot(u, ws_re_ref[g])
        sim_ref[g] = _dot(u, ws_im_ref[g])
    ar = a_ref[:, 0:1, :]
    ai = a_ref[:, 1:2, :]

    def step(k, carry):
        hr, hi = carry
        rows = pl.ds(pl.multiple_of(k * bsz, bsz), bsz)
        hre_ref[:, rows, :] = hr
        him_ref[:, rows, :] = hi
        sr = sre_ref[:, rows, :]
        si = sim_ref[:, rows, :]
        return ar * hr - ai * hi + sr, ar * hi + ai * hr + si

    zero = jnp.zeros((groups, bsz, SSM_STATE), F32)
    lax.fori_loop(0, n_chunks, step, (zero, zero))
    for g in range(groups):
        y = (_dot(u_ref[g], t0_ref[g]) + _dot(hre_ref[g].astype(BF16), wc_re_ref[g])
             + _dot(him_ref[g].astype(BF16), wc_im_ref[g]))
        o_ref[g] = _gelu_tanh(y).astype(o_ref.dtype)


def _s5_scan_gelu(u_rows, mats, *, bsz, groups_per_step=8):
    t0, ws_re, ws_im, wc_re, wc_im, a_chunk = mats
    g, rows, _ = u_rows.shape
    gs = groups_per_step
    p = SSM_STATE

    def spec(*shape):
        return pl.BlockSpec((gs,) + shape, lambda i: (i,) + (0,) * len(shape))

    return pl.pallas_call(
        functools.partial(_s5_scan_kernel, groups=gs, bsz=bsz, n_chunks=rows // bsz),
        grid=(g // gs,),
        in_specs=[spec(rows, SSM_ROW), spec(SSM_ROW, SSM_ROW), spec(SSM_ROW, p), spec(SSM_ROW, p),
                  spec(p, SSM_ROW), spec(p, SSM_ROW), spec(2, p)],
        out_specs=spec(rows, SSM_ROW),
        out_shape=jax.ShapeDtypeStruct((g, rows, SSM_ROW), BF16),
        scratch_shapes=[pltpu.VMEM((gs, rows, p), F32), pltpu.VMEM((gs, rows, p), F32),
                        pltpu.VMEM((gs, rows, p), F32), pltpu.VMEM((gs, rows, p), F32)],
        compiler_params=_params("parallel"),
        name="s5_scan_gelu",
    )(u_rows, t0, ws_re, ws_im, wc_re, wc_im, a_chunk)


def _sigmoid(x):
    return 1.0 / (1.0 + jnp.exp(-x))


def _glu_kernel(a_ref, wv_ref, wg_ref, o_ref):
    a = a_ref[...]
    o_ref[...] = (_dot(a, wv_ref[...]) * _sigmoid(_dot(a, wg_ref[...]))).astype(o_ref.dtype)


def _glu(a, w, *, bm, bn):
    m, k = a.shape
    n = w.shape[1] // 2
    nt = n // bn
    return pl.pallas_call(
        _glu_kernel,
        grid=(m // bm, nt),
        in_specs=[pl.BlockSpec((bm, k), lambda i, j: (i, 0)),
                  pl.BlockSpec((k, bn), lambda i, j: (0, j)),
                  pl.BlockSpec((k, bn), lambda i, j: (0, j + nt))],
        out_specs=pl.BlockSpec((bm, bn), lambda i, j: (i, j)),
        out_shape=jax.ShapeDtypeStruct((m, n), BF16),
        compiler_params=_params("parallel", "arbitrary"),
        name="glu",
    )(a, w, w)


def _ffn_up_kernel(x_ref, halo_ref, wg_ref, wv_ref, cwg_ref, cbg_ref, cwv_ref, cbv_ref, o_ref,
                   xe_ref, *, bm, halo, blocks_per_seq):
    i = pl.program_id(0)

    @pl.when(pl.program_id(1) == 0)
    def _():
        keep = (i % blocks_per_seq != 0).astype(xe_ref.dtype)
        xe_ref[0:halo, :] = halo_ref[...] * keep
        xe_ref[halo:halo + bm, :] = x_ref[...]

    xe = xe_ref[...]

    def conv(w_ref, cw_ref, cb_ref):
        hx = _dot(xe, w_ref[...])
        return (cb_ref[...] + cw_ref[0:1, :] * hx[halo - 2:halo - 2 + bm]
                + cw_ref[1:2, :] * hx[halo - 1:halo - 1 + bm] + cw_ref[2:3, :] * hx[halo:halo + bm])

    gate = conv(wg_ref, cwg_ref, cbg_ref)
    val = conv(wv_ref, cwv_ref, cbv_ref)
    o_ref[...] = (gate * _sigmoid(gate) * val).astype(o_ref.dtype)


def _ffn_up(x16, w_gate, w_val, cw_gate, cb_gate, cw_val, cb_val, *, seq_len, bm, bn):
    m, d = x16.shape
    f = w_gate.shape[1]
    halo = V7X_BF16_SUBLANES
    hb = bm // halo
    w_spec = pl.BlockSpec((d, bn), lambda i, j: (0, j))
    cw_spec = pl.BlockSpec((CONV_WIDTH, bn), lambda i, j: (0, j))
    cb_spec = pl.BlockSpec((1, bn), lambda i, j: (0, j))
    return pl.pallas_call(
        functools.partial(_ffn_up_kernel, bm=bm, halo=halo, blocks_per_seq=seq_len // bm),
        grid=(m // bm, pl.cdiv(f, bn)),
        in_specs=[pl.BlockSpec((bm, d), lambda i, j: (i, 0)),
                  pl.BlockSpec((halo, d), lambda i, j: (jnp.maximum(i * hb - 1, 0), 0)),
                  w_spec, w_spec, cw_spec, cb_spec, cw_spec, cb_spec],
        out_specs=pl.BlockSpec((bm, bn), lambda i, j: (i, j)),
        out_shape=jax.ShapeDtypeStruct((m, f), BF16),
        scratch_shapes=[pltpu.VMEM((halo + bm, d), BF16)],
        compiler_params=_params("parallel", "arbitrary"),
        name="ffn_up_conv_gate",
    )(x16, x16, w_gate, w_val, cw_gate, cb_gate, cw_val, cb_val)


def _conv_ffn(x32, x16, w_up, conv_w, conv_b, w_down, ln_g, ln_b, *, seq_len):
    f = w_down.shape[0]
    act = _ffn_up(x16, w_up[:, :f].astype(BF16), w_up[:, f:].astype(BF16),
                  conv_w[:, :f].astype(F32), conv_b[:f].reshape(1, f).astype(F32),
                  conv_w[:, f:].astype(F32), conv_b[f:].reshape(1, f).astype(F32),
                  seq_len=seq_len, bm=1024, bn=512)
    return _matmul_residual_layernorm([(act, w_down.astype(BF16))], x32, ln_g, ln_b, bm=512, bn=512)


def _rope_proj_kernel(a_ref, w_ref, pos_ref, o_ref, cos_ref, sin_ref, *, rope_tiles, q_tiles, q_scale):
    j = pl.program_id(1)
    hd, half = SWA_HEAD_DIM, ROPE_DIM // 2

    @pl.when(j == 0)
    def _():
        lane = lax.broadcasted_iota(jnp.int32, (1, V7X_LANES), 1) % hd
        idx = (lane % half).astype(F32)
        inv_freq = jnp.where(lane < ROPE_DIM, jnp.exp(idx * (-math.log(ROPE_THETA) / half)), 0.0)
        ang = pos_ref[...].astype(F32) * inv_freq
        sign = jnp.where(lane < half, -1.0, 1.0)
        cos_ref[...] = jnp.cos(ang)
        sin_ref[...] = jnp.sin(ang) * sign

    acc = _dot(a_ref[...], w_ref[...])

    @pl.when(j < rope_tiles)
    def _():
        bm, bn = acc.shape
        reps = bn // V7X_LANES
        cos = jnp.concatenate([cos_ref[...]] * reps, axis=1)
        sin = jnp.concatenate([sin_ref[...]] * reps, axis=1)
        lane = lax.broadcasted_iota(jnp.int32, acc.shape, 1) % hd
        partner = jnp.where(lane < half, pltpu.roll(acc, bn - half, 1), pltpu.roll(acc, half, 1))
        rot = acc * cos + partner * sin
        o_ref[...] = (rot * jnp.where(j < q_tiles, q_scale, 1.0)).astype(o_ref.dtype)

    @pl.when(j >= rope_tiles)
    def _():
        o_ref[...] = acc.astype(o_ref.dtype)


def _rope_projection(x16, w, pos_col, *, rope_cols, q_cols, q_scale, bm, bn):
    m, k = x16.shape
    n = w.shape[1]
    return pl.pallas_call(
        functools.partial(_rope_proj_kernel, rope_tiles=rope_cols // bn, q_tiles=q_cols // bn,
                          q_scale=q_scale),
        grid=(m // bm, n // bn),
        in_specs=[pl.BlockSpec((bm, k), lambda i, j: (i, 0)),
                  pl.BlockSpec((k, bn), lambda i, j: (0, j)),
                  pl.BlockSpec((bm, 1), lambda i, j: (i, 0))],
        out_specs=pl.BlockSpec((bm, bn), lambda i, j: (i, j)),
        out_shape=jax.ShapeDtypeStruct((m, n), BF16),
        scratch_shapes=[pltpu.VMEM((bm, V7X_LANES), F32), pltpu.VMEM((bm, V7X_LANES), F32)],
        compiler_params=_params("parallel", "arbitrary"),
        name="rope_projection",
    )(x16, w, pos_col)


def _swa_kernel(sink_ref, q_ref, kp_ref, kc_ref, vp_ref, vc_ref, o_ref):
    n = pl.program_id(1)
    w, hd, lanes = SWA_WINDOW, SWA_HEAD_DIM, V7X_LANES
    pairs = SWA_GROUPS // 2
    qi = lax.broadcasted_iota(jnp.int32, (w, 2 * w), 0)
    kj = lax.broadcasted_iota(jnp.int32, (w, 2 * w), 1)
    rel = w + qi - kj
    valid = (rel >= 0) & (rel < w) & ((n > 0) | (kj >= w))
    bias = jnp.where(valid, 0.0, MASK_VALUE)
    lane_kv = lax.broadcasted_iota(jnp.int32, (2 * w, lanes), 1)
    lane_o = lax.broadcasted_iota(jnp.int32, (w, lanes), 1)

    def both_halves(prev_ref, cur_ref, hk):
        slab = slice((hk // 2) * lanes, (hk // 2 + 1) * lanes)
        x = jnp.concatenate([prev_ref[:, slab], cur_ref[:, slab]], axis=0).astype(F32)
        upper = hk % 2 == 1
        own = jnp.where((lane_kv >= hd) if upper else (lane_kv < hd), x, 0.0)
        other = pltpu.roll(own, hd, 1)
        lo, hi = (other, own) if upper else (own, other)
        return jnp.concatenate([lo, hi], axis=0).astype(BF16)

    for hk in range(SWA_KV_HEADS):
        kk = both_halves(kp_ref, kc_ref, hk)
        vv = both_halves(vp_ref, vc_ref, hk)
        slabs = [slice((hk * pairs + r) * lanes, (hk * pairs + r + 1) * lanes) for r in range(pairs)]
        qs = jnp.concatenate([q_ref[:, sl] for sl in slabs], axis=0)
        s_all = _dot_nt(qs, kk)
        probs, inv_den = [], []
        for r in range(pairs):
            halves, inv = [], []
            for half in range(2):
                sink = sink_ref[hk * SWA_GROUPS + 2 * r + half]
                s = s_all[r * w:(r + 1) * w, half * 2 * w:(half + 1) * 2 * w] + bias
                m = jnp.maximum(jnp.max(s, axis=1, keepdims=True), sink)
                p = jnp.exp(s - m)
                inv.append(1.0 / (jnp.sum(p, axis=1, keepdims=True) + jnp.exp(sink - m)))
                halves.append(p.astype(BF16))
            probs.append(jnp.concatenate(halves, axis=1))
            inv_den.append(jnp.where(lane_o < hd, inv[0], inv[1]))
        o_all = _dot(jnp.concatenate(probs, axis=0), vv)
        for r in range(pairs):
            o_ref[:, slabs[r]] = (o_all[r * w:(r + 1) * w] * inv_den[r]).astype(o_ref.dtype)


def _sliding_window_attention(qkv, sinks):
    bsz, s_len, _ = qkv.shape
    w, hd = SWA_WINDOW, SWA_HEAD_DIM
    qw = SWA_HEADS * hd
    kw = SWA_KV_HEADS * hd
    k_blk = qw // kw
    v_blk = k_blk + 1
    prev = lambda n: jnp.maximum(n - 1, 0)
    return pl.pallas_call(
        _swa_kernel,
        grid=(bsz, s_len // w),
        in_specs=[pl.BlockSpec(memory_space=pltpu.SMEM),
                  pl.BlockSpec((None, w, qw), lambda b, n: (b, n, 0)),
                  pl.BlockSpec((None, w, kw), lambda b, n: (b, prev(n), k_blk)),
                  pl.BlockSpec((None, w, kw), lambda b, n: (b, n, k_blk)),
                  pl.BlockSpec((None, w, kw), lambda b, n: (b, prev(n), v_blk)),
                  pl.BlockSpec((None, w, kw), lambda b, n: (b, n, v_blk))],
        out_specs=pl.BlockSpec((None, w, qw), lambda b, n: (b, n, 0)),
        out_shape=jax.ShapeDtypeStruct((bsz, s_len, qw), BF16),
        compiler_params=_params("parallel", "arbitrary"),
        name="sliding_window_attention",
    )(sinks.astype(F32), qkv, qkv, qkv, qkv, qkv)


def _even_mixer(x32, x16, bsz, s_len, w_in, b_f, lam_re, lam_im, log_step, b_re, b_im, c_re, c_im,
                d_skip, w_glu, w_out, ln_g, ln_b):
    m, d = x16.shape
    fw = FOX_WIDTH
    nh = FOX_HEADS
    ssm_w = w_glu.shape[0]
    n_groups = ssm_w // SSM_GROUP
    n_chunks = s_len // SSM_CHUNK
    w_main = jnp.concatenate([w_in[:, :3 * fw], w_in[:, 3 * fw + nh:]], axis=1).astype(BF16)
    w_f = jnp.pad(w_in[:, 3 * fw:3 * fw + nh], ((0, 0), (0, V7X_LANES - nh))).astype(BF16)
    bias_f = jnp.pad(b_f, (0, V7X_LANES - nh)).reshape(1, V7X_LANES).astype(F32)

    qkvu = _matmul(x16, w_main, bm=1024, bn=1024, out_dtype=BF16).reshape(bsz, s_len, -1)
    c_col = _forget_gate_cumsum(x16.reshape(bsz, s_len, d), w_f, bias_f)
    c_row = jnp.swapaxes(c_col[:, :, :nh], 1, 2)
    fox = _forgetting_attention(qkvu, c_col, c_row).reshape(m, fw)

    mats = _s5_prepare(lam_re, lam_im, log_step, b_re, b_im, c_re, c_im, d_skip)
    u = qkvu[:, :, 3 * fw:].reshape(bsz, n_chunks, SSM_CHUNK, n_groups, SSM_GROUP)
    u_rows = jnp.transpose(u, (3, 1, 0, 2, 4)).reshape(n_groups, n_chunks * bsz, SSM_ROW)
    gy_rows = _s5_scan_gelu(u_rows, mats, bsz=bsz)
    gy = jnp.transpose(gy_rows.reshape(n_groups, n_chunks, bsz, SSM_CHUNK, SSM_GROUP),
                       (2, 1, 3, 0, 4)).reshape(m, ssm_w)
    ssm = _glu(gy, w_glu.astype(BF16), bm=1024, bn=512)
    w_out16 = w_out.astype(BF16)
    return _matmul_residual_layernorm([(fox, w_out16[:fw]), (ssm, w_out16[fw:])], x32, ln_g, ln_b,
                                      bm=512, bn=512)


def _odd_mixer(x32, x16, bsz, s_len, positions, w_in, sinks, w_out, ln_g, ln_b):
    m = x16.shape[0]
    rope_cols = (SWA_HEADS + SWA_KV_HEADS) * SWA_HEAD_DIM
    qkv = _rope_projection(x16, w_in.astype(BF16), positions.reshape(m, 1).astype(jnp.int32),
                           rope_cols=rope_cols, q_cols=SWA_HEADS * SWA_HEAD_DIM,
                           q_scale=1.0 / math.sqrt(SWA_HEAD_DIM), bm=1024, bn=256)
    o = _sliding_window_attention(qkv.reshape(bsz, s_len, -1), sinks).reshape(m, -1)
    return _matmul_residual_layernorm([(o, w_out.astype(BF16))], x32, ln_g, ln_b, bm=512, bn=512)


def kernel(x, positions, ev_w_in, ev_b_f, ev_lambda_re, ev_lambda_im, ev_log_step, ev_ssm_b_re,
           ev_ssm_b_im, ev_ssm_c_re, ev_ssm_c_im, ev_ssm_d, ev_w_glu, ev_w_out, od_w_in, od_sinks,
           od_w_out, ln_mix_g, ln_mix_b, ffn_w_up, ffn_conv_w, ffn_conv_b, ffn_w_down, ln_ffn_g,
           ln_ffn_b):
    bsz, s_len, d = x.shape
    x32 = x.reshape(bsz * s_len, d).astype(F32)
    x16 = x32.astype(BF16)
    for i in range(DEPTH):
        j = i // 2
        if i % 2 == 0:
            x32, x16 = _even_mixer(x32, x16, bsz, s_len, ev_w_in[j], ev_b_f[j], ev_lambda_re[j],
                                   ev_lambda_im[j], ev_log_step[j], ev_ssm_b_re[j], ev_ssm_b_im[j],
                                   ev_ssm_c_re[j], ev_ssm_c_im[j], ev_ssm_d[j], ev_w_glu[j],
                                   ev_w_out[j], ln_mix_g[i], ln_mix_b[i])
        else:
            x32, x16 = _odd_mixer(x32, x16, bsz, s_len, positions, od_w_in[j], od_sinks[j],
                                  od_w_out[j], ln_mix_g[i], ln_mix_b[i])
        x32, x16 = _conv_ffn(x32, x16, ffn_w_up[i], ffn_conv_w[i], ffn_conv_b[i], ffn_w_down[i],
                             ln_ffn_g[i], ln_ffn_b[i], seq_len=s_len)
    return x32.reshape(bsz, s_len, d).astype(x.dtype)
```

```python
import functools
import math

import jax
import jax.numpy as jnp
from jax import lax
from jax.experimental import pallas as pl
from jax.experimental.pallas import tpu as pltpu

F32 = jnp.float32
BF16 = jnp.bfloat16

DEPTH = 2
FOX_HEADS = 8
FOX_HEAD_DIM = 128
FOX_WIDTH = FOX_HEADS * FOX_HEAD_DIM
SSM_GROUP = 16
SSM_STATE = 64
SSM_CHUNK = 16
SSM_ROW = SSM_CHUNK * SSM_GROUP
SWA_HEADS = 32
SWA_KV_HEADS = 4
SWA_HEAD_DIM = 64
SWA_GROUPS = SWA_HEADS // SWA_KV_HEADS
SWA_WINDOW = 128
ROPE_DIM = SWA_HEAD_DIM // 4
ROPE_THETA = 500000.0
CONV_WIDTH = 3
LN_EPS = 1e-5
DEEPNORM_ALPHA = (2.0 * DEPTH) ** 0.25

V7X_LANES = 128
V7X_VMEM_LIMIT_BYTES = 56 * 1024 * 1024
MASK_VALUE = -1e30

HIGHEST = lax.Precision.HIGHEST


def _params(*semantics):
    return pltpu.CompilerParams(dimension_semantics=semantics,
                                vmem_limit_bytes=V7X_VMEM_LIMIT_BYTES)


def _dot(a, b):
    return jnp.dot(a, b, preferred_element_type=F32)


def _dot_nt(a, b):
    return lax.dot_general(a, b, (((1,), (1,)), ((), ())), preferred_element_type=F32)


def _mm_kernel(a_ref, w_ref, o_ref):
    o_ref[...] = _dot(a_ref[...], w_ref[...]).astype(o_ref.dtype)


def _matmul(a, w, *, bm, bn, out_dtype):
    m, k = a.shape
    n = w.shape[1]
    return pl.pallas_call(
        _mm_kernel,
        grid=(m // bm, n // bn),
        in_specs=[pl.BlockSpec((bm, k), lambda i, j: (i, 0)),
                  pl.BlockSpec((k, bn), lambda i, j: (0, j))],
        out_specs=pl.BlockSpec((bm, bn), lambda i, j: (i, j)),
        out_shape=jax.ShapeDtypeStruct((m, n), out_dtype),
        compiler_params=_params("parallel", "arbitrary"),
        name="matmul",
    )(a, w)


def _mm_res_ln_kernel(*refs, n_pairs, sub):
    a_refs = refs[:n_pairs]
    w_refs = refs[n_pairs:2 * n_pairs]
    x_ref, g_ref, b_ref, o32_ref, o16_ref = refs[2 * n_pairs:]
    for r in range(x_ref.shape[0] // sub):
        rows = slice(r * sub, (r + 1) * sub)
        pre = DEEPNORM_ALPHA * x_ref[rows, :]
        for a_ref, w_ref in zip(a_refs, w_refs):
            pre = pre + _dot(a_ref[rows, :], w_ref[...])
        mu = jnp.mean(pre, axis=-1, keepdims=True)
        cen = pre - mu
        var = jnp.mean(cen * cen, axis=-1, keepdims=True)
        y = cen * lax.rsqrt(var + LN_EPS) * g_ref[...] + b_ref[...]
        o32_ref[rows, :] = y
        o16_ref[rows, :] = y.astype(BF16)


def _matmul_residual_layernorm(pairs, x32, gain, bias, *, bm, sub=256):
    m, d = x32.shape
    n_pairs = len(pairs)
    a_list = [p[0] for p in pairs]
    w_list = [p[1] for p in pairs]
    row_spec = lambda width: pl.BlockSpec((bm, width), lambda i: (i, 0))
    const_spec = lambda rows: pl.BlockSpec((rows, d), lambda i: (0, 0), pipeline_mode=pl.Buffered(1))
    in_specs = ([row_spec(a.shape[1]) for a in a_list] + [const_spec(w.shape[0]) for w in w_list]
                + [row_spec(d), const_spec(1), const_spec(1)])
    return pl.pallas_call(
        functools.partial(_mm_res_ln_kernel, n_pairs=n_pairs, sub=sub),
        grid=(m // bm,),
        in_specs=in_specs,
        out_specs=[row_spec(d), row_spec(d)],
        out_shape=[jax.ShapeDtypeStruct((m, d), F32), jax.ShapeDtypeStruct((m, d), BF16)],
        compiler_params=_params("parallel"),
        name="matmul_residual_layernorm",
    )(*a_list, *w_list, x32, gain.reshape(1, d).astype(F32), bias.reshape(1, d).astype(F32))


def _bf16_split3(v):
    hi = v.astype(BF16)
    r1 = v - hi.astype(F32)
    mid = r1.astype(BF16)
    lo = (r1 - mid.astype(F32)).astype(BF16)
    return hi, mid, lo


def _forget_gate_kernel(x_ref, wf_ref, bf_ref, c_ref, *, chunk):
    s_len = x_ref.shape[0]
    f = _dot(x_ref[...], wf_ref[...]) + bf_ref[...]
    log_f = jnp.minimum(f, 0.0) - jnp.log(1.0 + jnp.exp(-jnp.abs(f)))
    row = lax.broadcasted_iota(jnp.int32, (chunk, chunk), 0)
    col = lax.broadcasted_iota(jnp.int32, (chunk, chunk), 1)
    tri = (row >= col).astype(BF16)
    carry = jnp.zeros((1, log_f.shape[1]), F32)
    for start in range(0, s_len, chunk):
        hi, mid, lo = _bf16_split3(log_f[start:start + chunk])
        part = _dot(tri, hi) + _dot(tri, mid) + _dot(tri, lo) + carry
        c_ref[start:start + chunk, :] = part
        carry = part[chunk - 1:chunk, :]


def _forget_gate_cumsum(x16, w_f, b_f, *, chunk=256):
    bsz, s_len, d = x16.shape
    n = w_f.shape[1]
    return pl.pallas_call(
        functools.partial(_forget_gate_kernel, chunk=chunk),
        grid=(bsz,),
        in_specs=[pl.BlockSpec((None, s_len, d), lambda b: (b, 0, 0)),
                  pl.BlockSpec((d, n), lambda b: (0, 0)),
                  pl.BlockSpec((1, n), lambda b: (0, 0))],
        out_specs=pl.BlockSpec((None, s_len, n), lambda b: (b, 0, 0)),
        out_shape=jax.ShapeDtypeStruct((bsz, s_len, n), F32),
        compiler_params=_params("parallel"),
        name="forget_gate_cumsum",
    )(x16, w_f, b_f)


def _fox_kernel(q_ref, k_ref, v_ref, ccol_ref, crow_ref, o_ref, *, tile, scale, heads):
    qi = pl.program_id(2)
    dh = FOX_HEAD_DIM
    h0 = pl.program_id(1) * heads
    lane = lax.broadcasted_iota(jnp.int32, ccol_ref.shape, 1)
    ccol = ccol_ref[...]
    cq = [jnp.sum(jnp.where(lane == h0 + hh, ccol, 0.0), axis=1, keepdims=True) for hh in range(heads)]

    def step(j, carry, masked):
        start = pl.multiple_of(j * tile, tile)
        out = []
        for hh in range(heads):
            m, l, acc = carry[hh]
            cols = slice(hh * dh, (hh + 1) * dh)
            k = k_ref[pl.ds(start, tile), cols]
            v = v_ref[pl.ds(start, tile), cols]
            ck = crow_ref[pl.ds(h0 + hh, 1), pl.ds(start, tile)]
            s = _dot_nt(q_ref[:, cols], k) * scale + (cq[hh] - ck)
            if masked:
                row = lax.broadcasted_iota(jnp.int32, s.shape, 0)
                col = lax.broadcasted_iota(jnp.int32, s.shape, 1)
                s = jnp.where(row >= col, s, MASK_VALUE)
            m_new = jnp.maximum(m, jnp.max(s, axis=1, keepdims=True))
            alpha = jnp.exp(m - m_new)
            p = jnp.exp(s - m_new)
            l = alpha * l + jnp.sum(p, axis=1, keepdims=True)
            acc = alpha * acc + _dot(p.astype(BF16), v)
            out.append((m_new, l, acc))
        return tuple(out)

    init = tuple((jnp.full((tile, 1), MASK_VALUE, F32), jnp.zeros((tile, 1), F32),
                  jnp.zeros((tile, dh), F32)) for _ in range(heads))
    carry = lax.fori_loop(0, qi, functools.partial(step, masked=False), init)
    carry = step(qi, carry, masked=True)
    for hh in range(heads):
        _, l, acc = carry[hh]
        o_ref[:, hh * dh:(hh + 1) * dh] = (acc / l).astype(o_ref.dtype)


def _forgetting_attention(qkvu, c_col, c_row, *, tile=256, heads=4):
    bsz, s_len, _ = qkvu.shape
    width = heads * FOX_HEAD_DIM
    nblk = FOX_WIDTH // width
    return pl.pallas_call(
        functools.partial(_fox_kernel, tile=tile, scale=1.0 / math.sqrt(FOX_HEAD_DIM), heads=heads),
        grid=(bsz, nblk, s_len // tile),
        in_specs=[pl.BlockSpec((None, tile, width), lambda b, h, i: (b, i, h)),
                  pl.BlockSpec((None, s_len, width), lambda b, h, i: (b, 0, nblk + h)),
                  pl.BlockSpec((None, s_len, width), lambda b, h, i: (b, 0, 2 * nblk + h)),
                  pl.BlockSpec((None, tile, c_col.shape[2]), lambda b, h, i: (b, i, 0)),
                  pl.BlockSpec((None, c_row.shape[1], s_len), lambda b, h, i: (b, 0, 0))],
        out_specs=pl.BlockSpec((None, tile, width), lambda b, h, i: (b, i, h)),
        out_shape=jax.ShapeDtypeStruct((bsz, s_len, FOX_WIDTH), BF16),
        compiler_params=_params("parallel", "parallel", "arbitrary"),
        name="forgetting_attention",
    )(qkvu, qkvu, qkvu, c_col, c_row)


def _cmul(ar, ai, br, bi):
    return ar * br - ai * bi, ar * bi + ai * br


def _cpow(ar, ai, e, n_bits):
    rr = jnp.ones_like(ar)
    ri = jnp.zeros_like(ai)
    for bit in range(n_bits):
        nr, ni = _cmul(rr, ri, ar, ai)
        take = ((e >> bit) & 1) == 1
        rr = jnp.where(take, nr, rr)
        ri = jnp.where(take, ni, ri)
        if bit + 1 < n_bits:
            ar, ai = _cmul(ar, ai, ar, ai)
    return rr, ri


def _discretise(lr, li, dt):
    mag = jnp.exp(lr * dt)
    return mag * jnp.cos(li * dt), mag * jnp.sin(li * dt)


def _s5_prep_kernel(lam_row_ref, lam_col_ref, dt_ref, bt_re_ref, bt_im_ref, ct_re_ref, ct_im_ref,
                    d_ref, t0_ref, ws_re_ref, ws_im_ref, wc_re_ref, wc_im_ref, a_chunk_ref):
    chunk, grp, p = SSM_CHUNK, SSM_GROUP, SSM_STATE
    n_bits = chunk.bit_length()
    dt = jnp.exp(dt_ref[...])
    lr_r, li_r = lam_row_ref[0:1, :], lam_row_ref[1:2, :]
    ar_r, ai_r = _discretise(lr_r, li_r, dt)
    den = lr_r * lr_r + li_r * li_r
    xr, xi = ar_r - 1.0, ai_r
    g_re = (xr * lr_r + xi * li_r) / den
    g_im = (xi * lr_r - xr * li_r) / den
    bbt_re, bbt_im = _cmul(g_re, g_im, bt_re_ref[...], bt_im_ref[...])
    bx_re = jnp.concatenate([bbt_re] * chunk, axis=0)
    bx_im = jnp.concatenate([bbt_im] * chunk, axis=0)
    row_i = lax.broadcasted_iota(jnp.int32, (SSM_ROW, p), 0) // grp
    pr, pi = _cpow(jnp.broadcast_to(ar_r, (SSM_ROW, p)), jnp.broadcast_to(ai_r, (SSM_ROW, p)),
                   chunk - 1 - row_i, n_bits)
    ws_re, ws_im = _cmul(pr, pi, bx_re, bx_im)
    ws_re_ref[...] = ws_re.astype(BF16)
    ws_im_ref[...] = ws_im.astype(BF16)
    ac_r, ac_i = _cpow(ar_r, ai_r, jnp.full((1, p), chunk, jnp.int32), n_bits)
    a_chunk_ref[0:1, :] = ac_r
    a_chunk_ref[1:2, :] = ac_i
    lr_c, li_c = lam_col_ref[:, 0:1], lam_col_ref[:, 1:2]
    ar_c, ai_c = _discretise(lr_c, li_c, dt)
    rep = (lax.broadcasted_iota(jnp.int32, (grp, SSM_ROW), 0)
           == lax.broadcasted_iota(jnp.int32, (grp, SSM_ROW), 1) % grp).astype(F32)
    cx_re = jnp.dot(ct_re_ref[...], rep, precision=HIGHEST, preferred_element_type=F32)
    cx_im = jnp.dot(ct_im_ref[...], rep, precision=HIGHEST, preferred_element_type=F32)
    dx = jnp.dot(d_ref[...], rep, precision=HIGHEST, preferred_element_type=F32)
    col_j = lax.broadcasted_iota(jnp.int32, (p, SSM_ROW), 1) // grp
    ab_r = jnp.broadcast_to(ar_c, (p, SSM_ROW))
    ab_i = jnp.broadcast_to(ai_c, (p, SSM_ROW))
    qr, qi = _cpow(ab_r, ab_i, col_j, n_bits)
    f_re, f_im = _cmul(qr, qi, cx_re, cx_im)
    e_re, e_im = _cmul(f_re, f_im, ab_r, ab_i)
    wc_re_ref[...] = e_re.astype(BF16)
    wc_im_ref[...] = (-e_im).astype(BF16)
    r0 = (jnp.dot(bbt_re, f_re, precision=HIGHEST, preferred_element_type=F32)
          - jnp.dot(bbt_im, f_im, precision=HIGHEST, preferred_element_type=F32))
    lane = lax.broadcasted_iota(jnp.int32, (grp, SSM_ROW), 1)
    sub = lax.broadcasted_iota(jnp.int32, (grp, SSM_ROW), 0)
    for i in range(chunk):
        blk = r0 if i == 0 else pltpu.roll(r0, i * grp, 1)
        blk = jnp.where(lane >= i * grp, blk, 0.0)
        blk = blk + jnp.where(lane == i * grp + sub, dx, 0.0)
        t0_ref[i * grp:(i + 1) * grp, :] = blk.astype(BF16)


def _s5_prepare(lam_re, lam_im, log_step, b_re, b_im, c_re, c_im, d_skip):
    g, p = lam_re.shape
    grp = SSM_GROUP
    lam_row = jnp.stack([lam_re, lam_im], axis=1).astype(F32)
    lam_col = jnp.stack([lam_re, lam_im], axis=2).astype(F32)
    dt = log_step.reshape(g, 1, 1).astype(F32)
    bt_re = jnp.swapaxes(b_re, 1, 2).astype(F32)
    bt_im = jnp.swapaxes(b_im, 1, 2).astype(F32)
    ct_re = jnp.swapaxes(c_re, 1, 2).astype(F32)
    ct_im = jnp.swapaxes(c_im, 1, 2).astype(F32)
    d3 = d_skip.reshape(g, 1, grp).astype(F32)

    def spec(*shape):
        return pl.BlockSpec((None,) + shape, lambda i: (i,) + (0,) * len(shape))

    return pl.pallas_call(
        _s5_prep_kernel,
        grid=(g,),
        in_specs=[spec(2, p), spec(p, 2), spec(1, 1), spec(grp, p), spec(grp, p),
                  spec(p, grp), spec(p, grp), spec(1, grp)],
        out_specs=[spec(SSM_ROW, SSM_ROW), spec(SSM_ROW, p), spec(SSM_ROW, p),
                   spec(p, SSM_ROW), spec(p, SSM_ROW), spec(2, p)],
        out_shape=[jax.ShapeDtypeStruct((g, SSM_ROW, SSM_ROW), BF16),
                   jax.ShapeDtypeStruct((g, SSM_ROW, p), BF16),
                   jax.ShapeDtypeStruct((g, SSM_ROW, p), BF16),
                   jax.ShapeDtypeStruct((g, p, SSM_ROW), BF16),
                   jax.ShapeDtypeStruct((g, p, SSM_ROW), BF16),
                   jax.ShapeDtypeStruct((g, 2, p), F32)],
        compiler_params=_params("parallel"),
        name="s5_prepare",
    )(lam_row, lam_col, dt, bt_re, bt_im, ct_re, ct_im, d3)


def _gelu_tanh(y):
    return 0.5 * y * (1.0 + jnp.tanh(math.sqrt(2.0 / math.pi) * (y + 0.044715 * (y * y * y))))


def _s5_scan_kernel(u_ref, t0_ref, ws_re_ref, ws_im_ref, wc_re_ref, wc_im_ref, a_ref, o_ref,
                    sre_ref, sim_ref, hre_ref, him_ref, *, groups, bsz, n_chunks):
    for g in range(groups):
        u = u_ref[g]
        sre_ref[g] = _dot(u, ws_re_ref[g])
        sim_ref[g] = _dot(u, ws_im_ref[g])
    ar = a_ref[:, 0:1, :]
    ai = a_ref[:, 1:2, :]

    def step(k, carry):
        hr, hi = carry
        rows = pl.ds(pl.multiple_of(k * bsz, bsz), bsz)
        hre_ref[:, rows, :] = hr
        him_ref[:, rows, :] = hi
        sr = sre_ref[:, rows, :]
        si = sim_ref[:, rows, :]
        return ar * hr - ai * hi + sr, ar * hi + ai * hr + si

    zero = jnp.zeros((groups, bsz, SSM_STATE), F32)
    lax.fori_loop(0, n_chunks, step, (zero, zero))
    for g in range(groups):
        y = (_dot(u_ref[g], t0_ref[g]) + _dot(hre_ref[g].astype(BF16), wc_re_ref[g])
             + _dot(him_ref[g].astype(BF16), wc_im_ref[g]))
        o_ref[g] = _gelu_tanh(y).astype(o_ref.dtype)


def _s5_scan_gelu(u_rows, mats, *, bsz, groups_per_step=8):
    t0, ws_re, ws_im, wc_re, wc_im, a_chunk = mats
    g, rows, _ = u_rows.shape
    gs = groups_per_step
    p = SSM_STATE

    def spec(*shape):
        return pl.BlockSpec((gs,) + shape, lambda i: (i,) + (0,) * len(shape))

    return pl.pallas_call(
        functools.partial(_s5_scan_kernel, groups=gs, bsz=bsz, n_chunks=rows // bsz),
        grid=(g // gs,),
        in_specs=[spec(rows, SSM_ROW), spec(SSM_ROW, SSM_ROW), spec(SSM_ROW, p), spec(SSM_ROW, p),
                  spec(p, SSM_ROW), spec(p, SSM_ROW), spec(2, p)],
        out_specs=spec(rows, SSM_ROW),
        out_shape=jax.ShapeDtypeStruct((g, rows, SSM_ROW), BF16),
        scratch_shapes=[pltpu.VMEM((gs, rows, p), F32), pltpu.VMEM((gs, rows, p), F32),
                        pltpu.VMEM((gs, rows, p), F32), pltpu.VMEM((gs, rows, p), F32)],
        compiler_params=_params("parallel"),
        name="s5_scan_gelu",
    )(u_rows, t0, ws_re, ws_im, wc_re, wc_im, a_chunk)


def _sigmoid(x):
    return 1.0 / (1.0 + jnp.exp(-x))


def _glu_kernel(a_ref, wv_ref, wg_ref, o_ref):
    a = a_ref[...]
    o_ref[...] = (_dot(a, wv_ref[...]) * _sigmoid(_dot(a, wg_ref[...]))).astype(o_ref.dtype)


def _glu(a, w, *, bm, bn):
    m, k = a.shape
    n = w.shape[1] // 2
    nt = n // bn
    return pl.pallas_call(
        _glu_kernel,
        grid=(m // bm, nt),
        in_specs=[pl.BlockSpec((bm, k), lambda i, j: (i, 0)),
                  pl.BlockSpec((k, bn), lambda i, j: (0, j)),
                  pl.BlockSpec((k, bn), lambda i, j: (0, j + nt))],
        out_specs=pl.BlockSpec((bm, bn), lambda i, j: (i, j)),
        out_shape=jax.ShapeDtypeStruct((m, n), BF16),
        compiler_params=_params("parallel", "arbitrary"),
        name="glu",
    )(a, w, w)


def _ffn_up_kernel(x_ref, wg_ref, wv_ref, cwg_ref, cbg_ref, cwv_ref, cbv_ref, o_ref,
                   wg16_ref, wv16_ref, hg_ref, hv_ref, *, blocks_per_seq):
    i = pl.program_id(1)
    bm, bn = o_ref.shape
    pad = hg_ref.shape[0] - bm

    @pl.when(i == 0)
    def _():
        wg16_ref[...] = wg_ref[...].astype(BF16)
        wv16_ref[...] = wv_ref[...].astype(BF16)

    @pl.when(i % blocks_per_seq == 0)
    def _():
        hg_ref[0:pad, :] = jnp.zeros((pad, bn), F32)
        hv_ref[0:pad, :] = jnp.zeros((pad, bn), F32)

    @pl.when(i % blocks_per_seq != 0)
    def _():
        hg_ref[0:pad, :] = hg_ref[bm:bm + pad, :]
        hv_ref[0:pad, :] = hv_ref[bm:bm + pad, :]

    x = x_ref[...]
    conv = []
    for w16, h_ref, cw, cb in ((wg16_ref, hg_ref, cwg_ref, cbg_ref), (wv16_ref, hv_ref, cwv_ref, cbv_ref)):
        h = _dot(x, w16[...])
        h_ref[pad:pad + bm, :] = h
        conv.append(cb[...] + cw[0:1, :] * h_ref[pad - 2:pad - 2 + bm, :]
                    + cw[1:2, :] * h_ref[pad - 1:pad - 1 + bm, :] + cw[2:3, :] * h)
    gate, val = conv
    o_ref[...] = (gate * _sigmoid(gate) * val).astype(o_ref.dtype)


def _ffn_up(x16, w_up, conv_w, conv_b, *, seq_len, bm, bn):
    m, d = x16.shape
    f = w_up.shape[1] // 2
    pad = 8
    cb = conv_b.reshape(1, 2 * f).astype(F32)
    col = lambda j: pl.multiple_of(jnp.minimum(j * bn, f - bn), V7X_LANES)
    gate_spec = lambda rows: pl.BlockSpec((pl.Element(rows), pl.Element(bn)), lambda j, i: (0, col(j)))
    val_spec = lambda rows: pl.BlockSpec((pl.Element(rows), pl.Element(bn)),
                                         lambda j, i: (0, pl.multiple_of(f + col(j), V7X_LANES)))
    return pl.pallas_call(
        functools.partial(_ffn_up_kernel, blocks_per_seq=seq_len // bm),
        grid=(pl.cdiv(f, bn), m // bm),
        in_specs=[pl.BlockSpec((bm, d), lambda j, i: (i, 0)),
                  gate_spec(d), val_spec(d), gate_spec(CONV_WIDTH), gate_spec(1),
                  val_spec(CONV_WIDTH), val_spec(1)],
        out_specs=pl.BlockSpec((pl.Element(bm), pl.Element(bn)),
                               lambda j, i: (pl.multiple_of(i * bm, bm), col(j))),
        out_shape=jax.ShapeDtypeStruct((m, f), BF16),
        scratch_shapes=[pltpu.VMEM((d, bn), BF16), pltpu.VMEM((d, bn), BF16),
                        pltpu.VMEM((pad + bm, bn), F32), pltpu.VMEM((pad + bm, bn), F32)],
        compiler_params=_params("arbitrary", "arbitrary"),
        name="ffn_up_conv_gate",
    )(x16, w_up, w_up, conv_w, cb, conv_w, cb)


def _conv_ffn(x32, x16, w_up, conv_w, conv_b, w_down, ln_g, ln_b, *, seq_len):
    act = _ffn_up(x16, w_up.astype(F32), conv_w.astype(F32), conv_b, seq_len=seq_len, bm=1024, bn=512)
    return _matmul_residual_layernorm([(act, w_down.astype(BF16))], x32, ln_g, ln_b, bm=256)


def _rope_proj_kernel(a_ref, w_ref, pos_ref, o_ref, cos_ref, sin_ref, *, rope_cols, q_tiles, q_scale):
    j = pl.program_id(1)
    hd, half = SWA_HEAD_DIM, ROPE_DIM // 2

    @pl.when(j == 0)
    def _():
        lane = lax.broadcasted_iota(jnp.int32, (1, V7X_LANES), 1) % hd
        idx = (lane % half).astype(F32)
        inv_freq = jnp.where(lane < ROPE_DIM, jnp.exp(idx * (-math.log(ROPE_THETA) / half)), 0.0)
        ang = pos_ref[...].astype(F32) * inv_freq
        sign = jnp.where(lane < half, -1.0, 1.0)
        cos_ref[...] = jnp.cos(ang)
        sin_ref[...] = jnp.sin(ang) * sign

    acc = _dot(a_ref[...], w_ref[...])
    bm, bn = acc.shape

    reps = bn // V7X_LANES
    cos = jnp.concatenate([cos_ref[...]] * reps, axis=1)
    sin = jnp.concatenate([sin_ref[...]] * reps, axis=1)
    lane = lax.broadcasted_iota(jnp.int32, (1, bn), 1)
    first = (lane & (hd - 1)) < half
    partner = jnp.where(first, pltpu.roll(acc, bn - half, 1), pltpu.roll(acc, half, 1))
    rot = acc * cos + partner * sin
    col = lane + j * bn
    out = jnp.where(col < rope_cols, rot, acc) * jnp.where(col < q_tiles * bn, q_scale, 1.0)
    o_ref[...] = out.astype(o_ref.dtype)


def _rope_projection(x16, w, pos_col, *, rope_cols, q_cols, q_scale, bm, bn):
    m, k = x16.shape
    n = w.shape[1]
    return pl.pallas_call(
        functools.partial(_rope_proj_kernel, rope_cols=rope_cols, q_tiles=q_cols // bn,
                          q_scale=q_scale),
        grid=(m // bm, n // bn),
        in_specs=[pl.BlockSpec((bm, k), lambda i, j: (i, 0)),
                  pl.BlockSpec((k, bn), lambda i, j: (0, j)),
                  pl.BlockSpec((bm, 1), lambda i, j: (i, 0))],
        out_specs=pl.BlockSpec((bm, bn), lambda i, j: (i, j)),
        out_shape=jax.ShapeDtypeStruct((m, n), BF16),
        scratch_shapes=[pltpu.VMEM((bm, V7X_LANES), F32), pltpu.VMEM((bm, V7X_LANES), F32)],
        compiler_params=_params("parallel", "arbitrary"),
        name="rope_projection",
    )(x16, w, pos_col)


def _swa_kernel(sink_ref, q_ref, kp_ref, kc_ref, vp_ref, vc_ref, o_ref):
    n = pl.program_id(1)
    w, hd, lanes = SWA_WINDOW, SWA_HEAD_DIM, V7X_LANES
    pairs = SWA_GROUPS // 2
    qi = lax.broadcasted_iota(jnp.int32, (w, 2 * w), 0)
    kj = lax.broadcasted_iota(jnp.int32, (w, 2 * w), 1)
    rel = w + qi - kj
    valid = (rel >= 0) & (rel < w) & ((n > 0) | (kj >= w))
    bias = jnp.where(valid, 0.0, MASK_VALUE)
    lane_kv = lax.broadcasted_iota(jnp.int32, (2 * w, lanes), 1)
    lane_o = lax.broadcasted_iota(jnp.int32, (w, lanes), 1)

    def both_halves(prev_ref, cur_ref, hk):
        slab = slice((hk // 2) * lanes, (hk // 2 + 1) * lanes)
        x = jnp.concatenate([prev_ref[:, slab], cur_ref[:, slab]], axis=0).astype(F32)
        upper = hk % 2 == 1
        own = jnp.where((lane_kv >= hd) if upper else (lane_kv < hd), x, 0.0)
        other = pltpu.roll(own, hd, 1)
        lo, hi = (other, own) if upper else (own, other)
        return jnp.concatenate([lo, hi], axis=0).astype(BF16)

    for hk in range(SWA_KV_HEADS):
        kk = both_halves(kp_ref, kc_ref, hk)
        vv = both_halves(vp_ref, vc_ref, hk)
        slabs = [slice((hk * pairs + r) * lanes, (hk * pairs + r + 1) * lanes) for r in range(pairs)]
        qs = jnp.concatenate([q_ref[:, sl] for sl in slabs], axis=0)
        s_all = _dot_nt(qs, kk)
        probs, inv_den = [], []
        for r in range(pairs):
            halves, inv = [], []
            for half in range(2):
                sink = sink_ref[hk * SWA_GROUPS + 2 * r + half]
                s = s_all[r * w:(r + 1) * w, half * 2 * w:(half + 1) * 2 * w] + bias
                m = jnp.maximum(jnp.max(s, axis=1, keepdims=True), sink)
                p = jnp.exp(s - m)
                inv.append(1.0 / (jnp.sum(p, axis=1, keepdims=True) + jnp.exp(sink - m)))
                halves.append(p.astype(BF16))
            probs.append(jnp.concatenate(halves, axis=1))
            inv_den.append(jnp.where(lane_o < hd, inv[0], inv[1]))
        o_all = _dot(jnp.concatenate(probs, axis=0), vv)
        for r in range(pairs):
            o_ref[:, slabs[r]] = (o_all[r * w:(r + 1) * w] * inv_den[r]).astype(o_ref.dtype)


def _sliding_window_attention(qkv, sinks):
    bsz, s_len, _ = qkv.shape
    w, hd = SWA_WINDOW, SWA_HEAD_DIM
    qw = SWA_HEADS * hd
    kw = SWA_KV_HEADS * hd
    k_blk = qw // kw
    v_blk = k_blk + 1
    prev = lambda n: jnp.maximum(n - 1, 0)
    return pl.pallas_call(
        _swa_kernel,
        grid=(bsz, s_len // w),
        in_specs=[pl.BlockSpec(memory_space=pltpu.SMEM),
                  pl.BlockSpec((None, w, qw), lambda b, n: (b, n, 0)),
                  pl.BlockSpec((None, w, kw), lambda b, n: (b, prev(n), k_blk)),
                  pl.BlockSpec((None, w, kw), lambda b, n: (b, n, k_blk)),
                  pl.BlockSpec((None, w, kw), lambda b, n: (b, prev(n), v_blk)),
                  pl.BlockSpec((None, w, kw), lambda b, n: (b, n, v_blk))],
        out_specs=pl.BlockSpec((None, w, qw), lambda b, n: (b, n, 0)),
        out_shape=jax.ShapeDtypeStruct((bsz, s_len, qw), BF16),
        compiler_params=_params("parallel", "arbitrary"),
        name="sliding_window_attention",
    )(sinks.astype(F32), qkv, qkv, qkv, qkv, qkv)


def _even_mixer(x32, x16, bsz, s_len, w_in, b_f, lam_re, lam_im, log_step, b_re, b_im, c_re, c_im,
                d_skip, w_glu, w_out, ln_g, ln_b):
    m, d = x16.shape
    fw = FOX_WIDTH
    nh = FOX_HEADS
    ssm_w = w_glu.shape[0]
    n_groups = ssm_w // SSM_GROUP
    n_chunks = s_len // SSM_CHUNK
    w_main = jnp.concatenate([w_in[:, :3 * fw], w_in[:, 3 * fw + nh:]], axis=1).astype(BF16)
    w_f = jnp.pad(w_in[:, 3 * fw:3 * fw + nh], ((0, 0), (0, V7X_LANES - nh))).astype(BF16)
    bias_f = jnp.pad(b_f, (0, V7X_LANES - nh)).reshape(1, V7X_LANES).astype(F32)

    qkvu = _matmul(x16, w_main, bm=1024, bn=2048, out_dtype=BF16).reshape(bsz, s_len, -1)
    c_col = _forget_gate_cumsum(x16.reshape(bsz, s_len, d), w_f, bias_f)
    c_row = jnp.swapaxes(c_col[:, :, :nh], 1, 2)
    fox = _forgetting_attention(qkvu, c_col, c_row).reshape(m, fw)

    mats = _s5_prepare(lam_re, lam_im, log_step, b_re, b_im, c_re, c_im, d_skip)
    u = qkvu[:, :, 3 * fw:].reshape(bsz, n_chunks, SSM_CHUNK, n_groups, SSM_GROUP)
    u_rows = jnp.transpose(u, (3, 1, 0, 2, 4)).reshape(n_groups, n_chunks * bsz, SSM_ROW)
    gy_rows = _s5_scan_gelu(u_rows, mats, bsz=bsz)
    gy = jnp.transpose(gy_rows.reshape(n_groups, n_chunks, bsz, SSM_CHUNK, SSM_GROUP),
                       (2, 1, 3, 0, 4)).reshape(m, ssm_w)
    ssm = _glu(gy, w_glu.astype(BF16), bm=1024, bn=512)
    w_out16 = w_out.astype(BF16)
    return _matmul_residual_layernorm([(fox, w_out16[:fw]), (ssm, w_out16[fw:])], x32, ln_g, ln_b,
                                      bm=512)


def _odd_mixer(x32, x16, bsz, s_len, positions, w_in, sinks, w_out, ln_g, ln_b):
    m = x16.shape[0]
    rope_cols = (SWA_HEADS + SWA_KV_HEADS) * SWA_HEAD_DIM
    qkv = _rope_projection(x16, w_in.astype(BF16), positions.reshape(m, 1).astype(jnp.int32),
                           rope_cols=rope_cols, q_cols=SWA_HEADS * SWA_HEAD_DIM,
                           q_scale=1.0 / math.sqrt(SWA_HEAD_DIM), bm=1024, bn=512)
    o = _sliding_window_attention(qkv.reshape(bsz, s_len, -1), sinks).reshape(m, -1)
    return _matmul_residual_layernorm([(o, w_out.astype(BF16))], x32, ln_g, ln_b, bm=512)


def kernel(x, positions, ev_w_in, ev_b_f, ev_lambda_re, ev_lambda_im, ev_log_step, ev_ssm_b_re,
           ev_ssm_b_im, ev_ssm_c_re, ev_ssm_c_im, ev_ssm_d, ev_w_glu, ev_w_out, od_w_in, od_sinks,
           od_w_out, ln_mix_g, ln_mix_b, ffn_w_up, ffn_conv_w, ffn_conv_b, ffn_w_down, ln_ffn_g,
           ln_ffn_b):
    bsz, s_len, d = x.shape
    x32 = x.reshape(bsz * s_len, d).astype(F32)
    x16 = x32.astype(BF16)
    for i in range(DEPTH):
        j = i // 2
        if i % 2 == 0:
            x32, x16 = _even_mixer(x32, x16, bsz, s_len, ev_w_in[j], ev_b_f[j], ev_lambda_re[j],
                                   ev_lambda_im[j], ev_log_step[j], ev_ssm_b_re[j], ev_ssm_b_im[j],
                                   ev_ssm_c_re[j], ev_ssm_c_im[j], ev_ssm_d[j], ev_w_glu[j],
                                   ev_w_out[j], ln_mix_g[i], ln_mix_b[i])
        else:
            x32, x16 = _odd_mixer(x32, x16, bsz, s_len, positions, od_w_in[j], od_sinks[j],
                                  od_w_out[j], ln_mix_g[i], ln_mix_b[i])
        x32, x16 = _conv_ffn(x32, x16, ffn_w_up[i], ffn_conv_w[i], ffn_conv_b[i], ffn_w_down[i],
                             ln_ffn_g[i], ln_ffn_b[i], seq_len=s_len)
    return x32.reshape(bsz, s_len, d).astype(x.dtype)
```

```python
import functools
import math

import jax
import jax.numpy as jnp
from jax import lax
from jax.experimental import pallas as pl
from jax.experimental.pallas import tpu as pltpu

F32 = jnp.float32
BF16 = jnp.bfloat16

DEPTH = 2
FOX_HEADS = 8
FOX_HEAD_DIM = 128
FOX_WIDTH = FOX_HEADS * FOX_HEAD_DIM
SSM_GROUP = 16
SSM_STATE = 64
SSM_CHUNK = 16
SSM_ROW = SSM_CHUNK * SSM_GROUP
SWA_HEADS = 32
SWA_KV_HEADS = 4
SWA_HEAD_DIM = 64
SWA_GROUPS = SWA_HEADS // SWA_KV_HEADS
SWA_WINDOW = 128
ROPE_DIM = SWA_HEAD_DIM // 4
ROPE_THETA = 500000.0
CONV_WIDTH = 3
LN_EPS = 1e-5
DEEPNORM_ALPHA = (2.0 * DEPTH) ** 0.25

V7X_LANES = 128
V7X_VMEM_LIMIT_BYTES = 56 * 1024 * 1024
GROUPS_PER_SLAB = V7X_LANES // SSM_GROUP
STAGE_PITCH = 136
MASK_VALUE = -1e30

HIGHEST = lax.Precision.HIGHEST


def _params(*semantics):
    return pltpu.CompilerParams(dimension_semantics=semantics,
                                vmem_limit_bytes=V7X_VMEM_LIMIT_BYTES)


def _dot(a, b):
    return jnp.dot(a, b, preferred_element_type=F32)


def _dot_nt(a, b):
    return lax.dot_general(a, b, (((1,), (1,)), ((), ())), preferred_element_type=F32)


def _mm_kernel(a_ref, w_ref, o_ref):
    o_ref[...] = _dot(a_ref[...], w_ref[...]).astype(o_ref.dtype)


def _matmul(a, w, *, bm, bn, out_dtype):
    m, k = a.shape
    n = w.shape[1]
    return pl.pallas_call(
        _mm_kernel,
        grid=(m // bm, n // bn),
        in_specs=[pl.BlockSpec((bm, k), lambda i, j: (i, 0)),
                  pl.BlockSpec((k, bn), lambda i, j: (0, j))],
        out_specs=pl.BlockSpec((bm, bn), lambda i, j: (i, j)),
        out_shape=jax.ShapeDtypeStruct((m, n), out_dtype),
        compiler_params=_params("parallel", "arbitrary"),
        name="matmul",
    )(a, w)


def _mm_res_ln_kernel(*refs, n_pairs, sub):
    a_refs = refs[:n_pairs]
    w_refs = refs[n_pairs:2 * n_pairs]
    x_ref, g_ref, b_ref, o32_ref, o16_ref = refs[2 * n_pairs:]
    for r in range(x_ref.shape[0] // sub):
        rows = slice(r * sub, (r + 1) * sub)
        pre = DEEPNORM_ALPHA * x_ref[rows, :]
        for a_ref, w_ref in zip(a_refs, w_refs):
            pre = pre + _dot(a_ref[rows, :], w_ref[...])
        mu = jnp.mean(pre, axis=-1, keepdims=True)
        cen = pre - mu
        var = jnp.mean(cen * cen, axis=-1, keepdims=True)
        y = cen * lax.rsqrt(var + LN_EPS) * g_ref[...] + b_ref[...]
        o32_ref[rows, :] = y
        o16_ref[rows, :] = y.astype(BF16)


def _matmul_residual_layernorm(pairs, x32, gain, bias, *, bm, sub=256):
    m, d = x32.shape
    n_pairs = len(pairs)
    a_list = [p[0] for p in pairs]
    w_list = [p[1] for p in pairs]
    row_spec = lambda width: pl.BlockSpec((bm, width), lambda i: (i, 0))
    const_spec = lambda rows: pl.BlockSpec((rows, d), lambda i: (0, 0), pipeline_mode=pl.Buffered(1))
    in_specs = ([row_spec(a.shape[1]) for a in a_list] + [const_spec(w.shape[0]) for w in w_list]
                + [row_spec(d), const_spec(1), const_spec(1)])
    return pl.pallas_call(
        functools.partial(_mm_res_ln_kernel, n_pairs=n_pairs, sub=sub),
        grid=(m // bm,),
        in_specs=in_specs,
        out_specs=[row_spec(d), row_spec(d)],
        out_shape=[jax.ShapeDtypeStruct((m, d), F32), jax.ShapeDtypeStruct((m, d), BF16)],
        compiler_params=_params("parallel"),
        name="matmul_residual_layernorm",
    )(*a_list, *w_list, x32, gain.reshape(1, d).astype(F32), bias.reshape(1, d).astype(F32))


def _bf16_split3(v):
    hi = v.astype(BF16)
    r1 = v - hi.astype(F32)
    mid = r1.astype(BF16)
    lo = (r1 - mid.astype(F32)).astype(BF16)
    return hi, mid, lo


def _forget_gate_kernel(x_ref, wf_ref, bf_ref, c_ref, *, chunk):
    s_len = x_ref.shape[0]
    f = _dot(x_ref[...], wf_ref[...]) + bf_ref[...]
    log_f = jnp.minimum(f, 0.0) - jnp.log(1.0 + jnp.exp(-jnp.abs(f)))
    row = lax.broadcasted_iota(jnp.int32, (chunk, chunk), 0)
    col = lax.broadcasted_iota(jnp.int32, (chunk, chunk), 1)
    tri = (row >= col).astype(BF16)
    carry = jnp.zeros((1, log_f.shape[1]), F32)
    for start in range(0, s_len, chunk):
        hi, mid, lo = _bf16_split3(log_f[start:start + chunk])
        part = _dot(tri, hi) + _dot(tri, mid) + _dot(tri, lo) + carry
        c_ref[start:start + chunk, :] = part
        carry = part[chunk - 1:chunk, :]


def _forget_gate_cumsum(x16, w_f, b_f, *, chunk=256):
    bsz, s_len, d = x16.shape
    n = w_f.shape[1]
    return pl.pallas_call(
        functools.partial(_forget_gate_kernel, chunk=chunk),
        grid=(bsz,),
        in_specs=[pl.BlockSpec((None, s_len, d), lambda b: (b, 0, 0)),
                  pl.BlockSpec((d, n), lambda b: (0, 0)),
                  pl.BlockSpec((1, n), lambda b: (0, 0))],
        out_specs=pl.BlockSpec((None, s_len, n), lambda b: (b, 0, 0)),
        out_shape=jax.ShapeDtypeStruct((bsz, s_len, n), F32),
        compiler_params=_params("parallel"),
        name="forget_gate_cumsum",
    )(x16, w_f, b_f)


def _fox_kernel(q_ref, k_ref, v_ref, ccol_ref, crow_ref, o_ref, *, tile, scale, heads):
    qi = pl.program_id(2)
    dh = FOX_HEAD_DIM
    h0 = pl.program_id(1) * heads
    lane = lax.broadcasted_iota(jnp.int32, ccol_ref.shape, 1)
    ccol = ccol_ref[...]
    cq = [jnp.sum(jnp.where(lane == h0 + hh, ccol, 0.0), axis=1, keepdims=True) for hh in range(heads)]

    def step(j, carry, masked):
        start = pl.multiple_of(j * tile, tile)
        out = []
        for hh in range(heads):
            m, l, acc = carry[hh]
            cols = slice(hh * dh, (hh + 1) * dh)
            k = k_ref[pl.ds(start, tile), cols]
            v = v_ref[pl.ds(start, tile), cols]
            ck = crow_ref[pl.ds(h0 + hh, 1), pl.ds(start, tile)]
            s = _dot_nt(q_ref[:, cols], k) * scale + (cq[hh] - ck)
            if masked:
                row = lax.broadcasted_iota(jnp.int32, s.shape, 0)
                col = lax.broadcasted_iota(jnp.int32, s.shape, 1)
                s = jnp.where(row >= col, s, MASK_VALUE)
            m_new = jnp.maximum(m, jnp.max(s, axis=1, keepdims=True))
            alpha = jnp.exp(m - m_new)
            p = jnp.exp(s - m_new)
            l = alpha * l + jnp.sum(p, axis=1, keepdims=True)
            acc = alpha * acc + _dot(p.astype(BF16), v)
            out.append((m_new, l, acc))
        return tuple(out)

    init = tuple((jnp.full((tile, 1), MASK_VALUE, F32), jnp.zeros((tile, 1), F32),
                  jnp.zeros((tile, dh), F32)) for _ in range(heads))
    carry = lax.fori_loop(0, qi, functools.partial(step, masked=False), init)
    carry = step(qi, carry, masked=True)
    for hh in range(heads):
        _, l, acc = carry[hh]
        o_ref[:, hh * dh:(hh + 1) * dh] = (acc / l).astype(o_ref.dtype)


def _forgetting_attention(qkvu, c_col, c_row, *, tile=256, heads=4):
    bsz, s_len, _ = qkvu.shape
    width = heads * FOX_HEAD_DIM
    nblk = FOX_WIDTH // width
    return pl.pallas_call(
        functools.partial(_fox_kernel, tile=tile, scale=1.0 / math.sqrt(FOX_HEAD_DIM), heads=heads),
        grid=(bsz, nblk, s_len // tile),
        in_specs=[pl.BlockSpec((None, tile, width), lambda b, h, i: (b, i, h)),
                  pl.BlockSpec((None, s_len, width), lambda b, h, i: (b, 0, nblk + h)),
                  pl.BlockSpec((None, s_len, width), lambda b, h, i: (b, 0, 2 * nblk + h)),
                  pl.BlockSpec((None, tile, c_col.shape[2]), lambda b, h, i: (b, i, 0)),
                  pl.BlockSpec((None, c_row.shape[1], s_len), lambda b, h, i: (b, 0, 0))],
        out_specs=pl.BlockSpec((None, tile, width), lambda b, h, i: (b, i, h)),
        out_shape=jax.ShapeDtypeStruct((bsz, s_len, FOX_WIDTH), BF16),
        compiler_params=_params("parallel", "parallel", "arbitrary"),
        name="forgetting_attention",
    )(qkvu, qkvu, qkvu, c_col, c_row)


def _cmul(ar, ai, br, bi):
    return ar * br - ai * bi, ar * bi + ai * br


def _cpow(ar, ai, e, n_bits):
    rr = jnp.ones_like(ar)
    ri = jnp.zeros_like(ai)
    for bit in range(n_bits):
        nr, ni = _cmul(rr, ri, ar, ai)
        take = ((e >> bit) & 1) == 1
        rr = jnp.where(take, nr, rr)
        ri = jnp.where(take, ni, ri)
        if bit + 1 < n_bits:
            ar, ai = _cmul(ar, ai, ar, ai)
    return rr, ri


def _discretise(lr, li, dt):
    mag = jnp.exp(lr * dt)
    return mag * jnp.cos(li * dt), mag * jnp.sin(li * dt)


def _s5_prep_kernel(lam_row_ref, lam_col_ref, dt_ref, bt_re_ref, bt_im_ref, ct_re_ref, ct_im_ref,
                    d_ref, t0_ref, ws_re_ref, ws_im_ref, wc_re_ref, wc_im_ref, a_chunk_ref):
    chunk, grp, p = SSM_CHUNK, SSM_GROUP, SSM_STATE
    n_bits = chunk.bit_length()
    dt = jnp.exp(dt_ref[...])
    lr_r, li_r = lam_row_ref[0:1, :], lam_row_ref[1:2, :]
    ar_r, ai_r = _discretise(lr_r, li_r, dt)
    den = lr_r * lr_r + li_r * li_r
    xr, xi = ar_r - 1.0, ai_r
    g_re = (xr * lr_r + xi * li_r) / den
    g_im = (xi * lr_r - xr * li_r) / den
    bbt_re, bbt_im = _cmul(g_re, g_im, bt_re_ref[...], bt_im_ref[...])
    bx_re = jnp.concatenate([bbt_re] * chunk, axis=0)
    bx_im = jnp.concatenate([bbt_im] * chunk, axis=0)
    row_i = lax.broadcasted_iota(jnp.int32, (SSM_ROW, p), 0) // grp
    pr, pi = _cpow(jnp.broadcast_to(ar_r, (SSM_ROW, p)), jnp.broadcast_to(ai_r, (SSM_ROW, p)),
                   chunk - 1 - row_i, n_bits)
    ws_re, ws_im = _cmul(pr, pi, bx_re, bx_im)
    ws_re_ref[...] = ws_re.astype(BF16)
    ws_im_ref[...] = ws_im.astype(BF16)
    ac_r, ac_i = _cpow(ar_r, ai_r, jnp.full((1, p), chunk, jnp.int32), n_bits)
    a_chunk_ref[0:1, :] = ac_r
    a_chunk_ref[1:2, :] = ac_i
    lr_c, li_c = lam_col_ref[:, 0:1], lam_col_ref[:, 1:2]
    ar_c, ai_c = _discretise(lr_c, li_c, dt)
    rep = (lax.broadcasted_iota(jnp.int32, (grp, SSM_ROW), 0)
           == lax.broadcasted_iota(jnp.int32, (grp, SSM_ROW), 1) % grp).astype(F32)
    cx_re = jnp.dot(ct_re_ref[...], rep, precision=HIGHEST, preferred_element_type=F32)
    cx_im = jnp.dot(ct_im_ref[...], rep, precision=HIGHEST, preferred_element_type=F32)
    dx = jnp.dot(d_ref[...], rep, precision=HIGHEST, preferred_element_type=F32)
    col_j = lax.broadcasted_iota(jnp.int32, (p, SSM_ROW), 1) // grp
    ab_r = jnp.broadcast_to(ar_c, (p, SSM_ROW))
    ab_i = jnp.broadcast_to(ai_c, (p, SSM_ROW))
    qr, qi = _cpow(ab_r, ab_i, col_j, n_bits)
    f_re, f_im = _cmul(qr, qi, cx_re, cx_im)
    e_re, e_im = _cmul(f_re, f_im, ab_r, ab_i)
    wc_re_ref[...] = e_re.astype(BF16)
    wc_im_ref[...] = (-e_im).astype(BF16)
    r0 = (jnp.dot(bbt_re, f_re, precision=HIGHEST, preferred_element_type=F32)
          - jnp.dot(bbt_im, f_im, precision=HIGHEST, preferred_element_type=F32))
    lane = lax.broadcasted_iota(jnp.int32, (grp, SSM_ROW), 1)
    sub = lax.broadcasted_iota(jnp.int32, (grp, SSM_ROW), 0)
    for i in range(chunk):
        blk = r0 if i == 0 else pltpu.roll(r0, i * grp, 1)
        blk = jnp.where(lane >= i * grp, blk, 0.0)
        blk = blk + jnp.where(lane == i * grp + sub, dx, 0.0)
        t0_ref[i * grp:(i + 1) * grp, :] = blk.astype(BF16)


def _s5_prepare(lam_re, lam_im, log_step, b_re, b_im, c_re, c_im, d_skip):
    g, p = lam_re.shape
    grp = SSM_GROUP
    lam_row = jnp.stack([lam_re, lam_im], axis=1).astype(F32)
    lam_col = jnp.stack([lam_re, lam_im], axis=2).astype(F32)
    dt = log_step.reshape(g, 1, 1).astype(F32)
    bt_re = jnp.swapaxes(b_re, 1, 2).astype(F32)
    bt_im = jnp.swapaxes(b_im, 1, 2).astype(F32)
    ct_re = jnp.swapaxes(c_re, 1, 2).astype(F32)
    ct_im = jnp.swapaxes(c_im, 1, 2).astype(F32)
    d3 = d_skip.reshape(g, 1, grp).astype(F32)

    def spec(*shape):
        return pl.BlockSpec((None,) + shape, lambda i: (i,) + (0,) * len(shape))

    return pl.pallas_call(
        _s5_prep_kernel,
        grid=(g,),
        in_specs=[spec(2, p), spec(p, 2), spec(1, 1), spec(grp, p), spec(grp, p),
                  spec(p, grp), spec(p, grp), spec(1, grp)],
        out_specs=[spec(SSM_ROW, SSM_ROW), spec(SSM_ROW, p), spec(SSM_ROW, p),
                   spec(p, SSM_ROW), spec(p, SSM_ROW), spec(2, p)],
        out_shape=[jax.ShapeDtypeStruct((g, SSM_ROW, SSM_ROW), BF16),
                   jax.ShapeDtypeStruct((g, SSM_ROW, p), BF16),
                   jax.ShapeDtypeStruct((g, SSM_ROW, p), BF16),
                   jax.ShapeDtypeStruct((g, p, SSM_ROW), BF16),
                   jax.ShapeDtypeStruct((g, p, SSM_ROW), BF16),
                   jax.ShapeDtypeStruct((g, 2, p), F32)],
        compiler_params=_params("parallel"),
        name="s5_prepare",
    )(lam_row, lam_col, dt, bt_re, bt_im, ct_re, ct_im, d3)


def _gelu_tanh(y):
    return 0.5 * y * (1.0 + jnp.tanh(math.sqrt(2.0 / math.pi) * (y + 0.044715 * (y * y * y))))


def _s5_scan_kernel(u_ref, t0_ref, ws_re_ref, ws_im_ref, wc_re_ref, wc_im_ref, a_ref, o_ref,
                    sre_ref, sim_ref, hre_ref, him_ref, *, groups, bsz, n_chunks):
    for g in range(groups):
        u = u_ref[g]
        sre_ref[g] = _dot(u, ws_re_ref[g])
        sim_ref[g] = _dot(u, ws_im_ref[g])
    ar = a_ref[:, 0:1, :]
    ai = a_ref[:, 1:2, :]

    def step(k, carry):
        hr, hi = carry
        rows = pl.ds(pl.multiple_of(k * bsz, bsz), bsz)
        hre_ref[:, rows, :] = hr
        him_ref[:, rows, :] = hi
        sr = sre_ref[:, rows, :]
        si = sim_ref[:, rows, :]
        return ar * hr - ai * hi + sr, ar * hi + ai * hr + si

    zero = jnp.zeros((groups, bsz, SSM_STATE), F32)
    lax.fori_loop(0, n_chunks, step, (zero, zero))
    for g in range(groups):
        y = (_dot(u_ref[g], t0_ref[g]) + _dot(hre_ref[g].astype(BF16), wc_re_ref[g])
             + _dot(him_ref[g].astype(BF16), wc_im_ref[g]))
        o_ref[g] = _gelu_tanh(y).astype(o_ref.dtype)


def _s5_scan_gelu(u_rows, mats, *, bsz, groups_per_step=8):
    t0, ws_re, ws_im, wc_re, wc_im, a_chunk = mats
    g, rows, _ = u_rows.shape
    gs = groups_per_step
    p = SSM_STATE

    def spec(*shape):
        return pl.BlockSpec((gs,) + shape, lambda i: (i,) + (0,) * len(shape))

    return pl.pallas_call(
        functools.partial(_s5_scan_kernel, groups=gs, bsz=bsz, n_chunks=rows // bsz),
        grid=(g // gs,),
        in_specs=[spec(rows, SSM_ROW), spec(SSM_ROW, SSM_ROW), spec(SSM_ROW, p), spec(SSM_ROW, p),
                  spec(p, SSM_ROW), spec(p, SSM_ROW), spec(2, p)],
        out_specs=spec(rows, SSM_ROW),
        out_shape=jax.ShapeDtypeStruct((g, rows, SSM_ROW), BF16),
        scratch_shapes=[pltpu.VMEM((gs, rows, p), F32), pltpu.VMEM((gs, rows, p), F32),
                        pltpu.VMEM((gs, rows, p), F32), pltpu.VMEM((gs, rows, p), F32)],
        compiler_params=_params("parallel"),
        name="s5_scan_gelu",
    )(u_rows, t0, ws_re, ws_im, wc_re, wc_im, a_chunk)


def _sigmoid(x):
    return 1.0 / (1.0 + jnp.exp(-x))


def _block_transpose(xs):
    n = GROUPS_PER_SLAB
    width = xs[0].shape[1]
    blk = (lax.broadcasted_iota(jnp.int32, (1, width), 1) // SSM_GROUP) % n
    xs = list(xs)
    d = n // 2
    while d >= 1:
        high = (blk & d) != 0
        nxt = list(xs)
        for i in range(n):
            if i & d:
                continue
            lo_arr, hi_arr = xs[i], xs[i + d]
            nxt[i] = jnp.where(high, pltpu.roll(hi_arr, SSM_GROUP * d, 1), lo_arr)
            nxt[i + d] = jnp.where(high, hi_arr, pltpu.roll(lo_arr, width - SSM_GROUP * d, 1))
        xs = nxt
        d //= 2
    return xs


def _u_rows_kernel(x_ref, w_ref, o_ref, stage_ref, *, bsz, chunks):
    tb = x_ref.shape[1]
    u = _dot(x_ref[...].reshape(bsz * tb, x_ref.shape[2]), w_ref[...])
    n_slabs = u.shape[1] // V7X_LANES
    for v in range(n_slabs):
        for b in range(bsz):
            stage_ref[v, b * STAGE_PITCH:b * STAGE_PITCH + tb, :] = (
                u[b * tb:(b + 1) * tb, v * V7X_LANES:(v + 1) * V7X_LANES])
    n = GROUPS_PER_SLAB
    for jh in range(SSM_CHUNK // n):
        xs = []
        for jl in range(n):
            j = jh * n + jl
            xs.append(jnp.concatenate(
                [jnp.concatenate([stage_ref.at[v][pl.ds(SSM_CHUNK * c + j, bsz, stride=STAGE_PITCH), :]
                                  for v in range(n_slabs)], axis=1) for c in range(chunks)], axis=0))
        ys = _block_transpose(xs)
        for e in range(n):
            for v in range(n_slabs):
                o_ref[n * v + e, :, jh * V7X_LANES:(jh + 1) * V7X_LANES] = (
                    ys[e][:, v * V7X_LANES:(v + 1) * V7X_LANES].astype(o_ref.dtype))


def _u_projection_rows(x16, w_u, *, chunks=8):
    bsz, s_len, d = x16.shape
    width = w_u.shape[1]
    g = width // SSM_GROUP
    tb = chunks * SSM_CHUNK
    rows = chunks * bsz
    return pl.pallas_call(
        functools.partial(_u_rows_kernel, bsz=bsz, chunks=chunks),
        grid=(s_len // tb,),
        in_specs=[pl.BlockSpec((bsz, tb, d), lambda i: (0, i, 0)),
                  pl.BlockSpec((d, width), lambda i: (0, 0), pipeline_mode=pl.Buffered(1))],
        out_specs=pl.BlockSpec((g, rows, SSM_ROW), lambda i: (0, i, 0)),
        out_shape=jax.ShapeDtypeStruct((g, (s_len // SSM_CHUNK) * bsz, SSM_ROW), BF16),
        scratch_shapes=[pltpu.VMEM((width // V7X_LANES, bsz * STAGE_PITCH, V7X_LANES), F32)],
        compiler_params=_params("parallel"),
        name="u_projection_rows",
    )(x16, w_u)


def _glu_rows_kernel(gy_ref, w_ref, o_ref, stage_ref, *, bsz, chunks):
    tb = o_ref.shape[1]
    n = GROUPS_PER_SLAB
    n_slabs = gy_ref.shape[0] // n
    for jh in range(SSM_CHUNK // n):
        ys = [jnp.concatenate([gy_ref[n * v + e, :, jh * V7X_LANES:(jh + 1) * V7X_LANES].astype(F32)
                               for v in range(n_slabs)], axis=1) for e in range(n)]
        xs = _block_transpose(ys)
        for jl in range(n):
            j = jh * n + jl
            for c in range(chunks):
                for v in range(n_slabs):
                    stage_ref.at[v][pl.ds(SSM_CHUNK * c + j, bsz, stride=STAGE_PITCH), :] = (
                        xs[jl][c * bsz:(c + 1) * bsz, v * V7X_LANES:(v + 1) * V7X_LANES])
    a = jnp.concatenate(
        [jnp.concatenate([stage_ref[v, b * STAGE_PITCH:b * STAGE_PITCH + tb, :] for v in range(n_slabs)], axis=1)
         for b in range(bsz)], axis=0).astype(BF16)
    z = _dot(a, w_ref[...])
    half = z.shape[1] // 2
    o_ref[...] = (z[:, :half] * _sigmoid(z[:, half:])).astype(o_ref.dtype).reshape(o_ref.shape)


def _glu_from_rows(gy_rows, w_glu, *, bsz, chunks=8):
    g, total_rows, _ = gy_rows.shape
    width = g * SSM_GROUP
    n_out = w_glu.shape[1] // 2
    s_len = total_rows // bsz * SSM_CHUNK
    tb = chunks * SSM_CHUNK
    rows = chunks * bsz
    return pl.pallas_call(
        functools.partial(_glu_rows_kernel, bsz=bsz, chunks=chunks),
        grid=(s_len // tb,),
        in_specs=[pl.BlockSpec((g, rows, SSM_ROW), lambda i: (0, i, 0)),
                  pl.BlockSpec((width, 2 * n_out), lambda i: (0, 0), pipeline_mode=pl.Buffered(1))],
        out_specs=pl.BlockSpec((bsz, tb, n_out), lambda i: (0, i, 0)),
        out_shape=jax.ShapeDtypeStruct((bsz, s_len, n_out), BF16),
        scratch_shapes=[pltpu.VMEM((width // V7X_LANES, bsz * STAGE_PITCH, V7X_LANES), F32)],
        compiler_params=_params("parallel"),
        name="glu_from_rows",
    )(gy_rows, w_glu)


def _ffn_up_kernel(x_ref, wg_ref, wv_ref, cwg_ref, cbg_ref, cwv_ref, cbv_ref, o_ref,
                   wg16_ref, wv16_ref, hg_ref, hv_ref, *, blocks_per_seq):
    i = pl.program_id(1)
    bm, bn = o_ref.shape
    pad = hg_ref.shape[0] - bm

    @pl.when(i == 0)
    def _():
        wg16_ref[...] = wg_ref[...].astype(BF16)
        wv16_ref[...] = wv_ref[...].astype(BF16)

    @pl.when(i % blocks_per_seq == 0)
    def _():
        hg_ref[0:pad, :] = jnp.zeros((pad, bn), F32)
        hv_ref[0:pad, :] = jnp.zeros((pad, bn), F32)

    @pl.when(i % blocks_per_seq != 0)
    def _():
        hg_ref[0:pad, :] = hg_ref[bm:bm + pad, :]
        hv_ref[0:pad, :] = hv_ref[bm:bm + pad, :]

    x = x_ref[...]
    conv = []
    for w16, h_ref, cw, cb in ((wg16_ref, hg_ref, cwg_ref, cbg_ref), (wv16_ref, hv_ref, cwv_ref, cbv_ref)):
        h = _dot(x, w16[...])
        h_ref[pad:pad + bm, :] = h
        conv.append(cb[...] + cw[0:1, :] * h_ref[pad - 2:pad - 2 + bm, :]
                    + cw[1:2, :] * h_ref[pad - 1:pad - 1 + bm, :] + cw[2:3, :] * h)
    gate, val = conv
    o_ref[...] = (gate * _sigmoid(gate) * val).astype(o_ref.dtype)


def _ffn_up(x16, w_up, conv_w, conv_b, *, col0, n_cols, seq_len, bm, bn):
    m, d = x16.shape
    f = w_up.shape[1] // 2
    pad = 8
    cb = conv_b.reshape(1, 2 * f).astype(F32)
    gate_col = lambda j: pl.multiple_of(col0 + j * bn, V7X_LANES)
    val_col = lambda j: pl.multiple_of(f + col0 + j * bn, V7X_LANES)
    gate_spec = lambda rows: pl.BlockSpec((pl.Element(rows), pl.Element(bn)), lambda j, i: (0, gate_col(j)))
    val_spec = lambda rows: pl.BlockSpec((pl.Element(rows), pl.Element(bn)), lambda j, i: (0, val_col(j)))
    return pl.pallas_call(
        functools.partial(_ffn_up_kernel, blocks_per_seq=seq_len // bm),
        grid=(n_cols // bn, m // bm),
        in_specs=[pl.BlockSpec((bm, d), lambda j, i: (i, 0)),
                  gate_spec(d), val_spec(d), gate_spec(CONV_WIDTH), gate_spec(1),
                  val_spec(CONV_WIDTH), val_spec(1)],
        out_specs=pl.BlockSpec((bm, bn), lambda j, i: (i, j)),
        out_shape=jax.ShapeDtypeStruct((m, n_cols), BF16),
        scratch_shapes=[pltpu.VMEM((d, bn), BF16), pltpu.VMEM((d, bn), BF16),
                        pltpu.VMEM((pad + bm, bn), F32), pltpu.VMEM((pad + bm, bn), F32)],
        compiler_params=_params("arbitrary", "arbitrary"),
        name="ffn_up_conv_gate",
    )(x16, w_up, w_up, conv_w, cb, conv_w, cb)


def _conv_ffn(x32, x16, w_up, conv_w, conv_b, w_down, ln_g, ln_b, *, seq_len, bn=512):
    f = w_down.shape[0]
    w_up, conv_w, w_down16 = w_up.astype(F32), conv_w.astype(F32), w_down.astype(BF16)
    main = (f // bn) * bn
    pairs = []
    for col0, n_cols, width in ((0, main, bn), (main, f - main, f - main)):
        if n_cols:
            act = _ffn_up(x16, w_up, conv_w, conv_b, col0=col0, n_cols=n_cols, seq_len=seq_len,
                          bm=1024, bn=width)
            pairs.append((act, w_down16[col0:col0 + n_cols]))
    return _matmul_residual_layernorm(pairs, x32, ln_g, ln_b, bm=256)


def _rope_proj_kernel(a_ref, w_ref, pos_ref, o_ref, cos_ref, sin_ref, *, rope_cols, q_tiles, q_scale):
    j = pl.program_id(1)
    hd, half = SWA_HEAD_DIM, ROPE_DIM // 2

    @pl.when(j == 0)
    def _():
        lane = lax.broadcasted_iota(jnp.int32, (1, V7X_LANES), 1) % hd
        idx = (lane % half).astype(F32)
        inv_freq = jnp.where(lane < ROPE_DIM, jnp.exp(idx * (-math.log(ROPE_THETA) / half)), 0.0)
        ang = pos_ref[...].astype(F32) * inv_freq
        sign = jnp.where(lane < half, -1.0, 1.0)
        cos_ref[...] = jnp.cos(ang)
        sin_ref[...] = jnp.sin(ang) * sign

    acc = _dot(a_ref[...], w_ref[...])
    bm, bn = acc.shape

    reps = bn // V7X_LANES
    cos = jnp.concatenate([cos_ref[...]] * reps, axis=1)
    sin = jnp.concatenate([sin_ref[...]] * reps, axis=1)
    lane = lax.broadcasted_iota(jnp.int32, (1, bn), 1)
    first = (lane & (hd - 1)) < half
    partner = jnp.where(first, pltpu.roll(acc, bn - half, 1), pltpu.roll(acc, half, 1))
    rot = acc * cos + partner * sin
    col = lane + j * bn
    out = jnp.where(col < rope_cols, rot, acc) * jnp.where(col < q_tiles * bn, q_scale, 1.0)
    o_ref[...] = out.astype(o_ref.dtype)


def _rope_projection(x16, w, pos_col, *, rope_cols, q_cols, q_scale, bm, bn):
    m, k = x16.shape
    n = w.shape[1]
    return pl.pallas_call(
        functools.partial(_rope_proj_kernel, rope_cols=rope_cols, q_tiles=q_cols // bn,
                          q_scale=q_scale),
        grid=(m // bm, n // bn),
        in_specs=[pl.BlockSpec((bm, k), lambda i, j: (i, 0)),
                  pl.BlockSpec((k, bn), lambda i, j: (0, j)),
                  pl.BlockSpec((bm, 1), lambda i, j: (i, 0))],
        out_specs=pl.BlockSpec((bm, bn), lambda i, j: (i, j)),
        out_shape=jax.ShapeDtypeStruct((m, n), BF16),
        scratch_shapes=[pltpu.VMEM((bm, V7X_LANES), F32), pltpu.VMEM((bm, V7X_LANES), F32)],
        compiler_params=_params("parallel", "arbitrary"),
        name="rope_projection",
    )(x16, w, pos_col)


def _swa_kernel(sink_ref, q_ref, kp_ref, kc_ref, vp_ref, vc_ref, o_ref):
    n = pl.program_id(1)
    w, hd, lanes = SWA_WINDOW, SWA_HEAD_DIM, V7X_LANES
    pairs = SWA_GROUPS // 2
    qi = lax.broadcasted_iota(jnp.int32, (w, 2 * w), 0)
    kj = lax.broadcasted_iota(jnp.int32, (w, 2 * w), 1)
    rel = w + qi - kj
    valid = (rel >= 0) & (rel < w) & ((n > 0) | (kj >= w))
    bias = jnp.where(valid, 0.0, MASK_VALUE)
    lane_kv = lax.broadcasted_iota(jnp.int32, (2 * w, lanes), 1)
    lane_o = lax.broadcasted_iota(jnp.int32, (w, lanes), 1)

    def both_halves(prev_ref, cur_ref, hk):
        slab = slice((hk // 2) * lanes, (hk // 2 + 1) * lanes)
        x = jnp.concatenate([prev_ref[:, slab], cur_ref[:, slab]], axis=0).astype(F32)
        upper = hk % 2 == 1
        own = jnp.where((lane_kv >= hd) if upper else (lane_kv < hd), x, 0.0)
        other = pltpu.roll(own, hd, 1)
        lo, hi = (other, own) if upper else (own, other)
        return jnp.concatenate([lo, hi], axis=0).astype(BF16)

    for hk in range(SWA_KV_HEADS):
        kk = both_halves(kp_ref, kc_ref, hk)
        vv = both_halves(vp_ref, vc_ref, hk)
        slabs = [slice((hk * pairs + r) * lanes, (hk * pairs + r + 1) * lanes) for r in range(pairs)]
        qs = jnp.concatenate([q_ref[:, sl] for sl in slabs], axis=0)
        s_all = _dot_nt(qs, kk)
        probs, inv_den = [], []
        for r in range(pairs):
            halves, inv = [], []
            for half in range(2):
                sink = sink_ref[hk * SWA_GROUPS + 2 * r + half]
                s = s_all[r * w:(r + 1) * w, half * 2 * w:(half + 1) * 2 * w] + bias
                m = jnp.maximum(jnp.max(s, axis=1, keepdims=True), sink)
                p = jnp.exp(s - m)
                inv.append(1.0 / (jnp.sum(p, axis=1, keepdims=True) + jnp.exp(sink - m)))
                halves.append(p.astype(BF16))
            probs.append(jnp.concatenate(halves, axis=1))
            inv_den.append(jnp.where(lane_o < hd, inv[0], inv[1]))
        o_all = _dot(jnp.concatenate(probs, axis=0), vv)
        for r in range(pairs):
            o_ref[:, slabs[r]] = (o_all[r * w:(r + 1) * w] * inv_den[r]).astype(o_ref.dtype)


def _sliding_window_attention(qkv, sinks):
    bsz, s_len, _ = qkv.shape
    w, hd = SWA_WINDOW, SWA_HEAD_DIM
    qw = SWA_HEADS * hd
    kw = SWA_KV_HEADS * hd
    k_blk = qw // kw
    v_blk = k_blk + 1
    prev = lambda n: jnp.maximum(n - 1, 0)
    return pl.pallas_call(
        _swa_kernel,
        grid=(bsz, s_len // w),
        in_specs=[pl.BlockSpec(memory_space=pltpu.SMEM),
                  pl.BlockSpec((None, w, qw), lambda b, n: (b, n, 0)),
                  pl.BlockSpec((None, w, kw), lambda b, n: (b, prev(n), k_blk)),
                  pl.BlockSpec((None, w, kw), lambda b, n: (b, n, k_blk)),
                  pl.BlockSpec((None, w, kw), lambda b, n: (b, prev(n), v_blk)),
                  pl.BlockSpec((None, w, kw), lambda b, n: (b, n, v_blk))],
        out_specs=pl.BlockSpec((None, w, qw), lambda b, n: (b, n, 0)),
        out_shape=jax.ShapeDtypeStruct((bsz, s_len, qw), BF16),
        compiler_params=_params("parallel", "arbitrary"),
        name="sliding_window_attention",
    )(sinks.astype(F32), qkv, qkv, qkv, qkv, qkv)


def _even_mixer(x32, x16, bsz, s_len, w_in, b_f, lam_re, lam_im, log_step, b_re, b_im, c_re, c_im,
                d_skip, w_glu, w_out, ln_g, ln_b):
    m, d = x16.shape
    fw = FOX_WIDTH
    nh = FOX_HEADS
    w_qkv = w_in[:, :3 * fw].astype(BF16)
    w_u = w_in[:, 3 * fw + nh:].astype(BF16)
    w_f = jnp.pad(w_in[:, 3 * fw:3 * fw + nh], ((0, 0), (0, V7X_LANES - nh))).astype(BF16)
    bias_f = jnp.pad(b_f, (0, V7X_LANES - nh)).reshape(1, V7X_LANES).astype(F32)
    x16_3d = x16.reshape(bsz, s_len, d)

    qkv = _matmul(x16, w_qkv, bm=1024, bn=3 * fw // 2, out_dtype=BF16).reshape(bsz, s_len, -1)
    c_col = _forget_gate_cumsum(x16_3d, w_f, bias_f)
    c_row = jnp.swapaxes(c_col[:, :, :nh], 1, 2)
    fox = _forgetting_attention(qkv, c_col, c_row).reshape(m, fw)

    mats = _s5_prepare(lam_re, lam_im, log_step, b_re, b_im, c_re, c_im, d_skip)
    u_rows = _u_projection_rows(x16_3d, w_u)
    gy_rows = _s5_scan_gelu(u_rows, mats, bsz=bsz)
    ssm = _glu_from_rows(gy_rows, w_glu.astype(BF16), bsz=bsz).reshape(m, -1)
    w_out16 = w_out.astype(BF16)
    return _matmul_residual_layernorm([(fox, w_out16[:fw]), (ssm, w_out16[fw:])], x32, ln_g, ln_b,
                                      bm=512)


def _odd_mixer(x32, x16, bsz, s_len, positions, w_in, sinks, w_out, ln_g, ln_b):
    m = x16.shape[0]
    rope_cols = (SWA_HEADS + SWA_KV_HEADS) * SWA_HEAD_DIM
    qkv = _rope_projection(x16, w_in.astype(BF16), positions.reshape(m, 1).astype(jnp.int32),
                           rope_cols=rope_cols, q_cols=SWA_HEADS * SWA_HEAD_DIM,
                           q_scale=1.0 / math.sqrt(SWA_HEAD_DIM), bm=1024, bn=512)
    o = _sliding_window_attention(qkv.reshape(bsz, s_len, -1), sinks).reshape(m, -1)
    return _matmul_residual_layernorm([(o, w_out.astype(BF16))], x32, ln_g, ln_b, bm=512)


def kernel(x, positions, ev_w_in, ev_b_f, ev_lambda_re, ev_lambda_im, ev_log_step, ev_ssm_b_re,
           ev_ssm_b_im, ev_ssm_c_re, ev_ssm_c_im, ev_ssm_d, ev_w_glu, ev_w_out, od_w_in, od_sinks,
           od_w_out, ln_mix_g, ln_mix_b, ffn_w_up, ffn_conv_w, ffn_conv_b, ffn_w_down, ln_ffn_g,
           ln_ffn_b):
    bsz, s_len, d = x.shape
    x32 = x.reshape(bsz * s_len, d).astype(F32)
    x16 = x32.astype(BF16)
    for i in range(DEPTH):
        j = i // 2
        if i % 2 == 0:
            x32, x16 = _even_mixer(x32, x16, bsz, s_len, ev_w_in[j], ev_b_f[j], ev_lambda_re[j],
                                   ev_lambda_im[j], ev_log_step[j], ev_ssm_b_re[j], ev_ssm_b_im[j],
                                   ev_ssm_c_re[j], ev_ssm_c_im[j], ev_ssm_d[j], ev_w_glu[j],
                                   ev_w_out[j], ln_mix_g[i], ln_mix_b[i])
        else:
            x32, x16 = _odd_mixer(x32, x16, bsz, s_len, positions, od_w_in[j], od_sinks[j],
                                  od_w_out[j], ln_mix_g[i], ln_mix_b[i])
        x32, x16 = _conv_ffn(x32, x16, ffn_w_up[i], ffn_conv_w[i], ffn_conv_b[i], ffn_w_down[i],
                             ln_ffn_g[i], ln_ffn_b[i], seq_len=s_len)
    return x32.reshape(bsz, s_len, d).astype(x.dtype)
```

```python
import functools
import math

import jax
import jax.numpy as jnp
from jax import lax
from jax.experimental import pallas as pl
from jax.experimental.pallas import tpu as pltpu

F32 = jnp.float32
BF16 = jnp.bfloat16

DEPTH = 2
FOX_HEADS = 8
FOX_HEAD_DIM = 128
FOX_WIDTH = FOX_HEADS * FOX_HEAD_DIM
SSM_GROUP = 16
SSM_STATE = 64
SSM_CHUNK = 16
SSM_ROW = SSM_CHUNK * SSM_GROUP
SWA_HEADS = 32
SWA_KV_HEADS = 4
SWA_HEAD_DIM = 64
SWA_GROUPS = SWA_HEADS // SWA_KV_HEADS
SWA_WINDOW = 128
ROPE_DIM = SWA_HEAD_DIM // 4
ROPE_THETA = 500000.0
CONV_WIDTH = 3
LN_EPS = 1e-5
DEEPNORM_ALPHA = (2.0 * DEPTH) ** 0.25

V7X_LANES = 128
V7X_VMEM_LIMIT_BYTES = 56 * 1024 * 1024
GROUPS_PER_SLAB = V7X_LANES // SSM_GROUP
STAGE_PITCH = 136
MASK_VALUE = -1e30
LOG2E = math.log2(math.e)

HIGHEST = lax.Precision.HIGHEST


def _params(*semantics):
    return pltpu.CompilerParams(dimension_semantics=semantics,
                                vmem_limit_bytes=V7X_VMEM_LIMIT_BYTES)


def _dot(a, b):
    return jnp.dot(a, b, preferred_element_type=F32)


def _dot_nt(a, b):
    return lax.dot_general(a, b, (((1,), (1,)), ((), ())), preferred_element_type=F32)


def _mm_kernel(a_ref, w_ref, o_ref):
    o_ref[...] = _dot(a_ref[...], w_ref[...]).astype(o_ref.dtype)


def _matmul(a, w, *, bm, bn, out_dtype):
    m, k = a.shape
    n = w.shape[1]
    return pl.pallas_call(
        _mm_kernel,
        grid=(m // bm, n // bn),
        in_specs=[pl.BlockSpec((bm, k), lambda i, j: (i, 0)),
                  pl.BlockSpec((k, bn), lambda i, j: (0, j))],
        out_specs=pl.BlockSpec((bm, bn), lambda i, j: (i, j)),
        out_shape=jax.ShapeDtypeStruct((m, n), out_dtype),
        compiler_params=_params("parallel", "arbitrary"),
        name="matmul",
    )(a, w)


def _mm_res_ln_kernel(*refs, n_pairs, sub):
    a_refs = refs[:n_pairs]
    w_refs = refs[n_pairs:2 * n_pairs]
    x_ref, g_ref, b_ref, o32_ref, o16_ref = refs[2 * n_pairs:]
    for r in range(x_ref.shape[0] // sub):
        rows = slice(r * sub, (r + 1) * sub)
        pre = DEEPNORM_ALPHA * x_ref[rows, :]
        for a_ref, w_ref in zip(a_refs, w_refs):
            pre = pre + _dot(a_ref[rows, :], w_ref[...])
        mu = jnp.mean(pre, axis=-1, keepdims=True)
        cen = pre - mu
        var = jnp.mean(cen * cen, axis=-1, keepdims=True)
        y = cen * lax.rsqrt(var + LN_EPS) * g_ref[...] + b_ref[...]
        o32_ref[rows, :] = y
        o16_ref[rows, :] = y.astype(BF16)


def _matmul_residual_layernorm(pairs, x32, gain, bias, *, bm, sub=256):
    m, d = x32.shape
    n_pairs = len(pairs)
    a_list = [p[0] for p in pairs]
    w_list = [p[1] for p in pairs]
    row_spec = lambda width: pl.BlockSpec((bm, width), lambda i: (i, 0))
    const_spec = lambda rows: pl.BlockSpec((rows, d), lambda i: (0, 0), pipeline_mode=pl.Buffered(1))
    in_specs = ([row_spec(a.shape[1]) for a in a_list] + [const_spec(w.shape[0]) for w in w_list]
                + [row_spec(d), const_spec(1), const_spec(1)])
    return pl.pallas_call(
        functools.partial(_mm_res_ln_kernel, n_pairs=n_pairs, sub=sub),
        grid=(m // bm,),
        in_specs=in_specs,
        out_specs=[row_spec(d), row_spec(d)],
        out_shape=[jax.ShapeDtypeStruct((m, d), F32), jax.ShapeDtypeStruct((m, d), BF16)],
        compiler_params=_params("parallel"),
        name="matmul_residual_layernorm",
    )(*a_list, *w_list, x32, gain.reshape(1, d).astype(F32), bias.reshape(1, d).astype(F32))


def _bf16_split3(v):
    hi = v.astype(BF16)
    r1 = v - hi.astype(F32)
    mid = r1.astype(BF16)
    lo = (r1 - mid.astype(F32)).astype(BF16)
    return hi, mid, lo


def _forget_gate_kernel(x_ref, wf_ref, bf_ref, c_ref, *, chunk):
    s_len = x_ref.shape[0]
    f = _dot(x_ref[...], wf_ref[...]) + bf_ref[...]
    log_f = jnp.minimum(f, 0.0) - jnp.log(1.0 + jnp.exp(-jnp.abs(f)))
    row = lax.broadcasted_iota(jnp.int32, (chunk, chunk), 0)
    col = lax.broadcasted_iota(jnp.int32, (chunk, chunk), 1)
    tri = (row >= col).astype(BF16)
    carry = jnp.zeros((1, log_f.shape[1]), F32)
    for start in range(0, s_len, chunk):
        hi, mid, lo = _bf16_split3(log_f[start:start + chunk])
        part = _dot(tri, hi) + _dot(tri, mid) + _dot(tri, lo) + carry
        c_ref[start:start + chunk, :] = part
        carry = part[chunk - 1:chunk, :]


def _forget_gate_cumsum(x16, w_f, b_f, *, chunk=256):
    bsz, s_len, d = x16.shape
    n = w_f.shape[1]
    return pl.pallas_call(
        functools.partial(_forget_gate_kernel, chunk=chunk),
        grid=(bsz,),
        in_specs=[pl.BlockSpec((None, s_len, d), lambda b: (b, 0, 0)),
                  pl.BlockSpec((d, n), lambda b: (0, 0)),
                  pl.BlockSpec((1, n), lambda b: (0, 0))],
        out_specs=pl.BlockSpec((None, s_len, n), lambda b: (b, 0, 0)),
        out_shape=jax.ShapeDtypeStruct((bsz, s_len, n), F32),
        compiler_params=_params("parallel"),
        name="forget_gate_cumsum",
    )(x16, w_f, b_f)


def _bias_lanes(c, first):
    hi, mid, lo = (piece.astype(F32) for piece in _bf16_split3(c))
    lane = lax.broadcasted_iota(jnp.int32, (c.shape[0], V7X_LANES), 1)
    ones = jnp.where(lane < 6, 1.0, 0.0)
    return jnp.where(lane == first, hi, jnp.where(lane == first + 1, mid, jnp.where(lane == first + 2, lo, ones)))


def _fox_kernel(q_ref, k_ref, v_ref, c_ref, o_ref, kaug_ref, *, tile, heads):
    qi = pl.program_id(2)
    dh = FOX_HEAD_DIM
    aug = 2 * dh
    h0 = pl.program_id(1) * heads

    def head_column(c_all, hh):
        lane = lax.broadcasted_iota(jnp.int32, c_all.shape, 1)
        return jnp.sum(jnp.where(lane == h0 + hh, c_all, 0.0), axis=1, keepdims=True) * LOG2E

    @pl.when(qi == 0)
    def _():
        c_all = c_ref[...]
        for hh in range(heads):
            kaug_ref[:, hh * aug:hh * aug + dh] = k_ref[:, hh * dh:(hh + 1) * dh]
            kaug_ref[:, hh * aug + dh:(hh + 1) * aug] = _bias_lanes(-head_column(c_all, hh), 3).astype(BF16)

    cq_all = c_ref[pl.ds(pl.multiple_of(qi * tile, tile), tile), :]
    q_aug = [jnp.concatenate([q_ref[:, hh * dh:(hh + 1) * dh],
                              _bias_lanes(head_column(cq_all, hh), 0).astype(BF16)], axis=1)
             for hh in range(heads)]

    def step(j, carry, masked):
        start = pl.multiple_of(j * tile, tile)
        out = []
        for hh in range(heads):
            m, l, acc = carry[hh]
            k = kaug_ref[pl.ds(start, tile), hh * aug:(hh + 1) * aug]
            v = v_ref[pl.ds(start, tile), hh * dh:(hh + 1) * dh]
            s = _dot_nt(q_aug[hh], k)
            if masked:
                row = lax.broadcasted_iota(jnp.int32, s.shape, 0)
                col = lax.broadcasted_iota(jnp.int32, s.shape, 1)
                s = jnp.where(row >= col, s, MASK_VALUE)
            m_new = jnp.maximum(m, jnp.max(s, axis=1, keepdims=True))
            alpha = jnp.exp2(m - m_new)
            p = jnp.exp2(s - m_new)
            l = alpha * l + jnp.sum(p, axis=1, keepdims=True)
            acc = alpha * acc + _dot(p.astype(BF16), v)
            out.append((m_new, l, acc))
        return tuple(out)

    init = tuple((jnp.full((tile, 1), MASK_VALUE, F32), jnp.zeros((tile, 1), F32),
                  jnp.zeros((tile, dh), F32)) for _ in range(heads))
    carry = lax.fori_loop(0, qi, functools.partial(step, masked=False), init)
    carry = step(qi, carry, masked=True)
    for hh in range(heads):
        _, l, acc = carry[hh]
        o_ref[:, hh * dh:(hh + 1) * dh] = (acc / l).astype(o_ref.dtype)


def _forgetting_attention(qkv, c_col, *, tile=1024, heads=2):
    bsz, s_len, _ = qkv.shape
    width = heads * FOX_HEAD_DIM
    nblk = FOX_WIDTH // width
    return pl.pallas_call(
        functools.partial(_fox_kernel, tile=tile, heads=heads),
        grid=(bsz, nblk, s_len // tile),
        in_specs=[pl.BlockSpec((None, tile, width), lambda b, h, i: (b, i, h)),
                  pl.BlockSpec((None, s_len, width), lambda b, h, i: (b, 0, nblk + h)),
                  pl.BlockSpec((None, s_len, width), lambda b, h, i: (b, 0, 2 * nblk + h)),
                  pl.BlockSpec((None, s_len, c_col.shape[2]), lambda b, h, i: (b, 0, 0))],
        out_specs=pl.BlockSpec((None, tile, width), lambda b, h, i: (b, i, h)),
        out_shape=jax.ShapeDtypeStruct((bsz, s_len, FOX_WIDTH), BF16),
        scratch_shapes=[pltpu.VMEM((s_len, 2 * width), BF16)],
        compiler_params=_params("parallel", "parallel", "arbitrary"),
        name="forgetting_attention",
    )(qkv, qkv, qkv, c_col)


def _cmul(ar, ai, br, bi):
    return ar * br - ai * bi, ar * bi + ai * br


def _cpow(ar, ai, e, n_bits):
    rr = jnp.ones_like(ar)
    ri = jnp.zeros_like(ai)
    for bit in range(n_bits):
        nr, ni = _cmul(rr, ri, ar, ai)
        take = ((e >> bit) & 1) == 1
        rr = jnp.where(take, nr, rr)
        ri = jnp.where(take, ni, ri)
        if bit + 1 < n_bits:
            ar, ai = _cmul(ar, ai, ar, ai)
    return rr, ri


def _discretise(lr, li, dt):
    mag = jnp.exp(lr * dt)
    return mag * jnp.cos(li * dt), mag * jnp.sin(li * dt)


def _s5_prep_kernel(lam_row_ref, lam_col_ref, dt_ref, bt_re_ref, bt_im_ref, ct_re_ref, ct_im_ref,
                    d_ref, t0_ref, ws_re_ref, ws_im_ref, wc_re_ref, wc_im_ref, a_chunk_ref):
    chunk, grp, p = SSM_CHUNK, SSM_GROUP, SSM_STATE
    n_bits = chunk.bit_length()
    dt = jnp.exp(dt_ref[...])
    lr_r, li_r = lam_row_ref[0:1, :], lam_row_ref[1:2, :]
    ar_r, ai_r = _discretise(lr_r, li_r, dt)
    den = lr_r * lr_r + li_r * li_r
    xr, xi = ar_r - 1.0, ai_r
    g_re = (xr * lr_r + xi * li_r) / den
    g_im = (xi * lr_r - xr * li_r) / den
    bbt_re, bbt_im = _cmul(g_re, g_im, bt_re_ref[...], bt_im_ref[...])
    bx_re = jnp.concatenate([bbt_re] * chunk, axis=0)
    bx_im = jnp.concatenate([bbt_im] * chunk, axis=0)
    row_i = lax.broadcasted_iota(jnp.int32, (SSM_ROW, p), 0) // grp
    pr, pi = _cpow(jnp.broadcast_to(ar_r, (SSM_ROW, p)), jnp.broadcast_to(ai_r, (SSM_ROW, p)),
                   chunk - 1 - row_i, n_bits)
    ws_re, ws_im = _cmul(pr, pi, bx_re, bx_im)
    ws_re_ref[...] = ws_re.astype(BF16)
    ws_im_ref[...] = ws_im.astype(BF16)
    ac_r, ac_i = _cpow(ar_r, ai_r, jnp.full((1, p), chunk, jnp.int32), n_bits)
    a_chunk_ref[0:1, :] = ac_r
    a_chunk_ref[1:2, :] = ac_i
    lr_c, li_c = lam_col_ref[:, 0:1], lam_col_ref[:, 1:2]
    ar_c, ai_c = _discretise(lr_c, li_c, dt)
    rep = (lax.broadcasted_iota(jnp.int32, (grp, SSM_ROW), 0)
           == lax.broadcasted_iota(jnp.int32, (grp, SSM_ROW), 1) % grp).astype(F32)
    cx_re = jnp.dot(ct_re_ref[...], rep, precision=HIGHEST, preferred_element_type=F32)
    cx_im = jnp.dot(ct_im_ref[...], rep, precision=HIGHEST, preferred_element_type=F32)
    dx = jnp.dot(d_ref[...], rep, precision=HIGHEST, preferred_element_type=F32)
    col_j = lax.broadcasted_iota(jnp.int32, (p, SSM_ROW), 1) // grp
    ab_r = jnp.broadcast_to(ar_c, (p, SSM_ROW))
    ab_i = jnp.broadcast_to(ai_c, (p, SSM_ROW))
    qr, qi = _cpow(ab_r, ab_i, col_j, n_bits)
    f_re, f_im = _cmul(qr, qi, cx_re, cx_im)
    e_re, e_im = _cmul(f_re, f_im, ab_r, ab_i)
    wc_re_ref[...] = e_re.astype(BF16)
    wc_im_ref[...] = (-e_im).astype(BF16)
    r0 = (jnp.dot(bbt_re, f_re, precision=HIGHEST, preferred_element_type=F32)
          - jnp.dot(bbt_im, f_im, precision=HIGHEST, preferred_element_type=F32))
    lane = lax.broadcasted_iota(jnp.int32, (grp, SSM_ROW), 1)
    sub = lax.broadcasted_iota(jnp.int32, (grp, SSM_ROW), 0)
    for i in range(chunk):
        blk = r0 if i == 0 else pltpu.roll(r0, i * grp, 1)
        blk = jnp.where(lane >= i * grp, blk, 0.0)
        blk = blk + jnp.where(lane == i * grp + sub, dx, 0.0)
        t0_ref[i * grp:(i + 1) * grp, :] = blk.astype(BF16)


def _s5_prepare(lam_re, lam_im, log_step, b_re, b_im, c_re, c_im, d_skip):
    g, p = lam_re.shape
    grp = SSM_GROUP
    lam_row = jnp.stack([lam_re, lam_im], axis=1).astype(F32)
    lam_col = jnp.stack([lam_re, lam_im], axis=2).astype(F32)
    dt = log_step.reshape(g, 1, 1).astype(F32)
    bt_re = jnp.swapaxes(b_re, 1, 2).astype(F32)
    bt_im = jnp.swapaxes(b_im, 1, 2).astype(F32)
    ct_re = jnp.swapaxes(c_re, 1, 2).astype(F32)
    ct_im = jnp.swapaxes(c_im, 1, 2).astype(F32)
    d3 = d_skip.reshape(g, 1, grp).astype(F32)

    def spec(*shape):
        return pl.BlockSpec((None,) + shape, lambda i: (i,) + (0,) * len(shape))

    return pl.pallas_call(
        _s5_prep_kernel,
        grid=(g,),
        in_specs=[spec(2, p), spec(p, 2), spec(1, 1), spec(grp, p), spec(grp, p),
                  spec(p, grp), spec(p, grp), spec(1, grp)],
        out_specs=[spec(SSM_ROW, SSM_ROW), spec(SSM_ROW, p), spec(SSM_ROW, p),
                   spec(p, SSM_ROW), spec(p, SSM_ROW), spec(2, p)],
        out_shape=[jax.ShapeDtypeStruct((g, SSM_ROW, SSM_ROW), BF16),
                   jax.ShapeDtypeStruct((g, SSM_ROW, p), BF16),
                   jax.ShapeDtypeStruct((g, SSM_ROW, p), BF16),
                   jax.ShapeDtypeStruct((g, p, SSM_ROW), BF16),
                   jax.ShapeDtypeStruct((g, p, SSM_ROW), BF16),
                   jax.ShapeDtypeStruct((g, 2, p), F32)],
        compiler_params=_params("parallel"),
        name="s5_prepare",
    )(lam_row, lam_col, dt, bt_re, bt_im, ct_re, ct_im, d3)


def _gelu_tanh(y):
    return 0.5 * y * (1.0 + jnp.tanh(math.sqrt(2.0 / math.pi) * (y + 0.044715 * (y * y * y))))


def _s5_scan_kernel(u_ref, t0_ref, ws_re_ref, ws_im_ref, wc_re_ref, wc_im_ref, a_ref, o_ref,
                    sre_ref, sim_ref, hre_ref, him_ref, *, groups, bsz, n_chunks):
    for g in range(groups):
        u = u_ref[g]
        sre_ref[g] = _dot(u, ws_re_ref[g])
        sim_ref[g] = _dot(u, ws_im_ref[g])
    ar = a_ref[:, 0:1, :]
    ai = a_ref[:, 1:2, :]

    def step(k, carry):
        hr, hi = carry
        rows = pl.ds(pl.multiple_of(k * bsz, bsz), bsz)
        hre_ref[:, rows, :] = hr
        him_ref[:, rows, :] = hi
        sr = sre_ref[:, rows, :]
        si = sim_ref[:, rows, :]
        return ar * hr - ai * hi + sr, ar * hi + ai * hr + si

    zero = jnp.zeros((groups, bsz, SSM_STATE), F32)
    lax.fori_loop(0, n_chunks, step, (zero, zero))
    for g in range(groups):
        y = (_dot(u_ref[g], t0_ref[g]) + _dot(hre_ref[g].astype(BF16), wc_re_ref[g])
             + _dot(him_ref[g].astype(BF16), wc_im_ref[g]))
        o_ref[g] = _gelu_tanh(y).astype(o_ref.dtype)


def _s5_scan_gelu(u_rows, mats, *, bsz, groups_per_step=8):
    t0, ws_re, ws_im, wc_re, wc_im, a_chunk = mats
    g, rows, _ = u_rows.shape
    gs = groups_per_step
    p = SSM_STATE

    def spec(*shape):
        return pl.BlockSpec((gs,) + shape, lambda i: (i,) + (0,) * len(shape))

    return pl.pallas_call(
        functools.partial(_s5_scan_kernel, groups=gs, bsz=bsz, n_chunks=rows // bsz),
        grid=(g // gs,),
        in_specs=[spec(rows, SSM_ROW), spec(SSM_ROW, SSM_ROW), spec(SSM_ROW, p), spec(SSM_ROW, p),
                  spec(p, SSM_ROW), spec(p, SSM_ROW), spec(2, p)],
        out_specs=spec(rows, SSM_ROW),
        out_shape=jax.ShapeDtypeStruct((g, rows, SSM_ROW), BF16),
        scratch_shapes=[pltpu.VMEM((gs, rows, p), F32), pltpu.VMEM((gs, rows, p), F32),
                        pltpu.VMEM((gs, rows, p), F32), pltpu.VMEM((gs, rows, p), F32)],
        compiler_params=_params("parallel"),
        name="s5_scan_gelu",
    )(u_rows, t0, ws_re, ws_im, wc_re, wc_im, a_chunk)


def _sigmoid(x):
    return 1.0 / (1.0 + jnp.exp(-x))


def _block_transpose(xs):
    n = GROUPS_PER_SLAB
    width = xs[0].shape[1]
    blk = (lax.broadcasted_iota(jnp.int32, (1, width), 1) // SSM_GROUP) % n
    xs = list(xs)
    d = n // 2
    while d >= 1:
        high = (blk & d) != 0
        nxt = list(xs)
        for i in range(n):
            if i & d:
                continue
            lo_arr, hi_arr = xs[i], xs[i + d]
            nxt[i] = jnp.where(high, pltpu.roll(hi_arr, SSM_GROUP * d, 1), lo_arr)
            nxt[i + d] = jnp.where(high, hi_arr, pltpu.roll(lo_arr, width - SSM_GROUP * d, 1))
        xs = nxt
        d //= 2
    return xs


def _u_rows_kernel(x_ref, w_ref, o_ref, stage_ref, *, bsz, chunks):
    tb = x_ref.shape[1]
    u = _dot(x_ref[...].reshape(bsz * tb, x_ref.shape[2]), w_ref[...])
    n_slabs = u.shape[1] // V7X_LANES
    for v in range(n_slabs):
        for b in range(bsz):
            stage_ref[v, b * STAGE_PITCH:b * STAGE_PITCH + tb, :] = (
                u[b * tb:(b + 1) * tb, v * V7X_LANES:(v + 1) * V7X_LANES])
    n = GROUPS_PER_SLAB
    for jh in range(SSM_CHUNK // n):
        xs = []
        for jl in range(n):
            j = jh * n + jl
            xs.append(jnp.concatenate(
                [jnp.concatenate([stage_ref.at[v][pl.ds(SSM_CHUNK * c + j, bsz, stride=STAGE_PITCH), :]
                                  for v in range(n_slabs)], axis=1) for c in range(chunks)], axis=0))
        ys = _block_transpose(xs)
        for e in range(n):
            for v in range(n_slabs):
                o_ref[n * v + e, :, jh * V7X_LANES:(jh + 1) * V7X_LANES] = (
                    ys[e][:, v * V7X_LANES:(v + 1) * V7X_LANES].astype(o_ref.dtype))


def _u_projection_rows(x16, w_u, *, chunks=8):
    bsz, s_len, d = x16.shape
    width = w_u.shape[1]
    g = width // SSM_GROUP
    tb = chunks * SSM_CHUNK
    rows = chunks * bsz
    return pl.pallas_call(
        functools.partial(_u_rows_kernel, bsz=bsz, chunks=chunks),
        grid=(s_len // tb,),
        in_specs=[pl.BlockSpec((bsz, tb, d), lambda i: (0, i, 0)),
                  pl.BlockSpec((d, width), lambda i: (0, 0), pipeline_mode=pl.Buffered(1))],
        out_specs=pl.BlockSpec((g, rows, SSM_ROW), lambda i: (0, i, 0)),
        out_shape=jax.ShapeDtypeStruct((g, (s_len // SSM_CHUNK) * bsz, SSM_ROW), BF16),
        scratch_shapes=[pltpu.VMEM((width // V7X_LANES, bsz * STAGE_PITCH, V7X_LANES), F32)],
        compiler_params=_params("parallel"),
        name="u_projection_rows",
    )(x16, w_u)


def _glu_rows_kernel(gy_ref, w_ref, o_ref, stage_ref, *, bsz, chunks):
    tb = o_ref.shape[1]
    n = GROUPS_PER_SLAB
    n_slabs = gy_ref.shape[0] // n
    for jh in range(SSM_CHUNK // n):
        ys = [jnp.concatenate([gy_ref[n * v + e, :, jh * V7X_LANES:(jh + 1) * V7X_LANES].astype(F32)
                               for v in range(n_slabs)], axis=1) for e in range(n)]
        xs = _block_transpose(ys)
        for jl in range(n):
            j = jh * n + jl
            for c in range(chunks):
                for v in range(n_slabs):
                    stage_ref.at[v][pl.ds(SSM_CHUNK * c + j, bsz, stride=STAGE_PITCH), :] = (
                        xs[jl][c * bsz:(c + 1) * bsz, v * V7X_LANES:(v + 1) * V7X_LANES])
    a = jnp.concatenate(
        [jnp.concatenate([stage_ref[v, b * STAGE_PITCH:b * STAGE_PITCH + tb, :] for v in range(n_slabs)], axis=1)
         for b in range(bsz)], axis=0).astype(BF16)
    z = _dot(a, w_ref[...])
    half = z.shape[1] // 2
    o_ref[...] = (z[:, :half] * _sigmoid(z[:, half:])).astype(o_ref.dtype).reshape(o_ref.shape)


def _glu_from_rows(gy_rows, w_glu, *, bsz, chunks=8):
    g, total_rows, _ = gy_rows.shape
    width = g * SSM_GROUP
    n_out = w_glu.shape[1] // 2
    s_len = total_rows // bsz * SSM_CHUNK
    tb = chunks * SSM_CHUNK
    rows = chunks * bsz
    return pl.pallas_call(
        functools.partial(_glu_rows_kernel, bsz=bsz, chunks=chunks),
        grid=(s_len // tb,),
        in_specs=[pl.BlockSpec((g, rows, SSM_ROW), lambda i: (0, i, 0)),
                  pl.BlockSpec((width, 2 * n_out), lambda i: (0, 0), pipeline_mode=pl.Buffered(1))],
        out_specs=pl.BlockSpec((bsz, tb, n_out), lambda i: (0, i, 0)),
        out_shape=jax.ShapeDtypeStruct((bsz, s_len, n_out), BF16),
        scratch_shapes=[pltpu.VMEM((width // V7X_LANES, bsz * STAGE_PITCH, V7X_LANES), F32)],
        compiler_params=_params("parallel"),
        name="glu_from_rows",
    )(gy_rows, w_glu)


def _ffn_up_kernel(x_ref, wg_ref, wv_ref, cwg_ref, cbg_ref, cwv_ref, cbv_ref, o_ref,
                   wg16_ref, wv16_ref, hg_ref, hv_ref, *, blocks_per_seq):
    i = pl.program_id(1)
    bm, bn = o_ref.shape
    pad = hg_ref.shape[0] - bm

    @pl.when(i == 0)
    def _():
        wg16_ref[...] = wg_ref[0].astype(BF16)
        wv16_ref[...] = wv_ref[0].astype(BF16)

    @pl.when(i % blocks_per_seq == 0)
    def _():
        hg_ref[0:pad, :] = jnp.zeros((pad, bn), F32)
        hv_ref[0:pad, :] = jnp.zeros((pad, bn), F32)

    @pl.when(i % blocks_per_seq != 0)
    def _():
        hg_ref[0:pad, :] = hg_ref[bm:bm + pad, :]
        hv_ref[0:pad, :] = hv_ref[bm:bm + pad, :]

    x = x_ref[...]
    conv = []
    for w16, h_ref, cw_ref, cb_ref in ((wg16_ref, hg_ref, cwg_ref, cbg_ref), (wv16_ref, hv_ref, cwv_ref, cbv_ref)):
        h = _dot(x, w16[...])
        h_ref[pad:pad + bm, :] = h
        cw = cw_ref[0]
        conv.append(cb_ref[0] + cw[0:1, :] * h_ref[pad - 2:pad - 2 + bm, :]
                    + cw[1:2, :] * h_ref[pad - 1:pad - 1 + bm, :] + cw[2:3, :] * h)
    gate, val = conv
    o_ref[...] = (gate * _sigmoid(gate) * val).astype(o_ref.dtype)


def _ffn_up(x16, layer, w_up, conv_w, conv_b, *, col0, n_cols, seq_len, bm, bn):
    m, d = x16.shape
    f = w_up.shape[2] // 2
    pad = 8
    cb = conv_b.reshape(conv_b.shape[0], 1, 2 * f)
    gate_col = lambda j: pl.multiple_of(col0 + j * bn, V7X_LANES)
    val_col = lambda j: pl.multiple_of(f + col0 + j * bn, V7X_LANES)
    window = lambda rows: (pl.Element(1), pl.Element(rows), pl.Element(bn))
    gate_spec = lambda rows: pl.BlockSpec(window(rows), lambda j, i: (layer, 0, gate_col(j)))
    val_spec = lambda rows: pl.BlockSpec(window(rows), lambda j, i: (layer, 0, val_col(j)))
    return pl.pallas_call(
        functools.partial(_ffn_up_kernel, blocks_per_seq=seq_len // bm),
        grid=(n_cols // bn, m // bm),
        in_specs=[pl.BlockSpec((bm, d), lambda j, i: (i, 0)),
                  gate_spec(d), val_spec(d), gate_spec(CONV_WIDTH), gate_spec(1),
                  val_spec(CONV_WIDTH), val_spec(1)],
        out_specs=pl.BlockSpec((bm, bn), lambda j, i: (i, j)),
        out_shape=jax.ShapeDtypeStruct((m, n_cols), BF16),
        scratch_shapes=[pltpu.VMEM((d, bn), BF16), pltpu.VMEM((d, bn), BF16),
                        pltpu.VMEM((pad + bm, bn), F32), pltpu.VMEM((pad + bm, bn), F32)],
        compiler_params=_params("arbitrary", "arbitrary"),
        name="ffn_up_conv_gate",
    )(x16, w_up, w_up, conv_w, cb, conv_w, cb)


def _conv_ffn(x32, x16, layer, w_up, conv_w, conv_b, w_down, ln_g, ln_b, *, seq_len, bn=512):
    f = w_down.shape[0]
    w_up, conv_w, conv_b = w_up.astype(F32), conv_w.astype(F32), conv_b.astype(F32)
    w_down16 = w_down.astype(BF16)
    main = (f // bn) * bn
    pairs = []
    for col0, n_cols, width in ((0, main, bn), (main, f - main, f - main)):
        if n_cols:
            act = _ffn_up(x16, layer, w_up, conv_w, conv_b, col0=col0, n_cols=n_cols, seq_len=seq_len,
                          bm=1024, bn=width)
            pairs.append((act, w_down16[col0:col0 + n_cols]))
    return _matmul_residual_layernorm(pairs, x32, ln_g, ln_b, bm=256)


def _rope_proj_kernel(a_ref, w_ref, pos_ref, o_ref, cos_ref, sin_ref, *, rope_cols, q_tiles, q_scale):
    j = pl.program_id(1)
    hd, half = SWA_HEAD_DIM, ROPE_DIM // 2

    @pl.when(j == 0)
    def _():
        lane = lax.broadcasted_iota(jnp.int32, (1, V7X_LANES), 1) % hd
        idx = (lane % half).astype(F32)
        inv_freq = jnp.where(lane < ROPE_DIM, jnp.exp(idx * (-math.log(ROPE_THETA) / half)), 0.0)
        ang = pos_ref[...].astype(F32) * inv_freq
        sign = jnp.where(lane < half, -1.0, 1.0)
        cos_ref[...] = jnp.cos(ang)
        sin_ref[...] = jnp.sin(ang) * sign

    acc = _dot(a_ref[...], w_ref[...])
    bm, bn = acc.shape

    reps = bn // V7X_LANES
    cos = jnp.concatenate([cos_ref[...]] * reps, axis=1)
    sin = jnp.concatenate([sin_ref[...]] * reps, axis=1)
    lane = lax.broadcasted_iota(jnp.int32, (1, bn), 1)
    first = (lane & (hd - 1)) < half
    partner = jnp.where(first, pltpu.roll(acc, bn - half, 1), pltpu.roll(acc, half, 1))
    rot = acc * cos + partner * sin
    col = lane + j * bn
    out = jnp.where(col < rope_cols, rot, acc) * jnp.where(col < q_tiles * bn, q_scale, 1.0)
    o_ref[...] = out.astype(o_ref.dtype)


def _rope_projection(x16, w, pos_col, *, rope_cols, q_cols, q_scale, bm, bn):
    m, k = x16.shape
    n = w.shape[1]
    return pl.pallas_call(
        functools.partial(_rope_proj_kernel, rope_cols=rope_cols, q_tiles=q_cols // bn,
                          q_scale=q_scale),
        grid=(m // bm, n // bn),
        in_specs=[pl.BlockSpec((bm, k), lambda i, j: (i, 0)),
                  pl.BlockSpec((k, bn), lambda i, j: (0, j)),
                  pl.BlockSpec((bm, 1), lambda i, j: (i, 0))],
        out_specs=pl.BlockSpec((bm, bn), lambda i, j: (i, j)),
        out_shape=jax.ShapeDtypeStruct((m, n), BF16),
        scratch_shapes=[pltpu.VMEM((bm, V7X_LANES), F32), pltpu.VMEM((bm, V7X_LANES), F32)],
        compiler_params=_params("parallel", "arbitrary"),
        name="rope_projection",
    )(x16, w, pos_col)


def _swa_kernel(sink_ref, q_ref, kp_ref, kc_ref, vp_ref, vc_ref, o_ref):
    n = pl.program_id(1)
    w, hd, lanes = SWA_WINDOW, SWA_HEAD_DIM, V7X_LANES
    pairs = SWA_GROUPS // 2
    qi = lax.broadcasted_iota(jnp.int32, (w, 2 * w), 0)
    kj = lax.broadcasted_iota(jnp.int32, (w, 2 * w), 1)
    rel = w + qi - kj
    valid = (rel >= 0) & (rel < w) & ((n > 0) | (kj >= w))
    bias = jnp.where(valid, 0.0, MASK_VALUE)
    lane_kv = lax.broadcasted_iota(jnp.int32, (2 * w, lanes), 1)
    lane_o = lax.broadcasted_iota(jnp.int32, (w, lanes), 1)

    def both_halves(prev_ref, cur_ref, hk):
        slab = slice((hk // 2) * lanes, (hk // 2 + 1) * lanes)
        x = jnp.concatenate([prev_ref[:, slab], cur_ref[:, slab]], axis=0).astype(F32)
        upper = hk % 2 == 1
        own = jnp.where((lane_kv >= hd) if upper else (lane_kv < hd), x, 0.0)
        other = pltpu.roll(own, hd, 1)
        lo, hi = (other, own) if upper else (own, other)
        return jnp.concatenate([lo, hi], axis=0).astype(BF16)

    for hk in range(SWA_KV_HEADS):
        kk = both_halves(kp_ref, kc_ref, hk)
        vv = both_halves(vp_ref, vc_ref, hk)
        slabs = [slice((hk * pairs + r) * lanes, (hk * pairs + r + 1) * lanes) for r in range(pairs)]
        qs = jnp.concatenate([q_ref[:, sl] for sl in slabs], axis=0)
        s_all = _dot_nt(qs, kk)
        probs, inv_den = [], []
        for r in range(pairs):
            halves, inv = [], []
            for half in range(2):
                sink = sink_ref[hk * SWA_GROUPS + 2 * r + half] * LOG2E
                s = s_all[r * w:(r + 1) * w, half * 2 * w:(half + 1) * 2 * w] + bias
                m = jnp.maximum(jnp.max(s, axis=1, keepdims=True), sink)
                p = jnp.exp2(s - m)
                inv.append(1.0 / (jnp.sum(p, axis=1, keepdims=True) + jnp.exp2(sink - m)))
                halves.append(p.astype(BF16))
            probs.append(jnp.concatenate(halves, axis=1))
            inv_den.append(jnp.where(lane_o < hd, inv[0], inv[1]))
        o_all = _dot(jnp.concatenate(probs, axis=0), vv)
        for r in range(pairs):
            o_ref[:, slabs[r]] = (o_all[r * w:(r + 1) * w] * inv_den[r]).astype(o_ref.dtype)


def _sliding_window_attention(qkv, sinks):
    bsz, s_len, _ = qkv.shape
    w, hd = SWA_WINDOW, SWA_HEAD_DIM
    qw = SWA_HEADS * hd
    kw = SWA_KV_HEADS * hd
    k_blk = qw // kw
    v_blk = k_blk + 1
    prev = lambda n: jnp.maximum(n - 1, 0)
    return pl.pallas_call(
        _swa_kernel,
        grid=(bsz, s_len // w),
        in_specs=[pl.BlockSpec(memory_space=pltpu.SMEM),
                  pl.BlockSpec((None, w, qw), lambda b, n: (b, n, 0)),
                  pl.BlockSpec((None, w, kw), lambda b, n: (b, prev(n), k_blk)),
                  pl.BlockSpec((None, w, kw), lambda b, n: (b, n, k_blk)),
                  pl.BlockSpec((None, w, kw), lambda b, n: (b, prev(n), v_blk)),
                  pl.BlockSpec((None, w, kw), lambda b, n: (b, n, v_blk))],
        out_specs=pl.BlockSpec((None, w, qw), lambda b, n: (b, n, 0)),
        out_shape=jax.ShapeDtypeStruct((bsz, s_len, qw), BF16),
        compiler_params=_params("parallel", "arbitrary"),
        name="sliding_window_attention",
    )(sinks.astype(F32), qkv, qkv, qkv, qkv, qkv)


def _even_mixer(x32, x16, bsz, s_len, w_in, b_f, lam_re, lam_im, log_step, b_re, b_im, c_re, c_im,
                d_skip, w_glu, w_out, ln_g, ln_b):
    m, d = x16.shape
    fw = FOX_WIDTH
    nh = FOX_HEADS
    q_factor = LOG2E / math.sqrt(FOX_HEAD_DIM)
    w_qkv = jnp.concatenate([w_in[:, :fw] * q_factor, w_in[:, fw:3 * fw]], axis=1).astype(BF16)
    w_u = w_in[:, 3 * fw + nh:].astype(BF16)
    w_f = jnp.pad(w_in[:, 3 * fw:3 * fw + nh], ((0, 0), (0, V7X_LANES - nh))).astype(BF16)
    bias_f = jnp.pad(b_f, (0, V7X_LANES - nh)).reshape(1, V7X_LANES).astype(F32)
    x16_3d = x16.reshape(bsz, s_len, d)

    qkv = _matmul(x16, w_qkv, bm=1024, bn=3 * fw // 2, out_dtype=BF16).reshape(bsz, s_len, -1)
    c_col = _forget_gate_cumsum(x16_3d, w_f, bias_f)
    fox = _forgetting_attention(qkv, c_col).reshape(m, fw)

    mats = _s5_prepare(lam_re, lam_im, log_step, b_re, b_im, c_re, c_im, d_skip)
    u_rows = _u_projection_rows(x16_3d, w_u)
    gy_rows = _s5_scan_gelu(u_rows, mats, bsz=bsz)
    ssm = _glu_from_rows(gy_rows, w_glu.astype(BF16), bsz=bsz).reshape(m, -1)
    w_out16 = w_out.astype(BF16)
    return _matmul_residual_layernorm([(fox, w_out16[:fw]), (ssm, w_out16[fw:])], x32, ln_g, ln_b,
                                      bm=512)


def _odd_mixer(x32, x16, bsz, s_len, positions, w_in, sinks, w_out, ln_g, ln_b):
    m = x16.shape[0]
    rope_cols = (SWA_HEADS + SWA_KV_HEADS) * SWA_HEAD_DIM
    qkv = _rope_projection(x16, w_in.astype(BF16), positions.reshape(m, 1).astype(jnp.int32),
                           rope_cols=rope_cols, q_cols=SWA_HEADS * SWA_HEAD_DIM,
                           q_scale=LOG2E / math.sqrt(SWA_HEAD_DIM), bm=1024, bn=512)
    o = _sliding_window_attention(qkv.reshape(bsz, s_len, -1), sinks).reshape(m, -1)
    return _matmul_residual_layernorm([(o, w_out.astype(BF16))], x32, ln_g, ln_b, bm=512)


def kernel(x, positions, ev_w_in, ev_b_f, ev_lambda_re, ev_lambda_im, ev_log_step, ev_ssm_b_re,
           ev_ssm_b_im, ev_ssm_c_re, ev_ssm_c_im, ev_ssm_d, ev_w_glu, ev_w_out, od_w_in, od_sinks,
           od_w_out, ln_mix_g, ln_mix_b, ffn_w_up, ffn_conv_w, ffn_conv_b, ffn_w_down, ln_ffn_g,
           ln_ffn_b):
    bsz, s_len, d = x.shape
    x32 = x.reshape(bsz * s_len, d).astype(F32)
    x16 = x32.astype(BF16)
    for i in range(DEPTH):
        j = i // 2
        if i % 2 == 0:
            x32, x16 = _even_mixer(x32, x16, bsz, s_len, ev_w_in[j], ev_b_f[j], ev_lambda_re[j],
                                   ev_lambda_im[j], ev_log_step[j], ev_ssm_b_re[j], ev_ssm_b_im[j],
                                   ev_ssm_c_re[j], ev_ssm_c_im[j], ev_ssm_d[j], ev_w_glu[j],
                                   ev_w_out[j], ln_mix_g[i], ln_mix_b[i])
        else:
            x32, x16 = _odd_mixer(x32, x16, bsz, s_len, positions, od_w_in[j], od_sinks[j],
                                  od_w_out[j], ln_mix_g[i], ln_mix_b[i])
        x32, x16 = _conv_ffn(x32, x16, i, ffn_w_up, ffn_conv_w, ffn_conv_b, ffn_w_down[i],
                             ln_ffn_g[i], ln_ffn_b[i], seq_len=s_len)
    return x32.reshape(bsz, s_len, d).astype(x.dtype)
```

```python
import functools
import math

import jax
import jax.numpy as jnp
from jax import lax
from jax.experimental import pallas as pl
from jax.experimental.pallas import tpu as pltpu

F32 = jnp.float32
BF16 = jnp.bfloat16

DEPTH = 2
FOX_HEADS = 8
FOX_HEAD_DIM = 128
FOX_WIDTH = FOX_HEADS * FOX_HEAD_DIM
SSM_GROUP = 16
SSM_STATE = 64
SSM_CHUNK = 16
SSM_ROW = SSM_CHUNK * SSM_GROUP
SWA_HEADS = 32
SWA_KV_HEADS = 4
SWA_HEAD_DIM = 64
SWA_GROUPS = SWA_HEADS // SWA_KV_HEADS
SWA_WINDOW = 128
ROPE_DIM = SWA_HEAD_DIM // 4
ROPE_THETA = 500000.0
CONV_WIDTH = 3
LN_EPS = 1e-5
DEEPNORM_ALPHA = (2.0 * DEPTH) ** 0.25

V7X_LANES = 128
V7X_VMEM_LIMIT_BYTES = 56 * 1024 * 1024
GROUPS_PER_SLAB = V7X_LANES // SSM_GROUP
STAGE_PITCH = 136
MASK_VALUE = -1e30
LOG2E = math.log2(math.e)

HIGHEST = lax.Precision.HIGHEST


def _params(*semantics):
    return pltpu.CompilerParams(dimension_semantics=semantics,
                                vmem_limit_bytes=V7X_VMEM_LIMIT_BYTES)


def _dot(a, b):
    return jnp.dot(a, b, preferred_element_type=F32)


def _dot_nt(a, b):
    return lax.dot_general(a, b, (((1,), (1,)), ((), ())), preferred_element_type=F32)


def _mm_kernel(a_ref, w_ref, o_ref):
    o_ref[...] = _dot(a_ref[...], w_ref[...]).astype(o_ref.dtype)


def _matmul(a, w, *, bm, bn, out_dtype):
    m, k = a.shape
    n = w.shape[1]
    return pl.pallas_call(
        _mm_kernel,
        grid=(m // bm, n // bn),
        in_specs=[pl.BlockSpec((bm, k), lambda i, j: (i, 0)),
                  pl.BlockSpec((k, bn), lambda i, j: (0, j))],
        out_specs=pl.BlockSpec((bm, bn), lambda i, j: (i, j)),
        out_shape=jax.ShapeDtypeStruct((m, n), out_dtype),
        compiler_params=_params("parallel", "arbitrary"),
        name="matmul",
    )(a, w)


def _mm_res_ln_kernel(*refs, n_pairs, sub):
    a_refs = refs[:n_pairs]
    w_refs = refs[n_pairs:2 * n_pairs]
    x_ref, g_ref, b_ref, o32_ref, o16_ref = refs[2 * n_pairs:]
    for r in range(x_ref.shape[0] // sub):
        rows = slice(r * sub, (r + 1) * sub)
        pre = DEEPNORM_ALPHA * x_ref[rows, :]
        for a_ref, w_ref in zip(a_refs, w_refs):
            pre = pre + _dot(a_ref[rows, :], w_ref[...])
        mu = jnp.mean(pre, axis=-1, keepdims=True)
        cen = pre - mu
        var = jnp.mean(cen * cen, axis=-1, keepdims=True)
        y = cen * lax.rsqrt(var + LN_EPS) * g_ref[...] + b_ref[...]
        o32_ref[rows, :] = y
        o16_ref[rows, :] = y.astype(BF16)


def _matmul_residual_layernorm(pairs, x32, gain, bias, *, bm, sub=256):
    m, d = x32.shape
    n_pairs = len(pairs)
    a_list = [p[0] for p in pairs]
    w_list = [p[1] for p in pairs]
    row_spec = lambda width: pl.BlockSpec((bm, width), lambda i: (i, 0))
    const_spec = lambda rows: pl.BlockSpec((rows, d), lambda i: (0, 0), pipeline_mode=pl.Buffered(1))
    window_spec = lambda rows, row0: pl.BlockSpec((pl.Element(rows), pl.Element(d)), lambda i: (row0, 0),
                                                  pipeline_mode=pl.Buffered(1))
    in_specs = ([row_spec(a.shape[1]) for a in a_list]
                + [window_spec(a.shape[1], row0) for a, _, row0 in pairs]
                + [row_spec(d), const_spec(1), const_spec(1)])
    return pl.pallas_call(
        functools.partial(_mm_res_ln_kernel, n_pairs=n_pairs, sub=sub),
        grid=(m // bm,),
        in_specs=in_specs,
        out_specs=[row_spec(d), row_spec(d)],
        out_shape=[jax.ShapeDtypeStruct((m, d), F32), jax.ShapeDtypeStruct((m, d), BF16)],
        compiler_params=_params("parallel"),
        name="matmul_residual_layernorm",
    )(*a_list, *w_list, x32, gain.reshape(1, d).astype(F32), bias.reshape(1, d).astype(F32))


def _bf16_split3(v):
    hi = v.astype(BF16)
    r1 = v - hi.astype(F32)
    mid = r1.astype(BF16)
    lo = (r1 - mid.astype(F32)).astype(BF16)
    return hi, mid, lo


def _forget_gate_kernel(x_ref, wf_ref, bf_ref, c_ref, *, chunk):
    s_len = x_ref.shape[0]
    f = _dot(x_ref[...], wf_ref[...]) + bf_ref[...]
    log_f = jnp.minimum(f, 0.0) - jnp.log(1.0 + jnp.exp(-jnp.abs(f)))
    row = lax.broadcasted_iota(jnp.int32, (chunk, chunk), 0)
    col = lax.broadcasted_iota(jnp.int32, (chunk, chunk), 1)
    tri = (row >= col).astype(BF16)
    carry = jnp.zeros((1, log_f.shape[1]), F32)
    for start in range(0, s_len, chunk):
        hi, mid, lo = _bf16_split3(log_f[start:start + chunk])
        part = _dot(tri, hi) + _dot(tri, mid) + _dot(tri, lo) + carry
        c_ref[start:start + chunk, :] = part
        carry = part[chunk - 1:chunk, :]


def _forget_gate_cumsum(x16, w_f, b_f, *, chunk=256):
    bsz, s_len, d = x16.shape
    n = w_f.shape[1]
    return pl.pallas_call(
        functools.partial(_forget_gate_kernel, chunk=chunk),
        grid=(bsz,),
        in_specs=[pl.BlockSpec((None, s_len, d), lambda b: (b, 0, 0)),
                  pl.BlockSpec((d, n), lambda b: (0, 0)),
                  pl.BlockSpec((1, n), lambda b: (0, 0))],
        out_specs=pl.BlockSpec((None, s_len, n), lambda b: (b, 0, 0)),
        out_shape=jax.ShapeDtypeStruct((bsz, s_len, n), F32),
        compiler_params=_params("parallel"),
        name="forget_gate_cumsum",
    )(x16, w_f, b_f)


def _bias_lanes(c, first):
    hi, mid, lo = (piece.astype(F32) for piece in _bf16_split3(c))
    lane = lax.broadcasted_iota(jnp.int32, (c.shape[0], V7X_LANES), 1)
    ones = jnp.where(lane < 6, 1.0, 0.0)
    return jnp.where(lane == first, hi, jnp.where(lane == first + 1, mid, jnp.where(lane == first + 2, lo, ones)))


def _fox_kernel(q_ref, k_ref, v_ref, c_ref, o_ref, kaug_ref, *, tile, heads):
    qi = pl.program_id(2)
    dh = FOX_HEAD_DIM
    aug = 2 * dh
    h0 = pl.program_id(1) * heads

    def head_column(c_all, hh):
        lane = lax.broadcasted_iota(jnp.int32, c_all.shape, 1)
        return jnp.sum(jnp.where(lane == h0 + hh, c_all, 0.0), axis=1, keepdims=True) * LOG2E

    @pl.when(qi == 0)
    def _():
        c_all = c_ref[...]
        for hh in range(heads):
            kaug_ref[:, hh * aug:hh * aug + dh] = k_ref[:, hh * dh:(hh + 1) * dh]
            kaug_ref[:, hh * aug + dh:(hh + 1) * aug] = _bias_lanes(-head_column(c_all, hh), 3).astype(BF16)

    cq_all = c_ref[pl.ds(pl.multiple_of(qi * tile, tile), tile), :]
    q_aug = [jnp.concatenate([q_ref[:, hh * dh:(hh + 1) * dh],
                              _bias_lanes(head_column(cq_all, hh), 0).astype(BF16)], axis=1)
             for hh in range(heads)]

    def update(state, s, v):
        m, l, acc = state
        m_new = jnp.maximum(m, jnp.max(s, axis=1, keepdims=True))
        alpha = jnp.exp2(m - m_new)
        p = jnp.exp2(s - m_new)
        return (m_new, alpha * l + jnp.sum(p, axis=1, keepdims=True), alpha * acc + _dot(p.astype(BF16), v))

    def step(j, carry):
        start = pl.multiple_of(j * tile, tile)
        out = []
        for hh in range(heads):
            k = kaug_ref[pl.ds(start, tile), hh * aug:(hh + 1) * aug]
            v = v_ref[pl.ds(start, tile), hh * dh:(hh + 1) * dh]
            out.append(update(carry[hh], _dot_nt(q_aug[hh], k), v))
        return tuple(out)

    def diagonal(carry):
        half = tile // 2
        start = pl.multiple_of(qi * tile, tile)
        out = []
        for hh in range(heads):
            state = carry[hh]
            for part in range(2):
                rows = slice(part * half, tile)
                keys = pl.ds(start + part * half, half)
                k = kaug_ref[keys, hh * aug:(hh + 1) * aug]
                v = v_ref[keys, hh * dh:(hh + 1) * dh]
                s = _dot_nt(q_aug[hh][rows], k)
                row = lax.broadcasted_iota(jnp.int32, s.shape, 0)
                col = lax.broadcasted_iota(jnp.int32, s.shape, 1)
                s = jnp.where(row >= col, s, MASK_VALUE)
                upd = update(tuple(x[rows] for x in state), s, v)
                state = upd if part == 0 else tuple(
                    jnp.concatenate([old[:half], new], axis=0) for old, new in zip(state, upd))
            out.append(state)
        return tuple(out)

    init = tuple((jnp.full((tile, 1), MASK_VALUE, F32), jnp.zeros((tile, 1), F32),
                  jnp.zeros((tile, dh), F32)) for _ in range(heads))
    carry = diagonal(lax.fori_loop(0, qi, step, init))
    for hh in range(heads):
        _, l, acc = carry[hh]
        o_ref[:, hh * dh:(hh + 1) * dh] = (acc / l).astype(o_ref.dtype)


def _forgetting_attention(qkv, c_col, *, tile=1024, heads=2):
    bsz, s_len, _ = qkv.shape
    width = heads * FOX_HEAD_DIM
    nblk = FOX_WIDTH // width
    return pl.pallas_call(
        functools.partial(_fox_kernel, tile=tile, heads=heads),
        grid=(bsz, nblk, s_len // tile),
        in_specs=[pl.BlockSpec((None, tile, width), lambda b, h, i: (b, i, h)),
                  pl.BlockSpec((None, s_len, width), lambda b, h, i: (b, 0, nblk + h)),
                  pl.BlockSpec((None, s_len, width), lambda b, h, i: (b, 0, 2 * nblk + h)),
                  pl.BlockSpec((None, s_len, c_col.shape[2]), lambda b, h, i: (b, 0, 0))],
        out_specs=pl.BlockSpec((None, tile, width), lambda b, h, i: (b, i, h)),
        out_shape=jax.ShapeDtypeStruct((bsz, s_len, FOX_WIDTH), BF16),
        scratch_shapes=[pltpu.VMEM((s_len, 2 * width), BF16)],
        compiler_params=_params("parallel", "parallel", "arbitrary"),
        name="forgetting_attention",
    )(qkv, qkv, qkv, c_col)


def _cmul(ar, ai, br, bi):
    return ar * br - ai * bi, ar * bi + ai * br


def _cpow(ar, ai, e, n_bits):
    rr = jnp.ones_like(ar)
    ri = jnp.zeros_like(ai)
    for bit in range(n_bits):
        nr, ni = _cmul(rr, ri, ar, ai)
        take = ((e >> bit) & 1) == 1
        rr = jnp.where(take, nr, rr)
        ri = jnp.where(take, ni, ri)
        if bit + 1 < n_bits:
            ar, ai = _cmul(ar, ai, ar, ai)
    return rr, ri


def _discretise(lr, li, dt):
    mag = jnp.exp(lr * dt)
    return mag * jnp.cos(li * dt), mag * jnp.sin(li * dt)


def _s5_prep_group(lam_row_ref, lam_col_ref, dt_ref, bt_re_ref, bt_im_ref, ct_re_ref, ct_im_ref,
                   d_ref, t0_ref, ws_re_ref, ws_im_ref, wc_re_ref, wc_im_ref, a_chunk_ref):
    chunk, grp, p = SSM_CHUNK, SSM_GROUP, SSM_STATE
    n_bits = chunk.bit_length()
    dt = jnp.exp(dt_ref[...])
    lr_r, li_r = lam_row_ref[0:1, :], lam_row_ref[1:2, :]
    ar_r, ai_r = _discretise(lr_r, li_r, dt)
    den = lr_r * lr_r + li_r * li_r
    xr, xi = ar_r - 1.0, ai_r
    g_re = (xr * lr_r + xi * li_r) / den
    g_im = (xi * lr_r - xr * li_r) / den
    bbt_re, bbt_im = _cmul(g_re, g_im, bt_re_ref[...], bt_im_ref[...])
    bx_re = jnp.concatenate([bbt_re] * chunk, axis=0)
    bx_im = jnp.concatenate([bbt_im] * chunk, axis=0)
    row_i = lax.broadcasted_iota(jnp.int32, (SSM_ROW, p), 0) // grp
    pr, pi = _cpow(jnp.broadcast_to(ar_r, (SSM_ROW, p)), jnp.broadcast_to(ai_r, (SSM_ROW, p)),
                   chunk - 1 - row_i, n_bits)
    ws_re, ws_im = _cmul(pr, pi, bx_re, bx_im)
    ws_re_ref[...] = ws_re.astype(BF16)
    ws_im_ref[...] = ws_im.astype(BF16)
    ac_r, ac_i = _cpow(ar_r, ai_r, jnp.full((1, p), chunk, jnp.int32), n_bits)
    a_chunk_ref[0:1, :] = ac_r
    a_chunk_ref[1:2, :] = ac_i
    lr_c, li_c = lam_col_ref[:, 0:1], lam_col_ref[:, 1:2]
    ar_c, ai_c = _discretise(lr_c, li_c, dt)
    rep = (lax.broadcasted_iota(jnp.int32, (grp, SSM_ROW), 0)
           == lax.broadcasted_iota(jnp.int32, (grp, SSM_ROW), 1) % grp).astype(F32)
    cx_re = jnp.dot(ct_re_ref[...], rep, precision=HIGHEST, preferred_element_type=F32)
    cx_im = jnp.dot(ct_im_ref[...], rep, precision=HIGHEST, preferred_element_type=F32)
    dx = jnp.dot(d_ref[...], rep, precision=HIGHEST, preferred_element_type=F32)
    col_j = lax.broadcasted_iota(jnp.int32, (p, SSM_ROW), 1) // grp
    ab_r = jnp.broadcast_to(ar_c, (p, SSM_ROW))
    ab_i = jnp.broadcast_to(ai_c, (p, SSM_ROW))
    qr, qi = _cpow(ab_r, ab_i, col_j, n_bits)
    f_re, f_im = _cmul(qr, qi, cx_re, cx_im)
    e_re, e_im = _cmul(f_re, f_im, ab_r, ab_i)
    wc_re_ref[...] = e_re.astype(BF16)
    wc_im_ref[...] = (-e_im).astype(BF16)
    r0 = (jnp.dot(bbt_re, f_re, precision=HIGHEST, preferred_element_type=F32)
          - jnp.dot(bbt_im, f_im, precision=HIGHEST, preferred_element_type=F32))
    lane = lax.broadcasted_iota(jnp.int32, (grp, SSM_ROW), 1)
    sub = lax.broadcasted_iota(jnp.int32, (grp, SSM_ROW), 0)
    for i in range(chunk):
        blk = r0 if i == 0 else pltpu.roll(r0, i * grp, 1)
        blk = jnp.where(lane >= i * grp, blk, 0.0)
        blk = blk + jnp.where(lane == i * grp + sub, dx, 0.0)
        t0_ref[i * grp:(i + 1) * grp, :] = blk.astype(BF16)


def _s5_prep_kernel(*refs, groups):
    for g in range(groups):
        _s5_prep_group(*(ref.at[g] for ref in refs))


def _s5_prepare(lam_re, lam_im, log_step, b_re, b_im, c_re, c_im, d_skip, *, groups_per_step=8):
    g, p = lam_re.shape
    grp = SSM_GROUP
    lam_row = jnp.stack([lam_re, lam_im], axis=1).astype(F32)
    lam_col = jnp.stack([lam_re, lam_im], axis=2).astype(F32)
    dt = log_step.reshape(g, 1, 1).astype(F32)
    bt_re = jnp.swapaxes(b_re, 1, 2).astype(F32)
    bt_im = jnp.swapaxes(b_im, 1, 2).astype(F32)
    ct_re = jnp.swapaxes(c_re, 1, 2).astype(F32)
    ct_im = jnp.swapaxes(c_im, 1, 2).astype(F32)
    d3 = d_skip.reshape(g, 1, grp).astype(F32)

    gs = groups_per_step

    def spec(*shape):
        return pl.BlockSpec((gs,) + shape, lambda i: (i,) + (0,) * len(shape))

    return pl.pallas_call(
        functools.partial(_s5_prep_kernel, groups=gs),
        grid=(g // gs,),
        in_specs=[spec(2, p), spec(p, 2), spec(1, 1), spec(grp, p), spec(grp, p),
                  spec(p, grp), spec(p, grp), spec(1, grp)],
        out_specs=[spec(SSM_ROW, SSM_ROW), spec(SSM_ROW, p), spec(SSM_ROW, p),
                   spec(p, SSM_ROW), spec(p, SSM_ROW), spec(2, p)],
        out_shape=[jax.ShapeDtypeStruct((g, SSM_ROW, SSM_ROW), BF16),
                   jax.ShapeDtypeStruct((g, SSM_ROW, p), BF16),
                   jax.ShapeDtypeStruct((g, SSM_ROW, p), BF16),
                   jax.ShapeDtypeStruct((g, p, SSM_ROW), BF16),
                   jax.ShapeDtypeStruct((g, p, SSM_ROW), BF16),
                   jax.ShapeDtypeStruct((g, 2, p), F32)],
        compiler_params=_params("parallel"),
        name="s5_prepare",
    )(lam_row, lam_col, dt, bt_re, bt_im, ct_re, ct_im, d3)


def _gelu_tanh(y):
    return 0.5 * y * (1.0 + jnp.tanh(math.sqrt(2.0 / math.pi) * (y + 0.044715 * (y * y * y))))


def _s5_scan_kernel(u_ref, t0_ref, ws_re_ref, ws_im_ref, wc_re_ref, wc_im_ref, a_ref, o_ref,
                    sre_ref, sim_ref, hre_ref, him_ref, *, groups, bsz, n_chunks):
    for g in range(groups):
        u = u_ref[g]
        sre_ref[g] = _dot(u, ws_re_ref[g])
        sim_ref[g] = _dot(u, ws_im_ref[g])
    ar = a_ref[:, 0:1, :]
    ai = a_ref[:, 1:2, :]

    def step(k, carry):
        hr, hi = carry
        rows = pl.ds(pl.multiple_of(k * bsz, bsz), bsz)
        hre_ref[:, rows, :] = hr
        him_ref[:, rows, :] = hi
        sr = sre_ref[:, rows, :]
        si = sim_ref[:, rows, :]
        return ar * hr - ai * hi + sr, ar * hi + ai * hr + si

    zero = jnp.zeros((groups, bsz, SSM_STATE), F32)
    lax.fori_loop(0, n_chunks, step, (zero, zero))
    for g in range(groups):
        y = (_dot(u_ref[g], t0_ref[g]) + _dot(hre_ref[g].astype(BF16), wc_re_ref[g])
             + _dot(him_ref[g].astype(BF16), wc_im_ref[g]))
        o_ref[g] = _gelu_tanh(y).astype(o_ref.dtype)


def _s5_scan_gelu(u_rows, mats, *, bsz, groups_per_step=8):
    t0, ws_re, ws_im, wc_re, wc_im, a_chunk = mats
    g, rows, _ = u_rows.shape
    gs = groups_per_step
    p = SSM_STATE

    def spec(*shape):
        return pl.BlockSpec((gs,) + shape, lambda i: (i,) + (0,) * len(shape))

    return pl.pallas_call(
        functools.partial(_s5_scan_kernel, groups=gs, bsz=bsz, n_chunks=rows // bsz),
        grid=(g // gs,),
        in_specs=[spec(rows, SSM_ROW), spec(SSM_ROW, SSM_ROW), spec(SSM_ROW, p), spec(SSM_ROW, p),
                  spec(p, SSM_ROW), spec(p, SSM_ROW), spec(2, p)],
        out_specs=spec(rows, SSM_ROW),
        out_shape=jax.ShapeDtypeStruct((g, rows, SSM_ROW), BF16),
        scratch_shapes=[pltpu.VMEM((gs, rows, p), F32), pltpu.VMEM((gs, rows, p), F32),
                        pltpu.VMEM((gs, rows, p), F32), pltpu.VMEM((gs, rows, p), F32)],
        compiler_params=_params("parallel"),
        name="s5_scan_gelu",
    )(u_rows, t0, ws_re, ws_im, wc_re, wc_im, a_chunk)


def _sigmoid(x):
    return 1.0 / (1.0 + jnp.exp(-x))


def _block_transpose(xs):
    n = GROUPS_PER_SLAB
    width = xs[0].shape[1]
    blk = (lax.broadcasted_iota(jnp.int32, (1, width), 1) // SSM_GROUP) % n
    xs = list(xs)
    d = n // 2
    while d >= 1:
        high = (blk & d) != 0
        nxt = list(xs)
        for i in range(n):
            if i & d:
                continue
            lo_arr, hi_arr = xs[i], xs[i + d]
            nxt[i] = jnp.where(high, pltpu.roll(hi_arr, SSM_GROUP * d, 1), lo_arr)
            nxt[i + d] = jnp.where(high, hi_arr, pltpu.roll(lo_arr, width - SSM_GROUP * d, 1))
        xs = nxt
        d //= 2
    return xs


def _u_rows_kernel(x_ref, w_ref, o_ref, stage_ref, *, bsz, chunks):
    tb = x_ref.shape[1]
    u = _dot(x_ref[...].reshape(bsz * tb, x_ref.shape[2]), w_ref[...])
    n_slabs = u.shape[1] // V7X_LANES
    for v in range(n_slabs):
        for b in range(bsz):
            stage_ref[v, b * STAGE_PITCH:b * STAGE_PITCH + tb, :] = (
                u[b * tb:(b + 1) * tb, v * V7X_LANES:(v + 1) * V7X_LANES])
    n = GROUPS_PER_SLAB
    for jh in range(SSM_CHUNK // n):
        xs = []
        for jl in range(n):
            j = jh * n + jl
            xs.append(jnp.concatenate(
                [jnp.concatenate([stage_ref.at[v][pl.ds(SSM_CHUNK * c + j, bsz, stride=STAGE_PITCH), :]
                                  for v in range(n_slabs)], axis=1) for c in range(chunks)], axis=0))
        ys = _block_transpose(xs)
        for e in range(n):
            for v in range(n_slabs):
                o_ref[n * v + e, :, jh * V7X_LANES:(jh + 1) * V7X_LANES] = (
                    ys[e][:, v * V7X_LANES:(v + 1) * V7X_LANES].astype(o_ref.dtype))


def _u_projection_rows(x16, w_u, *, chunks=8):
    bsz, s_len, d = x16.shape
    width = w_u.shape[1]
    g = width // SSM_GROUP
    tb = chunks * SSM_CHUNK
    rows = chunks * bsz
    return pl.pallas_call(
        functools.partial(_u_rows_kernel, bsz=bsz, chunks=chunks),
        grid=(s_len // tb,),
        in_specs=[pl.BlockSpec((bsz, tb, d), lambda i: (0, i, 0)),
                  pl.BlockSpec((d, width), lambda i: (0, 0), pipeline_mode=pl.Buffered(1))],
        out_specs=pl.BlockSpec((g, rows, SSM_ROW), lambda i: (0, i, 0)),
        out_shape=jax.ShapeDtypeStruct((g, (s_len // SSM_CHUNK) * bsz, SSM_ROW), BF16),
        scratch_shapes=[pltpu.VMEM((width // V7X_LANES, bsz * STAGE_PITCH, V7X_LANES), F32)],
        compiler_params=_params("parallel"),
        name="u_projection_rows",
    )(x16, w_u)


def _glu_rows_kernel(gy_ref, w_ref, o_ref, stage_ref, *, bsz, chunks):
    tb = o_ref.shape[1]
    n = GROUPS_PER_SLAB
    n_slabs = gy_ref.shape[0] // n
    for jh in range(SSM_CHUNK // n):
        ys = [jnp.concatenate([gy_ref[n * v + e, :, jh * V7X_LANES:(jh + 1) * V7X_LANES].astype(F32)
                               for v in range(n_slabs)], axis=1) for e in range(n)]
        xs = _block_transpose(ys)
        for jl in range(n):
            j = jh * n + jl
            for c in range(chunks):
                for v in range(n_slabs):
                    stage_ref.at[v][pl.ds(SSM_CHUNK * c + j, bsz, stride=STAGE_PITCH), :] = (
                        xs[jl][c * bsz:(c + 1) * bsz, v * V7X_LANES:(v + 1) * V7X_LANES])
    a = jnp.concatenate(
        [jnp.concatenate([stage_ref[v, b * STAGE_PITCH:b * STAGE_PITCH + tb, :] for v in range(n_slabs)], axis=1)
         for b in range(bsz)], axis=0).astype(BF16)
    z = _dot(a, w_ref[...])
    half = z.shape[1] // 2
    o_ref[...] = (z[:, :half] * _sigmoid(z[:, half:])).astype(o_ref.dtype).reshape(o_ref.shape)


def _glu_from_rows(gy_rows, w_glu, *, bsz, chunks=8):
    g, total_rows, _ = gy_rows.shape
    width = g * SSM_GROUP
    n_out = w_glu.shape[1] // 2
    s_len = total_rows // bsz * SSM_CHUNK
    tb = chunks * SSM_CHUNK
    rows = chunks * bsz
    return pl.pallas_call(
        functools.partial(_glu_rows_kernel, bsz=bsz, chunks=chunks),
        grid=(s_len // tb,),
        in_specs=[pl.BlockSpec((g, rows, SSM_ROW), lambda i: (0, i, 0)),
                  pl.BlockSpec((width, 2 * n_out), lambda i: (0, 0), pipeline_mode=pl.Buffered(1))],
        out_specs=pl.BlockSpec((bsz, tb, n_out), lambda i: (0, i, 0)),
        out_shape=jax.ShapeDtypeStruct((bsz, s_len, n_out), BF16),
        scratch_shapes=[pltpu.VMEM((width // V7X_LANES, bsz * STAGE_PITCH, V7X_LANES), F32)],
        compiler_params=_params("parallel"),
        name="glu_from_rows",
    )(gy_rows, w_glu)


def _ffn_up_kernel(x_ref, wg_ref, wv_ref, cwg_ref, cbg_ref, cwv_ref, cbv_ref, o_ref,
                   wg16_ref, wv16_ref, hg_ref, hv_ref, *, blocks_per_seq):
    i = pl.program_id(1)
    bm, bn = o_ref.shape
    pad = hg_ref.shape[0] - bm

    @pl.when(i == 0)
    def _():
        wg16_ref[...] = wg_ref[0].astype(BF16)
        wv16_ref[...] = wv_ref[0].astype(BF16)

    @pl.when(i % blocks_per_seq == 0)
    def _():
        hg_ref[0:pad, :] = jnp.zeros((pad, bn), F32)
        hv_ref[0:pad, :] = jnp.zeros((pad, bn), F32)

    @pl.when(i % blocks_per_seq != 0)
    def _():
        hg_ref[0:pad, :] = hg_ref[bm:bm + pad, :]
        hv_ref[0:pad, :] = hv_ref[bm:bm + pad, :]

    x = x_ref[...]
    conv = []
    for w16, h_ref, cw_ref, cb_ref in ((wg16_ref, hg_ref, cwg_ref, cbg_ref), (wv16_ref, hv_ref, cwv_ref, cbv_ref)):
        h = _dot(x, w16[...])
        h_ref[pad:pad + bm, :] = h
        cw = cw_ref[0]
        conv.append(cb_ref[0] + cw[0:1, :] * h_ref[pad - 2:pad - 2 + bm, :]
                    + cw[1:2, :] * h_ref[pad - 1:pad - 1 + bm, :] + cw[2:3, :] * h)
    gate, val = conv
    o_ref[...] = (gate * _sigmoid(gate) * val).astype(o_ref.dtype)


def _ffn_up(x16, layer, w_up, conv_w, conv_b, *, col0, n_cols, seq_len, bm, bn):
    m, d = x16.shape
    f = w_up.shape[2] // 2
    pad = 8
    cb = conv_b.reshape(conv_b.shape[0], 1, 2 * f)
    gate_col = lambda j: pl.multiple_of(col0 + j * bn, V7X_LANES)
    val_col = lambda j: pl.multiple_of(f + col0 + j * bn, V7X_LANES)
    window = lambda rows: (pl.Element(1), pl.Element(rows), pl.Element(bn))
    gate_spec = lambda rows: pl.BlockSpec(window(rows), lambda j, i: (layer, 0, gate_col(j)))
    val_spec = lambda rows: pl.BlockSpec(window(rows), lambda j, i: (layer, 0, val_col(j)))
    return pl.pallas_call(
        functools.partial(_ffn_up_kernel, blocks_per_seq=seq_len // bm),
        grid=(n_cols // bn, m // bm),
        in_specs=[pl.BlockSpec((bm, d), lambda j, i: (i, 0)),
                  gate_spec(d), val_spec(d), gate_spec(CONV_WIDTH), gate_spec(1),
                  val_spec(CONV_WIDTH), val_spec(1)],
        out_specs=pl.BlockSpec((bm, bn), lambda j, i: (i, j)),
        out_shape=jax.ShapeDtypeStruct((m, n_cols), BF16),
        scratch_shapes=[pltpu.VMEM((d, bn), BF16), pltpu.VMEM((d, bn), BF16),
                        pltpu.VMEM((pad + bm, bn), F32), pltpu.VMEM((pad + bm, bn), F32)],
        compiler_params=_params("arbitrary", "arbitrary"),
        name="ffn_up_conv_gate",
    )(x16, w_up, w_up, conv_w, cb, conv_w, cb)


def _cast_kernel(src_ref, dst_ref):
    dst_ref[...] = src_ref[...].astype(dst_ref.dtype)


def _layer_to_bf16(stacked, layer, *, steps=8):
    _, k, d = stacked.shape
    rows = k // steps
    return pl.pallas_call(
        _cast_kernel,
        grid=(steps,),
        in_specs=[pl.BlockSpec((None, rows, d), lambda r: (layer, r, 0))],
        out_specs=pl.BlockSpec((rows, d), lambda r: (r, 0)),
        out_shape=jax.ShapeDtypeStruct((k, d), BF16),
        compiler_params=_params("parallel"),
        name="layer_to_bf16",
    )(stacked)


def _conv_ffn(x32, x16, layer, w_up, conv_w, conv_b, w_down, ln_g, ln_b, *, seq_len, bn=512):
    f = w_down.shape[1]
    w_up, conv_w, conv_b = w_up.astype(F32), conv_w.astype(F32), conv_b.astype(F32)
    w_down16 = _layer_to_bf16(w_down, layer)
    main = (f // bn) * bn
    pairs = []
    for col0, n_cols, width in ((0, main, bn), (main, f - main, f - main)):
        if n_cols:
            act = _ffn_up(x16, layer, w_up, conv_w, conv_b, col0=col0, n_cols=n_cols, seq_len=seq_len,
                          bm=1024, bn=width)
            pairs.append((act, w_down16, col0))
    return _matmul_residual_layernorm(pairs, x32, ln_g, ln_b, bm=256)


def _rope_proj_kernel(a_ref, w_ref, pos_ref, o_ref, cos_ref, sin_ref, *, rope_cols, q_tiles, q_scale):
    j = pl.program_id(1)
    hd, half = SWA_HEAD_DIM, ROPE_DIM // 2

    @pl.when(j == 0)
    def _():
        lane = lax.broadcasted_iota(jnp.int32, (1, V7X_LANES), 1) % hd
        idx = (lane % half).astype(F32)
        inv_freq = jnp.where(lane < ROPE_DIM, jnp.exp(idx * (-math.log(ROPE_THETA) / half)), 0.0)
        ang = pos_ref[...].astype(F32) * inv_freq
        sign = jnp.where(lane < half, -1.0, 1.0)
        cos_ref[...] = jnp.cos(ang)
        sin_ref[...] = jnp.sin(ang) * sign

    acc = _dot(a_ref[...], w_ref[...])
    bm, bn = acc.shape

    reps = bn // V7X_LANES
    cos = jnp.concatenate([cos_ref[...]] * reps, axis=1)
    sin = jnp.concatenate([sin_ref[...]] * reps, axis=1)
    lane = lax.broadcasted_iota(jnp.int32, (1, bn), 1)
    first = (lane & (hd - 1)) < half
    partner = jnp.where(first, pltpu.roll(acc, bn - half, 1), pltpu.roll(acc, half, 1))
    rot = acc * cos + partner * sin
    col = lane + j * bn
    out = jnp.where(col < rope_cols, rot, acc) * jnp.where(col < q_tiles * bn, q_scale, 1.0)
    o_ref[...] = out.astype(o_ref.dtype)


def _rope_projection(x16, w, pos_col, *, rope_cols, q_cols, q_scale, bm, bn):
    m, k = x16.shape
    n = w.shape[1]
    return pl.pallas_call(
        functools.partial(_rope_proj_kernel, rope_cols=rope_cols, q_tiles=q_cols // bn,
                          q_scale=q_scale),
        grid=(m // bm, n // bn),
        in_specs=[pl.BlockSpec((bm, k), lambda i, j: (i, 0)),
                  pl.BlockSpec((k, bn), lambda i, j: (0, j)),
                  pl.BlockSpec((bm, 1), lambda i, j: (i, 0))],
        out_specs=pl.BlockSpec((bm, bn), lambda i, j: (i, j)),
        out_shape=jax.ShapeDtypeStruct((m, n), BF16),
        scratch_shapes=[pltpu.VMEM((bm, V7X_LANES), F32), pltpu.VMEM((bm, V7X_LANES), F32)],
        compiler_params=_params("parallel", "arbitrary"),
        name="rope_projection",
    )(x16, w, pos_col)


def _swa_kernel(sink_ref, q_ref, kp_ref, kc_ref, vp_ref, vc_ref, o_ref):
    n = pl.program_id(1)
    w, hd, lanes = SWA_WINDOW, SWA_HEAD_DIM, V7X_LANES
    pairs = SWA_GROUPS // 2
    qi = lax.broadcasted_iota(jnp.int32, (w, 2 * w), 0)
    kj = lax.broadcasted_iota(jnp.int32, (w, 2 * w), 1)
    rel = w + qi - kj
    valid = (rel >= 0) & (rel < w) & ((n > 0) | (kj >= w))
    bias = jnp.where(valid, 0.0, MASK_VALUE)
    lane_kv = lax.broadcasted_iota(jnp.int32, (2 * w, lanes), 1)
    lane_o = lax.broadcasted_iota(jnp.int32, (w, lanes), 1)

    def both_halves(prev_ref, cur_ref, hk):
        slab = slice((hk // 2) * lanes, (hk // 2 + 1) * lanes)
        x = jnp.concatenate([prev_ref[:, slab], cur_ref[:, slab]], axis=0).astype(F32)
        upper = hk % 2 == 1
        own = jnp.where((lane_kv >= hd) if upper else (lane_kv < hd), x, 0.0)
        other = pltpu.roll(own, hd, 1)
        lo, hi = (other, own) if upper else (own, other)
        return jnp.concatenate([lo, hi], axis=0).astype(BF16)

    for hk in range(SWA_KV_HEADS):
        kk = both_halves(kp_ref, kc_ref, hk)
        vv = both_halves(vp_ref, vc_ref, hk)
        slabs = [slice((hk * pairs + r) * lanes, (hk * pairs + r + 1) * lanes) for r in range(pairs)]
        qs = jnp.concatenate([q_ref[:, sl] for sl in slabs], axis=0)
        s_all = _dot_nt(qs, kk)
        probs, inv_den = [], []
        for r in range(pairs):
            halves, inv = [], []
            for half in range(2):
                sink = sink_ref[hk * SWA_GROUPS + 2 * r + half] * LOG2E
                s = s_all[r * w:(r + 1) * w, half * 2 * w:(half + 1) * 2 * w] + bias
                m = jnp.maximum(jnp.max(s, axis=1, keepdims=True), sink)
                p = jnp.exp2(s - m)
                inv.append(1.0 / (jnp.sum(p, axis=1, keepdims=True) + jnp.exp2(sink - m)))
                halves.append(p.astype(BF16))
            probs.append(jnp.concatenate(halves, axis=1))
            inv_den.append(jnp.where(lane_o < hd, inv[0], inv[1]))
        o_all = _dot(jnp.concatenate(probs, axis=0), vv)
        for r in range(pairs):
            o_ref[:, slabs[r]] = (o_all[r * w:(r + 1) * w] * inv_den[r]).astype(o_ref.dtype)


def _sliding_window_attention(qkv, sinks):
    bsz, s_len, _ = qkv.shape
    w, hd = SWA_WINDOW, SWA_HEAD_DIM
    qw = SWA_HEADS * hd
    kw = SWA_KV_HEADS * hd
    k_blk = qw // kw
    v_blk = k_blk + 1
    prev = lambda n: jnp.maximum(n - 1, 0)
    return pl.pallas_call(
        _swa_kernel,
        grid=(bsz, s_len // w),
        in_specs=[pl.BlockSpec(memory_space=pltpu.SMEM),
                  pl.BlockSpec((None, w, qw), lambda b, n: (b, n, 0)),
                  pl.BlockSpec((None, w, kw), lambda b, n: (b, prev(n), k_blk)),
                  pl.BlockSpec((None, w, kw), lambda b, n: (b, n, k_blk)),
                  pl.BlockSpec((None, w, kw), lambda b, n: (b, prev(n), v_blk)),
                  pl.BlockSpec((None, w, kw), lambda b, n: (b, n, v_blk))],
        out_specs=pl.BlockSpec((None, w, qw), lambda b, n: (b, n, 0)),
        out_shape=jax.ShapeDtypeStruct((bsz, s_len, qw), BF16),
        compiler_params=_params("parallel", "arbitrary"),
        name="sliding_window_attention",
    )(sinks.astype(F32), qkv, qkv, qkv, qkv, qkv)


def _even_mixer(x32, x16, bsz, s_len, w_in, b_f, lam_re, lam_im, log_step, b_re, b_im, c_re, c_im,
                d_skip, w_glu, w_out, ln_g, ln_b):
    m, d = x16.shape
    fw = FOX_WIDTH
    nh = FOX_HEADS
    q_factor = LOG2E / math.sqrt(FOX_HEAD_DIM)
    w_qkv = jnp.concatenate([w_in[:, :fw] * q_factor, w_in[:, fw:3 * fw]], axis=1).astype(BF16)
    w_u = w_in[:, 3 * fw + nh:].astype(BF16)
    w_f = jnp.pad(w_in[:, 3 * fw:3 * fw + nh], ((0, 0), (0, V7X_LANES - nh))).astype(BF16)
    bias_f = jnp.pad(b_f, (0, V7X_LANES - nh)).reshape(1, V7X_LANES).astype(F32)
    x16_3d = x16.reshape(bsz, s_len, d)

    qkv = _matmul(x16, w_qkv, bm=1024, bn=3 * fw // 2, out_dtype=BF16).reshape(bsz, s_len, -1)
    c_col = _forget_gate_cumsum(x16_3d, w_f, bias_f)
    fox = _forgetting_attention(qkv, c_col).reshape(m, fw)

    mats = _s5_prepare(lam_re, lam_im, log_step, b_re, b_im, c_re, c_im, d_skip)
    u_rows = _u_projection_rows(x16_3d, w_u)
    gy_rows = _s5_scan_gelu(u_rows, mats, bsz=bsz)
    ssm = _glu_from_rows(gy_rows, w_glu.astype(BF16), bsz=bsz).reshape(m, -1)
    w_out16 = w_out.astype(BF16)
    return _matmul_residual_layernorm([(fox, w_out16, 0), (ssm, w_out16, fw)], x32, ln_g, ln_b, bm=512)


def _odd_mixer(x32, x16, bsz, s_len, positions, w_in, sinks, w_out, ln_g, ln_b):
    m = x16.shape[0]
    rope_cols = (SWA_HEADS + SWA_KV_HEADS) * SWA_HEAD_DIM
    qkv = _rope_projection(x16, w_in.astype(BF16), positions.reshape(m, 1).astype(jnp.int32),
                           rope_cols=rope_cols, q_cols=SWA_HEADS * SWA_HEAD_DIM,
                           q_scale=LOG2E / math.sqrt(SWA_HEAD_DIM), bm=1024, bn=512)
    o = _sliding_window_attention(qkv.reshape(bsz, s_len, -1), sinks).reshape(m, -1)
    return _matmul_residual_layernorm([(o, w_out.astype(BF16), 0)], x32, ln_g, ln_b, bm=512)


def kernel(x, positions, ev_w_in, ev_b_f, ev_lambda_re, ev_lambda_im, ev_log_step, ev_ssm_b_re,
           ev_ssm_b_im, ev_ssm_c_re, ev_ssm_c_im, ev_ssm_d, ev_w_glu, ev_w_out, od_w_in, od_sinks,
           od_w_out, ln_mix_g, ln_mix_b, ffn_w_up, ffn_conv_w, ffn_conv_b, ffn_w_down, ln_ffn_g,
           ln_ffn_b):
    bsz, s_len, d = x.shape
    x32 = x.reshape(bsz * s_len, d).astype(F32)
    x16 = x32.astype(BF16)
    for i in range(DEPTH):
        j = i // 2
        if i % 2 == 0:
            x32, x16 = _even_mixer(x32, x16, bsz, s_len, ev_w_in[j], ev_b_f[j], ev_lambda_re[j],
                                   ev_lambda_im[j], ev_log_step[j], ev_ssm_b_re[j], ev_ssm_b_im[j],
                                   ev_ssm_c_re[j], ev_ssm_c_im[j], ev_ssm_d[j], ev_w_glu[j],
                                   ev_w_out[j], ln_mix_g[i], ln_mix_b[i])
        else:
            x32, x16 = _odd_mixer(x32, x16, bsz, s_len, positions, od_w_in[j], od_sinks[j],
                                  od_w_out[j], ln_mix_g[i], ln_mix_b[i])
        x32, x16 = _conv_ffn(x32, x16, i, ffn_w_up, ffn_conv_w, ffn_conv_b, ffn_w_down,
                             ln_ffn_g[i], ln_ffn_b[i], seq_len=s_len)
    return x32.reshape(bsz, s_len, d).astype(x.dtype)
```

```python
import functools
import math

import jax
import jax.numpy as jnp
from jax import lax
from jax.experimental import pallas as pl
from jax.experimental.pallas import tpu as pltpu

F32 = jnp.float32
BF16 = jnp.bfloat16

DEPTH = 2
FOX_HEADS = 8
FOX_HEAD_DIM = 128
FOX_WIDTH = FOX_HEADS * FOX_HEAD_DIM
SSM_GROUP = 16
SSM_STATE = 64
SSM_CHUNK = 16
SSM_ROW = SSM_CHUNK * SSM_GROUP
SWA_HEADS = 32
SWA_KV_HEADS = 4
SWA_HEAD_DIM = 64
SWA_GROUPS = SWA_HEADS // SWA_KV_HEADS
SWA_WINDOW = 128
ROPE_DIM = SWA_HEAD_DIM // 4
ROPE_THETA = 500000.0
CONV_WIDTH = 3
LN_EPS = 1e-5
DEEPNORM_ALPHA = (2.0 * DEPTH) ** 0.25

V7X_LANES = 128
V7X_VMEM_LIMIT_BYTES = 56 * 1024 * 1024
GROUPS_PER_SLAB = V7X_LANES // SSM_GROUP
STAGE_PITCH = 136
MASK_VALUE = -1e30
LOG2E = math.log2(math.e)

HIGHEST = lax.Precision.HIGHEST


def _params(*semantics):
    return pltpu.CompilerParams(dimension_semantics=semantics,
                                vmem_limit_bytes=V7X_VMEM_LIMIT_BYTES)


def _dot(a, b):
    return jnp.dot(a, b, preferred_element_type=F32)


def _dot_nt(a, b):
    return lax.dot_general(a, b, (((1,), (1,)), ((), ())), preferred_element_type=F32)


def _mm_kernel(a_ref, w_ref, o_ref):
    o_ref[...] = _dot(a_ref[...], w_ref[...]).astype(o_ref.dtype)


def _matmul(a, w, *, bm, bn, out_dtype):
    bsz, s_len, k = a.shape
    n = w.shape[1]
    per_seq = s_len // bm
    rows = lambda i, j: (i // per_seq, i % per_seq)
    return pl.pallas_call(
        _mm_kernel,
        grid=(bsz * per_seq, n // bn),
        in_specs=[pl.BlockSpec((None, bm, k), lambda i, j: rows(i, j) + (0,)),
                  pl.BlockSpec((k, bn), lambda i, j: (0, j))],
        out_specs=pl.BlockSpec((None, bm, bn), lambda i, j: rows(i, j) + (j,)),
        out_shape=jax.ShapeDtypeStruct((bsz, s_len, n), out_dtype),
        compiler_params=_params("parallel", "arbitrary"),
        name="matmul",
    )(a, w)


def _mm_res_ln_kernel(*refs, n_pairs, sub):
    a_refs = refs[:n_pairs]
    w_refs = refs[n_pairs:2 * n_pairs]
    x_ref, g_ref, b_ref, o32_ref, o16_ref = refs[2 * n_pairs:]
    for r in range(x_ref.shape[0] // sub):
        rows = slice(r * sub, (r + 1) * sub)
        pre = DEEPNORM_ALPHA * x_ref[rows, :]
        for a_ref, w_ref in zip(a_refs, w_refs):
            pre = pre + _dot(a_ref[rows, :], w_ref[...])
        mu = jnp.mean(pre, axis=-1, keepdims=True)
        cen = pre - mu
        var = jnp.mean(cen * cen, axis=-1, keepdims=True)
        y = cen * lax.rsqrt(var + LN_EPS) * g_ref[...] + b_ref[...]
        o32_ref[rows, :] = y
        o16_ref[rows, :] = y.astype(BF16)


def _matmul_residual_layernorm(pairs, x32, gain, bias, *, bm, sub=256):
    m, d = x32.shape
    n_pairs = len(pairs)
    a_list = [p[0] for p in pairs]
    w_list = [p[1] for p in pairs]
    row_spec = lambda width: pl.BlockSpec((bm, width), lambda i: (i, 0))
    const_spec = lambda rows: pl.BlockSpec((rows, d), lambda i: (0, 0), pipeline_mode=pl.Buffered(1))
    window_spec = lambda rows, row0: pl.BlockSpec((pl.Element(rows), pl.Element(d)), lambda i: (row0, 0),
                                                  pipeline_mode=pl.Buffered(1))
    in_specs = ([row_spec(a.shape[1]) for a in a_list]
                + [window_spec(a.shape[1], row0) for a, _, row0 in pairs]
                + [row_spec(d), const_spec(1), const_spec(1)])
    return pl.pallas_call(
        functools.partial(_mm_res_ln_kernel, n_pairs=n_pairs, sub=sub),
        grid=(m // bm,),
        in_specs=in_specs,
        out_specs=[row_spec(d), row_spec(d)],
        out_shape=[jax.ShapeDtypeStruct((m, d), F32), jax.ShapeDtypeStruct((m, d), BF16)],
        compiler_params=_params("parallel"),
        name="matmul_residual_layernorm",
    )(*a_list, *w_list, x32, gain.reshape(1, d).astype(F32), bias.reshape(1, d).astype(F32))


def _bf16_split3(v):
    hi = v.astype(BF16)
    r1 = v - hi.astype(F32)
    mid = r1.astype(BF16)
    lo = (r1 - mid.astype(F32)).astype(BF16)
    return hi, mid, lo


def _forget_gate_kernel(x_ref, wf_ref, bf_ref, c_ref, *, chunk):
    s_len = x_ref.shape[0]
    f = _dot(x_ref[...], wf_ref[...]) + bf_ref[...]
    log_f = jnp.minimum(f, 0.0) - jnp.log(1.0 + jnp.exp(-jnp.abs(f)))
    row = lax.broadcasted_iota(jnp.int32, (chunk, chunk), 0)
    col = lax.broadcasted_iota(jnp.int32, (chunk, chunk), 1)
    tri = (row >= col).astype(BF16)
    carry = jnp.zeros((1, log_f.shape[1]), F32)
    for start in range(0, s_len, chunk):
        hi, mid, lo = _bf16_split3(log_f[start:start + chunk])
        part = _dot(tri, hi) + _dot(tri, mid) + _dot(tri, lo) + carry
        c_ref[start:start + chunk, :] = part
        carry = part[chunk - 1:chunk, :]


def _forget_gate_cumsum(x16, w_f, b_f, *, chunk=256):
    bsz, s_len, d = x16.shape
    n = w_f.shape[1]
    return pl.pallas_call(
        functools.partial(_forget_gate_kernel, chunk=chunk),
        grid=(bsz,),
        in_specs=[pl.BlockSpec((None, s_len, d), lambda b: (b, 0, 0)),
                  pl.BlockSpec((d, n), lambda b: (0, 0)),
                  pl.BlockSpec((1, n), lambda b: (0, 0))],
        out_specs=pl.BlockSpec((None, s_len, n), lambda b: (b, 0, 0)),
        out_shape=jax.ShapeDtypeStruct((bsz, s_len, n), F32),
        compiler_params=_params("parallel"),
        name="forget_gate_cumsum",
    )(x16, w_f, b_f)


def _bias_lanes(c, first):
    hi, mid, lo = (piece.astype(F32) for piece in _bf16_split3(c))
    lane = lax.broadcasted_iota(jnp.int32, (c.shape[0], V7X_LANES), 1)
    ones = jnp.where(lane < 6, 1.0, 0.0)
    return jnp.where(lane == first, hi, jnp.where(lane == first + 1, mid, jnp.where(lane == first + 2, lo, ones)))


def _fox_kernel(q_ref, k_ref, v_ref, c_ref, o_ref, kaug_ref, *, tile, heads):
    qi = pl.program_id(2)
    dh = FOX_HEAD_DIM
    aug = 2 * dh
    h0 = pl.program_id(1) * heads

    def head_column(c_all, hh):
        lane = lax.broadcasted_iota(jnp.int32, c_all.shape, 1)
        return jnp.sum(jnp.where(lane == h0 + hh, c_all, 0.0), axis=1, keepdims=True) * LOG2E

    @pl.when(qi == 0)
    def _():
        c_all = c_ref[...]
        for hh in range(heads):
            kaug_ref[:, hh * aug:hh * aug + dh] = k_ref[:, hh * dh:(hh + 1) * dh]
            kaug_ref[:, hh * aug + dh:(hh + 1) * aug] = _bias_lanes(-head_column(c_all, hh), 3).astype(BF16)

    cq_all = c_ref[pl.ds(pl.multiple_of(qi * tile, tile), tile), :]
    q_aug = [jnp.concatenate([q_ref[:, hh * dh:(hh + 1) * dh],
                              _bias_lanes(head_column(cq_all, hh), 0).astype(BF16)], axis=1)
             for hh in range(heads)]

    def update(state, s, v):
        m, l, acc = state
        m_new = jnp.maximum(m, jnp.max(s, axis=1, keepdims=True))
        alpha = jnp.exp2(m - m_new)
        p = jnp.exp2(s - m_new)
        return (m_new, alpha * l + jnp.sum(p, axis=1, keepdims=True), alpha * acc + _dot(p.astype(BF16), v))

    def step(j, carry, masked):
        start = pl.multiple_of(j * tile, tile)
        out = []
        for hh in range(heads):
            k = kaug_ref[pl.ds(start, tile), hh * aug:(hh + 1) * aug]
            v = v_ref[pl.ds(start, tile), hh * dh:(hh + 1) * dh]
            s = _dot_nt(q_aug[hh], k)
            if masked:
                row = lax.broadcasted_iota(jnp.int32, s.shape, 0)
                col = lax.broadcasted_iota(jnp.int32, s.shape, 1)
                s = jnp.where(row >= col, s, MASK_VALUE)
            out.append(update(carry[hh], s, v))
        return tuple(out)

    init = tuple((jnp.full((tile, 1), MASK_VALUE, F32), jnp.zeros((tile, 1), F32),
                  jnp.zeros((tile, dh), F32)) for _ in range(heads))
    carry = lax.fori_loop(0, qi, functools.partial(step, masked=False), init)
    carry = step(qi, carry, masked=True)
    for hh in range(heads):
        _, l, acc = carry[hh]
        o_ref[:, hh * dh:(hh + 1) * dh] = (acc / l).astype(o_ref.dtype)


def _forgetting_attention(qkv, c_col, *, tile=1024, heads=2):
    bsz, s_len, _ = qkv.shape
    width = heads * FOX_HEAD_DIM
    nblk = FOX_WIDTH // width
    return pl.pallas_call(
        functools.partial(_fox_kernel, tile=tile, heads=heads),
        grid=(bsz, nblk, s_len // tile),
        in_specs=[pl.BlockSpec((None, tile, width), lambda b, h, i: (b, i, h)),
                  pl.BlockSpec((None, s_len, width), lambda b, h, i: (b, 0, nblk + h)),
                  pl.BlockSpec((None, s_len, width), lambda b, h, i: (b, 0, 2 * nblk + h)),
                  pl.BlockSpec((None, s_len, c_col.shape[2]), lambda b, h, i: (b, 0, 0))],
        out_specs=pl.BlockSpec((None, tile, width), lambda b, h, i: (b, i, h)),
        out_shape=jax.ShapeDtypeStruct((bsz, s_len, FOX_WIDTH), BF16),
        scratch_shapes=[pltpu.VMEM((s_len, 2 * width), BF16)],
        compiler_params=_params("parallel", "parallel", "arbitrary"),
        name="forgetting_attention",
    )(qkv, qkv, qkv, c_col)


def _cmul(ar, ai, br, bi):
    return ar * br - ai * bi, ar * bi + ai * br


def _cpow(ar, ai, e, n_bits):
    rr = jnp.ones_like(ar)
    ri = jnp.zeros_like(ai)
    for bit in range(n_bits):
        nr, ni = _cmul(rr, ri, ar, ai)
        take = ((e >> bit) & 1) == 1
        rr = jnp.where(take, nr, rr)
        ri = jnp.where(take, ni, ri)
        if bit + 1 < n_bits:
            ar, ai = _cmul(ar, ai, ar, ai)
    return rr, ri


def _discretise(lr, li, dt):
    mag = jnp.exp(lr * dt)
    return mag * jnp.cos(li * dt), mag * jnp.sin(li * dt)


def _s5_prep_group(lam_row_ref, lam_col_ref, dt_ref, bt_re_ref, bt_im_ref, ct_re_ref, ct_im_ref,
                   d_ref, t0_ref, ws_re_ref, ws_im_ref, wc_re_ref, wc_im_ref, a_chunk_ref):
    chunk, grp, p = SSM_CHUNK, SSM_GROUP, SSM_STATE
    n_bits = chunk.bit_length()
    dt = jnp.exp(dt_ref[...])
    lr_r, li_r = lam_row_ref[0:1, :], lam_row_ref[1:2, :]
    ar_r, ai_r = _discretise(lr_r, li_r, dt)
    den = lr_r * lr_r + li_r * li_r
    xr, xi = ar_r - 1.0, ai_r
    g_re = (xr * lr_r + xi * li_r) / den
    g_im = (xi * lr_r - xr * li_r) / den
    bbt_re, bbt_im = _cmul(g_re, g_im, bt_re_ref[...], bt_im_ref[...])
    bx_re = jnp.concatenate([bbt_re] * chunk, axis=0)
    bx_im = jnp.concatenate([bbt_im] * chunk, axis=0)
    row_i = lax.broadcasted_iota(jnp.int32, (SSM_ROW, p), 0) // grp
    pr, pi = _cpow(jnp.broadcast_to(ar_r, (SSM_ROW, p)), jnp.broadcast_to(ai_r, (SSM_ROW, p)),
                   chunk - 1 - row_i, n_bits)
    ws_re, ws_im = _cmul(pr, pi, bx_re, bx_im)
    ws_re_ref[...] = ws_re.astype(BF16)
    ws_im_ref[...] = ws_im.astype(BF16)
    ac_r, ac_i = _cpow(ar_r, ai_r, jnp.full((1, p), chunk, jnp.int32), n_bits)
    a_chunk_ref[0:1, :] = ac_r
    a_chunk_ref[1:2, :] = ac_i
    lr_c, li_c = lam_col_ref[:, 0:1], lam_col_ref[:, 1:2]
    ar_c, ai_c = _discretise(lr_c, li_c, dt)
    rep = (lax.broadcasted_iota(jnp.int32, (grp, SSM_ROW), 0)
           == lax.broadcasted_iota(jnp.int32, (grp, SSM_ROW), 1) % grp).astype(F32)
    cx_re = jnp.dot(ct_re_ref[...], rep, precision=HIGHEST, preferred_element_type=F32)
    cx_im = jnp.dot(ct_im_ref[...], rep, precision=HIGHEST, preferred_element_type=F32)
    dx = jnp.dot(d_ref[...], rep, precision=HIGHEST, preferred_element_type=F32)
    col_j = lax.broadcasted_iota(jnp.int32, (p, SSM_ROW), 1) // grp
    ab_r = jnp.broadcast_to(ar_c, (p, SSM_ROW))
    ab_i = jnp.broadcast_to(ai_c, (p, SSM_ROW))
    qr, qi = _cpow(ab_r, ab_i, col_j, n_bits)
    f_re, f_im = _cmul(qr, qi, cx_re, cx_im)
    e_re, e_im = _cmul(f_re, f_im, ab_r, ab_i)
    wc_re_ref[...] = e_re.astype(BF16)
    wc_im_ref[...] = (-e_im).astype(BF16)
    r0 = (jnp.dot(bbt_re, f_re, precision=HIGHEST, preferred_element_type=F32)
          - jnp.dot(bbt_im, f_im, precision=HIGHEST, preferred_element_type=F32))
    lane = lax.broadcasted_iota(jnp.int32, (grp, SSM_ROW), 1)
    sub = lax.broadcasted_iota(jnp.int32, (grp, SSM_ROW), 0)
    for i in range(chunk):
        blk = r0 if i == 0 else pltpu.roll(r0, i * grp, 1)
        blk = jnp.where(lane >= i * grp, blk, 0.0)
        blk = blk + jnp.where(lane == i * grp + sub, dx, 0.0)
        t0_ref[i * grp:(i + 1) * grp, :] = blk.astype(BF16)


def _s5_prep_kernel(*refs, groups):
    for g in range(groups):
        _s5_prep_group(*(ref.at[g] for ref in refs))


def _s5_prepare(lam_re, lam_im, log_step, b_re, b_im, c_re, c_im, d_skip, *, groups_per_step=8):
    g, p = lam_re.shape
    grp = SSM_GROUP
    lam_row = jnp.stack([lam_re, lam_im], axis=1).astype(F32)
    lam_col = jnp.stack([lam_re, lam_im], axis=2).astype(F32)
    dt = log_step.reshape(g, 1, 1).astype(F32)
    bt_re = jnp.swapaxes(b_re, 1, 2).astype(F32)
    bt_im = jnp.swapaxes(b_im, 1, 2).astype(F32)
    ct_re = jnp.swapaxes(c_re, 1, 2).astype(F32)
    ct_im = jnp.swapaxes(c_im, 1, 2).astype(F32)
    d3 = d_skip.reshape(g, 1, grp).astype(F32)

    gs = groups_per_step

    def spec(*shape):
        return pl.BlockSpec((gs,) + shape, lambda i: (i,) + (0,) * len(shape))

    return pl.pallas_call(
        functools.partial(_s5_prep_kernel, groups=gs),
        grid=(g // gs,),
        in_specs=[spec(2, p), spec(p, 2), spec(1, 1), spec(grp, p), spec(grp, p),
                  spec(p, grp), spec(p, grp), spec(1, grp)],
        out_specs=[spec(SSM_ROW, SSM_ROW), spec(SSM_ROW, p), spec(SSM_ROW, p),
                   spec(p, SSM_ROW), spec(p, SSM_ROW), spec(2, p)],
        out_shape=[jax.ShapeDtypeStruct((g, SSM_ROW, SSM_ROW), BF16),
                   jax.ShapeDtypeStruct((g, SSM_ROW, p), BF16),
                   jax.ShapeDtypeStruct((g, SSM_ROW, p), BF16),
                   jax.ShapeDtypeStruct((g, p, SSM_ROW), BF16),
                   jax.ShapeDtypeStruct((g, p, SSM_ROW), BF16),
                   jax.ShapeDtypeStruct((g, 2, p), F32)],
        compiler_params=_params("parallel"),
        name="s5_prepare",
    )(lam_row, lam_col, dt, bt_re, bt_im, ct_re, ct_im, d3)


def _gelu_tanh(y):
    return 0.5 * y * (1.0 + jnp.tanh(math.sqrt(2.0 / math.pi) * (y + 0.044715 * (y * y * y))))


def _s5_scan_kernel(u_ref, t0_ref, ws_re_ref, ws_im_ref, wc_re_ref, wc_im_ref, a_ref, o_ref,
                    sre_ref, sim_ref, hre_ref, him_ref, *, groups, bsz, n_chunks):
    for g in range(groups):
        u = u_ref[g]
        sre_ref[g] = _dot(u, ws_re_ref[g])
        sim_ref[g] = _dot(u, ws_im_ref[g])
    ar = a_ref[:, 0:1, :]
    ai = a_ref[:, 1:2, :]

    def step(k, carry):
        hr, hi = carry
        rows = pl.ds(pl.multiple_of(k * bsz, bsz), bsz)
        hre_ref[:, rows, :] = hr
        him_ref[:, rows, :] = hi
        sr = sre_ref[:, rows, :]
        si = sim_ref[:, rows, :]
        return ar * hr - ai * hi + sr, ar * hi + ai * hr + si

    zero = jnp.zeros((groups, bsz, SSM_STATE), F32)
    lax.fori_loop(0, n_chunks, step, (zero, zero))
    for g in range(groups):
        y = (_dot(u_ref[g], t0_ref[g]) + _dot(hre_ref[g].astype(BF16), wc_re_ref[g])
             + _dot(him_ref[g].astype(BF16), wc_im_ref[g]))
        o_ref[g] = _gelu_tanh(y).astype(o_ref.dtype)


def _s5_scan_gelu(u_rows, mats, *, bsz, groups_per_step=8):
    t0, ws_re, ws_im, wc_re, wc_im, a_chunk = mats
    g, rows, _ = u_rows.shape
    gs = groups_per_step
    p = SSM_STATE

    def spec(*shape):
        return pl.BlockSpec((gs,) + shape, lambda i: (i,) + (0,) * len(shape))

    return pl.pallas_call(
        functools.partial(_s5_scan_kernel, groups=gs, bsz=bsz, n_chunks=rows // bsz),
        grid=(g // gs,),
        in_specs=[spec(rows, SSM_ROW), spec(SSM_ROW, SSM_ROW), spec(SSM_ROW, p), spec(SSM_ROW, p),
                  spec(p, SSM_ROW), spec(p, SSM_ROW), spec(2, p)],
        out_specs=spec(rows, SSM_ROW),
        out_shape=jax.ShapeDtypeStruct((g, rows, SSM_ROW), BF16),
        scratch_shapes=[pltpu.VMEM((gs, rows, p), F32), pltpu.VMEM((gs, rows, p), F32),
                        pltpu.VMEM((gs, rows, p), F32), pltpu.VMEM((gs, rows, p), F32)],
        compiler_params=_params("parallel"),
        name="s5_scan_gelu",
    )(u_rows, t0, ws_re, ws_im, wc_re, wc_im, a_chunk)


def _sigmoid(x):
    return 1.0 / (1.0 + jnp.exp(-x))


def _block_transpose(xs):
    n = GROUPS_PER_SLAB
    width = xs[0].shape[1]
    blk = (lax.broadcasted_iota(jnp.int32, (1, width), 1) // SSM_GROUP) % n
    xs = list(xs)
    d = n // 2
    while d >= 1:
        high = (blk & d) != 0
        nxt = list(xs)
        for i in range(n):
            if i & d:
                continue
            lo_arr, hi_arr = xs[i], xs[i + d]
            nxt[i] = jnp.where(high, pltpu.roll(hi_arr, SSM_GROUP * d, 1), lo_arr)
            nxt[i + d] = jnp.where(high, hi_arr, pltpu.roll(lo_arr, width - SSM_GROUP * d, 1))
        xs = nxt
        d //= 2
    return xs


def _u_rows_kernel(x_ref, w_ref, o_ref, stage_ref, *, bsz, chunks):
    tb = x_ref.shape[1]
    u = _dot(x_ref[...].reshape(bsz * tb, x_ref.shape[2]), w_ref[...])
    n_slabs = u.shape[1] // V7X_LANES
    for v in range(n_slabs):
        for b in range(bsz):
            stage_ref[v, b * STAGE_PITCH:b * STAGE_PITCH + tb, :] = (
                u[b * tb:(b + 1) * tb, v * V7X_LANES:(v + 1) * V7X_LANES])
    n = GROUPS_PER_SLAB
    for jh in range(SSM_CHUNK // n):
        xs = []
        for jl in range(n):
            j = jh * n + jl
            xs.append(jnp.concatenate(
                [jnp.concatenate([stage_ref.at[v][pl.ds(SSM_CHUNK * c + j, bsz, stride=STAGE_PITCH), :]
                                  for v in range(n_slabs)], axis=1) for c in range(chunks)], axis=0))
        ys = _block_transpose(xs)
        for e in range(n):
            for v in range(n_slabs):
                o_ref[n * v + e, :, jh * V7X_LANES:(jh + 1) * V7X_LANES] = (
                    ys[e][:, v * V7X_LANES:(v + 1) * V7X_LANES].astype(o_ref.dtype))


def _u_projection_rows(x16, w_u, *, chunks=8):
    bsz, s_len, d = x16.shape
    width = w_u.shape[1]
    g = width // SSM_GROUP
    tb = chunks * SSM_CHUNK
    rows = chunks * bsz
    return pl.pallas_call(
        functools.partial(_u_rows_kernel, bsz=bsz, chunks=chunks),
        grid=(s_len // tb,),
        in_specs=[pl.BlockSpec((bsz, tb, d), lambda i: (0, i, 0)),
                  pl.BlockSpec((d, width), lambda i: (0, 0), pipeline_mode=pl.Buffered(1))],
        out_specs=pl.BlockSpec((g, rows, SSM_ROW), lambda i: (0, i, 0)),
        out_shape=jax.ShapeDtypeStruct((g, (s_len // SSM_CHUNK) * bsz, SSM_ROW), BF16),
        scratch_shapes=[pltpu.VMEM((width // V7X_LANES, bsz * STAGE_PITCH, V7X_LANES), F32)],
        compiler_params=_params("parallel"),
        name="u_projection_rows",
    )(x16, w_u)


def _glu_rows_kernel(gy_ref, w_ref, o_ref, stage_ref, *, bsz, chunks):
    tb = o_ref.shape[1]
    n = GROUPS_PER_SLAB
    n_slabs = gy_ref.shape[0] // n
    for jh in range(SSM_CHUNK // n):
        ys = [jnp.concatenate([gy_ref[n * v + e, :, jh * V7X_LANES:(jh + 1) * V7X_LANES].astype(F32)
                               for v in range(n_slabs)], axis=1) for e in range(n)]
        xs = _block_transpose(ys)
        for jl in range(n):
            j = jh * n + jl
            for c in range(chunks):
                for v in range(n_slabs):
                    stage_ref.at[v][pl.ds(SSM_CHUNK * c + j, bsz, stride=STAGE_PITCH), :] = (
                        xs[jl][c * bsz:(c + 1) * bsz, v * V7X_LANES:(v + 1) * V7X_LANES])
    a = jnp.concatenate(
        [jnp.concatenate([stage_ref[v, b * STAGE_PITCH:b * STAGE_PITCH + tb, :] for v in range(n_slabs)], axis=1)
         for b in range(bsz)], axis=0).astype(BF16)
    z = _dot(a, w_ref[...])
    half = z.shape[1] // 2
    o_ref[...] = (z[:, :half] * _sigmoid(z[:, half:])).astype(o_ref.dtype).reshape(o_ref.shape)


def _glu_from_rows(gy_rows, w_glu, *, bsz, chunks=8):
    g, total_rows, _ = gy_rows.shape
    width = g * SSM_GROUP
    n_out = w_glu.shape[1] // 2
    s_len = total_rows // bsz * SSM_CHUNK
    tb = chunks * SSM_CHUNK
    rows = chunks * bsz
    return pl.pallas_call(
        functools.partial(_glu_rows_kernel, bsz=bsz, chunks=chunks),
        grid=(s_len // tb,),
        in_specs=[pl.BlockSpec((g, rows, SSM_ROW), lambda i: (0, i, 0)),
                  pl.BlockSpec((width, 2 * n_out), lambda i: (0, 0), pipeline_mode=pl.Buffered(1))],
        out_specs=pl.BlockSpec((bsz, tb, n_out), lambda i: (0, i, 0)),
        out_shape=jax.ShapeDtypeStruct((bsz, s_len, n_out), BF16),
        scratch_shapes=[pltpu.VMEM((width // V7X_LANES, bsz * STAGE_PITCH, V7X_LANES), F32)],
        compiler_params=_params("parallel"),
        name="glu_from_rows",
    )(gy_rows, w_glu)


def _ffn_up_kernel(x_ref, wg_ref, wv_ref, cwg_ref, cbg_ref, cwv_ref, cbv_ref, o_ref,
                   wg16_ref, wv16_ref, hg_ref, hv_ref, *, blocks_per_seq):
    i = pl.program_id(1)
    bm, bn = o_ref.shape
    pad = hg_ref.shape[0] - bm

    @pl.when(i == 0)
    def _():
        wg16_ref[...] = wg_ref[0].astype(BF16)
        wv16_ref[...] = wv_ref[0].astype(BF16)

    @pl.when(i % blocks_per_seq == 0)
    def _():
        hg_ref[0:pad, :] = jnp.zeros((pad, bn), F32)
        hv_ref[0:pad, :] = jnp.zeros((pad, bn), F32)

    @pl.when(i % blocks_per_seq != 0)
    def _():
        hg_ref[0:pad, :] = hg_ref[bm:bm + pad, :]
        hv_ref[0:pad, :] = hv_ref[bm:bm + pad, :]

    x = x_ref[...]
    conv = []
    for w16, h_ref, cw_ref, cb_ref in ((wg16_ref, hg_ref, cwg_ref, cbg_ref), (wv16_ref, hv_ref, cwv_ref, cbv_ref)):
        h = _dot(x, w16[...])
        h_ref[pad:pad + bm, :] = h
        cw = cw_ref[0]
        conv.append(cb_ref[0] + cw[0:1, :] * h_ref[pad - 2:pad - 2 + bm, :]
                    + cw[1:2, :] * h_ref[pad - 1:pad - 1 + bm, :] + cw[2:3, :] * h)
    gate, val = conv
    o_ref[...] = (gate * _sigmoid(gate) * val).astype(o_ref.dtype)


def _ffn_up(x16, layer, w_up, conv_w, conv_b, *, col0, n_cols, seq_len, bm, bn):
    m, d = x16.shape
    f = w_up.shape[2] // 2
    pad = 8
    cb = conv_b.reshape(conv_b.shape[0], 1, 2 * f)
    gate_col = lambda j: pl.multiple_of(col0 + j * bn, V7X_LANES)
    val_col = lambda j: pl.multiple_of(f + col0 + j * bn, V7X_LANES)
    window = lambda rows: (pl.Element(1), pl.Element(rows), pl.Element(bn))
    gate_spec = lambda rows: pl.BlockSpec(window(rows), lambda j, i: (layer, 0, gate_col(j)))
    val_spec = lambda rows: pl.BlockSpec(window(rows), lambda j, i: (layer, 0, val_col(j)))
    return pl.pallas_call(
        functools.partial(_ffn_up_kernel, blocks_per_seq=seq_len // bm),
        grid=(n_cols // bn, m // bm),
        in_specs=[pl.BlockSpec((bm, d), lambda j, i: (i, 0)),
                  gate_spec(d), val_spec(d), gate_spec(CONV_WIDTH), gate_spec(1),
                  val_spec(CONV_WIDTH), val_spec(1)],
        out_specs=pl.BlockSpec((bm, bn), lambda j, i: (i, j)),
        out_shape=jax.ShapeDtypeStruct((m, n_cols), BF16),
        scratch_shapes=[pltpu.VMEM((d, bn), BF16), pltpu.VMEM((d, bn), BF16),
                        pltpu.VMEM((pad + bm, bn), F32), pltpu.VMEM((pad + bm, bn), F32)],
        compiler_params=_params("arbitrary", "arbitrary"),
        name="ffn_up_conv_gate",
    )(x16, w_up, w_up, conv_w, cb, conv_w, cb)


def _cast_kernel(src_ref, dst_ref):
    dst_ref[...] = src_ref[...].astype(dst_ref.dtype)


def _layer_to_bf16(stacked, layer, *, steps=8):
    _, k, d = stacked.shape
    rows = k // steps
    return pl.pallas_call(
        _cast_kernel,
        grid=(steps,),
        in_specs=[pl.BlockSpec((None, rows, d), lambda r: (layer, r, 0))],
        out_specs=pl.BlockSpec((rows, d), lambda r: (r, 0)),
        out_shape=jax.ShapeDtypeStruct((k, d), BF16),
        compiler_params=_params("parallel"),
        name="layer_to_bf16",
    )(stacked)


def _conv_ffn(x32, x16, layer, w_up, conv_w, conv_b, w_down, ln_g, ln_b, *, seq_len, bn=512):
    f = w_down.shape[1]
    w_up, conv_w, conv_b = w_up.astype(F32), conv_w.astype(F32), conv_b.astype(F32)
    w_down16 = _layer_to_bf16(w_down, layer)
    main = (f // bn) * bn
    pairs = []
    for col0, n_cols, width in ((0, main, bn), (main, f - main, f - main)):
        if n_cols:
            act = _ffn_up(x16, layer, w_up, conv_w, conv_b, col0=col0, n_cols=n_cols, seq_len=seq_len,
                          bm=1024, bn=width)
            pairs.append((act, w_down16, col0))
    return _matmul_residual_layernorm(pairs, x32, ln_g, ln_b, bm=256)


def _rope_proj_kernel(a_ref, w_ref, pos_ref, o_ref, cos_ref, sin_ref, *, rope_cols, q_tiles, q_scale):
    j = pl.program_id(1)
    hd, half = SWA_HEAD_DIM, ROPE_DIM // 2

    @pl.when(j == 0)
    def _():
        lane = lax.broadcasted_iota(jnp.int32, (1, V7X_LANES), 1) % hd
        idx = (lane % half).astype(F32)
        inv_freq = jnp.where(lane < ROPE_DIM, jnp.exp(idx * (-math.log(ROPE_THETA) / half)), 0.0)
        ang = pos_ref[...].astype(F32) * inv_freq
        sign = jnp.where(lane < half, -1.0, 1.0)
        cos_ref[...] = jnp.cos(ang)
        sin_ref[...] = jnp.sin(ang) * sign

    acc = _dot(a_ref[...], w_ref[...])
    bm, bn = acc.shape

    reps = bn // V7X_LANES
    cos = jnp.concatenate([cos_ref[...]] * reps, axis=1)
    sin = jnp.concatenate([sin_ref[...]] * reps, axis=1)
    lane = lax.broadcasted_iota(jnp.int32, (1, bn), 1)
    first = (lane & (hd - 1)) < half
    partner = jnp.where(first, pltpu.roll(acc, bn - half, 1), pltpu.roll(acc, half, 1))
    rot = acc * cos + partner * sin
    col = lane + j * bn
    out = jnp.where(col < rope_cols, rot, acc) * jnp.where(col < q_tiles * bn, q_scale, 1.0)
    o_ref[...] = out.astype(o_ref.dtype)


def _rope_projection(x16, w, pos_col, *, rope_cols, q_cols, q_scale, bm, bn):
    m, k = x16.shape
    n = w.shape[1]
    return pl.pallas_call(
        functools.partial(_rope_proj_kernel, rope_cols=rope_cols, q_tiles=q_cols // bn,
                          q_scale=q_scale),
        grid=(m // bm, n // bn),
        in_specs=[pl.BlockSpec((bm, k), lambda i, j: (i, 0)),
                  pl.BlockSpec((k, bn), lambda i, j: (0, j)),
                  pl.BlockSpec((bm, 1), lambda i, j: (i, 0))],
        out_specs=pl.BlockSpec((bm, bn), lambda i, j: (i, j)),
        out_shape=jax.ShapeDtypeStruct((m, n), BF16),
        scratch_shapes=[pltpu.VMEM((bm, V7X_LANES), F32), pltpu.VMEM((bm, V7X_LANES), F32)],
        compiler_params=_params("parallel", "arbitrary"),
        name="rope_projection",
    )(x16, w, pos_col)


def _swa_kernel(sink_ref, q_ref, kp_ref, kc_ref, vp_ref, vc_ref, o_ref):
    n = pl.program_id(1)
    w, hd, lanes = SWA_WINDOW, SWA_HEAD_DIM, V7X_LANES
    pairs = SWA_GROUPS // 2
    qi = lax.broadcasted_iota(jnp.int32, (w, 2 * w), 0)
    kj = lax.broadcasted_iota(jnp.int32, (w, 2 * w), 1)
    rel = w + qi - kj
    valid = (rel >= 0) & (rel < w) & ((n > 0) | (kj >= w))
    bias = jnp.where(valid, 0.0, MASK_VALUE)
    lane_kv = lax.broadcasted_iota(jnp.int32, (2 * w, lanes), 1)
    lane_o = lax.broadcasted_iota(jnp.int32, (w, lanes), 1)

    def both_halves(prev_ref, cur_ref, hk):
        slab = slice((hk // 2) * lanes, (hk // 2 + 1) * lanes)
        x = jnp.concatenate([prev_ref[:, slab], cur_ref[:, slab]], axis=0).astype(F32)
        upper = hk % 2 == 1
        own = jnp.where((lane_kv >= hd) if upper else (lane_kv < hd), x, 0.0)
        other = pltpu.roll(own, hd, 1)
        lo, hi = (other, own) if upper else (own, other)
        return jnp.concatenate([lo, hi], axis=0).astype(BF16)

    for hk in range(SWA_KV_HEADS):
        kk = both_halves(kp_ref, kc_ref, hk)
        vv = both_halves(vp_ref, vc_ref, hk)
        slabs = [slice((hk * pairs + r) * lanes, (hk * pairs + r + 1) * lanes) for r in range(pairs)]
        qs = jnp.concatenate([q_ref[:, sl] for sl in slabs], axis=0)
        s_all = _dot_nt(qs, kk)
        probs, inv_den = [], []
        for r in range(pairs):
            halves, inv = [], []
            for half in range(2):
                sink = sink_ref[hk * SWA_GROUPS + 2 * r + half] * LOG2E
                s = s_all[r * w:(r + 1) * w, half * 2 * w:(half + 1) * 2 * w] + bias
                m = jnp.maximum(jnp.max(s, axis=1, keepdims=True), sink)
                p = jnp.exp2(s - m)
                inv.append(1.0 / (jnp.sum(p, axis=1, keepdims=True) + jnp.exp2(sink - m)))
                halves.append(p.astype(BF16))
            probs.append(jnp.concatenate(halves, axis=1))
            inv_den.append(jnp.where(lane_o < hd, inv[0], inv[1]))
        o_all = _dot(jnp.concatenate(probs, axis=0), vv)
        for r in range(pairs):
            o_ref[:, slabs[r]] = (o_all[r * w:(r + 1) * w] * inv_den[r]).astype(o_ref.dtype)


def _sliding_window_attention(qkv, sinks):
    bsz, s_len, _ = qkv.shape
    w, hd = SWA_WINDOW, SWA_HEAD_DIM
    qw = SWA_HEADS * hd
    kw = SWA_KV_HEADS * hd
    k_blk = qw // kw
    v_blk = k_blk + 1
    prev = lambda n: jnp.maximum(n - 1, 0)
    return pl.pallas_call(
        _swa_kernel,
        grid=(bsz, s_len // w),
        in_specs=[pl.BlockSpec(memory_space=pltpu.SMEM),
                  pl.BlockSpec((None, w, qw), lambda b, n: (b, n, 0)),
                  pl.BlockSpec((None, w, kw), lambda b, n: (b, prev(n), k_blk)),
                  pl.BlockSpec((None, w, kw), lambda b, n: (b, n, k_blk)),
                  pl.BlockSpec((None, w, kw), lambda b, n: (b, prev(n), v_blk)),
                  pl.BlockSpec((None, w, kw), lambda b, n: (b, n, v_blk))],
        out_specs=pl.BlockSpec((None, w, qw), lambda b, n: (b, n, 0)),
        out_shape=jax.ShapeDtypeStruct((bsz, s_len, qw), BF16),
        compiler_params=_params("parallel", "arbitrary"),
        name="sliding_window_attention",
    )(sinks.astype(F32), qkv, qkv, qkv, qkv, qkv)


def _even_mixer(x32, x16, bsz, s_len, w_in, b_f, lam_re, lam_im, log_step, b_re, b_im, c_re, c_im,
                d_skip, w_glu, w_out, ln_g, ln_b):
    m, d = x16.shape
    fw = FOX_WIDTH
    nh = FOX_HEADS
    q_factor = LOG2E / math.sqrt(FOX_HEAD_DIM)
    w_qkv = jnp.concatenate([w_in[:, :fw] * q_factor, w_in[:, fw:3 * fw]], axis=1).astype(BF16)
    w_u = w_in[:, 3 * fw + nh:].astype(BF16)
    w_f = jnp.pad(w_in[:, 3 * fw:3 * fw + nh], ((0, 0), (0, V7X_LANES - nh))).astype(BF16)
    bias_f = jnp.pad(b_f, (0, V7X_LANES - nh)).reshape(1, V7X_LANES).astype(F32)
    x16_3d = x16.reshape(bsz, s_len, d)

    qkv = _matmul(x16_3d, w_qkv, bm=1024, bn=3 * fw // 2, out_dtype=BF16)
    c_col = _forget_gate_cumsum(x16_3d, w_f, bias_f)
    fox = _forgetting_attention(qkv, c_col).reshape(m, fw)

    mats = _s5_prepare(lam_re, lam_im, log_step, b_re, b_im, c_re, c_im, d_skip)
    u_rows = _u_projection_rows(x16_3d, w_u)
    gy_rows = _s5_scan_gelu(u_rows, mats, bsz=bsz)
    ssm = _glu_from_rows(gy_rows, w_glu.astype(BF16), bsz=bsz).reshape(m, -1)
    w_out16 = w_out.astype(BF16)
    return _matmul_residual_layernorm([(fox, w_out16, 0), (ssm, w_out16, fw)], x32, ln_g, ln_b, bm=512)


def _odd_mixer(x32, x16, bsz, s_len, positions, w_in, sinks, w_out, ln_g, ln_b):
    m = x16.shape[0]
    rope_cols = (SWA_HEADS + SWA_KV_HEADS) * SWA_HEAD_DIM
    qkv = _rope_projection(x16, w_in.astype(BF16), positions.reshape(m, 1).astype(jnp.int32),
                           rope_cols=rope_cols, q_cols=SWA_HEADS * SWA_HEAD_DIM,
                           q_scale=LOG2E / math.sqrt(SWA_HEAD_DIM), bm=1024, bn=512)
    o = _sliding_window_attention(qkv.reshape(bsz, s_len, -1), sinks).reshape(m, -1)
    return _matmul_residual_layernorm([(o, w_out.astype(BF16), 0)], x32, ln_g, ln_b, bm=512)


def kernel(x, positions, ev_w_in, ev_b_f, ev_lambda_re, ev_lambda_im, ev_log_step, ev_ssm_b_re,
           ev_ssm_b_im, ev_ssm_c_re, ev_ssm_c_im, ev_ssm_d, ev_w_glu, ev_w_out, od_w_in, od_sinks,
           od_w_out, ln_mix_g, ln_mix_b, ffn_w_up, ffn_conv_w, ffn_conv_b, ffn_w_down, ln_ffn_g,
           ln_ffn_b):
    bsz, s_len, d = x.shape
    x32 = x.reshape(bsz * s_len, d).astype(F32)
    x16 = x32.astype(BF16)
    for i in range(DEPTH):
        j = i // 2
        if i % 2 == 0:
            x32, x16 = _even_mixer(x32, x16, bsz, s_len, ev_w_in[j], ev_b_f[j], ev_lambda_re[j],
                                   ev_lambda_im[j], ev_log_step[j], ev_ssm_b_re[j], ev_ssm_b_im[j],
                                   ev_ssm_c_re[j], ev_ssm_c_im[j], ev_ssm_d[j], ev_w_glu[j],
                                   ev_w_out[j], ln_mix_g[i], ln_mix_b[i])
        else:
            x32, x16 = _odd_mixer(x32, x16, bsz, s_len, positions, od_w_in[j], od_sinks[j],
                                  od_w_out[j], ln_mix_g[i], ln_mix_b[i])
        x32, x16 = _conv_ffn(x32, x16, i, ffn_w_up, ffn_conv_w, ffn_conv_b, ffn_w_down,
                             ln_ffn_g[i], ln_ffn_b[i], seq_len=s_len)
    return x32.reshape(bsz, s_len, d).astype(x.dtype)
```

```python
import functools
import math

import jax
import jax.numpy as jnp
from jax import lax
from jax.experimental import pallas as pl
from jax.experimental.pallas import tpu as pltpu

F32 = jnp.float32
BF16 = jnp.bfloat16

DEPTH = 2
FOX_HEADS = 8
FOX_HEAD_DIM = 128
FOX_WIDTH = FOX_HEADS * FOX_HEAD_DIM
SSM_GROUP = 16
SSM_STATE = 64
SSM_CHUNK = 16
SSM_ROW = SSM_CHUNK * SSM_GROUP
SWA_HEADS = 32
SWA_KV_HEADS = 4
SWA_HEAD_DIM = 64
SWA_GROUPS = SWA_HEADS // SWA_KV_HEADS
SWA_WINDOW = 128
ROPE_DIM = SWA_HEAD_DIM // 4
ROPE_THETA = 500000.0
CONV_WIDTH = 3
LN_EPS = 1e-5
DEEPNORM_ALPHA = (2.0 * DEPTH) ** 0.25

V7X_LANES = 128
V7X_VMEM_LIMIT_BYTES = 56 * 1024 * 1024
GROUPS_PER_SLAB = V7X_LANES // SSM_GROUP
STAGE_PITCH = 136
MASK_VALUE = -1e30
LOG2E = math.log2(math.e)

HIGHEST = lax.Precision.HIGHEST


def _params(*semantics):
    return pltpu.CompilerParams(dimension_semantics=semantics,
                                vmem_limit_bytes=V7X_VMEM_LIMIT_BYTES)


def _dot(a, b):
    return jnp.dot(a, b, preferred_element_type=F32)


def _dot_nt(a, b):
    return lax.dot_general(a, b, (((1,), (1,)), ((), ())), preferred_element_type=F32)


def _mm_kernel(a_ref, w_ref, o_ref):
    o_ref[...] = _dot(a_ref[...].astype(BF16), w_ref[...]).astype(o_ref.dtype)


def _matmul(a, w, *, bm, bn, out_dtype):
    bsz, s_len, k = a.shape
    n = w.shape[1]
    per_seq = s_len // bm
    rows = lambda i, j: (i // per_seq, i % per_seq)
    return pl.pallas_call(
        _mm_kernel,
        grid=(bsz * per_seq, n // bn),
        in_specs=[pl.BlockSpec((None, bm, k), lambda i, j: rows(i, j) + (0,)),
                  pl.BlockSpec((k, bn), lambda i, j: (0, j))],
        out_specs=pl.BlockSpec((None, bm, bn), lambda i, j: rows(i, j) + (j,)),
        out_shape=jax.ShapeDtypeStruct((bsz, s_len, n), out_dtype),
        compiler_params=_params("parallel", "arbitrary"),
        name="matmul",
    )(a, w)


def _mm_res_ln_kernel(*refs, n_pairs, sub):
    a_refs = refs[:n_pairs]
    w_refs = refs[n_pairs:2 * n_pairs]
    x_ref, g_ref, b_ref, o32_ref, o16_ref = refs[2 * n_pairs:]
    for r in range(x_ref.shape[0] // sub):
        rows = slice(r * sub, (r + 1) * sub)
        pre = DEEPNORM_ALPHA * x_ref[rows, :]
        for a_ref, w_ref in zip(a_refs, w_refs):
            pre = pre + _dot(a_ref[rows, :], w_ref[...])
        mu = jnp.mean(pre, axis=-1, keepdims=True)
        cen = pre - mu
        var = jnp.mean(cen * cen, axis=-1, keepdims=True)
        y = cen * lax.rsqrt(var + LN_EPS) * g_ref[...] + b_ref[...]
        o32_ref[rows, :] = y
        o16_ref[rows, :] = y.astype(BF16)


def _matmul_residual_layernorm(pairs, x32, gain, bias, *, bm, sub=256):
    m, d = x32.shape
    n_pairs = len(pairs)
    a_list = [p[0] for p in pairs]
    w_list = [p[1] for p in pairs]
    row_spec = lambda width: pl.BlockSpec((bm, width), lambda i: (i, 0))
    const_spec = lambda rows: pl.BlockSpec((rows, d), lambda i: (0, 0), pipeline_mode=pl.Buffered(1))
    window_spec = lambda rows, row0: pl.BlockSpec((pl.Element(rows), pl.Element(d)), lambda i: (row0, 0),
                                                  pipeline_mode=pl.Buffered(1))
    in_specs = ([row_spec(a.shape[1]) for a in a_list]
                + [window_spec(a.shape[1], row0) for a, _, row0 in pairs]
                + [row_spec(d), const_spec(1), const_spec(1)])
    return pl.pallas_call(
        functools.partial(_mm_res_ln_kernel, n_pairs=n_pairs, sub=sub),
        grid=(m // bm,),
        in_specs=in_specs,
        out_specs=[row_spec(d), row_spec(d)],
        out_shape=[jax.ShapeDtypeStruct((m, d), F32), jax.ShapeDtypeStruct((m, d), BF16)],
        compiler_params=_params("parallel"),
        name="matmul_residual_layernorm",
    )(*a_list, *w_list, x32, gain.reshape(1, d).astype(F32), bias.reshape(1, d).astype(F32))


def _bf16_split3(v):
    hi = v.astype(BF16)
    r1 = v - hi.astype(F32)
    mid = r1.astype(BF16)
    lo = (r1 - mid.astype(F32)).astype(BF16)
    return hi, mid, lo


def _forget_gate_kernel(f_ref, bf_ref, c_ref, *, chunk):
    s_len = f_ref.shape[0]
    f = f_ref[...] + bf_ref[...]
    log_f = jnp.minimum(f, 0.0) - jnp.log(1.0 + jnp.exp(-jnp.abs(f)))
    row = lax.broadcasted_iota(jnp.int32, (chunk, chunk), 0)
    col = lax.broadcasted_iota(jnp.int32, (chunk, chunk), 1)
    tri = (row >= col).astype(BF16)
    carry = jnp.zeros((1, log_f.shape[1]), F32)
    for start in range(0, s_len, chunk):
        hi, mid, lo = _bf16_split3(log_f[start:start + chunk])
        part = _dot(tri, hi) + _dot(tri, mid) + _dot(tri, lo) + carry
        c_ref[start:start + chunk, :] = part
        carry = part[chunk - 1:chunk, :]


def _forget_gate_cumsum(f_logits, b_f, *, chunk=256):
    bsz, s_len, n = f_logits.shape
    return pl.pallas_call(
        functools.partial(_forget_gate_kernel, chunk=chunk),
        grid=(bsz,),
        in_specs=[pl.BlockSpec((None, s_len, n), lambda b: (b, 0, 0)),
                  pl.BlockSpec((1, n), lambda b: (0, 0))],
        out_specs=pl.BlockSpec((None, s_len, n), lambda b: (b, 0, 0)),
        out_shape=jax.ShapeDtypeStruct((bsz, s_len, n), F32),
        compiler_params=_params("parallel"),
        name="forget_gate_cumsum",
    )(f_logits, b_f)


def _bias_lanes(c, first):
    hi, mid, lo = (piece.astype(F32) for piece in _bf16_split3(c))
    lane = lax.broadcasted_iota(jnp.int32, (c.shape[0], V7X_LANES), 1)
    ones = jnp.where(lane < 6, 1.0, 0.0)
    return jnp.where(lane == first, hi, jnp.where(lane == first + 1, mid, jnp.where(lane == first + 2, lo, ones)))


def _fox_kernel(q_ref, k_ref, v_ref, c_ref, o_ref, kaug_ref, *, tile, heads):
    qi = pl.program_id(2)
    dh = FOX_HEAD_DIM
    aug = 2 * dh
    h0 = pl.program_id(1) * heads

    def head_column(c_all, hh):
        lane = lax.broadcasted_iota(jnp.int32, c_all.shape, 1)
        return jnp.sum(jnp.where(lane == h0 + hh, c_all, 0.0), axis=1, keepdims=True) * LOG2E

    @pl.when(qi == 0)
    def _():
        c_all = c_ref[...]
        for hh in range(heads):
            kaug_ref[:, hh * aug:hh * aug + dh] = k_ref[:, hh * dh:(hh + 1) * dh]
            kaug_ref[:, hh * aug + dh:(hh + 1) * aug] = _bias_lanes(-head_column(c_all, hh), 3).astype(BF16)

    cq_all = c_ref[pl.ds(pl.multiple_of(qi * tile, tile), tile), :]
    q_aug = [jnp.concatenate([q_ref[:, hh * dh:(hh + 1) * dh],
                              _bias_lanes(head_column(cq_all, hh), 0).astype(BF16)], axis=1)
             for hh in range(heads)]

    def update(state, s, v):
        m, l, acc = state
        m_new = jnp.maximum(m, jnp.max(s, axis=1, keepdims=True))
        alpha = jnp.exp2(m - m_new)
        p = jnp.exp2(s - m_new)
        return (m_new, alpha * l + jnp.sum(p, axis=1, keepdims=True), alpha * acc + _dot(p.astype(BF16), v))

    def step(j, carry, masked):
        start = pl.multiple_of(j * tile, tile)
        out = []
        for hh in range(heads):
            k = kaug_ref[pl.ds(start, tile), hh * aug:(hh + 1) * aug]
            v = v_ref[pl.ds(start, tile), hh * dh:(hh + 1) * dh]
            s = _dot_nt(q_aug[hh], k)
            if masked:
                row = lax.broadcasted_iota(jnp.int32, s.shape, 0)
                col = lax.broadcasted_iota(jnp.int32, s.shape, 1)
                s = jnp.where(row >= col, s, MASK_VALUE)
            out.append(update(carry[hh], s, v))
        return tuple(out)

    init = tuple((jnp.full((tile, 1), MASK_VALUE, F32), jnp.zeros((tile, 1), F32),
                  jnp.zeros((tile, dh), F32)) for _ in range(heads))
    carry = lax.fori_loop(0, qi, functools.partial(step, masked=False), init)
    carry = step(qi, carry, masked=True)
    for hh in range(heads):
        _, l, acc = carry[hh]
        o_ref[:, hh * dh:(hh + 1) * dh] = (acc / l).astype(o_ref.dtype)


def _forgetting_attention(qkv, c_col, *, tile=1024, heads=2):
    bsz, s_len, _ = qkv.shape
    width = heads * FOX_HEAD_DIM
    nblk = FOX_WIDTH // width
    return pl.pallas_call(
        functools.partial(_fox_kernel, tile=tile, heads=heads),
        grid=(bsz, nblk, s_len // tile),
        in_specs=[pl.BlockSpec((None, tile, width), lambda b, h, i: (b, i, h)),
                  pl.BlockSpec((None, s_len, width), lambda b, h, i: (b, 0, nblk + h)),
                  pl.BlockSpec((None, s_len, width), lambda b, h, i: (b, 0, 2 * nblk + h)),
                  pl.BlockSpec((None, s_len, c_col.shape[2]), lambda b, h, i: (b, 0, 0))],
        out_specs=pl.BlockSpec((None, tile, width), lambda b, h, i: (b, i, h)),
        out_shape=jax.ShapeDtypeStruct((bsz, s_len, FOX_WIDTH), BF16),
        scratch_shapes=[pltpu.VMEM((s_len, 2 * width), BF16)],
        compiler_params=_params("parallel", "parallel", "arbitrary"),
        name="forgetting_attention",
    )(qkv, qkv, qkv, c_col)


def _cmul(ar, ai, br, bi):
    return ar * br - ai * bi, ar * bi + ai * br


def _cpow(ar, ai, e, n_bits):
    rr = jnp.ones_like(ar)
    ri = jnp.zeros_like(ai)
    for bit in range(n_bits):
        nr, ni = _cmul(rr, ri, ar, ai)
        take = ((e >> bit) & 1) == 1
        rr = jnp.where(take, nr, rr)
        ri = jnp.where(take, ni, ri)
        if bit + 1 < n_bits:
            ar, ai = _cmul(ar, ai, ar, ai)
    return rr, ri


def _discretise(lr, li, dt):
    mag = jnp.exp(lr * dt)
    return mag * jnp.cos(li * dt), mag * jnp.sin(li * dt)


def _s5_prep_group(lam_row_ref, lam_col_ref, dt_ref, bt_re_ref, bt_im_ref, ct_re_ref, ct_im_ref,
                   d_ref, t0_ref, ws_re_ref, ws_im_ref, wc_re_ref, wc_im_ref, a_chunk_ref):
    chunk, grp, p = SSM_CHUNK, SSM_GROUP, SSM_STATE
    n_bits = chunk.bit_length()
    dt = jnp.exp(dt_ref[...])
    lr_r, li_r = lam_row_ref[0:1, :], lam_row_ref[1:2, :]
    ar_r, ai_r = _discretise(lr_r, li_r, dt)
    den = lr_r * lr_r + li_r * li_r
    xr, xi = ar_r - 1.0, ai_r
    g_re = (xr * lr_r + xi * li_r) / den
    g_im = (xi * lr_r - xr * li_r) / den
    bbt_re, bbt_im = _cmul(g_re, g_im, bt_re_ref[...], bt_im_ref[...])
    bx_re = jnp.concatenate([bbt_re] * chunk, axis=0)
    bx_im = jnp.concatenate([bbt_im] * chunk, axis=0)
    row_i = lax.broadcasted_iota(jnp.int32, (SSM_ROW, p), 0) // grp
    pr, pi = _cpow(jnp.broadcast_to(ar_r, (SSM_ROW, p)), jnp.broadcast_to(ai_r, (SSM_ROW, p)),
                   chunk - 1 - row_i, n_bits)
    ws_re, ws_im = _cmul(pr, pi, bx_re, bx_im)
    ws_re_ref[...] = ws_re.astype(BF16)
    ws_im_ref[...] = ws_im.astype(BF16)
    ac_r, ac_i = _cpow(ar_r, ai_r, jnp.full((1, p), chunk, jnp.int32), n_bits)
    a_chunk_ref[0:1, :] = ac_r
    a_chunk_ref[1:2, :] = ac_i
    lr_c, li_c = lam_col_ref[:, 0:1], lam_col_ref[:, 1:2]
    ar_c, ai_c = _discretise(lr_c, li_c, dt)
    rep = (lax.broadcasted_iota(jnp.int32, (grp, SSM_ROW), 0)
           == lax.broadcasted_iota(jnp.int32, (grp, SSM_ROW), 1) % grp).astype(F32)
    cx_re = jnp.dot(ct_re_ref[...], rep, precision=HIGHEST, preferred_element_type=F32)
    cx_im = jnp.dot(ct_im_ref[...], rep, precision=HIGHEST, preferred_element_type=F32)
    dx = jnp.dot(d_ref[...], rep, precision=HIGHEST, preferred_element_type=F32)
    col_j = lax.broadcasted_iota(jnp.int32, (p, SSM_ROW), 1) // grp
    ab_r = jnp.broadcast_to(ar_c, (p, SSM_ROW))
    ab_i = jnp.broadcast_to(ai_c, (p, SSM_ROW))
    qr, qi = _cpow(ab_r, ab_i, col_j, n_bits)
    f_re, f_im = _cmul(qr, qi, cx_re, cx_im)
    e_re, e_im = _cmul(f_re, f_im, ab_r, ab_i)
    wc_re_ref[...] = e_re.astype(BF16)
    wc_im_ref[...] = (-e_im).astype(BF16)
    r0 = (jnp.dot(bbt_re, f_re, precision=HIGHEST, preferred_element_type=F32)
          - jnp.dot(bbt_im, f_im, precision=HIGHEST, preferred_element_type=F32))
    lane = lax.broadcasted_iota(jnp.int32, (grp, SSM_ROW), 1)
    sub = lax.broadcasted_iota(jnp.int32, (grp, SSM_ROW), 0)
    for i in range(chunk):
        blk = r0 if i == 0 else pltpu.roll(r0, i * grp, 1)
        blk = jnp.where(lane >= i * grp, blk, 0.0)
        blk = blk + jnp.where(lane == i * grp + sub, dx, 0.0)
        t0_ref[i * grp:(i + 1) * grp, :] = blk.astype(BF16)


def _s5_prep_kernel(*refs, groups):
    for g in range(groups):
        _s5_prep_group(*(ref.at[g] for ref in refs))


def _s5_prepare(lam_re, lam_im, log_step, b_re, b_im, c_re, c_im, d_skip, *, groups_per_step=8):
    g, p = lam_re.shape
    grp = SSM_GROUP
    lam_row = jnp.stack([lam_re, lam_im], axis=1).astype(F32)
    lam_col = jnp.stack([lam_re, lam_im], axis=2).astype(F32)
    dt = log_step.reshape(g, 1, 1).astype(F32)
    bt_re = jnp.swapaxes(b_re, 1, 2).astype(F32)
    bt_im = jnp.swapaxes(b_im, 1, 2).astype(F32)
    ct_re = jnp.swapaxes(c_re, 1, 2).astype(F32)
    ct_im = jnp.swapaxes(c_im, 1, 2).astype(F32)
    d3 = d_skip.reshape(g, 1, grp).astype(F32)

    gs = groups_per_step

    def spec(*shape):
        return pl.BlockSpec((gs,) + shape, lambda i: (i,) + (0,) * len(shape))

    return pl.pallas_call(
        functools.partial(_s5_prep_kernel, groups=gs),
        grid=(g // gs,),
        in_specs=[spec(2, p), spec(p, 2), spec(1, 1), spec(grp, p), spec(grp, p),
                  spec(p, grp), spec(p, grp), spec(1, grp)],
        out_specs=[spec(SSM_ROW, SSM_ROW), spec(SSM_ROW, p), spec(SSM_ROW, p),
                   spec(p, SSM_ROW), spec(p, SSM_ROW), spec(2, p)],
        out_shape=[jax.ShapeDtypeStruct((g, SSM_ROW, SSM_ROW), BF16),
                   jax.ShapeDtypeStruct((g, SSM_ROW, p), BF16),
                   jax.ShapeDtypeStruct((g, SSM_ROW, p), BF16),
                   jax.ShapeDtypeStruct((g, p, SSM_ROW), BF16),
                   jax.ShapeDtypeStruct((g, p, SSM_ROW), BF16),
                   jax.ShapeDtypeStruct((g, 2, p), F32)],
        compiler_params=_params("parallel"),
        name="s5_prepare",
    )(lam_row, lam_col, dt, bt_re, bt_im, ct_re, ct_im, d3)


def _gelu_tanh(y):
    return 0.5 * y * (1.0 + jnp.tanh(math.sqrt(2.0 / math.pi) * (y + 0.044715 * (y * y * y))))


def _s5_scan_kernel(u_ref, t0_ref, ws_re_ref, ws_im_ref, wc_re_ref, wc_im_ref, a_ref, o_ref,
                    sre_ref, sim_ref, hre_ref, him_ref, *, groups, bsz, n_chunks):
    for g in range(groups):
        u = u_ref[g]
        sre_ref[g] = _dot(u, ws_re_ref[g])
        sim_ref[g] = _dot(u, ws_im_ref[g])
    ar = a_ref[:, 0:1, :]
    ai = a_ref[:, 1:2, :]

    def step(k, carry):
        hr, hi = carry
        rows = pl.ds(pl.multiple_of(k * bsz, bsz), bsz)
        hre_ref[:, rows, :] = hr
        him_ref[:, rows, :] = hi
        sr = sre_ref[:, rows, :]
        si = sim_ref[:, rows, :]
        return ar * hr - ai * hi + sr, ar * hi + ai * hr + si

    zero = jnp.zeros((groups, bsz, SSM_STATE), F32)
    lax.fori_loop(0, n_chunks, step, (zero, zero))
    for g in range(groups):
        y = (_dot(u_ref[g], t0_ref[g]) + _dot(hre_ref[g].astype(BF16), wc_re_ref[g])
             + _dot(him_ref[g].astype(BF16), wc_im_ref[g]))
        o_ref[g] = _gelu_tanh(y).astype(o_ref.dtype)


def _s5_scan_gelu(u_rows, mats, *, bsz, groups_per_step=8):
    t0, ws_re, ws_im, wc_re, wc_im, a_chunk = mats
    g, rows, _ = u_rows.shape
    gs = groups_per_step
    p = SSM_STATE

    def spec(*shape):
        return pl.BlockSpec((gs,) + shape, lambda i: (i,) + (0,) * len(shape))

    return pl.pallas_call(
        functools.partial(_s5_scan_kernel, groups=gs, bsz=bsz, n_chunks=rows // bsz),
        grid=(g // gs,),
        in_specs=[spec(rows, SSM_ROW), spec(SSM_ROW, SSM_ROW), spec(SSM_ROW, p), spec(SSM_ROW, p),
                  spec(p, SSM_ROW), spec(p, SSM_ROW), spec(2, p)],
        out_specs=spec(rows, SSM_ROW),
        out_shape=jax.ShapeDtypeStruct((g, rows, SSM_ROW), BF16),
        scratch_shapes=[pltpu.VMEM((gs, rows, p), F32), pltpu.VMEM((gs, rows, p), F32),
                        pltpu.VMEM((gs, rows, p), F32), pltpu.VMEM((gs, rows, p), F32)],
        compiler_params=_params("parallel"),
        name="s5_scan_gelu",
    )(u_rows, t0, ws_re, ws_im, wc_re, wc_im, a_chunk)


def _sigmoid(x):
    return 1.0 / (1.0 + jnp.exp(-x))


def _block_transpose(xs):
    n = GROUPS_PER_SLAB
    width = xs[0].shape[1]
    blk = (lax.broadcasted_iota(jnp.int32, (1, width), 1) // SSM_GROUP) % n
    xs = list(xs)
    d = n // 2
    while d >= 1:
        high = (blk & d) != 0
        nxt = list(xs)
        for i in range(n):
            if i & d:
                continue
            lo_arr, hi_arr = xs[i], xs[i + d]
            nxt[i] = jnp.where(high, pltpu.roll(hi_arr, SSM_GROUP * d, 1), lo_arr)
            nxt[i + d] = jnp.where(high, hi_arr, pltpu.roll(lo_arr, width - SSM_GROUP * d, 1))
        xs = nxt
        d //= 2
    return xs


def _u_rows_kernel(x_ref, w_ref, o_ref, f_ref, stage_ref, *, bsz, chunks):
    tb = x_ref.shape[1]
    uf = _dot(x_ref[...].reshape(bsz * tb, x_ref.shape[2]).astype(BF16), w_ref[...])
    n_slabs = o_ref.shape[0] // GROUPS_PER_SLAB
    u = uf[:, :n_slabs * V7X_LANES]
    f_ref[...] = uf[:, n_slabs * V7X_LANES:].reshape(f_ref.shape)
    for v in range(n_slabs):
        for b in range(bsz):
            stage_ref[v, b * STAGE_PITCH:b * STAGE_PITCH + tb, :] = (
                u[b * tb:(b + 1) * tb, v * V7X_LANES:(v + 1) * V7X_LANES])
    n = GROUPS_PER_SLAB
    for jh in range(SSM_CHUNK // n):
        xs = []
        for jl in range(n):
            j = jh * n + jl
            xs.append(jnp.concatenate(
                [jnp.concatenate([stage_ref.at[v][pl.ds(SSM_CHUNK * c + j, bsz, stride=STAGE_PITCH), :]
                                  for v in range(n_slabs)], axis=1) for c in range(chunks)], axis=0))
        ys = _block_transpose(xs)
        for e in range(n):
            for v in range(n_slabs):
                o_ref[n * v + e, :, jh * V7X_LANES:(jh + 1) * V7X_LANES] = (
                    ys[e][:, v * V7X_LANES:(v + 1) * V7X_LANES].astype(o_ref.dtype))


def _u_projection_rows(x, w_uf, *, chunks=8):
    bsz, s_len, d = x.shape
    width = w_uf.shape[1] - V7X_LANES
    g = width // SSM_GROUP
    tb = chunks * SSM_CHUNK
    rows = chunks * bsz
    return pl.pallas_call(
        functools.partial(_u_rows_kernel, bsz=bsz, chunks=chunks),
        grid=(s_len // tb,),
        in_specs=[pl.BlockSpec((bsz, tb, d), lambda i: (0, i, 0)),
                  pl.BlockSpec((d, width + V7X_LANES), lambda i: (0, 0), pipeline_mode=pl.Buffered(1))],
        out_specs=[pl.BlockSpec((g, rows, SSM_ROW), lambda i: (0, i, 0)),
                   pl.BlockSpec((bsz, tb, V7X_LANES), lambda i: (0, i, 0))],
        out_shape=[jax.ShapeDtypeStruct((g, (s_len // SSM_CHUNK) * bsz, SSM_ROW), BF16),
                   jax.ShapeDtypeStruct((bsz, s_len, V7X_LANES), F32)],
        scratch_shapes=[pltpu.VMEM((width // V7X_LANES, bsz * STAGE_PITCH, V7X_LANES), F32)],
        compiler_params=_params("parallel"),
        name="u_projection_rows",
    )(x, w_uf)


def _glu_rows_kernel(gy_ref, w_ref, o_ref, stage_ref, *, bsz, chunks):
    tb = o_ref.shape[1]
    n = GROUPS_PER_SLAB
    n_slabs = gy_ref.shape[0] // n
    for jh in range(SSM_CHUNK // n):
        ys = [jnp.concatenate([gy_ref[n * v + e, :, jh * V7X_LANES:(jh + 1) * V7X_LANES].astype(F32)
                               for v in range(n_slabs)], axis=1) for e in range(n)]
        xs = _block_transpose(ys)
        for jl in range(n):
            j = jh * n + jl
            for c in range(chunks):
                for v in range(n_slabs):
                    stage_ref.at[v][pl.ds(SSM_CHUNK * c + j, bsz, stride=STAGE_PITCH), :] = (
                        xs[jl][c * bsz:(c + 1) * bsz, v * V7X_LANES:(v + 1) * V7X_LANES])
    a = jnp.concatenate(
        [jnp.concatenate([stage_ref[v, b * STAGE_PITCH:b * STAGE_PITCH + tb, :] for v in range(n_slabs)], axis=1)
         for b in range(bsz)], axis=0).astype(BF16)
    z = _dot(a, w_ref[...])
    half = z.shape[1] // 2
    o_ref[...] = (z[:, :half] * _sigmoid(z[:, half:])).astype(o_ref.dtype).reshape(o_ref.shape)


def _glu_from_rows(gy_rows, w_glu, *, bsz, chunks=8):
    g, total_rows, _ = gy_rows.shape
    width = g * SSM_GROUP
    n_out = w_glu.shape[1] // 2
    s_len = total_rows // bsz * SSM_CHUNK
    tb = chunks * SSM_CHUNK
    rows = chunks * bsz
    return pl.pallas_call(
        functools.partial(_glu_rows_kernel, bsz=bsz, chunks=chunks),
        grid=(s_len // tb,),
        in_specs=[pl.BlockSpec((g, rows, SSM_ROW), lambda i: (0, i, 0)),
                  pl.BlockSpec((width, 2 * n_out), lambda i: (0, 0), pipeline_mode=pl.Buffered(1))],
        out_specs=pl.BlockSpec((bsz, tb, n_out), lambda i: (0, i, 0)),
        out_shape=jax.ShapeDtypeStruct((bsz, s_len, n_out), BF16),
        scratch_shapes=[pltpu.VMEM((width // V7X_LANES, bsz * STAGE_PITCH, V7X_LANES), F32)],
        compiler_params=_params("parallel"),
        name="glu_from_rows",
    )(gy_rows, w_glu)


def _ffn_up_kernel(x_ref, wg_ref, wv_ref, cwg_ref, cbg_ref, cwv_ref, cbv_ref, o_ref,
                   wg16_ref, wv16_ref, hg_ref, hv_ref, *, blocks_per_seq):
    i = pl.program_id(1)
    bm, bn = o_ref.shape
    pad = hg_ref.shape[0] - bm

    @pl.when(i == 0)
    def _():
        wg16_ref[...] = wg_ref[0].astype(BF16)
        wv16_ref[...] = wv_ref[0].astype(BF16)

    @pl.when(i % blocks_per_seq == 0)
    def _():
        hg_ref[0:pad, :] = jnp.zeros((pad, bn), F32)
        hv_ref[0:pad, :] = jnp.zeros((pad, bn), F32)

    @pl.when(i % blocks_per_seq != 0)
    def _():
        hg_ref[0:pad, :] = hg_ref[bm:bm + pad, :]
        hv_ref[0:pad, :] = hv_ref[bm:bm + pad, :]

    x = x_ref[...]
    conv = []
    for w16, h_ref, cw_ref, cb_ref in ((wg16_ref, hg_ref, cwg_ref, cbg_ref), (wv16_ref, hv_ref, cwv_ref, cbv_ref)):
        h = _dot(x, w16[...])
        h_ref[pad:pad + bm, :] = h
        cw = cw_ref[0]
        conv.append(cb_ref[0] + cw[0:1, :] * h_ref[pad - 2:pad - 2 + bm, :]
                    + cw[1:2, :] * h_ref[pad - 1:pad - 1 + bm, :] + cw[2:3, :] * h)
    gate, val = conv
    o_ref[...] = (gate * _sigmoid(gate) * val).astype(o_ref.dtype)


def _ffn_up(x16, layer, w_up, conv_w, conv_b, *, col0, n_cols, seq_len, bm, bn):
    m, d = x16.shape
    f = w_up.shape[2] // 2
    pad = 8
    cb = conv_b.reshape(conv_b.shape[0], 1, 2 * f)
    gate_col = lambda j: pl.multiple_of(col0 + j * bn, V7X_LANES)
    val_col = lambda j: pl.multiple_of(f + col0 + j * bn, V7X_LANES)
    window = lambda rows: (pl.Element(1), pl.Element(rows), pl.Element(bn))
    gate_spec = lambda rows: pl.BlockSpec(window(rows), lambda j, i: (layer, 0, gate_col(j)))
    val_spec = lambda rows: pl.BlockSpec(window(rows), lambda j, i: (layer, 0, val_col(j)))
    return pl.pallas_call(
        functools.partial(_ffn_up_kernel, blocks_per_seq=seq_len // bm),
        grid=(n_cols // bn, m // bm),
        in_specs=[pl.BlockSpec((bm, d), lambda j, i: (i, 0)),
                  gate_spec(d), val_spec(d), gate_spec(CONV_WIDTH), gate_spec(1),
                  val_spec(CONV_WIDTH), val_spec(1)],
        out_specs=pl.BlockSpec((bm, bn), lambda j, i: (i, j)),
        out_shape=jax.ShapeDtypeStruct((m, n_cols), BF16),
        scratch_shapes=[pltpu.VMEM((d, bn), BF16), pltpu.VMEM((d, bn), BF16),
                        pltpu.VMEM((pad + bm, bn), F32), pltpu.VMEM((pad + bm, bn), F32)],
        compiler_params=_params("arbitrary", "arbitrary"),
        name="ffn_up_conv_gate",
    )(x16, w_up, w_up, conv_w, cb, conv_w, cb)


def _cast_kernel(src_ref, dst_ref):
    dst_ref[...] = src_ref[...].astype(dst_ref.dtype)


def _layer_to_bf16(stacked, layer, *, steps=8):
    _, k, d = stacked.shape
    rows = k // steps
    return pl.pallas_call(
        _cast_kernel,
        grid=(steps,),
        in_specs=[pl.BlockSpec((None, rows, d), lambda r: (layer, r, 0))],
        out_specs=pl.BlockSpec((rows, d), lambda r: (r, 0)),
        out_shape=jax.ShapeDtypeStruct((k, d), BF16),
        compiler_params=_params("parallel"),
        name="layer_to_bf16",
    )(stacked)


def _conv_ffn(x32, x16, layer, w_up, conv_w, conv_b, w_down, ln_g, ln_b, *, seq_len, bn=512):
    f = w_down.shape[1]
    w_up, conv_w, conv_b = w_up.astype(F32), conv_w.astype(F32), conv_b.astype(F32)
    w_down16 = _layer_to_bf16(w_down, layer)
    main = (f // bn) * bn
    pairs = []
    for col0, n_cols, width in ((0, main, bn), (main, f - main, f - main)):
        if n_cols:
            act = _ffn_up(x16, layer, w_up, conv_w, conv_b, col0=col0, n_cols=n_cols, seq_len=seq_len,
                          bm=1024, bn=width)
            pairs.append((act, w_down16, col0))
    return _matmul_residual_layernorm(pairs, x32, ln_g, ln_b, bm=256)


def _rope_proj_kernel(a_ref, w_ref, pos_ref, o_ref, cos_ref, sin_ref, *, rope_cols, q_tiles, q_scale):
    j = pl.program_id(1)
    hd, half = SWA_HEAD_DIM, ROPE_DIM // 2

    @pl.when(j == 0)
    def _():
        lane = lax.broadcasted_iota(jnp.int32, (1, V7X_LANES), 1) % hd
        idx = (lane % half).astype(F32)
        inv_freq = jnp.where(lane < ROPE_DIM, jnp.exp(idx * (-math.log(ROPE_THETA) / half)), 0.0)
        ang = pos_ref[...].astype(F32) * inv_freq
        sign = jnp.where(lane < half, -1.0, 1.0)
        cos_ref[...] = jnp.cos(ang)
        sin_ref[...] = jnp.sin(ang) * sign

    acc = _dot(a_ref[...], w_ref[...])
    bm, bn = acc.shape

    reps = bn // V7X_LANES
    cos = jnp.concatenate([cos_ref[...]] * reps, axis=1)
    sin = jnp.concatenate([sin_ref[...]] * reps, axis=1)
    lane = lax.broadcasted_iota(jnp.int32, (1, bn), 1)
    first = (lane & (hd - 1)) < half
    partner = jnp.where(first, pltpu.roll(acc, bn - half, 1), pltpu.roll(acc, half, 1))
    rot = acc * cos + partner * sin
    col = lane + j * bn
    out = jnp.where(col < rope_cols, rot, acc) * jnp.where(col < q_tiles * bn, q_scale, 1.0)
    o_ref[...] = out.astype(o_ref.dtype)


def _rope_projection(x16, w, pos_col, *, rope_cols, q_cols, q_scale, bm, bn):
    m, k = x16.shape
    n = w.shape[1]
    return pl.pallas_call(
        functools.partial(_rope_proj_kernel, rope_cols=rope_cols, q_tiles=q_cols // bn,
                          q_scale=q_scale),
        grid=(m // bm, n // bn),
        in_specs=[pl.BlockSpec((bm, k), lambda i, j: (i, 0)),
                  pl.BlockSpec((k, bn), lambda i, j: (0, j)),
                  pl.BlockSpec((bm, 1), lambda i, j: (i, 0))],
        out_specs=pl.BlockSpec((bm, bn), lambda i, j: (i, j)),
        out_shape=jax.ShapeDtypeStruct((m, n), BF16),
        scratch_shapes=[pltpu.VMEM((bm, V7X_LANES), F32), pltpu.VMEM((bm, V7X_LANES), F32)],
        compiler_params=_params("parallel", "arbitrary"),
        name="rope_projection",
    )(x16, w, pos_col)


def _swa_kernel(sink_ref, q_ref, kp_ref, kc_ref, vp_ref, vc_ref, o_ref):
    n = pl.program_id(1)
    w, hd, lanes = SWA_WINDOW, SWA_HEAD_DIM, V7X_LANES
    pairs = SWA_GROUPS // 2
    qi = lax.broadcasted_iota(jnp.int32, (w, 2 * w), 0)
    kj = lax.broadcasted_iota(jnp.int32, (w, 2 * w), 1)
    rel = w + qi - kj
    valid = (rel >= 0) & (rel < w) & ((n > 0) | (kj >= w))
    bias = jnp.where(valid, 0.0, MASK_VALUE)
    lane_kv = lax.broadcasted_iota(jnp.int32, (2 * w, lanes), 1)
    lane_o = lax.broadcasted_iota(jnp.int32, (w, lanes), 1)

    def both_halves(prev_ref, cur_ref, hk):
        slab = slice((hk // 2) * lanes, (hk // 2 + 1) * lanes)
        x = jnp.concatenate([prev_ref[:, slab], cur_ref[:, slab]], axis=0).astype(F32)
        upper = hk % 2 == 1
        own = jnp.where((lane_kv >= hd) if upper else (lane_kv < hd), x, 0.0)
        other = pltpu.roll(own, hd, 1)
        lo, hi = (other, own) if upper else (own, other)
        return jnp.concatenate([lo, hi], axis=0).astype(BF16)

    for hk in range(SWA_KV_HEADS):
        kk = both_halves(kp_ref, kc_ref, hk)
        vv = both_halves(vp_ref, vc_ref, hk)
        slabs = [slice((hk * pairs + r) * lanes, (hk * pairs + r + 1) * lanes) for r in range(pairs)]
        qs = jnp.concatenate([q_ref[:, sl] for sl in slabs], axis=0)
        s_all = _dot_nt(qs, kk)
        probs, inv_den = [], []
        for r in range(pairs):
            halves, inv = [], []
            for half in range(2):
                sink = sink_ref[hk * SWA_GROUPS + 2 * r + half] * LOG2E
                s = s_all[r * w:(r + 1) * w, half * 2 * w:(half + 1) * 2 * w] + bias
                m = jnp.maximum(jnp.max(s, axis=1, keepdims=True), sink)
                p = jnp.exp2(s - m)
                inv.append(1.0 / (jnp.sum(p, axis=1, keepdims=True) + jnp.exp2(sink - m)))
                halves.append(p.astype(BF16))
            probs.append(jnp.concatenate(halves, axis=1))
            inv_den.append(jnp.where(lane_o < hd, inv[0], inv[1]))
        o_all = _dot(jnp.concatenate(probs, axis=0), vv)
        for r in range(pairs):
            o_ref[:, slabs[r]] = (o_all[r * w:(r + 1) * w] * inv_den[r]).astype(o_ref.dtype)


def _sliding_window_attention(qkv, sinks):
    bsz, s_len, _ = qkv.shape
    w, hd = SWA_WINDOW, SWA_HEAD_DIM
    qw = SWA_HEADS * hd
    kw = SWA_KV_HEADS * hd
    k_blk = qw // kw
    v_blk = k_blk + 1
    prev = lambda n: jnp.maximum(n - 1, 0)
    return pl.pallas_call(
        _swa_kernel,
        grid=(bsz, s_len // w),
        in_specs=[pl.BlockSpec(memory_space=pltpu.SMEM),
                  pl.BlockSpec((None, w, qw), lambda b, n: (b, n, 0)),
                  pl.BlockSpec((None, w, kw), lambda b, n: (b, prev(n), k_blk)),
                  pl.BlockSpec((None, w, kw), lambda b, n: (b, n, k_blk)),
                  pl.BlockSpec((None, w, kw), lambda b, n: (b, prev(n), v_blk)),
                  pl.BlockSpec((None, w, kw), lambda b, n: (b, n, v_blk))],
        out_specs=pl.BlockSpec((None, w, qw), lambda b, n: (b, n, 0)),
        out_shape=jax.ShapeDtypeStruct((bsz, s_len, qw), BF16),
        compiler_params=_params("parallel", "arbitrary"),
        name="sliding_window_attention",
    )(sinks.astype(F32), qkv, qkv, qkv, qkv, qkv)


def _even_mixer(x32, xa, bsz, s_len, w_in, b_f, lam_re, lam_im, log_step, b_re, b_im, c_re, c_im,
                d_skip, w_glu, w_out, ln_g, ln_b):
    m = x32.shape[0]
    fw = FOX_WIDTH
    nh = FOX_HEADS
    q_factor = LOG2E / math.sqrt(FOX_HEAD_DIM)
    w_qkv = jnp.concatenate([w_in[:, :fw] * q_factor, w_in[:, fw:3 * fw]], axis=1).astype(BF16)
    w_uf = jnp.concatenate([w_in[:, 3 * fw + nh:],
                            jnp.pad(w_in[:, 3 * fw:3 * fw + nh], ((0, 0), (0, V7X_LANES - nh)))],
                           axis=1).astype(BF16)
    bias_f = jnp.pad(b_f, (0, V7X_LANES - nh)).reshape(1, V7X_LANES).astype(F32)

    qkv = _matmul(xa, w_qkv, bm=1024, bn=3 * fw // 2, out_dtype=BF16)
    mats = _s5_prepare(lam_re, lam_im, log_step, b_re, b_im, c_re, c_im, d_skip)
    u_rows, f_logits = _u_projection_rows(xa, w_uf)
    c_col = _forget_gate_cumsum(f_logits, bias_f)
    fox = _forgetting_attention(qkv, c_col).reshape(m, fw)
    gy_rows = _s5_scan_gelu(u_rows, mats, bsz=bsz)
    ssm = _glu_from_rows(gy_rows, w_glu.astype(BF16), bsz=bsz).reshape(m, -1)
    w_out16 = w_out.astype(BF16)
    return _matmul_residual_layernorm([(fox, w_out16, 0), (ssm, w_out16, fw)], x32, ln_g, ln_b, bm=512)


def _odd_mixer(x32, x16, bsz, s_len, positions, w_in, sinks, w_out, ln_g, ln_b):
    m = x16.shape[0]
    rope_cols = (SWA_HEADS + SWA_KV_HEADS) * SWA_HEAD_DIM
    qkv = _rope_projection(x16, w_in.astype(BF16), positions.reshape(m, 1).astype(jnp.int32),
                           rope_cols=rope_cols, q_cols=SWA_HEADS * SWA_HEAD_DIM,
                           q_scale=LOG2E / math.sqrt(SWA_HEAD_DIM), bm=1024, bn=512)
    o = _sliding_window_attention(qkv.reshape(bsz, s_len, -1), sinks).reshape(m, -1)
    return _matmul_residual_layernorm([(o, w_out.astype(BF16), 0)], x32, ln_g, ln_b, bm=512)


def kernel(x, positions, ev_w_in, ev_b_f, ev_lambda_re, ev_lambda_im, ev_log_step, ev_ssm_b_re,
           ev_ssm_b_im, ev_ssm_c_re, ev_ssm_c_im, ev_ssm_d, ev_w_glu, ev_w_out, od_w_in, od_sinks,
           od_w_out, ln_mix_g, ln_mix_b, ffn_w_up, ffn_conv_w, ffn_conv_b, ffn_w_down, ln_ffn_g,
           ln_ffn_b):
    bsz, s_len, d = x.shape
    x32 = x.reshape(bsz * s_len, d).astype(F32)
    x16 = None
    for i in range(DEPTH):
        j = i // 2
        if i % 2 == 0:
            xa = x.astype(F32) if x16 is None else x16.reshape(bsz, s_len, d)
            x32, x16 = _even_mixer(x32, xa, bsz, s_len, ev_w_in[j], ev_b_f[j], ev_lambda_re[j],
                                   ev_lambda_im[j], ev_log_step[j], ev_ssm_b_re[j], ev_ssm_b_im[j],
                                   ev_ssm_c_re[j], ev_ssm_c_im[j], ev_ssm_d[j], ev_w_glu[j],
                                   ev_w_out[j], ln_mix_g[i], ln_mix_b[i])
        else:
            x16 = x32.astype(BF16) if x16 is None else x16
            x32, x16 = _odd_mixer(x32, x16, bsz, s_len, positions, od_w_in[j], od_sinks[j],
                                  od_w_out[j], ln_mix_g[i], ln_mix_b[i])
        x32, x16 = _conv_ffn(x32, x16, i, ffn_w_up, ffn_conv_w, ffn_conv_b, ffn_w_down,
                             ln_ffn_g[i], ln_ffn_b[i], seq_len=s_len)
    return x32.reshape(bsz, s_len, d).astype(x.dtype)
```

```python
import functools
import math

import jax
import jax.numpy as jnp
from jax import lax
from jax.experimental import pallas as pl
from jax.experimental.pallas import tpu as pltpu

F32 = jnp.float32
BF16 = jnp.bfloat16

DEPTH = 2
FOX_HEADS = 8
FOX_HEAD_DIM = 128
FOX_WIDTH = FOX_HEADS * FOX_HEAD_DIM
SSM_GROUP = 16
SSM_STATE = 64
SSM_CHUNK = 16
SSM_ROW = SSM_CHUNK * SSM_GROUP
SWA_HEADS = 32
SWA_KV_HEADS = 4
SWA_HEAD_DIM = 64
SWA_GROUPS = SWA_HEADS // SWA_KV_HEADS
SWA_WINDOW = 128
ROPE_DIM = SWA_HEAD_DIM // 4
ROPE_THETA = 500000.0
CONV_WIDTH = 3
LN_EPS = 1e-5
DEEPNORM_ALPHA = (2.0 * DEPTH) ** 0.25

V7X_LANES = 128
V7X_VMEM_LIMIT_BYTES = 56 * 1024 * 1024
GROUPS_PER_SLAB = V7X_LANES // SSM_GROUP
STAGE_PITCH = 136
MASK_VALUE = -1e30
LOG2E = math.log2(math.e)

HIGHEST = lax.Precision.HIGHEST


def _params(*semantics):
    return pltpu.CompilerParams(dimension_semantics=semantics,
                                vmem_limit_bytes=V7X_VMEM_LIMIT_BYTES)


def _dot(a, b):
    return jnp.dot(a, b, preferred_element_type=F32)


def _dot_nt(a, b):
    return lax.dot_general(a, b, (((1,), (1,)), ((), ())), preferred_element_type=F32)


def _mm_kernel(a_ref, w_ref, o_ref):
    o_ref[...] = _dot(a_ref[...].astype(BF16), w_ref[...]).astype(o_ref.dtype)


def _matmul(a, w, *, bm, bn, out_dtype):
    bsz, s_len, k = a.shape
    n = w.shape[1]
    assert s_len % bm == 0 and n % bn == 0, (a.shape, w.shape, bm, bn)
    per_seq = s_len // bm
    rows = lambda i, j: (i // per_seq, i % per_seq)
    return pl.pallas_call(
        _mm_kernel,
        grid=(bsz * per_seq, n // bn),
        in_specs=[pl.BlockSpec((None, bm, k), lambda i, j: rows(i, j) + (0,)),
                  pl.BlockSpec((k, bn), lambda i, j: (0, j))],
        out_specs=pl.BlockSpec((None, bm, bn), lambda i, j: rows(i, j) + (j,)),
        out_shape=jax.ShapeDtypeStruct((bsz, s_len, n), out_dtype),
        compiler_params=_params("parallel", "arbitrary"),
        name="matmul",
    )(a, w)


def _mm_res_ln_kernel(*refs, n_pairs, sub):
    a_refs = refs[:n_pairs]
    w_refs = refs[n_pairs:2 * n_pairs]
    x_ref, g_ref, b_ref, o32_ref, o16_ref = refs[2 * n_pairs:]
    for r in range(x_ref.shape[0] // sub):
        rows = slice(r * sub, (r + 1) * sub)
        pre = DEEPNORM_ALPHA * x_ref[rows, :]
        for a_ref, w_ref in zip(a_refs, w_refs):
            pre = pre + _dot(a_ref[rows, :], w_ref[...])
        mu = jnp.mean(pre, axis=-1, keepdims=True)
        cen = pre - mu
        var = jnp.mean(cen * cen, axis=-1, keepdims=True)
        y = cen * lax.rsqrt(var + LN_EPS) * g_ref[...] + b_ref[...]
        o32_ref[rows, :] = y
        o16_ref[rows, :] = y.astype(BF16)


def _matmul_residual_layernorm(pairs, x32, gain, bias, *, bm, sub=256):
    m, d = x32.shape
    assert m % bm == 0 and bm % sub == 0, (m, bm, sub)
    n_pairs = len(pairs)
    a_list = [p[0] for p in pairs]
    w_list = [p[1] for p in pairs]
    row_spec = lambda width: pl.BlockSpec((bm, width), lambda i: (i, 0))
    const_spec = lambda rows: pl.BlockSpec((rows, d), lambda i: (0, 0), pipeline_mode=pl.Buffered(1))
    window_spec = lambda rows, row0: pl.BlockSpec((pl.Element(rows), pl.Element(d)), lambda i: (row0, 0),
                                                  pipeline_mode=pl.Buffered(1))
    in_specs = ([row_spec(a.shape[1]) for a in a_list]
                + [window_spec(a.shape[1], row0) for a, _, row0 in pairs]
                + [row_spec(d), const_spec(1), const_spec(1)])
    return pl.pallas_call(
        functools.partial(_mm_res_ln_kernel, n_pairs=n_pairs, sub=sub),
        grid=(m // bm,),
        in_specs=in_specs,
        out_specs=[row_spec(d), row_spec(d)],
        out_shape=[jax.ShapeDtypeStruct((m, d), F32), jax.ShapeDtypeStruct((m, d), BF16)],
        compiler_params=_params("parallel"),
        name="matmul_residual_layernorm",
    )(*a_list, *w_list, x32, gain.reshape(1, d).astype(F32), bias.reshape(1, d).astype(F32))


def _bf16_split3(v):
    hi = v.astype(BF16)
    r1 = v - hi.astype(F32)
    mid = r1.astype(BF16)
    lo = (r1 - mid.astype(F32)).astype(BF16)
    return hi, mid, lo


def _forget_gate_kernel(f_ref, bf_ref, c_ref, *, chunk):
    s_len = f_ref.shape[0]
    f = f_ref[...] + bf_ref[...]
    log_f = jnp.minimum(f, 0.0) - jnp.log(1.0 + jnp.exp(-jnp.abs(f)))
    row = lax.broadcasted_iota(jnp.int32, (chunk, chunk), 0)
    col = lax.broadcasted_iota(jnp.int32, (chunk, chunk), 1)
    tri = (row >= col).astype(BF16)
    carry = jnp.zeros((1, log_f.shape[1]), F32)
    for start in range(0, s_len, chunk):
        hi, mid, lo = _bf16_split3(log_f[start:start + chunk])
        part = _dot(tri, hi) + _dot(tri, mid) + _dot(tri, lo) + carry
        c_ref[start:start + chunk, :] = part
        carry = part[chunk - 1:chunk, :]


def _forget_gate_cumsum(f_logits, b_f, *, chunk=256):
    bsz, s_len, n = f_logits.shape
    return pl.pallas_call(
        functools.partial(_forget_gate_kernel, chunk=chunk),
        grid=(bsz,),
        in_specs=[pl.BlockSpec((None, s_len, n), lambda b: (b, 0, 0)),
                  pl.BlockSpec((1, n), lambda b: (0, 0))],
        out_specs=pl.BlockSpec((None, s_len, n), lambda b: (b, 0, 0)),
        out_shape=jax.ShapeDtypeStruct((bsz, s_len, n), F32),
        compiler_params=_params("parallel"),
        name="forget_gate_cumsum",
    )(f_logits, b_f)


def _bias_lanes(c, first):
    hi, mid, lo = (piece.astype(F32) for piece in _bf16_split3(c))
    lane = lax.broadcasted_iota(jnp.int32, (c.shape[0], V7X_LANES), 1)
    ones = jnp.where(lane < 6, 1.0, 0.0)
    return jnp.where(lane == first, hi, jnp.where(lane == first + 1, mid, jnp.where(lane == first + 2, lo, ones)))


def _fox_kernel(q_ref, k_ref, v_ref, c_ref, o_ref, kaug_ref, *, tile, heads):
    dh = FOX_HEAD_DIM
    aug = 2 * dh
    h0 = pl.program_id(1) * heads
    n_tiles = q_ref.shape[0] // tile

    def head_column(c_all, hh):
        lane = lax.broadcasted_iota(jnp.int32, c_all.shape, 1)
        return jnp.sum(jnp.where(lane == h0 + hh, c_all, 0.0), axis=1, keepdims=True) * LOG2E

    c_all = c_ref[...]
    for hh in range(heads):
        kaug_ref[:, hh * aug:hh * aug + dh] = k_ref[:, hh * dh:(hh + 1) * dh]
        kaug_ref[:, hh * aug + dh:(hh + 1) * aug] = _bias_lanes(-head_column(c_all, hh), 3).astype(BF16)

    def update(state, s, v):
        m, l, acc = state
        m_new = jnp.maximum(m, jnp.max(s, axis=1, keepdims=True))
        alpha = jnp.exp2(m - m_new)
        p = jnp.exp2(s - m_new)
        return (m_new, alpha * l + jnp.sum(p, axis=1, keepdims=True), alpha * acc + _dot(p.astype(BF16), v))

    for qt in range(n_tiles):
        rows = slice(qt * tile, (qt + 1) * tile)
        cq_all = c_ref[rows, :]
        for hh in range(heads):
            q_aug = jnp.concatenate([q_ref[rows, hh * dh:(hh + 1) * dh],
                                     _bias_lanes(head_column(cq_all, hh), 0).astype(BF16)], axis=1)
            state = (jnp.full((tile, 1), MASK_VALUE, F32), jnp.zeros((tile, 1), F32), jnp.zeros((tile, dh), F32))
            for kt in range(qt + 1):
                keys = slice(kt * tile, (kt + 1) * tile)
                s = _dot_nt(q_aug, kaug_ref[keys, hh * aug:(hh + 1) * aug])
                if kt == qt:
                    row = lax.broadcasted_iota(jnp.int32, s.shape, 0)
                    col = lax.broadcasted_iota(jnp.int32, s.shape, 1)
                    s = jnp.where(row >= col, s, MASK_VALUE)
                state = update(state, s, v_ref[keys, hh * dh:(hh + 1) * dh])
            _, l, acc = state
            o_ref[rows, hh * dh:(hh + 1) * dh] = (acc / l).astype(o_ref.dtype)


def _forgetting_attention(qkv, c_col, *, tile=1024, heads=2):
    bsz, s_len, _ = qkv.shape
    assert s_len % tile == 0 and FOX_HEADS % heads == 0, (s_len, tile, heads)
    width = heads * FOX_HEAD_DIM
    nblk = FOX_WIDTH // width
    seq_spec = lambda off: pl.BlockSpec((None, s_len, width), lambda b, h: (b, 0, off + h))
    return pl.pallas_call(
        functools.partial(_fox_kernel, tile=tile, heads=heads),
        grid=(bsz, nblk),
        in_specs=[seq_spec(0), seq_spec(nblk), seq_spec(2 * nblk),
                  pl.BlockSpec((None, s_len, c_col.shape[2]), lambda b, h: (b, 0, 0))],
        out_specs=seq_spec(0),
        out_shape=jax.ShapeDtypeStruct((bsz, s_len, FOX_WIDTH), BF16),
        scratch_shapes=[pltpu.VMEM((s_len, 2 * width), BF16)],
        compiler_params=_params("parallel", "parallel"),
        name="forgetting_attention",
    )(qkv, qkv, qkv, c_col)


def _cmul(ar, ai, br, bi):
    return ar * br - ai * bi, ar * bi + ai * br


def _cpow(ar, ai, e, n_bits):
    rr = jnp.ones_like(ar)
    ri = jnp.zeros_like(ai)
    for bit in range(n_bits):
        nr, ni = _cmul(rr, ri, ar, ai)
        take = ((e >> bit) & 1) == 1
        rr = jnp.where(take, nr, rr)
        ri = jnp.where(take, ni, ri)
        if bit + 1 < n_bits:
            ar, ai = _cmul(ar, ai, ar, ai)
    return rr, ri


def _discretise(lr, li, dt):
    mag = jnp.exp(lr * dt)
    return mag * jnp.cos(li * dt), mag * jnp.sin(li * dt)


def _s5_prep_group(lam_row_ref, lam_col_ref, dt_ref, bt_re_ref, bt_im_ref, ct_re_ref, ct_im_ref,
                   d_ref, t0_ref, ws_re_ref, ws_im_ref, wc_re_ref, wc_im_ref, a_chunk_ref):
    chunk, grp, p = SSM_CHUNK, SSM_GROUP, SSM_STATE
    n_bits = chunk.bit_length()
    dt = jnp.exp(dt_ref[...])
    lr_r, li_r = lam_row_ref[0:1, :], lam_row_ref[1:2, :]
    ar_r, ai_r = _discretise(lr_r, li_r, dt)
    den = lr_r * lr_r + li_r * li_r
    xr, xi = ar_r - 1.0, ai_r
    g_re = (xr * lr_r + xi * li_r) / den
    g_im = (xi * lr_r - xr * li_r) / den
    bbt_re, bbt_im = _cmul(g_re, g_im, bt_re_ref[...], bt_im_ref[...])
    bx_re = jnp.concatenate([bbt_re] * chunk, axis=0)
    bx_im = jnp.concatenate([bbt_im] * chunk, axis=0)
    row_i = lax.broadcasted_iota(jnp.int32, (SSM_ROW, p), 0) // grp
    pr, pi = _cpow(jnp.broadcast_to(ar_r, (SSM_ROW, p)), jnp.broadcast_to(ai_r, (SSM_ROW, p)),
                   chunk - 1 - row_i, n_bits)
    ws_re, ws_im = _cmul(pr, pi, bx_re, bx_im)
    ws_re_ref[...] = ws_re.astype(BF16)
    ws_im_ref[...] = ws_im.astype(BF16)
    ac_r, ac_i = _cpow(ar_r, ai_r, jnp.full((1, p), chunk, jnp.int32), n_bits)
    a_chunk_ref[0:1, :] = ac_r
    a_chunk_ref[1:2, :] = ac_i
    lr_c, li_c = lam_col_ref[:, 0:1], lam_col_ref[:, 1:2]
    ar_c, ai_c = _discretise(lr_c, li_c, dt)
    rep = (lax.broadcasted_iota(jnp.int32, (grp, SSM_ROW), 0)
           == lax.broadcasted_iota(jnp.int32, (grp, SSM_ROW), 1) % grp).astype(F32)
    cx_re = jnp.dot(ct_re_ref[...], rep, precision=HIGHEST, preferred_element_type=F32)
    cx_im = jnp.dot(ct_im_ref[...], rep, precision=HIGHEST, preferred_element_type=F32)
    dx = jnp.dot(d_ref[...], rep, precision=HIGHEST, preferred_element_type=F32)
    col_j = lax.broadcasted_iota(jnp.int32, (p, SSM_ROW), 1) // grp
    ab_r = jnp.broadcast_to(ar_c, (p, SSM_ROW))
    ab_i = jnp.broadcast_to(ai_c, (p, SSM_ROW))
    qr, qi = _cpow(ab_r, ab_i, col_j, n_bits)
    f_re, f_im = _cmul(qr, qi, cx_re, cx_im)
    e_re, e_im = _cmul(f_re, f_im, ab_r, ab_i)
    wc_re_ref[...] = e_re.astype(BF16)
    wc_im_ref[...] = (-e_im).astype(BF16)
    r0 = (jnp.dot(bbt_re, f_re, precision=HIGHEST, preferred_element_type=F32)
          - jnp.dot(bbt_im, f_im, precision=HIGHEST, preferred_element_type=F32))
    lane = lax.broadcasted_iota(jnp.int32, (grp, SSM_ROW), 1)
    sub = lax.broadcasted_iota(jnp.int32, (grp, SSM_ROW), 0)
    for i in range(chunk):
        blk = r0 if i == 0 else pltpu.roll(r0, i * grp, 1)
        blk = jnp.where(lane >= i * grp, blk, 0.0)
        blk = blk + jnp.where(lane == i * grp + sub, dx, 0.0)
        t0_ref[i * grp:(i + 1) * grp, :] = blk.astype(BF16)


def _s5_prep_kernel(*refs, groups):
    for g in range(groups):
        _s5_prep_group(*(ref.at[g] for ref in refs))


def _s5_prepare(lam_re, lam_im, log_step, b_re, b_im, c_re, c_im, d_skip, *, groups_per_step=8):
    g, p = lam_re.shape
    grp = SSM_GROUP
    lam_row = jnp.stack([lam_re, lam_im], axis=1).astype(F32)
    lam_col = jnp.stack([lam_re, lam_im], axis=2).astype(F32)
    dt = log_step.reshape(g, 1, 1).astype(F32)
    bt_re = jnp.swapaxes(b_re, 1, 2).astype(F32)
    bt_im = jnp.swapaxes(b_im, 1, 2).astype(F32)
    ct_re = jnp.swapaxes(c_re, 1, 2).astype(F32)
    ct_im = jnp.swapaxes(c_im, 1, 2).astype(F32)
    d3 = d_skip.reshape(g, 1, grp).astype(F32)

    gs = groups_per_step

    def spec(*shape):
        return pl.BlockSpec((gs,) + shape, lambda i: (i,) + (0,) * len(shape))

    return pl.pallas_call(
        functools.partial(_s5_prep_kernel, groups=gs),
        grid=(g // gs,),
        in_specs=[spec(2, p), spec(p, 2), spec(1, 1), spec(grp, p), spec(grp, p),
                  spec(p, grp), spec(p, grp), spec(1, grp)],
        out_specs=[spec(SSM_ROW, SSM_ROW), spec(SSM_ROW, p), spec(SSM_ROW, p),
                   spec(p, SSM_ROW), spec(p, SSM_ROW), spec(2, p)],
        out_shape=[jax.ShapeDtypeStruct((g, SSM_ROW, SSM_ROW), BF16),
                   jax.ShapeDtypeStruct((g, SSM_ROW, p), BF16),
                   jax.ShapeDtypeStruct((g, SSM_ROW, p), BF16),
                   jax.ShapeDtypeStruct((g, p, SSM_ROW), BF16),
                   jax.ShapeDtypeStruct((g, p, SSM_ROW), BF16),
                   jax.ShapeDtypeStruct((g, 2, p), F32)],
        compiler_params=_params("parallel"),
        name="s5_prepare",
    )(lam_row, lam_col, dt, bt_re, bt_im, ct_re, ct_im, d3)


def _gelu_tanh(y):
    return 0.5 * y * (1.0 + jnp.tanh(math.sqrt(2.0 / math.pi) * (y + 0.044715 * (y * y * y))))


def _s5_scan_kernel(u_ref, t0_ref, ws_re_ref, ws_im_ref, wc_re_ref, wc_im_ref, a_ref, o_ref,
                    sre_ref, sim_ref, hre_ref, him_ref, *, groups, bsz, n_chunks):
    for g in range(groups):
        u = u_ref[g]
        sre_ref[g] = _dot(u, ws_re_ref[g])
        sim_ref[g] = _dot(u, ws_im_ref[g])
    ar = a_ref[:, 0:1, :]
    ai = a_ref[:, 1:2, :]

    def step(k, carry):
        hr, hi = carry
        rows = pl.ds(pl.multiple_of(k * bsz, bsz), bsz)
        hre_ref[:, rows, :] = hr
        him_ref[:, rows, :] = hi
        sr = sre_ref[:, rows, :]
        si = sim_ref[:, rows, :]
        return ar * hr - ai * hi + sr, ar * hi + ai * hr + si

    zero = jnp.zeros((groups, bsz, SSM_STATE), F32)
    lax.fori_loop(0, n_chunks, step, (zero, zero))
    for g in range(groups):
        y = (_dot(u_ref[g], t0_ref[g]) + _dot(hre_ref[g].astype(BF16), wc_re_ref[g])
             + _dot(him_ref[g].astype(BF16), wc_im_ref[g]))
        o_ref[g] = _gelu_tanh(y).astype(o_ref.dtype)


def _s5_scan_gelu(u_rows, mats, *, bsz, groups_per_step=8):
    t0, ws_re, ws_im, wc_re, wc_im, a_chunk = mats
    g, rows, _ = u_rows.shape
    gs = groups_per_step
    p = SSM_STATE
    assert g % gs == 0 and rows % bsz == 0 and bsz % 8 == 0, (u_rows.shape, bsz, gs)

    def spec(*shape):
        return pl.BlockSpec((gs,) + shape, lambda i: (i,) + (0,) * len(shape))

    return pl.pallas_call(
        functools.partial(_s5_scan_kernel, groups=gs, bsz=bsz, n_chunks=rows // bsz),
        grid=(g // gs,),
        in_specs=[spec(rows, SSM_ROW), spec(SSM_ROW, SSM_ROW), spec(SSM_ROW, p), spec(SSM_ROW, p),
                  spec(p, SSM_ROW), spec(p, SSM_ROW), spec(2, p)],
        out_specs=spec(rows, SSM_ROW),
        out_shape=jax.ShapeDtypeStruct((g, rows, SSM_ROW), BF16),
        scratch_shapes=[pltpu.VMEM((gs, rows, p), F32), pltpu.VMEM((gs, rows, p), F32),
                        pltpu.VMEM((gs, rows, p), F32), pltpu.VMEM((gs, rows, p), F32)],
        compiler_params=_params("parallel"),
        name="s5_scan_gelu",
    )(u_rows, t0, ws_re, ws_im, wc_re, wc_im, a_chunk)


def _sigmoid(x):
    return 1.0 / (1.0 + jnp.exp(-x))


def _block_transpose(xs):
    n = GROUPS_PER_SLAB
    width = xs[0].shape[1]
    blk = (lax.broadcasted_iota(jnp.int32, (1, width), 1) // SSM_GROUP) % n
    xs = list(xs)
    d = n // 2
    while d >= 1:
        high = (blk & d) != 0
        nxt = list(xs)
        for i in range(n):
            if i & d:
                continue
            lo_arr, hi_arr = xs[i], xs[i + d]
            nxt[i] = jnp.where(high, pltpu.roll(hi_arr, SSM_GROUP * d, 1), lo_arr)
            nxt[i + d] = jnp.where(high, hi_arr, pltpu.roll(lo_arr, width - SSM_GROUP * d, 1))
        xs = nxt
        d //= 2
    return xs


def _u_rows_kernel(x_ref, w_ref, o_ref, f_ref, stage_ref, *, bsz, chunks):
    tb = x_ref.shape[1]
    uf = _dot(x_ref[...].reshape(bsz * tb, x_ref.shape[2]).astype(BF16), w_ref[...])
    n_slabs = o_ref.shape[0] // GROUPS_PER_SLAB
    u = uf[:, :n_slabs * V7X_LANES]
    f_ref[...] = uf[:, n_slabs * V7X_LANES:].reshape(f_ref.shape)
    for v in range(n_slabs):
        for b in range(bsz):
            stage_ref[v, b * STAGE_PITCH:b * STAGE_PITCH + tb, :] = (
                u[b * tb:(b + 1) * tb, v * V7X_LANES:(v + 1) * V7X_LANES])
    n = GROUPS_PER_SLAB
    for jh in range(SSM_CHUNK // n):
        xs = []
        for jl in range(n):
            j = jh * n + jl
            xs.append(jnp.concatenate(
                [jnp.concatenate([stage_ref.at[v][pl.ds(SSM_CHUNK * c + j, bsz, stride=STAGE_PITCH), :]
                                  for v in range(n_slabs)], axis=1) for c in range(chunks)], axis=0))
        ys = _block_transpose(xs)
        for e in range(n):
            for v in range(n_slabs):
                o_ref[n * v + e, :, jh * V7X_LANES:(jh + 1) * V7X_LANES] = (
                    ys[e][:, v * V7X_LANES:(v + 1) * V7X_LANES].astype(o_ref.dtype))


def _u_projection_rows(x, w_uf, *, chunks=8):
    bsz, s_len, d = x.shape
    width = w_uf.shape[1] - V7X_LANES
    g = width // SSM_GROUP
    tb = chunks * SSM_CHUNK
    rows = chunks * bsz
    assert bsz % 8 == 0 and s_len % tb == 0 and tb <= STAGE_PITCH and g % GROUPS_PER_SLAB == 0, x.shape
    return pl.pallas_call(
        functools.partial(_u_rows_kernel, bsz=bsz, chunks=chunks),
        grid=(s_len // tb,),
        in_specs=[pl.BlockSpec((bsz, tb, d), lambda i: (0, i, 0)),
                  pl.BlockSpec((d, width + V7X_LANES), lambda i: (0, 0), pipeline_mode=pl.Buffered(1))],
        out_specs=[pl.BlockSpec((g, rows, SSM_ROW), lambda i: (0, i, 0)),
                   pl.BlockSpec((bsz, tb, V7X_LANES), lambda i: (0, i, 0))],
        out_shape=[jax.ShapeDtypeStruct((g, (s_len // SSM_CHUNK) * bsz, SSM_ROW), BF16),
                   jax.ShapeDtypeStruct((bsz, s_len, V7X_LANES), F32)],
        scratch_shapes=[pltpu.VMEM((width // V7X_LANES, bsz * STAGE_PITCH, V7X_LANES), F32)],
        compiler_params=_params("parallel"),
        name="u_projection_rows",
    )(x, w_uf)


def _glu_rows_kernel(gy_ref, w_ref, o_ref, stage_ref, *, bsz, chunks):
    tb = o_ref.shape[1]
    n = GROUPS_PER_SLAB
    n_slabs = gy_ref.shape[0] // n
    for jh in range(SSM_CHUNK // n):
        ys = [jnp.concatenate([gy_ref[n * v + e, :, jh * V7X_LANES:(jh + 1) * V7X_LANES].astype(F32)
                               for v in range(n_slabs)], axis=1) for e in range(n)]
        xs = _block_transpose(ys)
        for jl in range(n):
            j = jh * n + jl
            for c in range(chunks):
                for v in range(n_slabs):
                    stage_ref.at[v][pl.ds(SSM_CHUNK * c + j, bsz, stride=STAGE_PITCH), :] = (
                        xs[jl][c * bsz:(c + 1) * bsz, v * V7X_LANES:(v + 1) * V7X_LANES])
    a = jnp.concatenate(
        [jnp.concatenate([stage_ref[v, b * STAGE_PITCH:b * STAGE_PITCH + tb, :] for v in range(n_slabs)], axis=1)
         for b in range(bsz)], axis=0).astype(BF16)
    z = _dot(a, w_ref[...])
    half = z.shape[1] // 2
    o_ref[...] = (z[:, :half] * _sigmoid(z[:, half:])).astype(o_ref.dtype).reshape(o_ref.shape)


def _glu_from_rows(gy_rows, w_glu, *, bsz, chunks=8):
    g, total_rows, _ = gy_rows.shape
    width = g * SSM_GROUP
    n_out = w_glu.shape[1] // 2
    s_len = total_rows // bsz * SSM_CHUNK
    tb = chunks * SSM_CHUNK
    rows = chunks * bsz
    assert bsz % 8 == 0 and s_len % tb == 0 and tb <= STAGE_PITCH and g % GROUPS_PER_SLAB == 0, gy_rows.shape
    return pl.pallas_call(
        functools.partial(_glu_rows_kernel, bsz=bsz, chunks=chunks),
        grid=(s_len // tb,),
        in_specs=[pl.BlockSpec((g, rows, SSM_ROW), lambda i: (0, i, 0)),
                  pl.BlockSpec((width, 2 * n_out), lambda i: (0, 0), pipeline_mode=pl.Buffered(1))],
        out_specs=pl.BlockSpec((bsz, tb, n_out), lambda i: (0, i, 0)),
        out_shape=jax.ShapeDtypeStruct((bsz, s_len, n_out), BF16),
        scratch_shapes=[pltpu.VMEM((width // V7X_LANES, bsz * STAGE_PITCH, V7X_LANES), F32)],
        compiler_params=_params("parallel"),
        name="glu_from_rows",
    )(gy_rows, w_glu)


def _ffn_up_kernel(x_ref, wg_ref, wv_ref, cwg_ref, cbg_ref, cwv_ref, cbv_ref, o_ref,
                   wg16_ref, wv16_ref, hg_ref, hv_ref, *, blocks_per_seq):
    i = pl.program_id(1)
    bm, bn = o_ref.shape
    pad = hg_ref.shape[0] - bm

    @pl.when(i == 0)
    def _():
        wg16_ref[...] = wg_ref[0].astype(BF16)
        wv16_ref[...] = wv_ref[0].astype(BF16)

    @pl.when(i % blocks_per_seq == 0)
    def _():
        hg_ref[0:pad, :] = jnp.zeros((pad, bn), F32)
        hv_ref[0:pad, :] = jnp.zeros((pad, bn), F32)

    @pl.when(i % blocks_per_seq != 0)
    def _():
        hg_ref[0:pad, :] = hg_ref[bm:bm + pad, :]
        hv_ref[0:pad, :] = hv_ref[bm:bm + pad, :]

    x = x_ref[...]
    conv = []
    for w16, h_ref, cw_ref, cb_ref in ((wg16_ref, hg_ref, cwg_ref, cbg_ref), (wv16_ref, hv_ref, cwv_ref, cbv_ref)):
        h = _dot(x, w16[...])
        h_ref[pad:pad + bm, :] = h
        cw = cw_ref[0]
        conv.append(cb_ref[0] + cw[0:1, :] * h_ref[pad - 2:pad - 2 + bm, :]
                    + cw[1:2, :] * h_ref[pad - 1:pad - 1 + bm, :] + cw[2:3, :] * h)
    gate, val = conv
    o_ref[...] = (gate * _sigmoid(gate) * val).astype(o_ref.dtype)


def _ffn_up(x16, layer, w_up, conv_w, conv_b, *, col0, n_cols, seq_len, bm, bn):
    m, d = x16.shape
    f = w_up.shape[2] // 2
    assert m % bm == 0 and seq_len % bm == 0 and n_cols % bn == 0 and bn % V7X_LANES == 0, (m, bm, n_cols, bn)
    pad = 8
    cb = conv_b.reshape(conv_b.shape[0], 1, 2 * f)
    gate_col = lambda j: pl.multiple_of(col0 + j * bn, V7X_LANES)
    val_col = lambda j: pl.multiple_of(f + col0 + j * bn, V7X_LANES)
    window = lambda rows: (pl.Element(1), pl.Element(rows), pl.Element(bn))
    gate_spec = lambda rows: pl.BlockSpec(window(rows), lambda j, i: (layer, 0, gate_col(j)))
    val_spec = lambda rows: pl.BlockSpec(window(rows), lambda j, i: (layer, 0, val_col(j)))
    return pl.pallas_call(
        functools.partial(_ffn_up_kernel, blocks_per_seq=seq_len // bm),
        grid=(n_cols // bn, m // bm),
        in_specs=[pl.BlockSpec((bm, d), lambda j, i: (i, 0)),
                  gate_spec(d), val_spec(d), gate_spec(CONV_WIDTH), gate_spec(1),
                  val_spec(CONV_WIDTH), val_spec(1)],
        out_specs=pl.BlockSpec((bm, bn), lambda j, i: (i, j)),
        out_shape=jax.ShapeDtypeStruct((m, n_cols), BF16),
        scratch_shapes=[pltpu.VMEM((d, bn), BF16), pltpu.VMEM((d, bn), BF16),
                        pltpu.VMEM((pad + bm, bn), F32), pltpu.VMEM((pad + bm, bn), F32)],
        compiler_params=_params("arbitrary", "arbitrary"),
        name="ffn_up_conv_gate",
    )(x16, w_up, w_up, conv_w, cb, conv_w, cb)


def _cast_kernel(src_ref, dst_ref):
    dst_ref[...] = src_ref[...].astype(dst_ref.dtype)


def _layer_to_bf16(stacked, layer, *, steps=8):
    _, k, d = stacked.shape
    assert k % (16 * steps) == 0, (k, steps)
    rows = k // steps
    return pl.pallas_call(
        _cast_kernel,
        grid=(steps,),
        in_specs=[pl.BlockSpec((None, rows, d), lambda r: (layer, r, 0))],
        out_specs=pl.BlockSpec((rows, d), lambda r: (r, 0)),
        out_shape=jax.ShapeDtypeStruct((k, d), BF16),
        compiler_params=_params("parallel"),
        name="layer_to_bf16",
    )(stacked)


def _conv_ffn(x32, x16, layer, w_up, conv_w, conv_b, w_down, ln_g, ln_b, *, seq_len, bn=512):
    f = w_down.shape[1]
    w_up, conv_w, conv_b = w_up.astype(F32), conv_w.astype(F32), conv_b.astype(F32)
    w_down16 = _layer_to_bf16(w_down, layer)
    main = (f // bn) * bn
    pairs = []
    for col0, n_cols, width in ((0, main, bn), (main, f - main, f - main)):
        if n_cols:
            act = _ffn_up(x16, layer, w_up, conv_w, conv_b, col0=col0, n_cols=n_cols, seq_len=seq_len,
                          bm=1024, bn=width)
            pairs.append((act, w_down16, col0))
    return _matmul_residual_layernorm(pairs, x32, ln_g, ln_b, bm=256)


def _rope_proj_kernel(a_ref, w_ref, pos_ref, o_ref, cos_ref, sin_ref, *, rope_cols, q_tiles, q_scale):
    j = pl.program_id(1)
    hd, half = SWA_HEAD_DIM, ROPE_DIM // 2

    @pl.when(j == 0)
    def _():
        lane = lax.broadcasted_iota(jnp.int32, (1, V7X_LANES), 1) % hd
        idx = (lane % half).astype(F32)
        inv_freq = jnp.where(lane < ROPE_DIM, jnp.exp(idx * (-math.log(ROPE_THETA) / half)), 0.0)
        ang = pos_ref[...].astype(F32) * inv_freq
        sign = jnp.where(lane < half, -1.0, 1.0)
        cos_ref[...] = jnp.cos(ang)
        sin_ref[...] = jnp.sin(ang) * sign

    acc = _dot(a_ref[...], w_ref[...])
    bm, bn = acc.shape

    reps = bn // V7X_LANES
    cos = jnp.concatenate([cos_ref[...]] * reps, axis=1)
    sin = jnp.concatenate([sin_ref[...]] * reps, axis=1)
    lane = lax.broadcasted_iota(jnp.int32, (1, bn), 1)
    first = (lane & (hd - 1)) < half
    partner = jnp.where(first, pltpu.roll(acc, bn - half, 1), pltpu.roll(acc, half, 1))
    rot = acc * cos + partner * sin
    col = lane + j * bn
    out = jnp.where(col < rope_cols, rot, acc) * jnp.where(col < q_tiles * bn, q_scale, 1.0)
    o_ref[...] = out.astype(o_ref.dtype)


def _rope_projection(x16, w, pos_col, *, rope_cols, q_cols, q_scale, bm, bn):
    m, k = x16.shape
    n = w.shape[1]
    assert m % bm == 0 and n % bn == 0 and q_cols % bn == 0 and bn % V7X_LANES == 0, (m, n, bm, bn)
    return pl.pallas_call(
        functools.partial(_rope_proj_kernel, rope_cols=rope_cols, q_tiles=q_cols // bn,
                          q_scale=q_scale),
        grid=(m // bm, n // bn),
        in_specs=[pl.BlockSpec((bm, k), lambda i, j: (i, 0)),
                  pl.BlockSpec((k, bn), lambda i, j: (0, j)),
                  pl.BlockSpec((bm, 1), lambda i, j: (i, 0))],
        out_specs=pl.BlockSpec((bm, bn), lambda i, j: (i, j)),
        out_shape=jax.ShapeDtypeStruct((m, n), BF16),
        scratch_shapes=[pltpu.VMEM((bm, V7X_LANES), F32), pltpu.VMEM((bm, V7X_LANES), F32)],
        compiler_params=_params("parallel", "arbitrary"),
        name="rope_projection",
    )(x16, w, pos_col)


def _swa_kernel(sink_ref, q_ref, kp_ref, kc_ref, vp_ref, vc_ref, o_ref):
    n = pl.program_id(1)
    w, hd, lanes = SWA_WINDOW, SWA_HEAD_DIM, V7X_LANES
    pairs = SWA_GROUPS // 2
    qi = lax.broadcasted_iota(jnp.int32, (w, 2 * w), 0)
    kj = lax.broadcasted_iota(jnp.int32, (w, 2 * w), 1)
    rel = w + qi - kj
    valid = (rel >= 0) & (rel < w) & ((n > 0) | (kj >= w))
    bias = jnp.where(valid, 0.0, MASK_VALUE)
    lane_kv = lax.broadcasted_iota(jnp.int32, (2 * w, lanes), 1)
    lane_o = lax.broadcasted_iota(jnp.int32, (w, lanes), 1)

    def both_halves(prev_ref, cur_ref, hk):
        slab = slice((hk // 2) * lanes, (hk // 2 + 1) * lanes)
        x = jnp.concatenate([prev_ref[:, slab], cur_ref[:, slab]], axis=0).astype(F32)
        upper = hk % 2 == 1
        own = jnp.where((lane_kv >= hd) if upper else (lane_kv < hd), x, 0.0)
        other = pltpu.roll(own, hd, 1)
        lo, hi = (other, own) if upper else (own, other)
        return jnp.concatenate([lo, hi], axis=0).astype(BF16)

    for hk in range(SWA_KV_HEADS):
        kk = both_halves(kp_ref, kc_ref, hk)
        vv = both_halves(vp_ref, vc_ref, hk)
        slabs = [slice((hk * pairs + r) * lanes, (hk * pairs + r + 1) * lanes) for r in range(pairs)]
        qs = jnp.concatenate([q_ref[:, sl] for sl in slabs], axis=0)
        s_all = _dot_nt(qs, kk)
        probs, inv_den = [], []
        for r in range(pairs):
            halves, inv = [], []
            for half in range(2):
                sink = sink_ref[hk * SWA_GROUPS + 2 * r + half] * LOG2E
                s = s_all[r * w:(r + 1) * w, half * 2 * w:(half + 1) * 2 * w] + bias
                m = jnp.maximum(jnp.max(s, axis=1, keepdims=True), sink)
                p = jnp.exp2(s - m)
                inv.append(1.0 / (jnp.sum(p, axis=1, keepdims=True) + jnp.exp2(sink - m)))
                halves.append(p.astype(BF16))
            probs.append(jnp.concatenate(halves, axis=1))
            inv_den.append(jnp.where(lane_o < hd, inv[0], inv[1]))
        o_all = _dot(jnp.concatenate(probs, axis=0), vv)
        for r in range(pairs):
            o_ref[:, slabs[r]] = (o_all[r * w:(r + 1) * w] * inv_den[r]).astype(o_ref.dtype)


def _sliding_window_attention(qkv, sinks):
    bsz, s_len, _ = qkv.shape
    w, hd = SWA_WINDOW, SWA_HEAD_DIM
    assert s_len % w == 0, (s_len, w)
    qw = SWA_HEADS * hd
    kw = SWA_KV_HEADS * hd
    k_blk = qw // kw
    v_blk = k_blk + 1
    prev = lambda n: jnp.maximum(n - 1, 0)
    return pl.pallas_call(
        _swa_kernel,
        grid=(bsz, s_len // w),
        in_specs=[pl.BlockSpec(memory_space=pltpu.SMEM),
                  pl.BlockSpec((None, w, qw), lambda b, n: (b, n, 0)),
                  pl.BlockSpec((None, w, kw), lambda b, n: (b, prev(n), k_blk)),
                  pl.BlockSpec((None, w, kw), lambda b, n: (b, n, k_blk)),
                  pl.BlockSpec((None, w, kw), lambda b, n: (b, prev(n), v_blk)),
                  pl.BlockSpec((None, w, kw), lambda b, n: (b, n, v_blk))],
        out_specs=pl.BlockSpec((None, w, qw), lambda b, n: (b, n, 0)),
        out_shape=jax.ShapeDtypeStruct((bsz, s_len, qw), BF16),
        compiler_params=_params("parallel", "arbitrary"),
        name="sliding_window_attention",
    )(sinks.astype(F32), qkv, qkv, qkv, qkv, qkv)


def _even_mixer(x32, xa, bsz, s_len, w_in, b_f, lam_re, lam_im, log_step, b_re, b_im, c_re, c_im,
                d_skip, w_glu, w_out, ln_g, ln_b):
    m = x32.shape[0]
    fw = FOX_WIDTH
    nh = FOX_HEADS
    q_factor = LOG2E / math.sqrt(FOX_HEAD_DIM)
    w_qkv = jnp.concatenate([w_in[:, :fw] * q_factor, w_in[:, fw:3 * fw]], axis=1).astype(BF16)
    w_uf = jnp.concatenate([w_in[:, 3 * fw + nh:],
                            jnp.pad(w_in[:, 3 * fw:3 * fw + nh], ((0, 0), (0, V7X_LANES - nh)))],
                           axis=1).astype(BF16)
    bias_f = jnp.pad(b_f, (0, V7X_LANES - nh)).reshape(1, V7X_LANES).astype(F32)

    qkv = _matmul(xa, w_qkv, bm=1024, bn=3 * fw // 2, out_dtype=BF16)
    mats = _s5_prepare(lam_re, lam_im, log_step, b_re, b_im, c_re, c_im, d_skip)
    u_rows, f_logits = _u_projection_rows(xa, w_uf)
    c_col = _forget_gate_cumsum(f_logits, bias_f)
    fox = _forgetting_attention(qkv, c_col).reshape(m, fw)
    gy_rows = _s5_scan_gelu(u_rows, mats, bsz=bsz)
    ssm = _glu_from_rows(gy_rows, w_glu.astype(BF16), bsz=bsz).reshape(m, -1)
    w_out16 = w_out.astype(BF16)
    return _matmul_residual_layernorm([(fox, w_out16, 0), (ssm, w_out16, fw)], x32, ln_g, ln_b, bm=512)


def _odd_mixer(x32, x16, bsz, s_len, positions, w_in, sinks, w_out, ln_g, ln_b):
    m = x16.shape[0]
    rope_cols = (SWA_HEADS + SWA_KV_HEADS) * SWA_HEAD_DIM
    qkv = _rope_projection(x16, w_in.astype(BF16), positions.reshape(m, 1).astype(jnp.int32),
                           rope_cols=rope_cols, q_cols=SWA_HEADS * SWA_HEAD_DIM,
                           q_scale=LOG2E / math.sqrt(SWA_HEAD_DIM), bm=1024, bn=512)
    o = _sliding_window_attention(qkv.reshape(bsz, s_len, -1), sinks).reshape(m, -1)
    return _matmul_residual_layernorm([(o, w_out.astype(BF16), 0)], x32, ln_g, ln_b, bm=512)


def kernel(x, positions, ev_w_in, ev_b_f, ev_lambda_re, ev_lambda_im, ev_log_step, ev_ssm_b_re,
           ev_ssm_b_im, ev_ssm_c_re, ev_ssm_c_im, ev_ssm_d, ev_w_glu, ev_w_out, od_w_in, od_sinks,
           od_w_out, ln_mix_g, ln_mix_b, ffn_w_up, ffn_conv_w, ffn_conv_b, ffn_w_down, ln_ffn_g,
           ln_ffn_b):
    bsz, s_len, d = x.shape
    x32 = x.reshape(bsz * s_len, d).astype(F32)
    x16 = None
    for i in range(DEPTH):
        j = i // 2
        if i % 2 == 0:
            xa = x.astype(F32) if x16 is None else x16.reshape(bsz, s_len, d)
            x32, x16 = _even_mixer(x32, xa, bsz, s_len, ev_w_in[j], ev_b_f[j], ev_lambda_re[j],
                                   ev_lambda_im[j], ev_log_step[j], ev_ssm_b_re[j], ev_ssm_b_im[j],
                                   ev_ssm_c_re[j], ev_ssm_c_im[j], ev_ssm_d[j], ev_w_glu[j],
                                   ev_w_out[j], ln_mix_g[i], ln_mix_b[i])
        else:
            x16 = x32.astype(BF16) if x16 is None else x16
            x32, x16 = _odd_mixer(x32, x16, bsz, s_len, positions, od_w_in[j], od_sinks[j],
                                  od_w_out[j], ln_mix_g[i], ln_mix_b[i])
        x32, x16 = _conv_ffn(x32, x16, i, ffn_w_up, ffn_conv_w, ffn_conv_b, ffn_w_down,
                             ln_ffn_g[i], ln_ffn_b[i], seq_len=s_len)
    return x32.reshape(bsz, s_len, d).astype(x.dtype)
```

```python
import functools
import math

import jax
import jax.numpy as jnp
from jax import lax
from jax.experimental import pallas as pl
from jax.experimental.pallas import tpu as pltpu

F32 = jnp.float32
BF16 = jnp.bfloat16

DEPTH = 2
FOX_HEADS = 8
FOX_HEAD_DIM = 128
FOX_WIDTH = FOX_HEADS * FOX_HEAD_DIM
SSM_GROUP = 16
SSM_STATE = 64
SSM_CHUNK = 16
SSM_ROW = SSM_CHUNK * SSM_GROUP
SWA_HEADS = 32
SWA_KV_HEADS = 4
SWA_HEAD_DIM = 64
SWA_GROUPS = SWA_HEADS // SWA_KV_HEADS
SWA_WINDOW = 128
ROPE_DIM = SWA_HEAD_DIM // 4
ROPE_THETA = 500000.0
CONV_WIDTH = 3
LN_EPS = 1e-5
DEEPNORM_ALPHA = (2.0 * DEPTH) ** 0.25

V7X_LANES = 128
V7X_VMEM_LIMIT_BYTES = 56 * 1024 * 1024
GROUPS_PER_SLAB = V7X_LANES // SSM_GROUP
STAGE_PITCH = 136
MASK_VALUE = -1e30
LOG2E = math.log2(math.e)

HIGHEST = lax.Precision.HIGHEST


def _params(*semantics):
    return pltpu.CompilerParams(dimension_semantics=semantics,
                                vmem_limit_bytes=V7X_VMEM_LIMIT_BYTES)


def _dot(a, b):
    return jnp.dot(a, b, preferred_element_type=F32)


def _dot_nt(a, b):
    return lax.dot_general(a, b, (((1,), (1,)), ((), ())), preferred_element_type=F32)


def _mm_kernel(a_ref, w_ref, o_ref):
    o_ref[...] = _dot(a_ref[...].astype(BF16), w_ref[...]).astype(o_ref.dtype)


def _matmul(a, w, *, bm, bn, out_dtype):
    bsz, s_len, k = a.shape
    n = w.shape[1]
    assert s_len % bm == 0 and n % bn == 0, (a.shape, w.shape, bm, bn)
    per_seq = s_len // bm
    rows = lambda i, j: (i // per_seq, i % per_seq)
    return pl.pallas_call(
        _mm_kernel,
        grid=(bsz * per_seq, n // bn),
        in_specs=[pl.BlockSpec((None, bm, k), lambda i, j: rows(i, j) + (0,)),
                  pl.BlockSpec((k, bn), lambda i, j: (0, j))],
        out_specs=pl.BlockSpec((None, bm, bn), lambda i, j: rows(i, j) + (j,)),
        out_shape=jax.ShapeDtypeStruct((bsz, s_len, n), out_dtype),
        compiler_params=_params("parallel", "arbitrary"),
        name="matmul",
    )(a, w)


def _mm_res_ln_kernel(*refs, n_pairs, sub):
    a_refs = refs[:n_pairs]
    w_refs = refs[n_pairs:2 * n_pairs]
    x_ref, g_ref, b_ref, o32_ref, o16_ref = refs[2 * n_pairs:]
    for r in range(x_ref.shape[0] // sub):
        rows = slice(r * sub, (r + 1) * sub)
        pre = DEEPNORM_ALPHA * x_ref[rows, :]
        for a_ref, w_ref in zip(a_refs, w_refs):
            pre = pre + _dot(a_ref[rows, :], w_ref[...])
        mu = jnp.mean(pre, axis=-1, keepdims=True)
        cen = pre - mu
        var = jnp.mean(cen * cen, axis=-1, keepdims=True)
        y = cen * lax.rsqrt(var + LN_EPS) * g_ref[...] + b_ref[...]
        o32_ref[rows, :] = y
        o16_ref[rows, :] = y.astype(BF16)


def _matmul_residual_layernorm(pairs, x32, gain, bias, *, bm, sub=256):
    m, d = x32.shape
    assert m % bm == 0 and bm % sub == 0, (m, bm, sub)
    n_pairs = len(pairs)
    a_list = [p[0] for p in pairs]
    w_list = [p[1] for p in pairs]
    row_spec = lambda width: pl.BlockSpec((bm, width), lambda i: (i, 0))
    const_spec = lambda rows: pl.BlockSpec((rows, d), lambda i: (0, 0), pipeline_mode=pl.Buffered(1))
    window_spec = lambda rows, row0: pl.BlockSpec((pl.Element(rows), pl.Element(d)), lambda i: (row0, 0),
                                                  pipeline_mode=pl.Buffered(1))
    in_specs = ([row_spec(a.shape[1]) for a in a_list]
                + [window_spec(a.shape[1], row0) for a, _, row0 in pairs]
                + [row_spec(d), const_spec(1), const_spec(1)])
    return pl.pallas_call(
        functools.partial(_mm_res_ln_kernel, n_pairs=n_pairs, sub=sub),
        grid=(m // bm,),
        in_specs=in_specs,
        out_specs=[row_spec(d), row_spec(d)],
        out_shape=[jax.ShapeDtypeStruct((m, d), F32), jax.ShapeDtypeStruct((m, d), BF16)],
        compiler_params=_params("parallel"),
        name="matmul_residual_layernorm",
    )(*a_list, *w_list, x32, gain.reshape(1, d).astype(F32), bias.reshape(1, d).astype(F32))


def _bf16_split3(v):
    hi = v.astype(BF16)
    r1 = v - hi.astype(F32)
    mid = r1.astype(BF16)
    lo = (r1 - mid.astype(F32)).astype(BF16)
    return hi, mid, lo


def _forget_gate_kernel(f_ref, bf_ref, c_ref, *, chunk):
    s_len = f_ref.shape[0]
    f = f_ref[...] + bf_ref[...]
    log_f = jnp.minimum(f, 0.0) - jnp.log(1.0 + jnp.exp(-jnp.abs(f)))
    row = lax.broadcasted_iota(jnp.int32, (chunk, chunk), 0)
    col = lax.broadcasted_iota(jnp.int32, (chunk, chunk), 1)
    tri = (row >= col).astype(BF16)
    carry = jnp.zeros((1, log_f.shape[1]), F32)
    for start in range(0, s_len, chunk):
        hi, mid, lo = _bf16_split3(log_f[start:start + chunk])
        part = _dot(tri, hi) + _dot(tri, mid) + _dot(tri, lo) + carry
        c_ref[start:start + chunk, :] = part
        carry = part[chunk - 1:chunk, :]


def _forget_gate_cumsum(f_logits, b_f, *, chunk=256):
    bsz, s_len, n = f_logits.shape
    return pl.pallas_call(
        functools.partial(_forget_gate_kernel, chunk=chunk),
        grid=(bsz,),
        in_specs=[pl.BlockSpec((None, s_len, n), lambda b: (b, 0, 0)),
                  pl.BlockSpec((1, n), lambda b: (0, 0))],
        out_specs=pl.BlockSpec((None, s_len, n), lambda b: (b, 0, 0)),
        out_shape=jax.ShapeDtypeStruct((bsz, s_len, n), F32),
        compiler_params=_params("parallel"),
        name="forget_gate_cumsum",
    )(f_logits, b_f)


def _bias_lanes(c, first):
    hi, mid, lo = (piece.astype(F32) for piece in _bf16_split3(c))
    lane = lax.broadcasted_iota(jnp.int32, (c.shape[0], V7X_LANES), 1)
    ones = jnp.where(lane < 6, 1.0, 0.0)
    return jnp.where(lane == first, hi, jnp.where(lane == first + 1, mid, jnp.where(lane == first + 2, lo, ones)))


def _fox_kernel(q_ref, k_ref, v_ref, c_ref, o_ref, kaug_ref, *, tile, heads):
    dh = FOX_HEAD_DIM
    aug = 2 * dh
    h0 = pl.program_id(1) * heads
    n_tiles = q_ref.shape[0] // tile

    def head_column(c_all, hh):
        lane = lax.broadcasted_iota(jnp.int32, c_all.shape, 1)
        return jnp.sum(jnp.where(lane == h0 + hh, c_all, 0.0), axis=1, keepdims=True) * LOG2E

    c_all = c_ref[...]
    for hh in range(heads):
        kaug_ref[:, hh * aug:hh * aug + dh] = k_ref[:, hh * dh:(hh + 1) * dh]
        kaug_ref[:, hh * aug + dh:(hh + 1) * aug] = _bias_lanes(-head_column(c_all, hh), 3).astype(BF16)

    def update(state, s, v):
        m, l, acc = state
        m_new = jnp.maximum(m, jnp.max(s, axis=1, keepdims=True))
        alpha = jnp.exp2(m - m_new)
        p = jnp.exp2(s - m_new)
        return (m_new, alpha * l + jnp.sum(p, axis=1, keepdims=True), alpha * acc + _dot(p.astype(BF16), v))

    for qt in range(n_tiles):
        rows = slice(qt * tile, (qt + 1) * tile)
        cq_all = c_ref[rows, :]
        for hh in range(heads):
            q_aug = jnp.concatenate([q_ref[rows, hh * dh:(hh + 1) * dh],
                                     _bias_lanes(head_column(cq_all, hh), 0).astype(BF16)], axis=1)
            state = (jnp.full((tile, 1), MASK_VALUE, F32), jnp.zeros((tile, 1), F32), jnp.zeros((tile, dh), F32))
            for kt in range(qt + 1):
                keys = slice(kt * tile, (kt + 1) * tile)
                s = _dot_nt(q_aug, kaug_ref[keys, hh * aug:(hh + 1) * aug])
                if kt == qt:
                    row = lax.broadcasted_iota(jnp.int32, s.shape, 0)
                    col = lax.broadcasted_iota(jnp.int32, s.shape, 1)
                    s = jnp.where(row >= col, s, MASK_VALUE)
                state = update(state, s, v_ref[keys, hh * dh:(hh + 1) * dh])
            _, l, acc = state
            o_ref[rows, hh * dh:(hh + 1) * dh] = (acc / l).astype(o_ref.dtype)


def _forgetting_attention(qkv, c_col, *, tile=1024, heads=2):
    bsz, s_len, _ = qkv.shape
    assert s_len % tile == 0 and FOX_HEADS % heads == 0, (s_len, tile, heads)
    width = heads * FOX_HEAD_DIM
    nblk = FOX_WIDTH // width
    seq_spec = lambda off: pl.BlockSpec((None, s_len, width), lambda b, h: (b, 0, off + h))
    return pl.pallas_call(
        functools.partial(_fox_kernel, tile=tile, heads=heads),
        grid=(bsz, nblk),
        in_specs=[seq_spec(0), seq_spec(nblk), seq_spec(2 * nblk),
                  pl.BlockSpec((None, s_len, c_col.shape[2]), lambda b, h: (b, 0, 0))],
        out_specs=seq_spec(0),
        out_shape=jax.ShapeDtypeStruct((bsz, s_len, FOX_WIDTH), BF16),
        scratch_shapes=[pltpu.VMEM((s_len, 2 * width), BF16)],
        compiler_params=_params("parallel", "parallel"),
        name="forgetting_attention",
    )(qkv, qkv, qkv, c_col)


def _cmul(ar, ai, br, bi):
    return ar * br - ai * bi, ar * bi + ai * br


def _cpow(ar, ai, e, n_bits):
    rr = jnp.ones_like(ar)
    ri = jnp.zeros_like(ai)
    for bit in range(n_bits):
        nr, ni = _cmul(rr, ri, ar, ai)
        take = ((e >> bit) & 1) == 1
        rr = jnp.where(take, nr, rr)
        ri = jnp.where(take, ni, ri)
        if bit + 1 < n_bits:
            ar, ai = _cmul(ar, ai, ar, ai)
    return rr, ri


def _discretise(lr, li, dt):
    mag = jnp.exp(lr * dt)
    return mag * jnp.cos(li * dt), mag * jnp.sin(li * dt)


def _s5_prep_group(lam_row_ref, lam_col_ref, dt_ref, bt_re_ref, bt_im_ref, ct_re_ref, ct_im_ref,
                   d_ref, t0_ref, ws_re_ref, ws_im_ref, wc_re_ref, wc_im_ref, a_chunk_ref):
    chunk, grp, p = SSM_CHUNK, SSM_GROUP, SSM_STATE
    n_bits = chunk.bit_length()
    dt = jnp.exp(dt_ref[...])
    lr_r, li_r = lam_row_ref[0:1, :], lam_row_ref[1:2, :]
    ar_r, ai_r = _discretise(lr_r, li_r, dt)
    den = lr_r * lr_r + li_r * li_r
    xr, xi = ar_r - 1.0, ai_r
    g_re = (xr * lr_r + xi * li_r) / den
    g_im = (xi * lr_r - xr * li_r) / den
    bbt_re, bbt_im = _cmul(g_re, g_im, bt_re_ref[...], bt_im_ref[...])
    bx_re = jnp.concatenate([bbt_re] * chunk, axis=0)
    bx_im = jnp.concatenate([bbt_im] * chunk, axis=0)
    row_i = lax.broadcasted_iota(jnp.int32, (SSM_ROW, p), 0) // grp
    pr, pi = _cpow(jnp.broadcast_to(ar_r, (SSM_ROW, p)), jnp.broadcast_to(ai_r, (SSM_ROW, p)),
                   chunk - 1 - row_i, n_bits)
    ws_re, ws_im = _cmul(pr, pi, bx_re, bx_im)
    ws_re_ref[...] = ws_re.astype(BF16)
    ws_im_ref[...] = ws_im.astype(BF16)
    ac_r, ac_i = _cpow(ar_r, ai_r, jnp.full((1, p), chunk, jnp.int32), n_bits)
    a_chunk_ref[0:1, :] = ac_r
    a_chunk_ref[1:2, :] = ac_i
    lr_c, li_c = lam_col_ref[:, 0:1], lam_col_ref[:, 1:2]
    ar_c, ai_c = _discretise(lr_c, li_c, dt)
    rep = (lax.broadcasted_iota(jnp.int32, (grp, SSM_ROW), 0)
           == lax.broadcasted_iota(jnp.int32, (grp, SSM_ROW), 1) % grp).astype(F32)
    cx_re = jnp.dot(ct_re_ref[...], rep, precision=HIGHEST, preferred_element_type=F32)
    cx_im = jnp.dot(ct_im_ref[...], rep, precision=HIGHEST, preferred_element_type=F32)
    dx = jnp.dot(d_ref[...], rep, precision=HIGHEST, preferred_element_type=F32)
    col_j = lax.broadcasted_iota(jnp.int32, (p, SSM_ROW), 1) // grp
    ab_r = jnp.broadcast_to(ar_c, (p, SSM_ROW))
    ab_i = jnp.broadcast_to(ai_c, (p, SSM_ROW))
    qr, qi = _cpow(ab_r, ab_i, col_j, n_bits)
    f_re, f_im = _cmul(qr, qi, cx_re, cx_im)
    e_re, e_im = _cmul(f_re, f_im, ab_r, ab_i)
    wc_re_ref[...] = e_re.astype(BF16)
    wc_im_ref[...] = (-e_im).astype(BF16)
    r0 = (jnp.dot(bbt_re, f_re, precision=HIGHEST, preferred_element_type=F32)
          - jnp.dot(bbt_im, f_im, precision=HIGHEST, preferred_element_type=F32))
    lane = lax.broadcasted_iota(jnp.int32, (grp, SSM_ROW), 1)
    sub = lax.broadcasted_iota(jnp.int32, (grp, SSM_ROW), 0)
    for i in range(chunk):
        blk = r0 if i == 0 else pltpu.roll(r0, i * grp, 1)
        blk = jnp.where(lane >= i * grp, blk, 0.0)
        blk = blk + jnp.where(lane == i * grp + sub, dx, 0.0)
        t0_ref[i * grp:(i + 1) * grp, :] = blk.astype(BF16)


def _s5_prep_kernel(*refs, groups):
    for g in range(groups):
        _s5_prep_group(*(ref.at[g] for ref in refs))


def _s5_prepare(lam_re, lam_im, log_step, b_re, b_im, c_re, c_im, d_skip, *, groups_per_step=8):
    g, p = lam_re.shape
    grp = SSM_GROUP
    lam_row = jnp.stack([lam_re, lam_im], axis=1).astype(F32)
    lam_col = jnp.stack([lam_re, lam_im], axis=2).astype(F32)
    dt = log_step.reshape(g, 1, 1).astype(F32)
    bt_re = jnp.swapaxes(b_re, 1, 2).astype(F32)
    bt_im = jnp.swapaxes(b_im, 1, 2).astype(F32)
    ct_re = jnp.swapaxes(c_re, 1, 2).astype(F32)
    ct_im = jnp.swapaxes(c_im, 1, 2).astype(F32)
    d3 = d_skip.reshape(g, 1, grp).astype(F32)

    gs = groups_per_step

    def spec(*shape):
        return pl.BlockSpec((gs,) + shape, lambda i: (i,) + (0,) * len(shape))

    return pl.pallas_call(
        functools.partial(_s5_prep_kernel, groups=gs),
        grid=(g // gs,),
        in_specs=[spec(2, p), spec(p, 2), spec(1, 1), spec(grp, p), spec(grp, p),
                  spec(p, grp), spec(p, grp), spec(1, grp)],
        out_specs=[spec(SSM_ROW, SSM_ROW), spec(SSM_ROW, p), spec(SSM_ROW, p),
                   spec(p, SSM_ROW), spec(p, SSM_ROW), spec(2, p)],
        out_shape=[jax.ShapeDtypeStruct((g, SSM_ROW, SSM_ROW), BF16),
                   jax.ShapeDtypeStruct((g, SSM_ROW, p), BF16),
                   jax.ShapeDtypeStruct((g, SSM_ROW, p), BF16),
                   jax.ShapeDtypeStruct((g, p, SSM_ROW), BF16),
                   jax.ShapeDtypeStruct((g, p, SSM_ROW), BF16),
                   jax.ShapeDtypeStruct((g, 2, p), F32)],
        compiler_params=_params("parallel"),
        name="s5_prepare",
    )(lam_row, lam_col, dt, bt_re, bt_im, ct_re, ct_im, d3)


def _gelu_tanh(y):
    return 0.5 * y * (1.0 + jnp.tanh(math.sqrt(2.0 / math.pi) * (y + 0.044715 * (y * y * y))))


def _s5_scan_kernel(u_ref, t0_ref, ws_re_ref, ws_im_ref, wc_re_ref, wc_im_ref, a_ref, o_ref,
                    sre_ref, sim_ref, hre_ref, him_ref, *, groups, bsz, n_chunks):
    for g in range(groups):
        u = u_ref[g]
        sre_ref[g] = _dot(u, ws_re_ref[g])
        sim_ref[g] = _dot(u, ws_im_ref[g])
    ar = a_ref[:, 0:1, :]
    ai = a_ref[:, 1:2, :]

    def step(k, carry):
        hr, hi = carry
        rows = pl.ds(pl.multiple_of(k * bsz, bsz), bsz)
        hre_ref[:, rows, :] = hr
        him_ref[:, rows, :] = hi
        sr = sre_ref[:, rows, :]
        si = sim_ref[:, rows, :]
        return ar * hr - ai * hi + sr, ar * hi + ai * hr + si

    zero = jnp.zeros((groups, bsz, SSM_STATE), F32)
    lax.fori_loop(0, n_chunks, step, (zero, zero))
    for g in range(groups):
        y = (_dot(u_ref[g], t0_ref[g]) + _dot(hre_ref[g].astype(BF16), wc_re_ref[g])
             + _dot(him_ref[g].astype(BF16), wc_im_ref[g]))
        o_ref[g] = _gelu_tanh(y).astype(o_ref.dtype)


def _s5_scan_gelu(u_rows, mats, *, bsz, groups_per_step=8):
    t0, ws_re, ws_im, wc_re, wc_im, a_chunk = mats
    g, rows, _ = u_rows.shape
    gs = groups_per_step
    p = SSM_STATE
    assert g % gs == 0 and rows % bsz == 0 and bsz % 8 == 0, (u_rows.shape, bsz, gs)

    def spec(*shape):
        return pl.BlockSpec((gs,) + shape, lambda i: (i,) + (0,) * len(shape))

    return pl.pallas_call(
        functools.partial(_s5_scan_kernel, groups=gs, bsz=bsz, n_chunks=rows // bsz),
        grid=(g // gs,),
        in_specs=[spec(rows, SSM_ROW), spec(SSM_ROW, SSM_ROW), spec(SSM_ROW, p), spec(SSM_ROW, p),
                  spec(p, SSM_ROW), spec(p, SSM_ROW), spec(2, p)],
        out_specs=spec(rows, SSM_ROW),
        out_shape=jax.ShapeDtypeStruct((g, rows, SSM_ROW), BF16),
        scratch_shapes=[pltpu.VMEM((gs, rows, p), F32), pltpu.VMEM((gs, rows, p), F32),
                        pltpu.VMEM((gs, rows, p), F32), pltpu.VMEM((gs, rows, p), F32)],
        compiler_params=_params("parallel"),
        name="s5_scan_gelu",
    )(u_rows, t0, ws_re, ws_im, wc_re, wc_im, a_chunk)


def _sigmoid(x):
    return 1.0 / (1.0 + jnp.exp(-x))


def _block_transpose(xs):
    n = GROUPS_PER_SLAB
    width = xs[0].shape[1]
    blk = (lax.broadcasted_iota(jnp.int32, (1, width), 1) // SSM_GROUP) % n
    xs = list(xs)
    d = n // 2
    while d >= 1:
        high = (blk & d) != 0
        nxt = list(xs)
        for i in range(n):
            if i & d:
                continue
            lo_arr, hi_arr = xs[i], xs[i + d]
            nxt[i] = jnp.where(high, pltpu.roll(hi_arr, SSM_GROUP * d, 1), lo_arr)
            nxt[i + d] = jnp.where(high, hi_arr, pltpu.roll(lo_arr, width - SSM_GROUP * d, 1))
        xs = nxt
        d //= 2
    return xs


def _u_rows_kernel(x_ref, w_ref, o_ref, f_ref, stage_ref, *, bsz, chunks):
    tb = x_ref.shape[1]
    uf = _dot(x_ref[...].reshape(bsz * tb, x_ref.shape[2]).astype(BF16), w_ref[...])
    n_slabs = o_ref.shape[0] // GROUPS_PER_SLAB
    u = uf[:, :n_slabs * V7X_LANES]
    f_ref[...] = uf[:, n_slabs * V7X_LANES:].reshape(f_ref.shape)
    for v in range(n_slabs):
        for b in range(bsz):
            stage_ref[v, b * STAGE_PITCH:b * STAGE_PITCH + tb, :] = (
                u[b * tb:(b + 1) * tb, v * V7X_LANES:(v + 1) * V7X_LANES])
    n = GROUPS_PER_SLAB
    for jh in range(SSM_CHUNK // n):
        xs = []
        for jl in range(n):
            j = jh * n + jl
            xs.append(jnp.concatenate(
                [jnp.concatenate([stage_ref.at[v][pl.ds(SSM_CHUNK * c + j, bsz, stride=STAGE_PITCH), :]
                                  for v in range(n_slabs)], axis=1) for c in range(chunks)], axis=0))
        ys = _block_transpose(xs)
        for e in range(n):
            for v in range(n_slabs):
                o_ref[n * v + e, :, jh * V7X_LANES:(jh + 1) * V7X_LANES] = (
                    ys[e][:, v * V7X_LANES:(v + 1) * V7X_LANES].astype(o_ref.dtype))


def _u_projection_rows(x, w_uf, *, chunks=8):
    bsz, s_len, d = x.shape
    width = w_uf.shape[1] - V7X_LANES
    g = width // SSM_GROUP
    tb = chunks * SSM_CHUNK
    rows = chunks * bsz
    assert bsz % 8 == 0 and s_len % tb == 0 and tb <= STAGE_PITCH and g % GROUPS_PER_SLAB == 0, x.shape
    return pl.pallas_call(
        functools.partial(_u_rows_kernel, bsz=bsz, chunks=chunks),
        grid=(s_len // tb,),
        in_specs=[pl.BlockSpec((bsz, tb, d), lambda i: (0, i, 0)),
                  pl.BlockSpec((d, width + V7X_LANES), lambda i: (0, 0), pipeline_mode=pl.Buffered(1))],
        out_specs=[pl.BlockSpec((g, rows, SSM_ROW), lambda i: (0, i, 0)),
                   pl.BlockSpec((bsz, tb, V7X_LANES), lambda i: (0, i, 0))],
        out_shape=[jax.ShapeDtypeStruct((g, (s_len // SSM_CHUNK) * bsz, SSM_ROW), BF16),
                   jax.ShapeDtypeStruct((bsz, s_len, V7X_LANES), F32)],
        scratch_shapes=[pltpu.VMEM((width // V7X_LANES, bsz * STAGE_PITCH, V7X_LANES), F32)],
        compiler_params=_params("parallel"),
        name="u_projection_rows",
    )(x, w_uf)


def _glu_rows_kernel(gy_ref, w_ref, o_ref, stage_ref, *, bsz, chunks):
    tb = o_ref.shape[1]
    n = GROUPS_PER_SLAB
    n_slabs = gy_ref.shape[0] // n
    for jh in range(SSM_CHUNK // n):
        ys = [jnp.concatenate([gy_ref[n * v + e, :, jh * V7X_LANES:(jh + 1) * V7X_LANES].astype(F32)
                               for v in range(n_slabs)], axis=1) for e in range(n)]
        xs = _block_transpose(ys)
        for jl in range(n):
            j = jh * n + jl
            for c in range(chunks):
                for v in range(n_slabs):
                    stage_ref.at[v][pl.ds(SSM_CHUNK * c + j, bsz, stride=STAGE_PITCH), :] = (
                        xs[jl][c * bsz:(c + 1) * bsz, v * V7X_LANES:(v + 1) * V7X_LANES])
    a = jnp.concatenate(
        [jnp.concatenate([stage_ref[v, b * STAGE_PITCH:b * STAGE_PITCH + tb, :] for v in range(n_slabs)], axis=1)
         for b in range(bsz)], axis=0).astype(BF16)
    z = _dot(a, w_ref[...])
    half = z.shape[1] // 2
    o_ref[...] = (z[:, :half] * _sigmoid(z[:, half:])).astype(o_ref.dtype).reshape(o_ref.shape)


def _glu_from_rows(gy_rows, w_glu, *, bsz, chunks=8):
    g, total_rows, _ = gy_rows.shape
    width = g * SSM_GROUP
    n_out = w_glu.shape[1] // 2
    s_len = total_rows // bsz * SSM_CHUNK
    tb = chunks * SSM_CHUNK
    rows = chunks * bsz
    assert bsz % 8 == 0 and s_len % tb == 0 and tb <= STAGE_PITCH and g % GROUPS_PER_SLAB == 0, gy_rows.shape
    return pl.pallas_call(
        functools.partial(_glu_rows_kernel, bsz=bsz, chunks=chunks),
        grid=(s_len // tb,),
        in_specs=[pl.BlockSpec((g, rows, SSM_ROW), lambda i: (0, i, 0)),
                  pl.BlockSpec((width, 2 * n_out), lambda i: (0, 0), pipeline_mode=pl.Buffered(1))],
        out_specs=pl.BlockSpec((bsz, tb, n_out), lambda i: (0, i, 0)),
        out_shape=jax.ShapeDtypeStruct((bsz, s_len, n_out), BF16),
        scratch_shapes=[pltpu.VMEM((width // V7X_LANES, bsz * STAGE_PITCH, V7X_LANES), F32)],
        compiler_params=_params("parallel"),
        name="glu_from_rows",
    )(gy_rows, w_glu)


def _ffn_up_kernel(x_ref, wg_ref, wv_ref, cwg_ref, cbg_ref, cwv_ref, cbv_ref, o_ref,
                   wg16_ref, wv16_ref, carry_ref, *, blocks_per_seq, sub):
    i = pl.program_id(1)
    bm, bn = o_ref.shape

    @pl.when(i == 0)
    def _():
        wg16_ref[...] = wg_ref[0].astype(BF16)
        wv16_ref[...] = wv_ref[0].astype(BF16)

    @pl.when(i % blocks_per_seq == 0)
    def _():
        carry_ref[...] = jnp.zeros_like(carry_ref)

    row8 = lax.broadcasted_iota(jnp.int32, (8, bn), 0)
    prev = [carry_ref[t] for t in range(4)]
    cws = (cwg_ref[0], cwv_ref[0])
    cbs = (cbg_ref[0], cbv_ref[0])
    for r in range(bm // sub):
        rows = slice(r * sub, (r + 1) * sub)
        xs = x_ref[rows, :]
        conv, nxt = [], []
        for t, w16 in enumerate((wg16_ref, wv16_ref)):
            h = _dot(xs, w16[...])
            h1 = pltpu.roll(h, 1, 0)
            h2 = pltpu.roll(h, 2, 0)
            hm1 = jnp.concatenate([jnp.where(row8 < 1, prev[2 * t], h1[:8]), h1[8:]], axis=0)
            hm2 = jnp.concatenate([jnp.where(row8 < 2, prev[2 * t + 1], h2[:8]), h2[8:]], axis=0)
            cw = cws[t]
            conv.append(cbs[t] + cw[0:1, :] * hm2 + cw[1:2, :] * hm1 + cw[2:3, :] * h)
            nxt += [h1[:8], h2[:8]]
        prev = nxt
        gate, val = conv
        o_ref[rows, :] = (gate * _sigmoid(gate) * val).astype(o_ref.dtype)
    for t in range(4):
        carry_ref[t] = prev[t]


def _ffn_up(x16, layer, w_up, conv_w, conv_b, *, col0, n_cols, seq_len, bm, bn, sub=128):
    m, d = x16.shape
    f = w_up.shape[2] // 2
    assert m % bm == 0 and seq_len % bm == 0 and n_cols % bn == 0 and bn % V7X_LANES == 0, (m, bm, n_cols, bn)
    assert bm % sub == 0 and sub % 8 == 0, (bm, sub)
    cb = conv_b.reshape(conv_b.shape[0], 1, 2 * f)
    gate_col = lambda j: pl.multiple_of(col0 + j * bn, V7X_LANES)
    val_col = lambda j: pl.multiple_of(f + col0 + j * bn, V7X_LANES)
    window = lambda rows: (pl.Element(1), pl.Element(rows), pl.Element(bn))
    gate_spec = lambda rows: pl.BlockSpec(window(rows), lambda j, i: (layer, 0, gate_col(j)))
    val_spec = lambda rows: pl.BlockSpec(window(rows), lambda j, i: (layer, 0, val_col(j)))
    return pl.pallas_call(
        functools.partial(_ffn_up_kernel, blocks_per_seq=seq_len // bm, sub=sub),
        grid=(n_cols // bn, m // bm),
        in_specs=[pl.BlockSpec((bm, d), lambda j, i: (i, 0)),
                  gate_spec(d), val_spec(d), gate_spec(CONV_WIDTH), gate_spec(1),
                  val_spec(CONV_WIDTH), val_spec(1)],
        out_specs=pl.BlockSpec((bm, bn), lambda j, i: (i, j)),
        out_shape=jax.ShapeDtypeStruct((m, n_cols), BF16),
        scratch_shapes=[pltpu.VMEM((d, bn), BF16), pltpu.VMEM((d, bn), BF16),
                        pltpu.VMEM((4, 8, bn), F32)],
        compiler_params=_params("arbitrary", "arbitrary"),
        name="ffn_up_conv_gate",
    )(x16, w_up, w_up, conv_w, cb, conv_w, cb)


def _cast_kernel(src_ref, dst_ref):
    dst_ref[...] = src_ref[...].astype(dst_ref.dtype)


def _layer_to_bf16(stacked, layer, *, steps=8):
    _, k, d = stacked.shape
    assert k % (16 * steps) == 0, (k, steps)
    rows = k // steps
    return pl.pallas_call(
        _cast_kernel,
        grid=(steps,),
        in_specs=[pl.BlockSpec((None, rows, d), lambda r: (layer, r, 0))],
        out_specs=pl.BlockSpec((rows, d), lambda r: (r, 0)),
        out_shape=jax.ShapeDtypeStruct((k, d), BF16),
        compiler_params=_params("parallel"),
        name="layer_to_bf16",
    )(stacked)


def _conv_ffn(x32, x16, layer, w_up, conv_w, conv_b, w_down, ln_g, ln_b, *, seq_len, bn=1024):
    f = w_down.shape[1]
    w_up, conv_w, conv_b = w_up.astype(F32), conv_w.astype(F32), conv_b.astype(F32)
    w_down16 = _layer_to_bf16(w_down, layer)
    main = (f // bn) * bn
    pairs = []
    for col0, n_cols, width in ((0, main, bn), (main, f - main, f - main)):
        if n_cols:
            act = _ffn_up(x16, layer, w_up, conv_w, conv_b, col0=col0, n_cols=n_cols, seq_len=seq_len,
                          bm=1024, bn=width)
            pairs.append((act, w_down16, col0))
    return _matmul_residual_layernorm(pairs, x32, ln_g, ln_b, bm=256)


def _rope_proj_kernel(a_ref, w_ref, pos_ref, o_ref, *, rope_cols, q_cols, q_scale, sub):
    hd, half = SWA_HEAD_DIM, ROPE_DIM // 2
    bm, n = o_ref.shape
    lane1 = lax.broadcasted_iota(jnp.int32, (1, V7X_LANES), 1) % hd
    inv_freq = jnp.where(lane1 < ROPE_DIM,
                         jnp.exp((lane1 % half).astype(F32) * (-math.log(ROPE_THETA) / half)), 0.0)
    sign = jnp.where(lane1 < half, -1.0, 1.0)
    lane = lax.broadcasted_iota(jnp.int32, (1, n), 1)
    first = (lane & (hd - 1)) < half
    reps = n // V7X_LANES
    for r in range(bm // sub):
        rows = slice(r * sub, (r + 1) * sub)
        ang = pos_ref[rows, :].astype(F32) * inv_freq
        cos = jnp.concatenate([jnp.cos(ang)] * reps, axis=1)
        sin = jnp.concatenate([jnp.sin(ang) * sign] * reps, axis=1)
        acc = _dot(a_ref[rows, :], w_ref[...])
        partner = jnp.where(first, pltpu.roll(acc, n - half, 1), pltpu.roll(acc, half, 1))
        rot = acc * cos + partner * sin
        out = jnp.where(lane < rope_cols, rot, acc) * jnp.where(lane < q_cols, q_scale, 1.0)
        o_ref[rows, :] = out.astype(o_ref.dtype)


def _rope_projection(x16, w, pos_col, *, rope_cols, q_cols, q_scale, bm, sub=256):
    m, k = x16.shape
    n = w.shape[1]
    assert m % bm == 0 and bm % sub == 0 and n % V7X_LANES == 0, (m, n, bm, sub)
    return pl.pallas_call(
        functools.partial(_rope_proj_kernel, rope_cols=rope_cols, q_cols=q_cols, q_scale=q_scale, sub=sub),
        grid=(m // bm,),
        in_specs=[pl.BlockSpec((bm, k), lambda i: (i, 0)),
                  pl.BlockSpec((k, n), lambda i: (0, 0), pipeline_mode=pl.Buffered(1)),
                  pl.BlockSpec((bm, 1), lambda i: (i, 0))],
        out_specs=pl.BlockSpec((bm, n), lambda i: (i, 0)),
        out_shape=jax.ShapeDtypeStruct((m, n), BF16),
        compiler_params=_params("parallel"),
        name="rope_projection",
    )(x16, w, pos_col)


def _swa_kernel(sink_ref, q_ref, kp_ref, kc_ref, vp_ref, vc_ref, o_ref):
    n = pl.program_id(1)
    w, hd, lanes = SWA_WINDOW, SWA_HEAD_DIM, V7X_LANES
    pairs = SWA_GROUPS // 2
    qi = lax.broadcasted_iota(jnp.int32, (w, 2 * w), 0)
    kj = lax.broadcasted_iota(jnp.int32, (w, 2 * w), 1)
    rel = w + qi - kj
    valid = (rel >= 0) & (rel < w) & ((n > 0) | (kj >= w))
    bias = jnp.where(valid, 0.0, MASK_VALUE)
    lane_kv = lax.broadcasted_iota(jnp.int32, (2 * w, lanes), 1)
    lane_o = lax.broadcasted_iota(jnp.int32, (w, lanes), 1)

    def both_halves(prev_ref, cur_ref, hk):
        slab = slice((hk // 2) * lanes, (hk // 2 + 1) * lanes)
        x = jnp.concatenate([prev_ref[:, slab], cur_ref[:, slab]], axis=0).astype(F32)
        upper = hk % 2 == 1
        own = jnp.where((lane_kv >= hd) if upper else (lane_kv < hd), x, 0.0)
        other = pltpu.roll(own, hd, 1)
        lo, hi = (other, own) if upper else (own, other)
        return jnp.concatenate([lo, hi], axis=0).astype(BF16)

    for hk in range(SWA_KV_HEADS):
        kk = both_halves(kp_ref, kc_ref, hk)
        vv = both_halves(vp_ref, vc_ref, hk)
        slabs = [slice((hk * pairs + r) * lanes, (hk * pairs + r + 1) * lanes) for r in range(pairs)]
        qs = jnp.concatenate([q_ref[:, sl] for sl in slabs], axis=0)
        s_all = _dot_nt(qs, kk)
        probs, inv_den = [], []
        for r in range(pairs):
            halves, inv = [], []
            for half in range(2):
                sink = sink_ref[hk * SWA_GROUPS + 2 * r + half] * LOG2E
                s = s_all[r * w:(r + 1) * w, half * 2 * w:(half + 1) * 2 * w] + bias
                m = jnp.maximum(jnp.max(s, axis=1, keepdims=True), sink)
                p = jnp.exp2(s - m)
                inv.append(1.0 / (jnp.sum(p, axis=1, keepdims=True) + jnp.exp2(sink - m)))
                halves.append(p.astype(BF16))
            probs.append(jnp.concatenate(halves, axis=1))
            inv_den.append(jnp.where(lane_o < hd, inv[0], inv[1]))
        o_all = _dot(jnp.concatenate(probs, axis=0), vv)
        for r in range(pairs):
            o_ref[:, slabs[r]] = (o_all[r * w:(r + 1) * w] * inv_den[r]).astype(o_ref.dtype)


def _sliding_window_attention(qkv, sinks):
    bsz, s_len, _ = qkv.shape
    w, hd = SWA_WINDOW, SWA_HEAD_DIM
    assert s_len % w == 0, (s_len, w)
    qw = SWA_HEADS * hd
    kw = SWA_KV_HEADS * hd
    k_blk = qw // kw
    v_blk = k_blk + 1
    prev = lambda n: jnp.maximum(n - 1, 0)
    return pl.pallas_call(
        _swa_kernel,
        grid=(bsz, s_len // w),
        in_specs=[pl.BlockSpec(memory_space=pltpu.SMEM),
                  pl.BlockSpec((None, w, qw), lambda b, n: (b, n, 0)),
                  pl.BlockSpec((None, w, kw), lambda b, n: (b, prev(n), k_blk)),
                  pl.BlockSpec((None, w, kw), lambda b, n: (b, n, k_blk)),
                  pl.BlockSpec((None, w, kw), lambda b, n: (b, prev(n), v_blk)),
                  pl.BlockSpec((None, w, kw), lambda b, n: (b, n, v_blk))],
        out_specs=pl.BlockSpec((None, w, qw), lambda b, n: (b, n, 0)),
        out_shape=jax.ShapeDtypeStruct((bsz, s_len, qw), BF16),
        compiler_params=_params("parallel", "arbitrary"),
        name="sliding_window_attention",
    )(sinks.astype(F32), qkv, qkv, qkv, qkv, qkv)


def _even_mixer(x32, xa, bsz, s_len, w_in, b_f, lam_re, lam_im, log_step, b_re, b_im, c_re, c_im,
                d_skip, w_glu, w_out, ln_g, ln_b):
    m = x32.shape[0]
    fw = FOX_WIDTH
    nh = FOX_HEADS
    q_factor = LOG2E / math.sqrt(FOX_HEAD_DIM)
    w_qkv = jnp.concatenate([w_in[:, :fw] * q_factor, w_in[:, fw:3 * fw]], axis=1).astype(BF16)
    w_uf = jnp.concatenate([w_in[:, 3 * fw + nh:],
                            jnp.pad(w_in[:, 3 * fw:3 * fw + nh], ((0, 0), (0, V7X_LANES - nh)))],
                           axis=1).astype(BF16)
    bias_f = jnp.pad(b_f, (0, V7X_LANES - nh)).reshape(1, V7X_LANES).astype(F32)

    qkv = _matmul(xa, w_qkv, bm=1024, bn=3 * fw // 2, out_dtype=BF16)
    mats = _s5_prepare(lam_re, lam_im, log_step, b_re, b_im, c_re, c_im, d_skip)
    u_rows, f_logits = _u_projection_rows(xa, w_uf)
    c_col = _forget_gate_cumsum(f_logits, bias_f)
    fox = _forgetting_attention(qkv, c_col).reshape(m, fw)
    gy_rows = _s5_scan_gelu(u_rows, mats, bsz=bsz)
    ssm = _glu_from_rows(gy_rows, w_glu.astype(BF16), bsz=bsz).reshape(m, -1)
    w_out16 = w_out.astype(BF16)
    return _matmul_residual_layernorm([(fox, w_out16, 0), (ssm, w_out16, fw)], x32, ln_g, ln_b, bm=512)


def _odd_mixer(x32, x16, bsz, s_len, positions, w_in, sinks, w_out, ln_g, ln_b):
    m = x16.shape[0]
    rope_cols = (SWA_HEADS + SWA_KV_HEADS) * SWA_HEAD_DIM
    qkv = _rope_projection(x16, w_in.astype(BF16), positions.reshape(m, 1).astype(jnp.int32),
                           rope_cols=rope_cols, q_cols=SWA_HEADS * SWA_HEAD_DIM,
                           q_scale=LOG2E / math.sqrt(SWA_HEAD_DIM), bm=1024)
    o = _sliding_window_attention(qkv.reshape(bsz, s_len, -1), sinks).reshape(m, -1)
    return _matmul_residual_layernorm([(o, w_out.astype(BF16), 0)], x32, ln_g, ln_b, bm=512)


def kernel(x, positions, ev_w_in, ev_b_f, ev_lambda_re, ev_lambda_im, ev_log_step, ev_ssm_b_re,
           ev_ssm_b_im, ev_ssm_c_re, ev_ssm_c_im, ev_ssm_d, ev_w_glu, ev_w_out, od_w_in, od_sinks,
           od_w_out, ln_mix_g, ln_mix_b, ffn_w_up, ffn_conv_w, ffn_conv_b, ffn_w_down, ln_ffn_g,
           ln_ffn_b):
    bsz, s_len, d = x.shape
    x32 = x.reshape(bsz * s_len, d).astype(F32)
    x16 = None
    for i in range(DEPTH):
        j = i // 2
        if i % 2 == 0:
            xa = x.astype(F32) if x16 is None else x16.reshape(bsz, s_len, d)
            x32, x16 = _even_mixer(x32, xa, bsz, s_len, ev_w_in[j], ev_b_f[j], ev_lambda_re[j],
                                   ev_lambda_im[j], ev_log_step[j], ev_ssm_b_re[j], ev_ssm_b_im[j],
                                   ev_ssm_c_re[j], ev_ssm_c_im[j], ev_ssm_d[j], ev_w_glu[j],
                                   ev_w_out[j], ln_mix_g[i], ln_mix_b[i])
        else:
            x16 = x32.astype(BF16) if x16 is None else x16
            x32, x16 = _odd_mixer(x32, x16, bsz, s_len, positions, od_w_in[j], od_sinks[j],
                                  od_w_out[j], ln_mix_g[i], ln_mix_b[i])
        x32, x16 = _conv_ffn(x32, x16, i, ffn_w_up, ffn_conv_w, ffn_conv_b, ffn_w_down,
                             ln_ffn_g[i], ln_ffn_b[i], seq_len=s_len)
    return x32.reshape(bsz, s_len, d).astype(x.dtype)
```

```python
import functools
import math

import jax
import jax.numpy as jnp
from jax import lax
from jax.experimental import pallas as pl
from jax.experimental.pallas import tpu as pltpu

F32 = jnp.float32
BF16 = jnp.bfloat16

DEPTH = 2
FOX_HEADS = 8
FOX_HEAD_DIM = 128
FOX_WIDTH = FOX_HEADS * FOX_HEAD_DIM
SSM_GROUP = 16
SSM_STATE = 64
SSM_CHUNK = 16
SSM_ROW = SSM_CHUNK * SSM_GROUP
SWA_HEADS = 32
SWA_KV_HEADS = 4
SWA_HEAD_DIM = 64
SWA_GROUPS = SWA_HEADS // SWA_KV_HEADS
SWA_WINDOW = 128
ROPE_DIM = SWA_HEAD_DIM // 4
ROPE_THETA = 500000.0
CONV_WIDTH = 3
LN_EPS = 1e-5
DEEPNORM_ALPHA = (2.0 * DEPTH) ** 0.25

V7X_LANES = 128
V7X_VMEM_LIMIT_BYTES = 56 * 1024 * 1024
GROUPS_PER_SLAB = V7X_LANES // SSM_GROUP
STAGE_PITCH = 136
MASK_VALUE = -1e30
LOG2E = math.log2(math.e)

HIGHEST = lax.Precision.HIGHEST


def _params(*semantics):
    return pltpu.CompilerParams(dimension_semantics=semantics,
                                vmem_limit_bytes=V7X_VMEM_LIMIT_BYTES)


def _dot(a, b):
    return jnp.dot(a, b, preferred_element_type=F32)


def _dot_nt(a, b):
    return lax.dot_general(a, b, (((1,), (1,)), ((), ())), preferred_element_type=F32)


def _mm_kernel(a_ref, w_ref, o_ref):
    o_ref[...] = _dot(a_ref[...].astype(BF16), w_ref[...]).astype(o_ref.dtype)


def _matmul(a, w, *, bm, bn, out_dtype):
    bsz, s_len, k = a.shape
    n = w.shape[1]
    assert s_len % bm == 0 and n % bn == 0, (a.shape, w.shape, bm, bn)
    per_seq = s_len // bm
    rows = lambda i, j: (i // per_seq, i % per_seq)
    return pl.pallas_call(
        _mm_kernel,
        grid=(bsz * per_seq, n // bn),
        in_specs=[pl.BlockSpec((None, bm, k), lambda i, j: rows(i, j) + (0,)),
                  pl.BlockSpec((k, bn), lambda i, j: (0, j))],
        out_specs=pl.BlockSpec((None, bm, bn), lambda i, j: rows(i, j) + (j,)),
        out_shape=jax.ShapeDtypeStruct((bsz, s_len, n), out_dtype),
        compiler_params=_params("parallel", "arbitrary"),
        name="matmul",
    )(a, w)


def _mm_res_ln_kernel(*refs, n_pairs, sub):
    a_refs = refs[:n_pairs]
    w_refs = refs[n_pairs:2 * n_pairs]
    x_ref, g_ref, b_ref, o32_ref, o16_ref = refs[2 * n_pairs:]
    for r in range(x_ref.shape[0] // sub):
        rows = slice(r * sub, (r + 1) * sub)
        pre = DEEPNORM_ALPHA * x_ref[rows, :]
        for a_ref, w_ref in zip(a_refs, w_refs):
            pre = pre + _dot(a_ref[rows, :], w_ref[...])
        mu = jnp.mean(pre, axis=-1, keepdims=True)
        cen = pre - mu
        var = jnp.mean(cen * cen, axis=-1, keepdims=True)
        y = cen * lax.rsqrt(var + LN_EPS) * g_ref[...] + b_ref[...]
        o32_ref[rows, :] = y
        o16_ref[rows, :] = y.astype(BF16)


def _matmul_residual_layernorm(pairs, x32, gain, bias, *, bm, sub=256):
    m, d = x32.shape
    assert m % bm == 0 and bm % sub == 0, (m, bm, sub)
    n_pairs = len(pairs)
    a_list = [p[0] for p in pairs]
    w_list = [p[1] for p in pairs]
    row_spec = lambda width: pl.BlockSpec((bm, width), lambda i: (i, 0))
    const_spec = lambda rows: pl.BlockSpec((rows, d), lambda i: (0, 0), pipeline_mode=pl.Buffered(1))
    window_spec = lambda rows, row0: pl.BlockSpec((pl.Element(rows), pl.Element(d)), lambda i: (row0, 0),
                                                  pipeline_mode=pl.Buffered(1))
    in_specs = ([row_spec(a.shape[1]) for a in a_list]
                + [window_spec(a.shape[1], row0) for a, _, row0 in pairs]
                + [row_spec(d), const_spec(1), const_spec(1)])
    return pl.pallas_call(
        functools.partial(_mm_res_ln_kernel, n_pairs=n_pairs, sub=sub),
        grid=(m // bm,),
        in_specs=in_specs,
        out_specs=[row_spec(d), row_spec(d)],
        out_shape=[jax.ShapeDtypeStruct((m, d), F32), jax.ShapeDtypeStruct((m, d), BF16)],
        compiler_params=_params("parallel"),
        name="matmul_residual_layernorm",
    )(*a_list, *w_list, x32, gain.reshape(1, d).astype(F32), bias.reshape(1, d).astype(F32))


def _bf16_split3(v):
    hi = v.astype(BF16)
    r1 = v - hi.astype(F32)
    mid = r1.astype(BF16)
    lo = (r1 - mid.astype(F32)).astype(BF16)
    return hi, mid, lo


def _forget_gate_kernel(f_ref, bf_ref, c_ref, *, chunk):
    s_len = f_ref.shape[0]
    f = f_ref[...] + bf_ref[...]
    log_f = jnp.minimum(f, 0.0) - jnp.log(1.0 + jnp.exp(-jnp.abs(f)))
    row = lax.broadcasted_iota(jnp.int32, (chunk, chunk), 0)
    col = lax.broadcasted_iota(jnp.int32, (chunk, chunk), 1)
    tri = (row >= col).astype(BF16)
    carry = jnp.zeros((1, log_f.shape[1]), F32)
    for start in range(0, s_len, chunk):
        hi, mid, lo = _bf16_split3(log_f[start:start + chunk])
        part = _dot(tri, hi) + _dot(tri, mid) + _dot(tri, lo) + carry
        c_ref[start:start + chunk, :] = part
        carry = part[chunk - 1:chunk, :]


def _forget_gate_cumsum(f_logits, b_f, *, chunk=256):
    bsz, s_len, n = f_logits.shape
    return pl.pallas_call(
        functools.partial(_forget_gate_kernel, chunk=chunk),
        grid=(bsz,),
        in_specs=[pl.BlockSpec((None, s_len, n), lambda b: (b, 0, 0)),
                  pl.BlockSpec((1, n), lambda b: (0, 0))],
        out_specs=pl.BlockSpec((None, s_len, n), lambda b: (b, 0, 0)),
        out_shape=jax.ShapeDtypeStruct((bsz, s_len, n), F32),
        compiler_params=_params("parallel"),
        name="forget_gate_cumsum",
    )(f_logits, b_f)


def _bias_lanes(c, first):
    hi, mid, lo = (piece.astype(F32) for piece in _bf16_split3(c))
    lane = lax.broadcasted_iota(jnp.int32, (c.shape[0], V7X_LANES), 1)
    ones = jnp.where(lane < 6, 1.0, 0.0)
    return jnp.where(lane == first, hi, jnp.where(lane == first + 1, mid, jnp.where(lane == first + 2, lo, ones)))


def _fox_kernel(q_ref, k_ref, v_ref, c_ref, o_ref, kaug_ref, *, tile, heads):
    dh = FOX_HEAD_DIM
    aug = 2 * dh
    h0 = pl.program_id(1) * heads
    n_tiles = q_ref.shape[0] // tile

    def head_column(c_all, hh):
        lane = lax.broadcasted_iota(jnp.int32, c_all.shape, 1)
        return jnp.sum(jnp.where(lane == h0 + hh, c_all, 0.0), axis=1, keepdims=True) * LOG2E

    c_all = c_ref[...]
    for hh in range(heads):
        kaug_ref[:, hh * aug:hh * aug + dh] = k_ref[:, hh * dh:(hh + 1) * dh]
        kaug_ref[:, hh * aug + dh:(hh + 1) * aug] = _bias_lanes(-head_column(c_all, hh), 3).astype(BF16)

    def update(state, s, v):
        m, l, acc = state
        m_new = jnp.maximum(m, jnp.max(s, axis=1, keepdims=True))
        alpha = jnp.exp2(m - m_new)
        p = jnp.exp2(s - m_new)
        return (m_new, alpha * l + jnp.sum(p, axis=1, keepdims=True), alpha * acc + _dot(p.astype(BF16), v))

    for qt in range(n_tiles):
        rows = slice(qt * tile, (qt + 1) * tile)
        cq_all = c_ref[rows, :]
        for hh in range(heads):
            q_aug = jnp.concatenate([q_ref[rows, hh * dh:(hh + 1) * dh],
                                     _bias_lanes(head_column(cq_all, hh), 0).astype(BF16)], axis=1)
            state = (jnp.full((tile, 1), MASK_VALUE, F32), jnp.zeros((tile, 1), F32), jnp.zeros((tile, dh), F32))
            for kt in range(qt + 1):
                keys = slice(kt * tile, (kt + 1) * tile)
                s = _dot_nt(q_aug, kaug_ref[keys, hh * aug:(hh + 1) * aug])
                if kt == qt:
                    row = lax.broadcasted_iota(jnp.int32, s.shape, 0)
                    col = lax.broadcasted_iota(jnp.int32, s.shape, 1)
                    s = jnp.where(row >= col, s, MASK_VALUE)
                state = update(state, s, v_ref[keys, hh * dh:(hh + 1) * dh])
            _, l, acc = state
            o_ref[rows, hh * dh:(hh + 1) * dh] = (acc / l).astype(o_ref.dtype)


def _forgetting_attention(qkv, c_col, *, tile=1024, heads=2):
    bsz, s_len, _ = qkv.shape
    assert s_len % tile == 0 and FOX_HEADS % heads == 0, (s_len, tile, heads)
    width = heads * FOX_HEAD_DIM
    nblk = FOX_WIDTH // width
    seq_spec = lambda off: pl.BlockSpec((None, s_len, width), lambda b, h: (b, 0, off + h))
    return pl.pallas_call(
        functools.partial(_fox_kernel, tile=tile, heads=heads),
        grid=(bsz, nblk),
        in_specs=[seq_spec(0), seq_spec(nblk), seq_spec(2 * nblk),
                  pl.BlockSpec((None, s_len, c_col.shape[2]), lambda b, h: (b, 0, 0))],
        out_specs=seq_spec(0),
        out_shape=jax.ShapeDtypeStruct((bsz, s_len, FOX_WIDTH), BF16),
        scratch_shapes=[pltpu.VMEM((s_len, 2 * width), BF16)],
        compiler_params=_params("parallel", "parallel"),
        name="forgetting_attention",
    )(qkv, qkv, qkv, c_col)


def _cmul(ar, ai, br, bi):
    return ar * br - ai * bi, ar * bi + ai * br


def _cpow(ar, ai, e, n_bits):
    rr = jnp.ones_like(ar)
    ri = jnp.zeros_like(ai)
    for bit in range(n_bits):
        nr, ni = _cmul(rr, ri, ar, ai)
        take = ((e >> bit) & 1) == 1
        rr = jnp.where(take, nr, rr)
        ri = jnp.where(take, ni, ri)
        if bit + 1 < n_bits:
            ar, ai = _cmul(ar, ai, ar, ai)
    return rr, ri


def _discretise(lr, li, dt):
    mag = jnp.exp(lr * dt)
    return mag * jnp.cos(li * dt), mag * jnp.sin(li * dt)


def _s5_prep_group(lam_row_ref, lam_col_ref, dt_ref, bt_re_ref, bt_im_ref, ct_re_ref, ct_im_ref,
                   d_ref, t0_ref, ws_re_ref, ws_im_ref, wc_re_ref, wc_im_ref, a_chunk_ref):
    chunk, grp, p = SSM_CHUNK, SSM_GROUP, SSM_STATE
    n_bits = chunk.bit_length()
    dt = jnp.exp(dt_ref[...])
    lr_r, li_r = lam_row_ref[0:1, :], lam_row_ref[1:2, :]
    ar_r, ai_r = _discretise(lr_r, li_r, dt)
    den = lr_r * lr_r + li_r * li_r
    xr, xi = ar_r - 1.0, ai_r
    g_re = (xr * lr_r + xi * li_r) / den
    g_im = (xi * lr_r - xr * li_r) / den
    bbt_re, bbt_im = _cmul(g_re, g_im, bt_re_ref[...], bt_im_ref[...])
    bx_re = jnp.concatenate([bbt_re] * chunk, axis=0)
    bx_im = jnp.concatenate([bbt_im] * chunk, axis=0)
    row_i = lax.broadcasted_iota(jnp.int32, (SSM_ROW, p), 0) // grp
    pr, pi = _cpow(jnp.broadcast_to(ar_r, (SSM_ROW, p)), jnp.broadcast_to(ai_r, (SSM_ROW, p)),
                   chunk - 1 - row_i, n_bits)
    ws_re, ws_im = _cmul(pr, pi, bx_re, bx_im)
    ws_re_ref[...] = ws_re.astype(BF16)
    ws_im_ref[...] = ws_im.astype(BF16)
    ac_r, ac_i = _cpow(ar_r, ai_r, jnp.full((1, p), chunk, jnp.int32), n_bits)
    a_chunk_ref[0:1, :] = ac_r
    a_chunk_ref[1:2, :] = ac_i
    lr_c, li_c = lam_col_ref[:, 0:1], lam_col_ref[:, 1:2]
    ar_c, ai_c = _discretise(lr_c, li_c, dt)
    rep = (lax.broadcasted_iota(jnp.int32, (grp, SSM_ROW), 0)
           == lax.broadcasted_iota(jnp.int32, (grp, SSM_ROW), 1) % grp).astype(F32)
    cx_re = jnp.dot(ct_re_ref[...], rep, precision=HIGHEST, preferred_element_type=F32)
    cx_im = jnp.dot(ct_im_ref[...], rep, precision=HIGHEST, preferred_element_type=F32)
    dx = jnp.dot(d_ref[...], rep, precision=HIGHEST, preferred_element_type=F32)
    col_j = lax.broadcasted_iota(jnp.int32, (p, SSM_ROW), 1) // grp
    ab_r = jnp.broadcast_to(ar_c, (p, SSM_ROW))
    ab_i = jnp.broadcast_to(ai_c, (p, SSM_ROW))
    qr, qi = _cpow(ab_r, ab_i, col_j, n_bits)
    f_re, f_im = _cmul(qr, qi, cx_re, cx_im)
    e_re, e_im = _cmul(f_re, f_im, ab_r, ab_i)
    wc_re_ref[...] = e_re.astype(BF16)
    wc_im_ref[...] = (-e_im).astype(BF16)
    r0 = (jnp.dot(bbt_re, f_re, precision=HIGHEST, preferred_element_type=F32)
          - jnp.dot(bbt_im, f_im, precision=HIGHEST, preferred_element_type=F32))
    lane = lax.broadcasted_iota(jnp.int32, (grp, SSM_ROW), 1)
    sub = lax.broadcasted_iota(jnp.int32, (grp, SSM_ROW), 0)
    for i in range(chunk):
        blk = r0 if i == 0 else pltpu.roll(r0, i * grp, 1)
        blk = jnp.where(lane >= i * grp, blk, 0.0)
        blk = blk + jnp.where(lane == i * grp + sub, dx, 0.0)
        t0_ref[i * grp:(i + 1) * grp, :] = blk.astype(BF16)


def _s5_prep_kernel(*refs, groups):
    for g in range(groups):
        _s5_prep_group(*(ref.at[g] for ref in refs))


def _s5_prepare(lam_re, lam_im, log_step, b_re, b_im, c_re, c_im, d_skip, *, groups_per_step=8):
    g, p = lam_re.shape
    grp = SSM_GROUP
    lam_row = jnp.stack([lam_re, lam_im], axis=1).astype(F32)
    lam_col = jnp.stack([lam_re, lam_im], axis=2).astype(F32)
    dt = log_step.reshape(g, 1, 1).astype(F32)
    bt_re = jnp.swapaxes(b_re, 1, 2).astype(F32)
    bt_im = jnp.swapaxes(b_im, 1, 2).astype(F32)
    ct_re = jnp.swapaxes(c_re, 1, 2).astype(F32)
    ct_im = jnp.swapaxes(c_im, 1, 2).astype(F32)
    d3 = d_skip.reshape(g, 1, grp).astype(F32)

    gs = groups_per_step

    def spec(*shape):
        return pl.BlockSpec((gs,) + shape, lambda i: (i,) + (0,) * len(shape))

    return pl.pallas_call(
        functools.partial(_s5_prep_kernel, groups=gs),
        grid=(g // gs,),
        in_specs=[spec(2, p), spec(p, 2), spec(1, 1), spec(grp, p), spec(grp, p),
                  spec(p, grp), spec(p, grp), spec(1, grp)],
        out_specs=[spec(SSM_ROW, SSM_ROW), spec(SSM_ROW, p), spec(SSM_ROW, p),
                   spec(p, SSM_ROW), spec(p, SSM_ROW), spec(2, p)],
        out_shape=[jax.ShapeDtypeStruct((g, SSM_ROW, SSM_ROW), BF16),
                   jax.ShapeDtypeStruct((g, SSM_ROW, p), BF16),
                   jax.ShapeDtypeStruct((g, SSM_ROW, p), BF16),
                   jax.ShapeDtypeStruct((g, p, SSM_ROW), BF16),
                   jax.ShapeDtypeStruct((g, p, SSM_ROW), BF16),
                   jax.ShapeDtypeStruct((g, 2, p), F32)],
        compiler_params=_params("parallel"),
        name="s5_prepare",
    )(lam_row, lam_col, dt, bt_re, bt_im, ct_re, ct_im, d3)


def _gelu_tanh(y):
    return 0.5 * y * (1.0 + jnp.tanh(math.sqrt(2.0 / math.pi) * (y + 0.044715 * (y * y * y))))


def _s5_scan_kernel(u_ref, t0_ref, ws_re_ref, ws_im_ref, wc_re_ref, wc_im_ref, a_ref, o_ref,
                    sre_ref, sim_ref, hre_ref, him_ref, *, groups, bsz, n_chunks):
    for g in range(groups):
        u = u_ref[g]
        sre_ref[g] = _dot(u, ws_re_ref[g])
        sim_ref[g] = _dot(u, ws_im_ref[g])
    ar = a_ref[:, 0:1, :]
    ai = a_ref[:, 1:2, :]

    def step(k, carry):
        hr, hi = carry
        rows = pl.ds(pl.multiple_of(k * bsz, bsz), bsz)
        hre_ref[:, rows, :] = hr
        him_ref[:, rows, :] = hi
        sr = sre_ref[:, rows, :]
        si = sim_ref[:, rows, :]
        return ar * hr - ai * hi + sr, ar * hi + ai * hr + si

    zero = jnp.zeros((groups, bsz, SSM_STATE), F32)
    lax.fori_loop(0, n_chunks, step, (zero, zero))
    for g in range(groups):
        y = (_dot(u_ref[g], t0_ref[g]) + _dot(hre_ref[g].astype(BF16), wc_re_ref[g])
             + _dot(him_ref[g].astype(BF16), wc_im_ref[g]))
        o_ref[g] = _gelu_tanh(y).astype(o_ref.dtype)


def _s5_scan_gelu(u_rows, mats, *, bsz, groups_per_step=8):
    t0, ws_re, ws_im, wc_re, wc_im, a_chunk = mats
    g, rows, _ = u_rows.shape
    gs = groups_per_step
    p = SSM_STATE
    assert g % gs == 0 and rows % bsz == 0 and bsz % 8 == 0, (u_rows.shape, bsz, gs)

    def spec(*shape):
        return pl.BlockSpec((gs,) + shape, lambda i: (i,) + (0,) * len(shape))

    return pl.pallas_call(
        functools.partial(_s5_scan_kernel, groups=gs, bsz=bsz, n_chunks=rows // bsz),
        grid=(g // gs,),
        in_specs=[spec(rows, SSM_ROW), spec(SSM_ROW, SSM_ROW), spec(SSM_ROW, p), spec(SSM_ROW, p),
                  spec(p, SSM_ROW), spec(p, SSM_ROW), spec(2, p)],
        out_specs=spec(rows, SSM_ROW),
        out_shape=jax.ShapeDtypeStruct((g, rows, SSM_ROW), BF16),
        scratch_shapes=[pltpu.VMEM((gs, rows, p), F32), pltpu.VMEM((gs, rows, p), F32),
                        pltpu.VMEM((gs, rows, p), F32), pltpu.VMEM((gs, rows, p), F32)],
        compiler_params=_params("parallel"),
        name="s5_scan_gelu",
    )(u_rows, t0, ws_re, ws_im, wc_re, wc_im, a_chunk)


def _sigmoid(x):
    return 1.0 / (1.0 + jnp.exp(-x))


def _block_transpose(xs):
    n = GROUPS_PER_SLAB
    width = xs[0].shape[1]
    blk = (lax.broadcasted_iota(jnp.int32, (1, width), 1) // SSM_GROUP) % n
    xs = list(xs)
    d = n // 2
    while d >= 1:
        high = (blk & d) != 0
        nxt = list(xs)
        for i in range(n):
            if i & d:
                continue
            lo_arr, hi_arr = xs[i], xs[i + d]
            nxt[i] = jnp.where(high, pltpu.roll(hi_arr, SSM_GROUP * d, 1), lo_arr)
            nxt[i + d] = jnp.where(high, hi_arr, pltpu.roll(lo_arr, width - SSM_GROUP * d, 1))
        xs = nxt
        d //= 2
    return xs


def _u_rows_kernel(x_ref, w_ref, o_ref, f_ref, stage_ref, *, bsz, chunks):
    tb = x_ref.shape[1]
    uf = _dot(x_ref[...].reshape(bsz * tb, x_ref.shape[2]).astype(BF16), w_ref[...])
    n_slabs = o_ref.shape[0] // GROUPS_PER_SLAB
    u = uf[:, :n_slabs * V7X_LANES]
    f_ref[...] = uf[:, n_slabs * V7X_LANES:].reshape(f_ref.shape)
    for v in range(n_slabs):
        for b in range(bsz):
            stage_ref[v, b * STAGE_PITCH:b * STAGE_PITCH + tb, :] = (
                u[b * tb:(b + 1) * tb, v * V7X_LANES:(v + 1) * V7X_LANES])
    n = GROUPS_PER_SLAB
    for jh in range(SSM_CHUNK // n):
        xs = []
        for jl in range(n):
            j = jh * n + jl
            xs.append(jnp.concatenate(
                [jnp.concatenate([stage_ref.at[v][pl.ds(SSM_CHUNK * c + j, bsz, stride=STAGE_PITCH), :]
                                  for v in range(n_slabs)], axis=1) for c in range(chunks)], axis=0))
        ys = _block_transpose(xs)
        for e in range(n):
            for v in range(n_slabs):
                o_ref[n * v + e, :, jh * V7X_LANES:(jh + 1) * V7X_LANES] = (
                    ys[e][:, v * V7X_LANES:(v + 1) * V7X_LANES].astype(o_ref.dtype))


def _u_projection_rows(x, w_uf, *, chunks=8):
    bsz, s_len, d = x.shape
    width = w_uf.shape[1] - V7X_LANES
    g = width // SSM_GROUP
    tb = chunks * SSM_CHUNK
    rows = chunks * bsz
    assert bsz % 8 == 0 and s_len % tb == 0 and tb <= STAGE_PITCH and g % GROUPS_PER_SLAB == 0, x.shape
    return pl.pallas_call(
        functools.partial(_u_rows_kernel, bsz=bsz, chunks=chunks),
        grid=(s_len // tb,),
        in_specs=[pl.BlockSpec((bsz, tb, d), lambda i: (0, i, 0)),
                  pl.BlockSpec((d, width + V7X_LANES), lambda i: (0, 0), pipeline_mode=pl.Buffered(1))],
        out_specs=[pl.BlockSpec((g, rows, SSM_ROW), lambda i: (0, i, 0)),
                   pl.BlockSpec((bsz, tb, V7X_LANES), lambda i: (0, i, 0))],
        out_shape=[jax.ShapeDtypeStruct((g, (s_len // SSM_CHUNK) * bsz, SSM_ROW), BF16),
                   jax.ShapeDtypeStruct((bsz, s_len, V7X_LANES), F32)],
        scratch_shapes=[pltpu.VMEM((width // V7X_LANES, bsz * STAGE_PITCH, V7X_LANES), F32)],
        compiler_params=_params("parallel"),
        name="u_projection_rows",
    )(x, w_uf)


def _glu_rows_kernel(gy_ref, w_ref, o_ref, stage_ref, *, bsz, chunks):
    tb = o_ref.shape[1]
    n = GROUPS_PER_SLAB
    n_slabs = gy_ref.shape[0] // n
    for jh in range(SSM_CHUNK // n):
        ys = [jnp.concatenate([gy_ref[n * v + e, :, jh * V7X_LANES:(jh + 1) * V7X_LANES].astype(F32)
                               for v in range(n_slabs)], axis=1) for e in range(n)]
        xs = _block_transpose(ys)
        for jl in range(n):
            j = jh * n + jl
            for c in range(chunks):
                for v in range(n_slabs):
                    stage_ref.at[v][pl.ds(SSM_CHUNK * c + j, bsz, stride=STAGE_PITCH), :] = (
                        xs[jl][c * bsz:(c + 1) * bsz, v * V7X_LANES:(v + 1) * V7X_LANES])
    a = jnp.concatenate(
        [jnp.concatenate([stage_ref[v, b * STAGE_PITCH:b * STAGE_PITCH + tb, :] for v in range(n_slabs)], axis=1)
         for b in range(bsz)], axis=0).astype(BF16)
    z = _dot(a, w_ref[...])
    half = z.shape[1] // 2
    o_ref[...] = (z[:, :half] * _sigmoid(z[:, half:])).astype(o_ref.dtype).reshape(o_ref.shape)


def _glu_from_rows(gy_rows, w_glu, *, bsz, chunks=8):
    g, total_rows, _ = gy_rows.shape
    width = g * SSM_GROUP
    n_out = w_glu.shape[1] // 2
    s_len = total_rows // bsz * SSM_CHUNK
    tb = chunks * SSM_CHUNK
    rows = chunks * bsz
    assert bsz % 8 == 0 and s_len % tb == 0 and tb <= STAGE_PITCH and g % GROUPS_PER_SLAB == 0, gy_rows.shape
    return pl.pallas_call(
        functools.partial(_glu_rows_kernel, bsz=bsz, chunks=chunks),
        grid=(s_len // tb,),
        in_specs=[pl.BlockSpec((g, rows, SSM_ROW), lambda i: (0, i, 0)),
                  pl.BlockSpec((width, 2 * n_out), lambda i: (0, 0), pipeline_mode=pl.Buffered(1))],
        out_specs=pl.BlockSpec((bsz, tb, n_out), lambda i: (0, i, 0)),
        out_shape=jax.ShapeDtypeStruct((bsz, s_len, n_out), BF16),
        scratch_shapes=[pltpu.VMEM((width // V7X_LANES, bsz * STAGE_PITCH, V7X_LANES), F32)],
        compiler_params=_params("parallel"),
        name="glu_from_rows",
    )(gy_rows, w_glu)


def _ffn_up_kernel(x_ref, wg_ref, wv_ref, cwg_ref, cbg_ref, cwv_ref, cbv_ref, o_ref,
                   wg16_ref, wv16_ref, hg_ref, hv_ref, *, blocks_per_seq):
    i = pl.program_id(1)
    bm, bn = o_ref.shape
    pad = hg_ref.shape[0] - bm

    @pl.when(i == 0)
    def _():
        wg16_ref[...] = wg_ref[0].astype(BF16)
        wv16_ref[...] = wv_ref[0].astype(BF16)

    @pl.when(i % blocks_per_seq == 0)
    def _():
        hg_ref[0:pad, :] = jnp.zeros((pad, bn), F32)
        hv_ref[0:pad, :] = jnp.zeros((pad, bn), F32)

    @pl.when(i % blocks_per_seq != 0)
    def _():
        hg_ref[0:pad, :] = hg_ref[bm:bm + pad, :]
        hv_ref[0:pad, :] = hv_ref[bm:bm + pad, :]

    x = x_ref[...]
    conv = []
    for w16, h_ref, cw_ref, cb_ref in ((wg16_ref, hg_ref, cwg_ref, cbg_ref), (wv16_ref, hv_ref, cwv_ref, cbv_ref)):
        h = _dot(x, w16[...])
        h_ref[pad:pad + bm, :] = h
        cw = cw_ref[0]
        conv.append(cb_ref[0] + cw[0:1, :] * h_ref[pad - 2:pad - 2 + bm, :]
                    + cw[1:2, :] * h_ref[pad - 1:pad - 1 + bm, :] + cw[2:3, :] * h)
    gate, val = conv
    o_ref[...] = (gate * _sigmoid(gate) * val).astype(o_ref.dtype)


def _ffn_up(x16, layer, w_up, conv_w, conv_b, *, col0, n_cols, seq_len, bm, bn):
    m, d = x16.shape
    f = w_up.shape[2] // 2
    assert m % bm == 0 and seq_len % bm == 0 and n_cols % bn == 0 and bn % V7X_LANES == 0, (m, bm, n_cols, bn)
    pad = 8
    cb = conv_b.reshape(conv_b.shape[0], 1, 2 * f)
    gate_col = lambda j: pl.multiple_of(col0 + j * bn, V7X_LANES)
    val_col = lambda j: pl.multiple_of(f + col0 + j * bn, V7X_LANES)
    window = lambda rows: (pl.Element(1), pl.Element(rows), pl.Element(bn))
    gate_spec = lambda rows: pl.BlockSpec(window(rows), lambda j, i: (layer, 0, gate_col(j)))
    val_spec = lambda rows: pl.BlockSpec(window(rows), lambda j, i: (layer, 0, val_col(j)))
    return pl.pallas_call(
        functools.partial(_ffn_up_kernel, blocks_per_seq=seq_len // bm),
        grid=(n_cols // bn, m // bm),
        in_specs=[pl.BlockSpec((bm, d), lambda j, i: (i, 0)),
                  gate_spec(d), val_spec(d), gate_spec(CONV_WIDTH), gate_spec(1),
                  val_spec(CONV_WIDTH), val_spec(1)],
        out_specs=pl.BlockSpec((bm, bn), lambda j, i: (i, j)),
        out_shape=jax.ShapeDtypeStruct((m, n_cols), BF16),
        scratch_shapes=[pltpu.VMEM((d, bn), BF16), pltpu.VMEM((d, bn), BF16),
                        pltpu.VMEM((pad + bm, bn), F32), pltpu.VMEM((pad + bm, bn), F32)],
        compiler_params=_params("arbitrary", "arbitrary"),
        name="ffn_up_conv_gate",
    )(x16, w_up, w_up, conv_w, cb, conv_w, cb)


def _cast_kernel(src_ref, dst_ref):
    dst_ref[...] = src_ref[...].astype(dst_ref.dtype)


def _layer_to_bf16(stacked, layer, *, steps=8):
    _, k, d = stacked.shape
    assert k % (16 * steps) == 0, (k, steps)
    rows = k // steps
    return pl.pallas_call(
        _cast_kernel,
        grid=(steps,),
        in_specs=[pl.BlockSpec((None, rows, d), lambda r: (layer, r, 0))],
        out_specs=pl.BlockSpec((rows, d), lambda r: (r, 0)),
        out_shape=jax.ShapeDtypeStruct((k, d), BF16),
        compiler_params=_params("parallel"),
        name="layer_to_bf16",
    )(stacked)


def _conv_ffn(x32, x16, layer, w_up, conv_w, conv_b, w_down, ln_g, ln_b, *, seq_len, bn=512):
    f = w_down.shape[1]
    w_up, conv_w, conv_b = w_up.astype(F32), conv_w.astype(F32), conv_b.astype(F32)
    w_down16 = _layer_to_bf16(w_down, layer)
    main = (f // bn) * bn
    pairs = []
    for col0, n_cols, width in ((0, main, bn), (main, f - main, f - main)):
        if n_cols:
            act = _ffn_up(x16, layer, w_up, conv_w, conv_b, col0=col0, n_cols=n_cols, seq_len=seq_len,
                          bm=1024, bn=width)
            pairs.append((act, w_down16, col0))
    return _matmul_residual_layernorm(pairs, x32, ln_g, ln_b, bm=256)


def _rope_proj_kernel(a_ref, w_ref, pos_ref, o_ref, *, rope_cols, q_cols, q_scale, sub):
    hd, half = SWA_HEAD_DIM, ROPE_DIM // 2
    bm, n = o_ref.shape
    lane1 = lax.broadcasted_iota(jnp.int32, (1, V7X_LANES), 1) % hd
    inv_freq = jnp.where(lane1 < ROPE_DIM,
                         jnp.exp((lane1 % half).astype(F32) * (-math.log(ROPE_THETA) / half)), 0.0)
    sign = jnp.where(lane1 < half, -1.0, 1.0)
    lane = lax.broadcasted_iota(jnp.int32, (1, n), 1)
    first = (lane & (hd - 1)) < half
    reps = n // V7X_LANES
    for r in range(bm // sub):
        rows = slice(r * sub, (r + 1) * sub)
        ang = pos_ref[rows, :].astype(F32) * inv_freq
        cos = jnp.concatenate([jnp.cos(ang)] * reps, axis=1)
        sin = jnp.concatenate([jnp.sin(ang) * sign] * reps, axis=1)
        acc = _dot(a_ref[rows, :], w_ref[...])
        partner = jnp.where(first, pltpu.roll(acc, n - half, 1), pltpu.roll(acc, half, 1))
        rot = acc * cos + partner * sin
        out = jnp.where(lane < rope_cols, rot, acc) * jnp.where(lane < q_cols, q_scale, 1.0)
        o_ref[rows, :] = out.astype(o_ref.dtype)


def _rope_projection(x16, w, pos_col, *, rope_cols, q_cols, q_scale, bm, sub=256):
    m, k = x16.shape
    n = w.shape[1]
    assert m % bm == 0 and bm % sub == 0 and n % V7X_LANES == 0, (m, n, bm, sub)
    return pl.pallas_call(
        functools.partial(_rope_proj_kernel, rope_cols=rope_cols, q_cols=q_cols, q_scale=q_scale, sub=sub),
        grid=(m // bm,),
        in_specs=[pl.BlockSpec((bm, k), lambda i: (i, 0)),
                  pl.BlockSpec((k, n), lambda i: (0, 0), pipeline_mode=pl.Buffered(1)),
                  pl.BlockSpec((bm, 1), lambda i: (i, 0))],
        out_specs=pl.BlockSpec((bm, n), lambda i: (i, 0)),
        out_shape=jax.ShapeDtypeStruct((m, n), BF16),
        compiler_params=_params("parallel"),
        name="rope_projection",
    )(x16, w, pos_col)


def _swa_kernel(sink_ref, q_ref, kp_ref, kc_ref, vp_ref, vc_ref, o_ref):
    n = pl.program_id(1)
    w, hd, lanes = SWA_WINDOW, SWA_HEAD_DIM, V7X_LANES
    pairs = SWA_GROUPS // 2
    qi = lax.broadcasted_iota(jnp.int32, (w, 2 * w), 0)
    kj = lax.broadcasted_iota(jnp.int32, (w, 2 * w), 1)
    rel = w + qi - kj
    valid = (rel >= 0) & (rel < w) & ((n > 0) | (kj >= w))
    bias = jnp.where(valid, 0.0, MASK_VALUE)
    lane_kv = lax.broadcasted_iota(jnp.int32, (2 * w, lanes), 1)
    lane_o = lax.broadcasted_iota(jnp.int32, (w, lanes), 1)

    def both_halves(prev_ref, cur_ref, hk):
        slab = slice((hk // 2) * lanes, (hk // 2 + 1) * lanes)
        x = jnp.concatenate([prev_ref[:, slab], cur_ref[:, slab]], axis=0).astype(F32)
        upper = hk % 2 == 1
        own = jnp.where((lane_kv >= hd) if upper else (lane_kv < hd), x, 0.0)
        other = pltpu.roll(own, hd, 1)
        lo, hi = (other, own) if upper else (own, other)
        return jnp.concatenate([lo, hi], axis=0).astype(BF16)

    for hk in range(SWA_KV_HEADS):
        kk = both_halves(kp_ref, kc_ref, hk)
        vv = both_halves(vp_ref, vc_ref, hk)
        slabs = [slice((hk * pairs + r) * lanes, (hk * pairs + r + 1) * lanes) for r in range(pairs)]
        qs = jnp.concatenate([q_ref[:, sl] for sl in slabs], axis=0)
        s_all = _dot_nt(qs, kk)
        probs, inv_den = [], []
        for r in range(pairs):
            halves, inv = [], []
            for half in range(2):
                sink = sink_ref[hk * SWA_GROUPS + 2 * r + half] * LOG2E
                s = s_all[r * w:(r + 1) * w, half * 2 * w:(half + 1) * 2 * w] + bias
                m = jnp.maximum(jnp.max(s, axis=1, keepdims=True), sink)
                p = jnp.exp2(s - m)
                inv.append(1.0 / (jnp.sum(p, axis=1, keepdims=True) + jnp.exp2(sink - m)))
                halves.append(p.astype(BF16))
            probs.append(jnp.concatenate(halves, axis=1))
            inv_den.append(jnp.where(lane_o < hd, inv[0], inv[1]))
        o_all = _dot(jnp.concatenate(probs, axis=0), vv)
        for r in range(pairs):
            o_ref[:, slabs[r]] = (o_all[r * w:(r + 1) * w] * inv_den[r]).astype(o_ref.dtype)


def _sliding_window_attention(qkv, sinks):
    bsz, s_len, _ = qkv.shape
    w, hd = SWA_WINDOW, SWA_HEAD_DIM
    assert s_len % w == 0, (s_len, w)
    qw = SWA_HEADS * hd
    kw = SWA_KV_HEADS * hd
    k_blk = qw // kw
    v_blk = k_blk + 1
    prev = lambda n: jnp.maximum(n - 1, 0)
    return pl.pallas_call(
        _swa_kernel,
        grid=(bsz, s_len // w),
        in_specs=[pl.BlockSpec(memory_space=pltpu.SMEM),
                  pl.BlockSpec((None, w, qw), lambda b, n: (b, n, 0)),
                  pl.BlockSpec((None, w, kw), lambda b, n: (b, prev(n), k_blk)),
                  pl.BlockSpec((None, w, kw), lambda b, n: (b, n, k_blk)),
                  pl.BlockSpec((None, w, kw), lambda b, n: (b, prev(n), v_blk)),
                  pl.BlockSpec((None, w, kw), lambda b, n: (b, n, v_blk))],
        out_specs=pl.BlockSpec((None, w, qw), lambda b, n: (b, n, 0)),
        out_shape=jax.ShapeDtypeStruct((bsz, s_len, qw), BF16),
        compiler_params=_params("parallel", "arbitrary"),
        name="sliding_window_attention",
    )(sinks.astype(F32), qkv, qkv, qkv, qkv, qkv)


def _even_mixer(x32, xa, bsz, s_len, w_in, b_f, lam_re, lam_im, log_step, b_re, b_im, c_re, c_im,
                d_skip, w_glu, w_out, ln_g, ln_b):
    m = x32.shape[0]
    fw = FOX_WIDTH
    nh = FOX_HEADS
    q_factor = LOG2E / math.sqrt(FOX_HEAD_DIM)
    w_qkv = jnp.concatenate([w_in[:, :fw] * q_factor, w_in[:, fw:3 * fw]], axis=1).astype(BF16)
    w_uf = jnp.concatenate([w_in[:, 3 * fw + nh:],
                            jnp.pad(w_in[:, 3 * fw:3 * fw + nh], ((0, 0), (0, V7X_LANES - nh)))],
                           axis=1).astype(BF16)
    bias_f = jnp.pad(b_f, (0, V7X_LANES - nh)).reshape(1, V7X_LANES).astype(F32)

    qkv = _matmul(xa, w_qkv, bm=1024, bn=3 * fw // 2, out_dtype=BF16)
    mats = _s5_prepare(lam_re, lam_im, log_step, b_re, b_im, c_re, c_im, d_skip)
    u_rows, f_logits = _u_projection_rows(xa, w_uf)
    c_col = _forget_gate_cumsum(f_logits, bias_f)
    fox = _forgetting_attention(qkv, c_col).reshape(m, fw)
    gy_rows = _s5_scan_gelu(u_rows, mats, bsz=bsz)
    ssm = _glu_from_rows(gy_rows, w_glu.astype(BF16), bsz=bsz).reshape(m, -1)
    w_out16 = w_out.astype(BF16)
    return _matmul_residual_layernorm([(fox, w_out16, 0), (ssm, w_out16, fw)], x32, ln_g, ln_b, bm=512)


def _odd_mixer(x32, x16, bsz, s_len, positions, w_in, sinks, w_out, ln_g, ln_b):
    m = x16.shape[0]
    rope_cols = (SWA_HEADS + SWA_KV_HEADS) * SWA_HEAD_DIM
    qkv = _rope_projection(x16, w_in.astype(BF16), positions.reshape(m, 1).astype(jnp.int32),
                           rope_cols=rope_cols, q_cols=SWA_HEADS * SWA_HEAD_DIM,
                           q_scale=LOG2E / math.sqrt(SWA_HEAD_DIM), bm=1024)
    o = _sliding_window_attention(qkv.reshape(bsz, s_len, -1), sinks).reshape(m, -1)
    return _matmul_residual_layernorm([(o, w_out.astype(BF16), 0)], x32, ln_g, ln_b, bm=512)


def kernel(x, positions, ev_w_in, ev_b_f, ev_lambda_re, ev_lambda_im, ev_log_step, ev_ssm_b_re,
           ev_ssm_b_im, ev_ssm_c_re, ev_ssm_c_im, ev_ssm_d, ev_w_glu, ev_w_out, od_w_in, od_sinks,
           od_w_out, ln_mix_g, ln_mix_b, ffn_w_up, ffn_conv_w, ffn_conv_b, ffn_w_down, ln_ffn_g,
           ln_ffn_b):
    bsz, s_len, d = x.shape
    x32 = x.reshape(bsz * s_len, d).astype(F32)
    x16 = None
    for i in range(DEPTH):
        j = i // 2
        if i % 2 == 0:
            xa = x.astype(F32) if x16 is None else x16.reshape(bsz, s_len, d)
            x32, x16 = _even_mixer(x32, xa, bsz, s_len, ev_w_in[j], ev_b_f[j], ev_lambda_re[j],
                                   ev_lambda_im[j], ev_log_step[j], ev_ssm_b_re[j], ev_ssm_b_im[j],
                                   ev_ssm_c_re[j], ev_ssm_c_im[j], ev_ssm_d[j], ev_w_glu[j],
                                   ev_w_out[j], ln_mix_g[i], ln_mix_b[i])
        else:
            x16 = x32.astype(BF16) if x16 is None else x16
            x32, x16 = _odd_mixer(x32, x16, bsz, s_len, positions, od_w_in[j], od_sinks[j],
                                  od_w_out[j], ln_mix_g[i], ln_mix_b[i])
        x32, x16 = _conv_ffn(x32, x16, i, ffn_w_up, ffn_conv_w, ffn_conv_b, ffn_w_down,
                             ln_ffn_g[i], ln_ffn_b[i], seq_len=s_len)
    return x32.reshape(bsz, s_len, d).astype(x.dtype)
```

```python
import functools
import math

import jax
import jax.numpy as jnp
from jax import lax
from jax.experimental import pallas as pl
from jax.experimental.pallas import tpu as pltpu

F32 = jnp.float32
BF16 = jnp.bfloat16

DEPTH = 2
FOX_HEADS = 8
FOX_HEAD_DIM = 128
FOX_WIDTH = FOX_HEADS * FOX_HEAD_DIM
SSM_GROUP = 16
SSM_STATE = 64
SSM_CHUNK = 16
SSM_ROW = SSM_CHUNK * SSM_GROUP
SWA_HEADS = 32
SWA_KV_HEADS = 4
SWA_HEAD_DIM = 64
SWA_GROUPS = SWA_HEADS // SWA_KV_HEADS
SWA_WINDOW = 128
ROPE_DIM = SWA_HEAD_DIM // 4
ROPE_THETA = 500000.0
CONV_WIDTH = 3
LN_EPS = 1e-5
DEEPNORM_ALPHA = (2.0 * DEPTH) ** 0.25

V7X_LANES = 128
V7X_VMEM_LIMIT_BYTES = 56 * 1024 * 1024
GROUPS_PER_SLAB = V7X_LANES // SSM_GROUP
STAGE_PITCH = 136
MASK_VALUE = -1e30
LOG2E = math.log2(math.e)

HIGHEST = lax.Precision.HIGHEST


def _params(*semantics):
    return pltpu.CompilerParams(dimension_semantics=semantics,
                                vmem_limit_bytes=V7X_VMEM_LIMIT_BYTES)


def _dot(a, b):
    return jnp.dot(a, b, preferred_element_type=F32)


def _dot_nt(a, b):
    return lax.dot_general(a, b, (((1,), (1,)), ((), ())), preferred_element_type=F32)


def _mm_kernel(a_ref, w_ref, o_ref):
    o_ref[...] = _dot(a_ref[...].astype(BF16), w_ref[...]).astype(o_ref.dtype)


def _matmul(a, w, *, bm, bn, out_dtype):
    bsz, s_len, k = a.shape
    n = w.shape[1]
    assert s_len % bm == 0 and n % bn == 0, (a.shape, w.shape, bm, bn)
    per_seq = s_len // bm
    rows = lambda i, j: (i // per_seq, i % per_seq)
    return pl.pallas_call(
        _mm_kernel,
        grid=(bsz * per_seq, n // bn),
        in_specs=[pl.BlockSpec((None, bm, k), lambda i, j: rows(i, j) + (0,)),
                  pl.BlockSpec((k, bn), lambda i, j: (0, j))],
        out_specs=pl.BlockSpec((None, bm, bn), lambda i, j: rows(i, j) + (j,)),
        out_shape=jax.ShapeDtypeStruct((bsz, s_len, n), out_dtype),
        compiler_params=_params("parallel", "arbitrary"),
        name="matmul",
    )(a, w)


def _mm_res_ln_kernel(*refs, n_pairs, sub):
    a_refs = refs[:n_pairs]
    w_refs = refs[n_pairs:2 * n_pairs]
    x_ref, g_ref, b_ref, o32_ref, o16_ref = refs[2 * n_pairs:]
    for r in range(x_ref.shape[0] // sub):
        rows = slice(r * sub, (r + 1) * sub)
        pre = DEEPNORM_ALPHA * x_ref[rows, :]
        for a_ref, w_ref in zip(a_refs, w_refs):
            pre = pre + _dot(a_ref[rows, :], w_ref[...])
        mu = jnp.mean(pre, axis=-1, keepdims=True)
        cen = pre - mu
        var = jnp.mean(cen * cen, axis=-1, keepdims=True)
        y = cen * lax.rsqrt(var + LN_EPS) * g_ref[...] + b_ref[...]
        o32_ref[rows, :] = y
        o16_ref[rows, :] = y.astype(BF16)


def _matmul_residual_layernorm(pairs, x32, gain, bias, *, bm, sub=256):
    m, d = x32.shape
    assert m % bm == 0 and bm % sub == 0, (m, bm, sub)
    n_pairs = len(pairs)
    a_list = [p[0] for p in pairs]
    w_list = [p[1] for p in pairs]
    row_spec = lambda width: pl.BlockSpec((bm, width), lambda i: (i, 0))
    const_spec = lambda rows: pl.BlockSpec((rows, d), lambda i: (0, 0), pipeline_mode=pl.Buffered(1))
    window_spec = lambda rows, row0: pl.BlockSpec((pl.Element(rows), pl.Element(d)), lambda i: (row0, 0),
                                                  pipeline_mode=pl.Buffered(1))
    in_specs = ([row_spec(a.shape[1]) for a in a_list]
                + [window_spec(a.shape[1], row0) for a, _, row0 in pairs]
                + [row_spec(d), const_spec(1), const_spec(1)])
    return pl.pallas_call(
        functools.partial(_mm_res_ln_kernel, n_pairs=n_pairs, sub=sub),
        grid=(m // bm,),
        in_specs=in_specs,
        out_specs=[row_spec(d), row_spec(d)],
        out_shape=[jax.ShapeDtypeStruct((m, d), F32), jax.ShapeDtypeStruct((m, d), BF16)],
        compiler_params=_params("parallel"),
        name="matmul_residual_layernorm",
    )(*a_list, *w_list, x32, gain.reshape(1, d).astype(F32), bias.reshape(1, d).astype(F32))


def _bf16_split3(v):
    hi = v.astype(BF16)
    r1 = v - hi.astype(F32)
    mid = r1.astype(BF16)
    lo = (r1 - mid.astype(F32)).astype(BF16)
    return hi, mid, lo


def _forget_gate_kernel(f_ref, bf_ref, c_ref, *, chunk):
    s_len = f_ref.shape[0]
    f = f_ref[...] + bf_ref[...]
    log_f = jnp.minimum(f, 0.0) - jnp.log(1.0 + jnp.exp(-jnp.abs(f)))
    row = lax.broadcasted_iota(jnp.int32, (chunk, chunk), 0)
    col = lax.broadcasted_iota(jnp.int32, (chunk, chunk), 1)
    tri = (row >= col).astype(BF16)
    carry = jnp.zeros((1, log_f.shape[1]), F32)
    for start in range(0, s_len, chunk):
        hi, mid, lo = _bf16_split3(log_f[start:start + chunk])
        part = _dot(tri, hi) + _dot(tri, mid) + _dot(tri, lo) + carry
        c_ref[start:start + chunk, :] = part
        carry = part[chunk - 1:chunk, :]


def _forget_gate_cumsum(f_logits, b_f, *, chunk=256):
    bsz, s_len, n = f_logits.shape
    return pl.pallas_call(
        functools.partial(_forget_gate_kernel, chunk=chunk),
        grid=(bsz,),
        in_specs=[pl.BlockSpec((None, s_len, n), lambda b: (b, 0, 0)),
                  pl.BlockSpec((1, n), lambda b: (0, 0))],
        out_specs=pl.BlockSpec((None, s_len, n), lambda b: (b, 0, 0)),
        out_shape=jax.ShapeDtypeStruct((bsz, s_len, n), F32),
        compiler_params=_params("parallel"),
        name="forget_gate_cumsum",
    )(f_logits, b_f)


def _bias_lanes(c, first):
    hi, mid, lo = (piece.astype(F32) for piece in _bf16_split3(c))
    lane = lax.broadcasted_iota(jnp.int32, (c.shape[0], V7X_LANES), 1)
    ones = jnp.where(lane < 6, 1.0, 0.0)
    return jnp.where(lane == first, hi, jnp.where(lane == first + 1, mid, jnp.where(lane == first + 2, lo, ones)))


def _fox_kernel(q_ref, k_ref, v_ref, c_ref, o_ref, kaug_ref, *, tile, heads):
    dh = FOX_HEAD_DIM
    aug = 2 * dh
    h0 = pl.program_id(1) * heads
    n_tiles = q_ref.shape[0] // tile

    def head_column(c_all, hh):
        lane = lax.broadcasted_iota(jnp.int32, c_all.shape, 1)
        return jnp.sum(jnp.where(lane == h0 + hh, c_all, 0.0), axis=1, keepdims=True) * LOG2E

    c_all = c_ref[...]
    for hh in range(heads):
        kaug_ref[:, hh * aug:hh * aug + dh] = k_ref[:, hh * dh:(hh + 1) * dh]
        kaug_ref[:, hh * aug + dh:(hh + 1) * aug] = _bias_lanes(-head_column(c_all, hh), 3).astype(BF16)

    def update(state, s, v):
        m, l, acc = state
        m_new = jnp.maximum(m, jnp.max(s, axis=1, keepdims=True))
        alpha = jnp.exp2(m - m_new)
        p = jnp.exp2(s - m_new)
        return (m_new, alpha * l + jnp.sum(p, axis=1, keepdims=True), alpha * acc + _dot(p.astype(BF16), v))

    for qt in range(n_tiles):
        rows = slice(qt * tile, (qt + 1) * tile)
        cq_all = c_ref[rows, :]
        for hh in range(heads):
            q_aug = jnp.concatenate([q_ref[rows, hh * dh:(hh + 1) * dh],
                                     _bias_lanes(head_column(cq_all, hh), 0).astype(BF16)], axis=1)
            state = (jnp.full((tile, 1), MASK_VALUE, F32), jnp.zeros((tile, 1), F32), jnp.zeros((tile, dh), F32))
            for kt in range(qt + 1):
                keys = slice(kt * tile, (kt + 1) * tile)
                s = _dot_nt(q_aug, kaug_ref[keys, hh * aug:(hh + 1) * aug])
                if kt == qt:
                    row = lax.broadcasted_iota(jnp.int32, s.shape, 0)
                    col = lax.broadcasted_iota(jnp.int32, s.shape, 1)
                    s = jnp.where(row >= col, s, MASK_VALUE)
                state = update(state, s, v_ref[keys, hh * dh:(hh + 1) * dh])
            _, l, acc = state
            o_ref[rows, hh * dh:(hh + 1) * dh] = (acc / l).astype(o_ref.dtype)


def _forgetting_attention(qkv, c_col, *, tile=1024, heads=2):
    bsz, s_len, _ = qkv.shape
    assert s_len % tile == 0 and FOX_HEADS % heads == 0, (s_len, tile, heads)
    width = heads * FOX_HEAD_DIM
    nblk = FOX_WIDTH // width
    seq_spec = lambda off: pl.BlockSpec((None, s_len, width), lambda b, h: (b, 0, off + h))
    return pl.pallas_call(
        functools.partial(_fox_kernel, tile=tile, heads=heads),
        grid=(bsz, nblk),
        in_specs=[seq_spec(0), seq_spec(nblk), seq_spec(2 * nblk),
                  pl.BlockSpec((None, s_len, c_col.shape[2]), lambda b, h: (b, 0, 0))],
        out_specs=seq_spec(0),
        out_shape=jax.ShapeDtypeStruct((bsz, s_len, FOX_WIDTH), BF16),
        scratch_shapes=[pltpu.VMEM((s_len, 2 * width), BF16)],
        compiler_params=_params("parallel", "parallel"),
        name="forgetting_attention",
    )(qkv, qkv, qkv, c_col)


def _cmul(ar, ai, br, bi):
    return ar * br - ai * bi, ar * bi + ai * br


def _cpow(ar, ai, e, n_bits):
    rr = jnp.ones_like(ar)
    ri = jnp.zeros_like(ai)
    for bit in range(n_bits):
        nr, ni = _cmul(rr, ri, ar, ai)
        take = ((e >> bit) & 1) == 1
        rr = jnp.where(take, nr, rr)
        ri = jnp.where(take, ni, ri)
        if bit + 1 < n_bits:
            ar, ai = _cmul(ar, ai, ar, ai)
    return rr, ri


def _discretise(lr, li, dt):
    mag = jnp.exp(lr * dt)
    return mag * jnp.cos(li * dt), mag * jnp.sin(li * dt)


def _s5_prep_group(lam_row_ref, lam_col_ref, dt_ref, bt_re_ref, bt_im_ref, ct_re_ref, ct_im_ref,
                   d_ref, t0_ref, ws_re_ref, ws_im_ref, wc_re_ref, wc_im_ref, a_chunk_ref):
    chunk, grp, p = SSM_CHUNK, SSM_GROUP, SSM_STATE
    n_bits = chunk.bit_length()
    dt = jnp.exp(dt_ref[...])
    lr_r, li_r = lam_row_ref[0:1, :], lam_row_ref[1:2, :]
    ar_r, ai_r = _discretise(lr_r, li_r, dt)
    den = lr_r * lr_r + li_r * li_r
    xr, xi = ar_r - 1.0, ai_r
    g_re = (xr * lr_r + xi * li_r) / den
    g_im = (xi * lr_r - xr * li_r) / den
    bbt_re, bbt_im = _cmul(g_re, g_im, bt_re_ref[...], bt_im_ref[...])
    bx_re = jnp.concatenate([bbt_re] * chunk, axis=0)
    bx_im = jnp.concatenate([bbt_im] * chunk, axis=0)
    row_i = lax.broadcasted_iota(jnp.int32, (SSM_ROW, p), 0) // grp
    pr, pi = _cpow(jnp.broadcast_to(ar_r, (SSM_ROW, p)), jnp.broadcast_to(ai_r, (SSM_ROW, p)),
                   chunk - 1 - row_i, n_bits)
    ws_re, ws_im = _cmul(pr, pi, bx_re, bx_im)
    ws_re_ref[...] = ws_re.astype(BF16)
    ws_im_ref[...] = ws_im.astype(BF16)
    ac_r, ac_i = _cpow(ar_r, ai_r, jnp.full((1, p), chunk, jnp.int32), n_bits)
    a_chunk_ref[0:1, :] = ac_r
    a_chunk_ref[1:2, :] = ac_i
    lr_c, li_c = lam_col_ref[:, 0:1], lam_col_ref[:, 1:2]
    ar_c, ai_c = _discretise(lr_c, li_c, dt)
    rep = (lax.broadcasted_iota(jnp.int32, (grp, SSM_ROW), 0)
           == lax.broadcasted_iota(jnp.int32, (grp, SSM_ROW), 1) % grp).astype(F32)
    cx_re = jnp.dot(ct_re_ref[...], rep, precision=HIGHEST, preferred_element_type=F32)
    cx_im = jnp.dot(ct_im_ref[...], rep, precision=HIGHEST, preferred_element_type=F32)
    dx = jnp.dot(d_ref[...], rep, precision=HIGHEST, preferred_element_type=F32)
    col_j = lax.broadcasted_iota(jnp.int32, (p, SSM_ROW), 1) // grp
    ab_r = jnp.broadcast_to(ar_c, (p, SSM_ROW))
    ab_i = jnp.broadcast_to(ai_c, (p, SSM_ROW))
    qr, qi = _cpow(ab_r, ab_i, col_j, n_bits)
    f_re, f_im = _cmul(qr, qi, cx_re, cx_im)
    e_re, e_im = _cmul(f_re, f_im, ab_r, ab_i)
    wc_re_ref[...] = e_re.astype(BF16)
    wc_im_ref[...] = (-e_im).astype(BF16)
    r0 = (jnp.dot(bbt_re, f_re, precision=HIGHEST, preferred_element_type=F32)
          - jnp.dot(bbt_im, f_im, precision=HIGHEST, preferred_element_type=F32))
    lane = lax.broadcasted_iota(jnp.int32, (grp, SSM_ROW), 1)
    sub = lax.broadcasted_iota(jnp.int32, (grp, SSM_ROW), 0)
    for i in range(chunk):
        blk = r0 if i == 0 else pltpu.roll(r0, i * grp, 1)
        blk = jnp.where(lane >= i * grp, blk, 0.0)
        blk = blk + jnp.where(lane == i * grp + sub, dx, 0.0)
        t0_ref[i * grp:(i + 1) * grp, :] = blk.astype(BF16)


def _s5_prep_kernel(*refs, groups):
    for g in range(groups):
        _s5_prep_group(*(ref.at[g] for ref in refs))


def _s5_prepare(lam_re, lam_im, log_step, b_re, b_im, c_re, c_im, d_skip, *, groups_per_step=8):
    g, p = lam_re.shape
    grp = SSM_GROUP
    lam_row = jnp.stack([lam_re, lam_im], axis=1).astype(F32)
    lam_col = jnp.stack([lam_re, lam_im], axis=2).astype(F32)
    dt = log_step.reshape(g, 1, 1).astype(F32)
    bt_re = jnp.swapaxes(b_re, 1, 2).astype(F32)
    bt_im = jnp.swapaxes(b_im, 1, 2).astype(F32)
    ct_re = jnp.swapaxes(c_re, 1, 2).astype(F32)
    ct_im = jnp.swapaxes(c_im, 1, 2).astype(F32)
    d3 = d_skip.reshape(g, 1, grp).astype(F32)

    gs = groups_per_step

    def spec(*shape):
        return pl.BlockSpec((gs,) + shape, lambda i: (i,) + (0,) * len(shape))

    return pl.pallas_call(
        functools.partial(_s5_prep_kernel, groups=gs),
        grid=(g // gs,),
        in_specs=[spec(2, p), spec(p, 2), spec(1, 1), spec(grp, p), spec(grp, p),
                  spec(p, grp), spec(p, grp), spec(1, grp)],
        out_specs=[spec(SSM_ROW, SSM_ROW), spec(SSM_ROW, p), spec(SSM_ROW, p),
                   spec(p, SSM_ROW), spec(p, SSM_ROW), spec(2, p)],
        out_shape=[jax.ShapeDtypeStruct((g, SSM_ROW, SSM_ROW), BF16),
                   jax.ShapeDtypeStruct((g, SSM_ROW, p), BF16),
                   jax.ShapeDtypeStruct((g, SSM_ROW, p), BF16),
                   jax.ShapeDtypeStruct((g, p, SSM_ROW), BF16),
                   jax.ShapeDtypeStruct((g, p, SSM_ROW), BF16),
                   jax.ShapeDtypeStruct((g, 2, p), F32)],
        compiler_params=_params("parallel"),
        name="s5_prepare",
    )(lam_row, lam_col, dt, bt_re, bt_im, ct_re, ct_im, d3)


def _gelu_tanh(y):
    return 0.5 * y * (1.0 + jnp.tanh(math.sqrt(2.0 / math.pi) * (y + 0.044715 * (y * y * y))))


def _s5_scan_kernel(u_ref, t0_ref, ws_re_ref, ws_im_ref, wc_re_ref, wc_im_ref, a_ref, o_ref,
                    sre_ref, sim_ref, hre_ref, him_ref, *, groups, bsz, n_chunks):
    for g in range(groups):
        u = u_ref[g]
        sre_ref[g] = _dot(u, ws_re_ref[g])
        sim_ref[g] = _dot(u, ws_im_ref[g])
    ar = a_ref[:, 0:1, :]
    ai = a_ref[:, 1:2, :]

    def step(k, carry):
        hr, hi = carry
        rows = pl.ds(pl.multiple_of(k * bsz, bsz), bsz)
        hre_ref[:, rows, :] = hr
        him_ref[:, rows, :] = hi
        sr = sre_ref[:, rows, :]
        si = sim_ref[:, rows, :]
        return ar * hr - ai * hi + sr, ar * hi + ai * hr + si

    zero = jnp.zeros((groups, bsz, SSM_STATE), F32)
    lax.fori_loop(0, n_chunks, step, (zero, zero))
    for g in range(groups):
        y = (_dot(u_ref[g], t0_ref[g]) + _dot(hre_ref[g].astype(BF16), wc_re_ref[g])
             + _dot(him_ref[g].astype(BF16), wc_im_ref[g]))
        o_ref[g] = _gelu_tanh(y).astype(o_ref.dtype)


def _s5_scan_gelu(u_rows, mats, *, bsz, groups_per_step=8):
    t0, ws_re, ws_im, wc_re, wc_im, a_chunk = mats
    g, rows, _ = u_rows.shape
    gs = groups_per_step
    p = SSM_STATE
    assert g % gs == 0 and rows % bsz == 0 and bsz % 8 == 0, (u_rows.shape, bsz, gs)

    def spec(*shape):
        return pl.BlockSpec((gs,) + shape, lambda i: (i,) + (0,) * len(shape))

    return pl.pallas_call(
        functools.partial(_s5_scan_kernel, groups=gs, bsz=bsz, n_chunks=rows // bsz),
        grid=(g // gs,),
        in_specs=[spec(rows, SSM_ROW), spec(SSM_ROW, SSM_ROW), spec(SSM_ROW, p), spec(SSM_ROW, p),
                  spec(p, SSM_ROW), spec(p, SSM_ROW), spec(2, p)],
        out_specs=spec(rows, SSM_ROW),
        out_shape=jax.ShapeDtypeStruct((g, rows, SSM_ROW), BF16),
        scratch_shapes=[pltpu.VMEM((gs, rows, p), F32), pltpu.VMEM((gs, rows, p), F32),
                        pltpu.VMEM((gs, rows, p), F32), pltpu.VMEM((gs, rows, p), F32)],
        compiler_params=_params("parallel"),
        name="s5_scan_gelu",
    )(u_rows, t0, ws_re, ws_im, wc_re, wc_im, a_chunk)


def _sigmoid(x):
    return 1.0 / (1.0 + jnp.exp(-x))


def _block_transpose(xs):
    n = GROUPS_PER_SLAB
    width = xs[0].shape[1]
    blk = (lax.broadcasted_iota(jnp.int32, (1, width), 1) // SSM_GROUP) % n
    xs = list(xs)
    d = n // 2
    while d >= 1:
        high = (blk & d) != 0
        nxt = list(xs)
        for i in range(n):
            if i & d:
                continue
            lo_arr, hi_arr = xs[i], xs[i + d]
            nxt[i] = jnp.where(high, pltpu.roll(hi_arr, SSM_GROUP * d, 1), lo_arr)
            nxt[i + d] = jnp.where(high, hi_arr, pltpu.roll(lo_arr, width - SSM_GROUP * d, 1))
        xs = nxt
        d //= 2
    return xs


def _u_rows_kernel(x_ref, w_ref, o_ref, f_ref, stage_ref, *, bsz, chunks):
    tb = x_ref.shape[1]
    uf = _dot(x_ref[...].reshape(bsz * tb, x_ref.shape[2]).astype(BF16), w_ref[...])
    n_slabs = o_ref.shape[0] // GROUPS_PER_SLAB
    u = uf[:, :n_slabs * V7X_LANES]
    f_ref[...] = uf[:, n_slabs * V7X_LANES:].reshape(f_ref.shape)
    for v in range(n_slabs):
        for b in range(bsz):
            stage_ref[v, b * STAGE_PITCH:b * STAGE_PITCH + tb, :] = (
                u[b * tb:(b + 1) * tb, v * V7X_LANES:(v + 1) * V7X_LANES])
    n = GROUPS_PER_SLAB
    for jh in range(SSM_CHUNK // n):
        xs = []
        for jl in range(n):
            j = jh * n + jl
            xs.append(jnp.concatenate(
                [jnp.concatenate([stage_ref.at[v][pl.ds(SSM_CHUNK * c + j, bsz, stride=STAGE_PITCH), :]
                                  for v in range(n_slabs)], axis=1) for c in range(chunks)], axis=0))
        ys = _block_transpose(xs)
        for e in range(n):
            for v in range(n_slabs):
                o_ref[n * v + e, :, jh * V7X_LANES:(jh + 1) * V7X_LANES] = (
                    ys[e][:, v * V7X_LANES:(v + 1) * V7X_LANES].astype(o_ref.dtype))


def _u_projection_rows(x, w_uf, *, chunks=8):
    bsz, s_len, d = x.shape
    width = w_uf.shape[1] - V7X_LANES
    g = width // SSM_GROUP
    tb = chunks * SSM_CHUNK
    rows = chunks * bsz
    assert bsz % 8 == 0 and s_len % tb == 0 and tb <= STAGE_PITCH and g % GROUPS_PER_SLAB == 0, x.shape
    return pl.pallas_call(
        functools.partial(_u_rows_kernel, bsz=bsz, chunks=chunks),
        grid=(s_len // tb,),
        in_specs=[pl.BlockSpec((bsz, tb, d), lambda i: (0, i, 0)),
                  pl.BlockSpec((d, width + V7X_LANES), lambda i: (0, 0), pipeline_mode=pl.Buffered(1))],
        out_specs=[pl.BlockSpec((g, rows, SSM_ROW), lambda i: (0, i, 0)),
                   pl.BlockSpec((bsz, tb, V7X_LANES), lambda i: (0, i, 0))],
        out_shape=[jax.ShapeDtypeStruct((g, (s_len // SSM_CHUNK) * bsz, SSM_ROW), BF16),
                   jax.ShapeDtypeStruct((bsz, s_len, V7X_LANES), F32)],
        scratch_shapes=[pltpu.VMEM((width // V7X_LANES, bsz * STAGE_PITCH, V7X_LANES), F32)],
        compiler_params=_params("parallel"),
        name="u_projection_rows",
    )(x, w_uf)


def _glu_rows_kernel(gy_ref, w_ref, o_ref, stage_ref, *, bsz, chunks):
    tb = o_ref.shape[1]
    n = GROUPS_PER_SLAB
    n_slabs = gy_ref.shape[0] // n
    for jh in range(SSM_CHUNK // n):
        ys = [jnp.concatenate([gy_ref[n * v + e, :, jh * V7X_LANES:(jh + 1) * V7X_LANES].astype(F32)
                               for v in range(n_slabs)], axis=1) for e in range(n)]
        xs = _block_transpose(ys)
        for jl in range(n):
            j = jh * n + jl
            for c in range(chunks):
                for v in range(n_slabs):
                    stage_ref.at[v][pl.ds(SSM_CHUNK * c + j, bsz, stride=STAGE_PITCH), :] = (
                        xs[jl][c * bsz:(c + 1) * bsz, v * V7X_LANES:(v + 1) * V7X_LANES])
    a = jnp.concatenate(
        [jnp.concatenate([stage_ref[v, b * STAGE_PITCH:b * STAGE_PITCH + tb, :] for v in range(n_slabs)], axis=1)
         for b in range(bsz)], axis=0).astype(BF16)
    z = _dot(a, w_ref[...])
    half = z.shape[1] // 2
    o_ref[...] = (z[:, :half] * _sigmoid(z[:, half:])).astype(o_ref.dtype).reshape(o_ref.shape)


def _glu_from_rows(gy_rows, w_glu, *, bsz, chunks=8):
    g, total_rows, _ = gy_rows.shape
    width = g * SSM_GROUP
    n_out = w_glu.shape[1] // 2
    s_len = total_rows // bsz * SSM_CHUNK
    tb = chunks * SSM_CHUNK
    rows = chunks * bsz
    assert bsz % 8 == 0 and s_len % tb == 0 and tb <= STAGE_PITCH and g % GROUPS_PER_SLAB == 0, gy_rows.shape
    return pl.pallas_call(
        functools.partial(_glu_rows_kernel, bsz=bsz, chunks=chunks),
        grid=(s_len // tb,),
        in_specs=[pl.BlockSpec((g, rows, SSM_ROW), lambda i: (0, i, 0)),
                  pl.BlockSpec((width, 2 * n_out), lambda i: (0, 0), pipeline_mode=pl.Buffered(1))],
        out_specs=pl.BlockSpec((bsz, tb, n_out), lambda i: (0, i, 0)),
        out_shape=jax.ShapeDtypeStruct((bsz, s_len, n_out), BF16),
        scratch_shapes=[pltpu.VMEM((width // V7X_LANES, bsz * STAGE_PITCH, V7X_LANES), F32)],
        compiler_params=_params("parallel"),
        name="glu_from_rows",
    )(gy_rows, w_glu)


def _ffn_up_kernel(x_ref, wg_ref, wv_ref, cwg_ref, cbg_ref, cwv_ref, cbv_ref, o_ref,
                   wg16_ref, wv16_ref, hg_ref, hv_ref, *, blocks_per_seq):
    i = pl.program_id(1)
    bm, bn = o_ref.shape
    pad = hg_ref.shape[0] - bm

    @pl.when(i == 0)
    def _():
        wg16_ref[...] = wg_ref[0].astype(BF16)
        wv16_ref[...] = wv_ref[0].astype(BF16)

    @pl.when(i % blocks_per_seq == 0)
    def _():
        hg_ref[0:pad, :] = jnp.zeros((pad, bn), F32)
        hv_ref[0:pad, :] = jnp.zeros((pad, bn), F32)

    @pl.when(i % blocks_per_seq != 0)
    def _():
        hg_ref[0:pad, :] = hg_ref[bm:bm + pad, :]
        hv_ref[0:pad, :] = hv_ref[bm:bm + pad, :]

    x = x_ref[...]
    conv = []
    for w16, h_ref, cw_ref, cb_ref in ((wg16_ref, hg_ref, cwg_ref, cbg_ref), (wv16_ref, hv_ref, cwv_ref, cbv_ref)):
        h = _dot(x, w16[...])
        h_ref[pad:pad + bm, :] = h
        cw = cw_ref[0]
        conv.append(cb_ref[0] + cw[0:1, :] * h_ref[pad - 2:pad - 2 + bm, :]
                    + cw[1:2, :] * h_ref[pad - 1:pad - 1 + bm, :] + cw[2:3, :] * h)
    gate, val = conv
    o_ref[...] = (gate * _sigmoid(gate) * val).astype(o_ref.dtype)


def _ffn_up(x16, layer, w_up, conv_w, conv_b, *, col0, n_cols, seq_len, bm, bn):
    m, d = x16.shape
    f = w_up.shape[2] // 2
    assert m % bm == 0 and seq_len % bm == 0 and n_cols % bn == 0 and bn % V7X_LANES == 0, (m, bm, n_cols, bn)
    pad = 8
    cb = conv_b.reshape(conv_b.shape[0], 1, 2 * f)
    gate_col = lambda j: pl.multiple_of(col0 + j * bn, V7X_LANES)
    val_col = lambda j: pl.multiple_of(f + col0 + j * bn, V7X_LANES)
    window = lambda rows: (pl.Element(1), pl.Element(rows), pl.Element(bn))
    gate_spec = lambda rows: pl.BlockSpec(window(rows), lambda j, i: (layer, 0, gate_col(j)))
    val_spec = lambda rows: pl.BlockSpec(window(rows), lambda j, i: (layer, 0, val_col(j)))
    return pl.pallas_call(
        functools.partial(_ffn_up_kernel, blocks_per_seq=seq_len // bm),
        grid=(n_cols // bn, m // bm),
        in_specs=[pl.BlockSpec((bm, d), lambda j, i: (i, 0)),
                  gate_spec(d), val_spec(d), gate_spec(CONV_WIDTH), gate_spec(1),
                  val_spec(CONV_WIDTH), val_spec(1)],
        out_specs=pl.BlockSpec((bm, bn), lambda j, i: (i, j)),
        out_shape=jax.ShapeDtypeStruct((m, n_cols), BF16),
        scratch_shapes=[pltpu.VMEM((d, bn), BF16), pltpu.VMEM((d, bn), BF16),
                        pltpu.VMEM((pad + bm, bn), F32), pltpu.VMEM((pad + bm, bn), F32)],
        compiler_params=_params("arbitrary", "arbitrary"),
        name="ffn_up_conv_gate",
    )(x16, w_up, w_up, conv_w, cb, conv_w, cb)


def _cast_kernel(src_ref, dst_ref):
    dst_ref[...] = src_ref[...].astype(dst_ref.dtype)


def _layer_to_bf16(stacked, layer, *, steps=8):
    _, k, d = stacked.shape
    assert k % (16 * steps) == 0, (k, steps)
    rows = k // steps
    return pl.pallas_call(
        _cast_kernel,
        grid=(steps,),
        in_specs=[pl.BlockSpec((None, rows, d), lambda r: (layer, r, 0))],
        out_specs=pl.BlockSpec((rows, d), lambda r: (r, 0)),
        out_shape=jax.ShapeDtypeStruct((k, d), BF16),
        compiler_params=_params("parallel"),
        name="layer_to_bf16",
    )(stacked)


def _conv_ffn(x32, x16, layer, w_up, conv_w, conv_b, w_down, ln_g, ln_b, *, seq_len, bn=512):
    f = w_down.shape[1]
    w_up, conv_w, conv_b = w_up.astype(F32), conv_w.astype(F32), conv_b.astype(F32)
    w_down16 = _layer_to_bf16(w_down, layer)
    main = (f // bn) * bn
    pairs = []
    for col0, n_cols, width in ((0, main, bn), (main, f - main, f - main)):
        if n_cols:
            act = _ffn_up(x16, layer, w_up, conv_w, conv_b, col0=col0, n_cols=n_cols, seq_len=seq_len,
                          bm=1024, bn=width)
            pairs.append((act, w_down16, col0))
    return _matmul_residual_layernorm(pairs, x32, ln_g, ln_b, bm=256)


def _rope_proj_kernel(a_ref, w_ref, pos_ref, o_ref, *, rope_cols, q_cols, q_scale, sub):
    hd, half = SWA_HEAD_DIM, ROPE_DIM // 2
    bm, n = o_ref.shape
    lane1 = lax.broadcasted_iota(jnp.int32, (1, V7X_LANES), 1) % hd
    inv_freq = jnp.where(lane1 < ROPE_DIM,
                         jnp.exp((lane1 % half).astype(F32) * (-math.log(ROPE_THETA) / half)), 0.0)
    sign = jnp.where(lane1 < half, -1.0, 1.0)
    lane = lax.broadcasted_iota(jnp.int32, (1, n), 1)
    first = (lane & (hd - 1)) < half
    reps = n // V7X_LANES
    for r in range(bm // sub):
        rows = slice(r * sub, (r + 1) * sub)
        ang = pos_ref[rows, :].astype(F32) * inv_freq
        cos = jnp.concatenate([jnp.cos(ang)] * reps, axis=1)
        sin = jnp.concatenate([jnp.sin(ang) * sign] * reps, axis=1)
        acc = _dot(a_ref[rows, :], w_ref[...])
        partner = jnp.where(first, pltpu.roll(acc, n - half, 1), pltpu.roll(acc, half, 1))
        rot = acc * cos + partner * sin
        out = jnp.where(lane < rope_cols, rot, acc) * jnp.where(lane < q_cols, q_scale, 1.0)
        o_ref[rows, :] = out.astype(o_ref.dtype)


def _rope_projection(x16, w, pos_col, *, rope_cols, q_cols, q_scale, bm, sub=256):
    m, k = x16.shape
    n = w.shape[1]
    assert m % bm == 0 and bm % sub == 0 and n % V7X_LANES == 0, (m, n, bm, sub)
    return pl.pallas_call(
        functools.partial(_rope_proj_kernel, rope_cols=rope_cols, q_cols=q_cols, q_scale=q_scale, sub=sub),
        grid=(m // bm,),
        in_specs=[pl.BlockSpec((bm, k), lambda i: (i, 0)),
                  pl.BlockSpec((k, n), lambda i: (0, 0), pipeline_mode=pl.Buffered(1)),
                  pl.BlockSpec((bm, 1), lambda i: (i, 0))],
        out_specs=pl.BlockSpec((bm, n), lambda i: (i, 0)),
        out_shape=jax.ShapeDtypeStruct((m, n), BF16),
        compiler_params=_params("parallel"),
        name="rope_projection",
    )(x16, w, pos_col)


def _swa_kernel(sink_ref, q_ref, kp_ref, kc_ref, vp_ref, vc_ref, o_ref, *, blocks):
    n = pl.program_id(1)
    w, hd, lanes = SWA_WINDOW, SWA_HEAD_DIM, V7X_LANES
    pairs = SWA_GROUPS // 2
    qi = lax.broadcasted_iota(jnp.int32, (w, 2 * w), 0)
    kj = lax.broadcasted_iota(jnp.int32, (w, 2 * w), 1)
    rel = w + qi - kj
    band = (rel >= 0) & (rel < w)
    lane_kv = lax.broadcasted_iota(jnp.int32, (2 * w, lanes), 1)
    lane_o = lax.broadcasted_iota(jnp.int32, (w, lanes), 1)

    def both_halves(prev, cur, hk):
        x = jnp.concatenate([prev, cur], axis=0).astype(F32)
        upper = hk % 2 == 1
        own = jnp.where((lane_kv >= hd) if upper else (lane_kv < hd), x, 0.0)
        other = pltpu.roll(own, hd, 1)
        lo, hi = (other, own) if upper else (own, other)
        return jnp.concatenate([lo, hi], axis=0).astype(BF16)

    for blk in range(blocks):
        rows = slice(blk * w, (blk + 1) * w)
        valid = band & ((n > 0) | (kj >= w)) if blk == 0 else band
        bias = jnp.where(valid, 0.0, MASK_VALUE)
        for hk in range(SWA_KV_HEADS):
            slab = slice((hk // 2) * lanes, (hk // 2 + 1) * lanes)
            prev_rows = slice((blk - 1) * w, blk * w)
            k_prev = kp_ref[:, slab] if blk == 0 else kc_ref[prev_rows, slab]
            v_prev = vp_ref[:, slab] if blk == 0 else vc_ref[prev_rows, slab]
            kk = both_halves(k_prev, kc_ref[rows, slab], hk)
            vv = both_halves(v_prev, vc_ref[rows, slab], hk)
            slabs = [slice((hk * pairs + r) * lanes, (hk * pairs + r + 1) * lanes) for r in range(pairs)]
            qs = jnp.concatenate([q_ref[rows, sl] for sl in slabs], axis=0)
            s_all = _dot_nt(qs, kk)
            probs, inv_den = [], []
            for r in range(pairs):
                halves, inv = [], []
                for half in range(2):
                    sink = sink_ref[hk * SWA_GROUPS + 2 * r + half] * LOG2E
                    s = s_all[r * w:(r + 1) * w, half * 2 * w:(half + 1) * 2 * w] + bias
                    m = jnp.maximum(jnp.max(s, axis=1, keepdims=True), sink)
                    p = jnp.exp2(s - m)
                    inv.append(1.0 / (jnp.sum(p, axis=1, keepdims=True) + jnp.exp2(sink - m)))
                    halves.append(p.astype(BF16))
                probs.append(jnp.concatenate(halves, axis=1))
                inv_den.append(jnp.where(lane_o < hd, inv[0], inv[1]))
            o_all = _dot(jnp.concatenate(probs, axis=0), vv)
            for r in range(pairs):
                o_ref[rows, slabs[r]] = (o_all[r * w:(r + 1) * w] * inv_den[r]).astype(o_ref.dtype)


def _sliding_window_attention(qkv, sinks, *, blocks=2):
    bsz, s_len, _ = qkv.shape
    w, hd = SWA_WINDOW, SWA_HEAD_DIM
    rows = blocks * w
    assert s_len % rows == 0, (s_len, rows)
    qw = SWA_HEADS * hd
    kw = SWA_KV_HEADS * hd
    k_blk = qw // kw
    v_blk = k_blk + 1
    prev = lambda n: jnp.maximum(n * blocks - 1, 0)
    return pl.pallas_call(
        functools.partial(_swa_kernel, blocks=blocks),
        grid=(bsz, s_len // rows),
        in_specs=[pl.BlockSpec(memory_space=pltpu.SMEM),
                  pl.BlockSpec((None, rows, qw), lambda b, n: (b, n, 0)),
                  pl.BlockSpec((None, w, kw), lambda b, n: (b, prev(n), k_blk)),
                  pl.BlockSpec((None, rows, kw), lambda b, n: (b, n, k_blk)),
                  pl.BlockSpec((None, w, kw), lambda b, n: (b, prev(n), v_blk)),
                  pl.BlockSpec((None, rows, kw), lambda b, n: (b, n, v_blk))],
        out_specs=pl.BlockSpec((None, rows, qw), lambda b, n: (b, n, 0)),
        out_shape=jax.ShapeDtypeStruct((bsz, s_len, qw), BF16),
        compiler_params=_params("parallel", "arbitrary"),
        name="sliding_window_attention",
    )(sinks.astype(F32), qkv, qkv, qkv, qkv, qkv)


def _even_mixer(x32, xa, bsz, s_len, w_in, b_f, lam_re, lam_im, log_step, b_re, b_im, c_re, c_im,
                d_skip, w_glu, w_out, ln_g, ln_b):
    m = x32.shape[0]
    fw = FOX_WIDTH
    nh = FOX_HEADS
    q_factor = LOG2E / math.sqrt(FOX_HEAD_DIM)
    w_qkv = jnp.concatenate([w_in[:, :fw] * q_factor, w_in[:, fw:3 * fw]], axis=1).astype(BF16)
    w_uf = jnp.concatenate([w_in[:, 3 * fw + nh:],
                            jnp.pad(w_in[:, 3 * fw:3 * fw + nh], ((0, 0), (0, V7X_LANES - nh)))],
                           axis=1).astype(BF16)
    bias_f = jnp.pad(b_f, (0, V7X_LANES - nh)).reshape(1, V7X_LANES).astype(F32)

    qkv = _matmul(xa, w_qkv, bm=1024, bn=3 * fw // 2, out_dtype=BF16)
    mats = _s5_prepare(lam_re, lam_im, log_step, b_re, b_im, c_re, c_im, d_skip)
    u_rows, f_logits = _u_projection_rows(xa, w_uf)
    c_col = _forget_gate_cumsum(f_logits, bias_f)
    fox = _forgetting_attention(qkv, c_col).reshape(m, fw)
    gy_rows = _s5_scan_gelu(u_rows, mats, bsz=bsz)
    ssm = _glu_from_rows(gy_rows, w_glu.astype(BF16), bsz=bsz).reshape(m, -1)
    w_out16 = w_out.astype(BF16)
    return _matmul_residual_layernorm([(fox, w_out16, 0), (ssm, w_out16, fw)], x32, ln_g, ln_b, bm=512)


def _odd_mixer(x32, x16, bsz, s_len, positions, w_in, sinks, w_out, ln_g, ln_b):
    m = x16.shape[0]
    rope_cols = (SWA_HEADS + SWA_KV_HEADS) * SWA_HEAD_DIM
    qkv = _rope_projection(x16, w_in.astype(BF16), positions.reshape(m, 1).astype(jnp.int32),
                           rope_cols=rope_cols, q_cols=SWA_HEADS * SWA_HEAD_DIM,
                           q_scale=LOG2E / math.sqrt(SWA_HEAD_DIM), bm=1024)
    o = _sliding_window_attention(qkv.reshape(bsz, s_len, -1), sinks).reshape(m, -1)
    return _matmul_residual_layernorm([(o, w_out.astype(BF16), 0)], x32, ln_g, ln_b, bm=512)


def kernel(x, positions, ev_w_in, ev_b_f, ev_lambda_re, ev_lambda_im, ev_log_step, ev_ssm_b_re,
           ev_ssm_b_im, ev_ssm_c_re, ev_ssm_c_im, ev_ssm_d, ev_w_glu, ev_w_out, od_w_in, od_sinks,
           od_w_out, ln_mix_g, ln_mix_b, ffn_w_up, ffn_conv_w, ffn_conv_b, ffn_w_down, ln_ffn_g,
           ln_ffn_b):
    bsz, s_len, d = x.shape
    x32 = x.reshape(bsz * s_len, d).astype(F32)
    x16 = None
    for i in range(DEPTH):
        j = i // 2
        if i % 2 == 0:
            xa = x.astype(F32) if x16 is None else x16.reshape(bsz, s_len, d)
            x32, x16 = _even_mixer(x32, xa, bsz, s_len, ev_w_in[j], ev_b_f[j], ev_lambda_re[j],
                                   ev_lambda_im[j], ev_log_step[j], ev_ssm_b_re[j], ev_ssm_b_im[j],
                                   ev_ssm_c_re[j], ev_ssm_c_im[j], ev_ssm_d[j], ev_w_glu[j],
                                   ev_w_out[j], ln_mix_g[i], ln_mix_b[i])
        else:
            x16 = x32.astype(BF16) if x16 is None else x16
            x32, x16 = _odd_mixer(x32, x16, bsz, s_len, positions, od_w_in[j], od_sinks[j],
                                  od_w_out[j], ln_mix_g[i], ln_mix_b[i])
        x32, x16 = _conv_ffn(x32, x16, i, ffn_w_up, ffn_conv_w, ffn_conv_b, ffn_w_down,
                             ln_ffn_g[i], ln_ffn_b[i], seq_len=s_len)
    return x32.reshape(bsz, s_len, d).astype(x.dtype)
```

```python
import functools
import math

import jax
import jax.numpy as jnp
from jax import lax
from jax.experimental import pallas as pl
from jax.experimental.pallas import tpu as pltpu

F32 = jnp.float32
BF16 = jnp.bfloat16

DEPTH = 2
FOX_HEADS = 8
FOX_HEAD_DIM = 128
FOX_WIDTH = FOX_HEADS * FOX_HEAD_DIM
SSM_GROUP = 16
SSM_STATE = 64
SSM_CHUNK = 16
SSM_ROW = SSM_CHUNK * SSM_GROUP
SWA_HEADS = 32
SWA_KV_HEADS = 4
SWA_HEAD_DIM = 64
SWA_GROUPS = SWA_HEADS // SWA_KV_HEADS
SWA_WINDOW = 128
ROPE_DIM = SWA_HEAD_DIM // 4
ROPE_THETA = 500000.0
CONV_WIDTH = 3
LN_EPS = 1e-5
DEEPNORM_ALPHA = (2.0 * DEPTH) ** 0.25

V7X_LANES = 128
V7X_VMEM_LIMIT_BYTES = 56 * 1024 * 1024
GROUPS_PER_SLAB = V7X_LANES // SSM_GROUP
STAGE_PITCH = 136
MASK_VALUE = -1e30
LOG2E = math.log2(math.e)

HIGHEST = lax.Precision.HIGHEST


def _params(*semantics):
    return pltpu.CompilerParams(dimension_semantics=semantics,
                                vmem_limit_bytes=V7X_VMEM_LIMIT_BYTES)


def _dot(a, b):
    return jnp.dot(a, b, preferred_element_type=F32)


def _dot_nt(a, b):
    return lax.dot_general(a, b, (((1,), (1,)), ((), ())), preferred_element_type=F32)


def _mm_kernel(a_ref, w_ref, o_ref):
    o_ref[...] = _dot(a_ref[...].astype(BF16), w_ref[...]).astype(o_ref.dtype)


def _matmul(a, w, *, bm, bn, out_dtype):
    bsz, s_len, k = a.shape
    n = w.shape[1]
    assert s_len % bm == 0 and n % bn == 0, (a.shape, w.shape, bm, bn)
    per_seq = s_len // bm
    rows = lambda i, j: (i // per_seq, i % per_seq)
    return pl.pallas_call(
        _mm_kernel,
        grid=(bsz * per_seq, n // bn),
        in_specs=[pl.BlockSpec((None, bm, k), lambda i, j: rows(i, j) + (0,)),
                  pl.BlockSpec((k, bn), lambda i, j: (0, j))],
        out_specs=pl.BlockSpec((None, bm, bn), lambda i, j: rows(i, j) + (j,)),
        out_shape=jax.ShapeDtypeStruct((bsz, s_len, n), out_dtype),
        compiler_params=_params("parallel", "arbitrary"),
        name="matmul",
    )(a, w)


def _mm_res_ln_kernel(*refs, n_pairs, sub):
    a_refs = refs[:n_pairs]
    w_refs = refs[n_pairs:2 * n_pairs]
    x_ref, g_ref, b_ref, o32_ref, o16_ref = refs[2 * n_pairs:]
    for r in range(x_ref.shape[0] // sub):
        rows = slice(r * sub, (r + 1) * sub)
        pre = DEEPNORM_ALPHA * x_ref[rows, :]
        for a_ref, w_ref in zip(a_refs, w_refs):
            pre = pre + _dot(a_ref[rows, :], w_ref[...])
        mu = jnp.mean(pre, axis=-1, keepdims=True)
        cen = pre - mu
        var = jnp.mean(cen * cen, axis=-1, keepdims=True)
        y = cen * lax.rsqrt(var + LN_EPS) * g_ref[...] + b_ref[...]
        o32_ref[rows, :] = y
        o16_ref[rows, :] = y.astype(BF16)


def _matmul_residual_layernorm(pairs, x32, gain, bias, *, bm, sub=256):
    m, d = x32.shape
    assert m % bm == 0 and bm % sub == 0, (m, bm, sub)
    n_pairs = len(pairs)
    a_list = [p[0] for p in pairs]
    w_list = [p[1] for p in pairs]
    row_spec = lambda width: pl.BlockSpec((bm, width), lambda i: (i, 0))
    const_spec = lambda rows: pl.BlockSpec((rows, d), lambda i: (0, 0), pipeline_mode=pl.Buffered(1))
    window_spec = lambda rows, row0: pl.BlockSpec((pl.Element(rows), pl.Element(d)), lambda i: (row0, 0),
                                                  pipeline_mode=pl.Buffered(1))
    in_specs = ([row_spec(a.shape[1]) for a in a_list]
                + [window_spec(a.shape[1], row0) for a, _, row0 in pairs]
                + [row_spec(d), const_spec(1), const_spec(1)])
    return pl.pallas_call(
        functools.partial(_mm_res_ln_kernel, n_pairs=n_pairs, sub=sub),
        grid=(m // bm,),
        in_specs=in_specs,
        out_specs=[row_spec(d), row_spec(d)],
        out_shape=[jax.ShapeDtypeStruct((m, d), F32), jax.ShapeDtypeStruct((m, d), BF16)],
        compiler_params=_params("parallel"),
        name="matmul_residual_layernorm",
    )(*a_list, *w_list, x32, gain.reshape(1, d).astype(F32), bias.reshape(1, d).astype(F32))


def _bf16_split3(v):
    hi = v.astype(BF16)
    r1 = v - hi.astype(F32)
    mid = r1.astype(BF16)
    lo = (r1 - mid.astype(F32)).astype(BF16)
    return hi, mid, lo


def _forget_gate_kernel(f_ref, bf_ref, c_ref, *, chunk):
    s_len = f_ref.shape[0]
    f = f_ref[...] + bf_ref[...]
    log_f = jnp.minimum(f, 0.0) - jnp.log(1.0 + jnp.exp(-jnp.abs(f)))
    row = lax.broadcasted_iota(jnp.int32, (chunk, chunk), 0)
    col = lax.broadcasted_iota(jnp.int32, (chunk, chunk), 1)
    tri = (row >= col).astype(BF16)
    carry = jnp.zeros((1, log_f.shape[1]), F32)
    for start in range(0, s_len, chunk):
        hi, mid, lo = _bf16_split3(log_f[start:start + chunk])
        part = _dot(tri, hi) + _dot(tri, mid) + _dot(tri, lo) + carry
        c_ref[start:start + chunk, :] = part
        carry = part[chunk - 1:chunk, :]


def _forget_gate_cumsum(f_logits, b_f, *, chunk=256):
    bsz, s_len, n = f_logits.shape
    return pl.pallas_call(
        functools.partial(_forget_gate_kernel, chunk=chunk),
        grid=(bsz,),
        in_specs=[pl.BlockSpec((None, s_len, n), lambda b: (b, 0, 0)),
                  pl.BlockSpec((1, n), lambda b: (0, 0))],
        out_specs=pl.BlockSpec((None, s_len, n), lambda b: (b, 0, 0)),
        out_shape=jax.ShapeDtypeStruct((bsz, s_len, n), F32),
        compiler_params=_params("parallel"),
        name="forget_gate_cumsum",
    )(f_logits, b_f)


def _bias_lanes(c, first):
    hi, mid, lo = (piece.astype(F32) for piece in _bf16_split3(c))
    lane = lax.broadcasted_iota(jnp.int32, (c.shape[0], V7X_LANES), 1)
    ones = jnp.where(lane < 6, 1.0, 0.0)
    return jnp.where(lane == first, hi, jnp.where(lane == first + 1, mid, jnp.where(lane == first + 2, lo, ones)))


def _fox_kernel(q_ref, k_ref, v_ref, c_ref, o_ref, kaug_ref, *, tile, heads):
    dh = FOX_HEAD_DIM
    aug = 2 * dh
    h0 = pl.program_id(1) * heads
    n_tiles = q_ref.shape[0] // tile

    def head_column(c_all, hh):
        lane = lax.broadcasted_iota(jnp.int32, c_all.shape, 1)
        return jnp.sum(jnp.where(lane == h0 + hh, c_all, 0.0), axis=1, keepdims=True) * LOG2E

    c_all = c_ref[...]
    for hh in range(heads):
        kaug_ref[:, hh * aug:hh * aug + dh] = k_ref[:, hh * dh:(hh + 1) * dh]
        kaug_ref[:, hh * aug + dh:(hh + 1) * aug] = _bias_lanes(-head_column(c_all, hh), 3).astype(BF16)

    def update(state, s, v):
        m, l, acc = state
        m_new = jnp.maximum(m, jnp.max(s, axis=1, keepdims=True))
        alpha = jnp.exp2(m - m_new)
        p = jnp.exp2(s - m_new)
        return (m_new, alpha * l + jnp.sum(p, axis=1, keepdims=True), alpha * acc + _dot(p.astype(BF16), v))

    for qt in range(n_tiles):
        rows = slice(qt * tile, (qt + 1) * tile)
        cq_all = c_ref[rows, :]
        for hh in range(heads):
            q_aug = jnp.concatenate([q_ref[rows, hh * dh:(hh + 1) * dh],
                                     _bias_lanes(head_column(cq_all, hh), 0).astype(BF16)], axis=1)
            state = (jnp.full((tile, 1), MASK_VALUE, F32), jnp.zeros((tile, 1), F32), jnp.zeros((tile, dh), F32))
            for kt in range(qt + 1):
                keys = slice(kt * tile, (kt + 1) * tile)
                s = _dot_nt(q_aug, kaug_ref[keys, hh * aug:(hh + 1) * aug])
                if kt == qt:
                    row = lax.broadcasted_iota(jnp.int32, s.shape, 0)
                    col = lax.broadcasted_iota(jnp.int32, s.shape, 1)
                    s = jnp.where(row >= col, s, MASK_VALUE)
                state = update(state, s, v_ref[keys, hh * dh:(hh + 1) * dh])
            _, l, acc = state
            o_ref[rows, hh * dh:(hh + 1) * dh] = (acc / l).astype(o_ref.dtype)


def _forgetting_attention(qkv, c_col, *, tile=1024, heads=2):
    bsz, s_len, _ = qkv.shape
    assert s_len % tile == 0 and FOX_HEADS % heads == 0, (s_len, tile, heads)
    width = heads * FOX_HEAD_DIM
    nblk = FOX_WIDTH // width
    seq_spec = lambda off: pl.BlockSpec((None, s_len, width), lambda b, h: (b, 0, off + h))
    return pl.pallas_call(
        functools.partial(_fox_kernel, tile=tile, heads=heads),
        grid=(bsz, nblk),
        in_specs=[seq_spec(0), seq_spec(nblk), seq_spec(2 * nblk),
                  pl.BlockSpec((None, s_len, c_col.shape[2]), lambda b, h: (b, 0, 0))],
        out_specs=seq_spec(0),
        out_shape=jax.ShapeDtypeStruct((bsz, s_len, FOX_WIDTH), BF16),
        scratch_shapes=[pltpu.VMEM((s_len, 2 * width), BF16)],
        compiler_params=_params("parallel", "parallel"),
        name="forgetting_attention",
    )(qkv, qkv, qkv, c_col)


def _cmul(ar, ai, br, bi):
    return ar * br - ai * bi, ar * bi + ai * br


def _cpow(ar, ai, e, n_bits):
    rr = jnp.ones_like(ar)
    ri = jnp.zeros_like(ai)
    for bit in range(n_bits):
        nr, ni = _cmul(rr, ri, ar, ai)
        take = ((e >> bit) & 1) == 1
        rr = jnp.where(take, nr, rr)
        ri = jnp.where(take, ni, ri)
        if bit + 1 < n_bits:
            ar, ai = _cmul(ar, ai, ar, ai)
    return rr, ri


def _discretise(lr, li, dt):
    mag = jnp.exp(lr * dt)
    return mag * jnp.cos(li * dt), mag * jnp.sin(li * dt)


def _s5_prep_group(lam_row_ref, lam_col_ref, dt_ref, bt_re_ref, bt_im_ref, ct_re_ref, ct_im_ref,
                   d_ref, t0_ref, ws_re_ref, ws_im_ref, wc_re_ref, wc_im_ref, a_chunk_ref):
    chunk, grp, p = SSM_CHUNK, SSM_GROUP, SSM_STATE
    n_bits = chunk.bit_length()
    dt = jnp.exp(dt_ref[...])
    lr_r, li_r = lam_row_ref[0:1, :], lam_row_ref[1:2, :]
    ar_r, ai_r = _discretise(lr_r, li_r, dt)
    den = lr_r * lr_r + li_r * li_r
    xr, xi = ar_r - 1.0, ai_r
    g_re = (xr * lr_r + xi * li_r) / den
    g_im = (xi * lr_r - xr * li_r) / den
    bbt_re, bbt_im = _cmul(g_re, g_im, bt_re_ref[...], bt_im_ref[...])
    bx_re = jnp.concatenate([bbt_re] * chunk, axis=0)
    bx_im = jnp.concatenate([bbt_im] * chunk, axis=0)
    row_i = lax.broadcasted_iota(jnp.int32, (SSM_ROW, p), 0) // grp
    pr, pi = _cpow(jnp.broadcast_to(ar_r, (SSM_ROW, p)), jnp.broadcast_to(ai_r, (SSM_ROW, p)),
                   chunk - 1 - row_i, n_bits)
    ws_re, ws_im = _cmul(pr, pi, bx_re, bx_im)
    ws_re_ref[...] = ws_re.astype(BF16)
    ws_im_ref[...] = ws_im.astype(BF16)
    ac_r, ac_i = _cpow(ar_r, ai_r, jnp.full((1, p), chunk, jnp.int32), n_bits)
    a_chunk_ref[0:1, :] = ac_r
    a_chunk_ref[1:2, :] = ac_i
    lr_c, li_c = lam_col_ref[:, 0:1], lam_col_ref[:, 1:2]
    ar_c, ai_c = _discretise(lr_c, li_c, dt)
    rep = (lax.broadcasted_iota(jnp.int32, (grp, SSM_ROW), 0)
           == lax.broadcasted_iota(jnp.int32, (grp, SSM_ROW), 1) % grp).astype(F32)
    cx_re = jnp.dot(ct_re_ref[...], rep, precision=HIGHEST, preferred_element_type=F32)
    cx_im = jnp.dot(ct_im_ref[...], rep, precision=HIGHEST, preferred_element_type=F32)
    dx = jnp.dot(d_ref[...], rep, precision=HIGHEST, preferred_element_type=F32)
    col_j = lax.broadcasted_iota(jnp.int32, (p, SSM_ROW), 1) // grp
    ab_r = jnp.broadcast_to(ar_c, (p, SSM_ROW))
    ab_i = jnp.broadcast_to(ai_c, (p, SSM_ROW))
    qr, qi = _cpow(ab_r, ab_i, col_j, n_bits)
    f_re, f_im = _cmul(qr, qi, cx_re, cx_im)
    e_re, e_im = _cmul(f_re, f_im, ab_r, ab_i)
    wc_re_ref[...] = e_re.astype(BF16)
    wc_im_ref[...] = (-e_im).astype(BF16)
    r0 = (jnp.dot(bbt_re, f_re, precision=HIGHEST, preferred_element_type=F32)
          - jnp.dot(bbt_im, f_im, precision=HIGHEST, preferred_element_type=F32))
    lane = lax.broadcasted_iota(jnp.int32, (grp, SSM_ROW), 1)
    sub = lax.broadcasted_iota(jnp.int32, (grp, SSM_ROW), 0)
    for i in range(chunk):
        blk = r0 if i == 0 else pltpu.roll(r0, i * grp, 1)
        blk = jnp.where(lane >= i * grp, blk, 0.0)
        blk = blk + jnp.where(lane == i * grp + sub, dx, 0.0)
        t0_ref[i * grp:(i + 1) * grp, :] = blk.astype(BF16)


def _s5_prep_kernel(*refs, groups):
    for g in range(groups):
        _s5_prep_group(*(ref.at[g] for ref in refs))


def _s5_prepare(lam_re, lam_im, log_step, b_re, b_im, c_re, c_im, d_skip, *, groups_per_step=8):
    g, p = lam_re.shape
    grp = SSM_GROUP
    lam_row = jnp.stack([lam_re, lam_im], axis=1).astype(F32)
    lam_col = jnp.stack([lam_re, lam_im], axis=2).astype(F32)
    dt = log_step.reshape(g, 1, 1).astype(F32)
    bt_re = jnp.swapaxes(b_re, 1, 2).astype(F32)
    bt_im = jnp.swapaxes(b_im, 1, 2).astype(F32)
    ct_re = jnp.swapaxes(c_re, 1, 2).astype(F32)
    ct_im = jnp.swapaxes(c_im, 1, 2).astype(F32)
    d3 = d_skip.reshape(g, 1, grp).astype(F32)

    gs = groups_per_step

    def spec(*shape):
        return pl.BlockSpec((gs,) + shape, lambda i: (i,) + (0,) * len(shape))

    return pl.pallas_call(
        functools.partial(_s5_prep_kernel, groups=gs),
        grid=(g // gs,),
        in_specs=[spec(2, p), spec(p, 2), spec(1, 1), spec(grp, p), spec(grp, p),
                  spec(p, grp), spec(p, grp), spec(1, grp)],
        out_specs=[spec(SSM_ROW, SSM_ROW), spec(SSM_ROW, p), spec(SSM_ROW, p),
                   spec(p, SSM_ROW), spec(p, SSM_ROW), spec(2, p)],
        out_shape=[jax.ShapeDtypeStruct((g, SSM_ROW, SSM_ROW), BF16),
                   jax.ShapeDtypeStruct((g, SSM_ROW, p), BF16),
                   jax.ShapeDtypeStruct((g, SSM_ROW, p), BF16),
                   jax.ShapeDtypeStruct((g, p, SSM_ROW), BF16),
                   jax.ShapeDtypeStruct((g, p, SSM_ROW), BF16),
                   jax.ShapeDtypeStruct((g, 2, p), F32)],
        compiler_params=_params("parallel"),
        name="s5_prepare",
    )(lam_row, lam_col, dt, bt_re, bt_im, ct_re, ct_im, d3)


def _gelu_tanh(y):
    return 0.5 * y * (1.0 + jnp.tanh(math.sqrt(2.0 / math.pi) * (y + 0.044715 * (y * y * y))))


def _s5_scan_kernel(u_ref, t0_ref, ws_re_ref, ws_im_ref, wc_re_ref, wc_im_ref, a_ref, o_ref,
                    sre_ref, sim_ref, hre_ref, him_ref, *, groups, bsz, n_chunks):
    for g in range(groups):
        u = u_ref[g]
        sre_ref[g] = _dot(u, ws_re_ref[g])
        sim_ref[g] = _dot(u, ws_im_ref[g])
    ar = a_ref[:, 0:1, :]
    ai = a_ref[:, 1:2, :]

    def step(k, carry):
        hr, hi = carry
        rows = pl.ds(pl.multiple_of(k * bsz, bsz), bsz)
        hre_ref[:, rows, :] = hr
        him_ref[:, rows, :] = hi
        sr = sre_ref[:, rows, :]
        si = sim_ref[:, rows, :]
        return ar * hr - ai * hi + sr, ar * hi + ai * hr + si

    zero = jnp.zeros((groups, bsz, SSM_STATE), F32)
    lax.fori_loop(0, n_chunks, step, (zero, zero))
    for g in range(groups):
        y = (_dot(u_ref[g], t0_ref[g]) + _dot(hre_ref[g].astype(BF16), wc_re_ref[g])
             + _dot(him_ref[g].astype(BF16), wc_im_ref[g]))
        o_ref[g] = _gelu_tanh(y).astype(o_ref.dtype)


def _s5_scan_gelu(u_rows, mats, *, bsz, groups_per_step=8):
    t0, ws_re, ws_im, wc_re, wc_im, a_chunk = mats
    g, rows, _ = u_rows.shape
    gs = groups_per_step
    p = SSM_STATE
    assert g % gs == 0 and rows % bsz == 0 and bsz % 8 == 0, (u_rows.shape, bsz, gs)

    def spec(*shape):
        return pl.BlockSpec((gs,) + shape, lambda i: (i,) + (0,) * len(shape))

    return pl.pallas_call(
        functools.partial(_s5_scan_kernel, groups=gs, bsz=bsz, n_chunks=rows // bsz),
        grid=(g // gs,),
        in_specs=[spec(rows, SSM_ROW), spec(SSM_ROW, SSM_ROW), spec(SSM_ROW, p), spec(SSM_ROW, p),
                  spec(p, SSM_ROW), spec(p, SSM_ROW), spec(2, p)],
        out_specs=spec(rows, SSM_ROW),
        out_shape=jax.ShapeDtypeStruct((g, rows, SSM_ROW), BF16),
        scratch_shapes=[pltpu.VMEM((gs, rows, p), F32), pltpu.VMEM((gs, rows, p), F32),
                        pltpu.VMEM((gs, rows, p), F32), pltpu.VMEM((gs, rows, p), F32)],
        compiler_params=_params("parallel"),
        name="s5_scan_gelu",
    )(u_rows, t0, ws_re, ws_im, wc_re, wc_im, a_chunk)


def _sigmoid(x):
    return 1.0 / (1.0 + jnp.exp(-x))


def _block_transpose(xs):
    n = GROUPS_PER_SLAB
    width = xs[0].shape[1]
    blk = (lax.broadcasted_iota(jnp.int32, (1, width), 1) // SSM_GROUP) % n
    xs = list(xs)
    d = n // 2
    while d >= 1:
        high = (blk & d) != 0
        nxt = list(xs)
        for i in range(n):
            if i & d:
                continue
            lo_arr, hi_arr = xs[i], xs[i + d]
            nxt[i] = jnp.where(high, pltpu.roll(hi_arr, SSM_GROUP * d, 1), lo_arr)
            nxt[i + d] = jnp.where(high, hi_arr, pltpu.roll(lo_arr, width - SSM_GROUP * d, 1))
        xs = nxt
        d //= 2
    return xs


def _u_rows_kernel(x_ref, w_ref, o_ref, f_ref, stage_ref, *, bsz, chunks):
    tb = x_ref.shape[1]
    uf = _dot(x_ref[...].reshape(bsz * tb, x_ref.shape[2]).astype(BF16), w_ref[...])
    n_slabs = o_ref.shape[0] // GROUPS_PER_SLAB
    u = uf[:, :n_slabs * V7X_LANES]
    f_ref[...] = uf[:, n_slabs * V7X_LANES:].reshape(f_ref.shape)
    for v in range(n_slabs):
        for b in range(bsz):
            stage_ref[v, b * STAGE_PITCH:b * STAGE_PITCH + tb, :] = (
                u[b * tb:(b + 1) * tb, v * V7X_LANES:(v + 1) * V7X_LANES])
    n = GROUPS_PER_SLAB
    for jh in range(SSM_CHUNK // n):
        xs = []
        for jl in range(n):
            j = jh * n + jl
            xs.append(jnp.concatenate(
                [jnp.concatenate([stage_ref.at[v][pl.ds(SSM_CHUNK * c + j, bsz, stride=STAGE_PITCH), :]
                                  for v in range(n_slabs)], axis=1) for c in range(chunks)], axis=0))
        ys = _block_transpose(xs)
        for e in range(n):
            for v in range(n_slabs):
                o_ref[n * v + e, :, jh * V7X_LANES:(jh + 1) * V7X_LANES] = (
                    ys[e][:, v * V7X_LANES:(v + 1) * V7X_LANES].astype(o_ref.dtype))


def _u_projection_rows(x, w_uf, *, chunks=8):
    bsz, s_len, d = x.shape
    width = w_uf.shape[1] - V7X_LANES
    g = width // SSM_GROUP
    tb = chunks * SSM_CHUNK
    rows = chunks * bsz
    assert bsz % 8 == 0 and s_len % tb == 0 and tb <= STAGE_PITCH and g % GROUPS_PER_SLAB == 0, x.shape
    return pl.pallas_call(
        functools.partial(_u_rows_kernel, bsz=bsz, chunks=chunks),
        grid=(s_len // tb,),
        in_specs=[pl.BlockSpec((bsz, tb, d), lambda i: (0, i, 0)),
                  pl.BlockSpec((d, width + V7X_LANES), lambda i: (0, 0), pipeline_mode=pl.Buffered(1))],
        out_specs=[pl.BlockSpec((g, rows, SSM_ROW), lambda i: (0, i, 0)),
                   pl.BlockSpec((bsz, tb, V7X_LANES), lambda i: (0, i, 0))],
        out_shape=[jax.ShapeDtypeStruct((g, (s_len // SSM_CHUNK) * bsz, SSM_ROW), BF16),
                   jax.ShapeDtypeStruct((bsz, s_len, V7X_LANES), F32)],
        scratch_shapes=[pltpu.VMEM((width // V7X_LANES, bsz * STAGE_PITCH, V7X_LANES), F32)],
        compiler_params=_params("parallel"),
        name="u_projection_rows",
    )(x, w_uf)


def _glu_rows_kernel(gy_ref, w_ref, o_ref, stage_ref, *, bsz, chunks):
    tb = o_ref.shape[1]
    n = GROUPS_PER_SLAB
    n_slabs = gy_ref.shape[0] // n
    for jh in range(SSM_CHUNK // n):
        ys = [jnp.concatenate([gy_ref[n * v + e, :, jh * V7X_LANES:(jh + 1) * V7X_LANES].astype(F32)
                               for v in range(n_slabs)], axis=1) for e in range(n)]
        xs = _block_transpose(ys)
        for jl in range(n):
            j = jh * n + jl
            for c in range(chunks):
                for v in range(n_slabs):
                    stage_ref.at[v][pl.ds(SSM_CHUNK * c + j, bsz, stride=STAGE_PITCH), :] = (
                        xs[jl][c * bsz:(c + 1) * bsz, v * V7X_LANES:(v + 1) * V7X_LANES])
    a = jnp.concatenate(
        [jnp.concatenate([stage_ref[v, b * STAGE_PITCH:b * STAGE_PITCH + tb, :] for v in range(n_slabs)], axis=1)
         for b in range(bsz)], axis=0).astype(BF16)
    z = _dot(a, w_ref[...])
    half = z.shape[1] // 2
    o_ref[...] = (z[:, :half] * _sigmoid(z[:, half:])).astype(o_ref.dtype).reshape(o_ref.shape)


def _glu_from_rows(gy_rows, w_glu, *, bsz, chunks=8):
    g, total_rows, _ = gy_rows.shape
    width = g * SSM_GROUP
    n_out = w_glu.shape[1] // 2
    s_len = total_rows // bsz * SSM_CHUNK
    tb = chunks * SSM_CHUNK
    rows = chunks * bsz
    assert bsz % 8 == 0 and s_len % tb == 0 and tb <= STAGE_PITCH and g % GROUPS_PER_SLAB == 0, gy_rows.shape
    return pl.pallas_call(
        functools.partial(_glu_rows_kernel, bsz=bsz, chunks=chunks),
        grid=(s_len // tb,),
        in_specs=[pl.BlockSpec((g, rows, SSM_ROW), lambda i: (0, i, 0)),
                  pl.BlockSpec((width, 2 * n_out), lambda i: (0, 0), pipeline_mode=pl.Buffered(1))],
        out_specs=pl.BlockSpec((bsz, tb, n_out), lambda i: (0, i, 0)),
        out_shape=jax.ShapeDtypeStruct((bsz, s_len, n_out), BF16),
        scratch_shapes=[pltpu.VMEM((width // V7X_LANES, bsz * STAGE_PITCH, V7X_LANES), F32)],
        compiler_params=_params("parallel"),
        name="glu_from_rows",
    )(gy_rows, w_glu)


def _ffn_up_kernel(x_ref, wg_ref, wv_ref, cwg_ref, cbg_ref, cwv_ref, cbv_ref, o_ref,
                   wg16_ref, wv16_ref, hg_ref, hv_ref, *, blocks_per_seq):
    i = pl.program_id(1)
    bm, bn = o_ref.shape
    pad = hg_ref.shape[0] - bm

    @pl.when(i == 0)
    def _():
        wg16_ref[...] = wg_ref[0].astype(BF16)
        wv16_ref[...] = wv_ref[0].astype(BF16)

    @pl.when(i % blocks_per_seq == 0)
    def _():
        hg_ref[0:pad, :] = jnp.zeros((pad, bn), F32)
        hv_ref[0:pad, :] = jnp.zeros((pad, bn), F32)

    @pl.when(i % blocks_per_seq != 0)
    def _():
        hg_ref[0:pad, :] = hg_ref[bm:bm + pad, :]
        hv_ref[0:pad, :] = hv_ref[bm:bm + pad, :]

    x = x_ref[...]
    conv = []
    for w16, h_ref, cw_ref, cb_ref in ((wg16_ref, hg_ref, cwg_ref, cbg_ref), (wv16_ref, hv_ref, cwv_ref, cbv_ref)):
        h = _dot(x, w16[...])
        h_ref[pad:pad + bm, :] = h
        cw = cw_ref[0]
        conv.append(cb_ref[0] + cw[0:1, :] * h_ref[pad - 2:pad - 2 + bm, :]
                    + cw[1:2, :] * h_ref[pad - 1:pad - 1 + bm, :] + cw[2:3, :] * h)
    gate, val = conv
    o_ref[...] = (gate * _sigmoid(gate) * val).astype(o_ref.dtype)


def _ffn_up(x16, layer, w_up, conv_w, conv_b, *, col0, n_cols, seq_len, bm, bn):
    m, d = x16.shape
    f = w_up.shape[2] // 2
    assert m % bm == 0 and seq_len % bm == 0 and n_cols % bn == 0 and bn % V7X_LANES == 0, (m, bm, n_cols, bn)
    pad = 8
    cb = conv_b.reshape(conv_b.shape[0], 1, 2 * f)
    gate_col = lambda j: pl.multiple_of(col0 + j * bn, V7X_LANES)
    val_col = lambda j: pl.multiple_of(f + col0 + j * bn, V7X_LANES)
    window = lambda rows: (pl.Element(1), pl.Element(rows), pl.Element(bn))
    gate_spec = lambda rows: pl.BlockSpec(window(rows), lambda j, i: (layer, 0, gate_col(j)))
    val_spec = lambda rows: pl.BlockSpec(window(rows), lambda j, i: (layer, 0, val_col(j)))
    return pl.pallas_call(
        functools.partial(_ffn_up_kernel, blocks_per_seq=seq_len // bm),
        grid=(n_cols // bn, m // bm),
        in_specs=[pl.BlockSpec((bm, d), lambda j, i: (i, 0)),
                  gate_spec(d), val_spec(d), gate_spec(CONV_WIDTH), gate_spec(1),
                  val_spec(CONV_WIDTH), val_spec(1)],
        out_specs=pl.BlockSpec((bm, bn), lambda j, i: (i, j)),
        out_shape=jax.ShapeDtypeStruct((m, n_cols), BF16),
        scratch_shapes=[pltpu.VMEM((d, bn), BF16), pltpu.VMEM((d, bn), BF16),
                        pltpu.VMEM((pad + bm, bn), F32), pltpu.VMEM((pad + bm, bn), F32)],
        compiler_params=_params("arbitrary", "arbitrary"),
        name="ffn_up_conv_gate",
    )(x16, w_up, w_up, conv_w, cb, conv_w, cb)


def _cast_kernel(src_ref, dst_ref, *, scale_cols, scale):
    x = src_ref[...]
    if scale_cols:
        lane = lax.broadcasted_iota(jnp.int32, (1, x.shape[1]), 1)
        x = x * jnp.where(lane < scale_cols, scale, 1.0)
    dst_ref[...] = x.astype(dst_ref.dtype)


def _layer_to_bf16(stacked, layer, *, steps=8, scale_cols=0, scale=1.0):
    _, k, d = stacked.shape
    assert k % (16 * steps) == 0, (k, steps)
    rows = k // steps
    return pl.pallas_call(
        functools.partial(_cast_kernel, scale_cols=scale_cols, scale=scale),
        grid=(steps,),
        in_specs=[pl.BlockSpec((None, rows, d), lambda r: (layer, r, 0))],
        out_specs=pl.BlockSpec((rows, d), lambda r: (r, 0)),
        out_shape=jax.ShapeDtypeStruct((k, d), BF16),
        compiler_params=_params("parallel"),
        name="layer_to_bf16",
    )(stacked)


def _conv_ffn(x32, x16, layer, w_up, conv_w, conv_b, w_down, ln_g, ln_b, *, seq_len, bn=512):
    f = w_down.shape[1]
    w_up, conv_w, conv_b = w_up.astype(F32), conv_w.astype(F32), conv_b.astype(F32)
    w_down16 = _layer_to_bf16(w_down, layer)
    main = (f // bn) * bn
    pairs = []
    for col0, n_cols, width in ((0, main, bn), (main, f - main, f - main)):
        if n_cols:
            act = _ffn_up(x16, layer, w_up, conv_w, conv_b, col0=col0, n_cols=n_cols, seq_len=seq_len,
                          bm=1024, bn=width)
            pairs.append((act, w_down16, col0))
    return _matmul_residual_layernorm(pairs, x32, ln_g, ln_b, bm=256)


def _rope_proj_kernel(a_ref, w_ref, pos_ref, o_ref, *, rope_cols, q_cols, q_scale, sub):
    hd, half = SWA_HEAD_DIM, ROPE_DIM // 2
    bm, n = o_ref.shape
    lane1 = lax.broadcasted_iota(jnp.int32, (1, V7X_LANES), 1) % hd
    inv_freq = jnp.where(lane1 < ROPE_DIM,
                         jnp.exp((lane1 % half).astype(F32) * (-math.log(ROPE_THETA) / half)), 0.0)
    sign = jnp.where(lane1 < half, -1.0, 1.0)
    lane = lax.broadcasted_iota(jnp.int32, (1, n), 1)
    first = (lane & (hd - 1)) < half
    reps = n // V7X_LANES
    for r in range(bm // sub):
        rows = slice(r * sub, (r + 1) * sub)
        ang = pos_ref[rows, :].astype(F32) * inv_freq
        cos = jnp.concatenate([jnp.cos(ang)] * reps, axis=1)
        sin = jnp.concatenate([jnp.sin(ang) * sign] * reps, axis=1)
        acc = _dot(a_ref[rows, :], w_ref[...])
        partner = jnp.where(first, pltpu.roll(acc, n - half, 1), pltpu.roll(acc, half, 1))
        rot = acc * cos + partner * sin
        out = jnp.where(lane < rope_cols, rot, acc) * jnp.where(lane < q_cols, q_scale, 1.0)
        o_ref[rows, :] = out.astype(o_ref.dtype)


def _rope_projection(x16, w, pos_col, *, rope_cols, q_cols, q_scale, bm, sub=256):
    m, k = x16.shape
    n = w.shape[1]
    assert m % bm == 0 and bm % sub == 0 and n % V7X_LANES == 0, (m, n, bm, sub)
    return pl.pallas_call(
        functools.partial(_rope_proj_kernel, rope_cols=rope_cols, q_cols=q_cols, q_scale=q_scale, sub=sub),
        grid=(m // bm,),
        in_specs=[pl.BlockSpec((bm, k), lambda i: (i, 0)),
                  pl.BlockSpec((k, n), lambda i: (0, 0), pipeline_mode=pl.Buffered(1)),
                  pl.BlockSpec((bm, 1), lambda i: (i, 0))],
        out_specs=pl.BlockSpec((bm, n), lambda i: (i, 0)),
        out_shape=jax.ShapeDtypeStruct((m, n), BF16),
        compiler_params=_params("parallel"),
        name="rope_projection",
    )(x16, w, pos_col)


def _swa_kernel(sink_ref, q_ref, kp_ref, kc_ref, vp_ref, vc_ref, o_ref, *, blocks):
    n = pl.program_id(1)
    w, hd, lanes = SWA_WINDOW, SWA_HEAD_DIM, V7X_LANES
    pairs = SWA_GROUPS // 2
    qi = lax.broadcasted_iota(jnp.int32, (w, 2 * w), 0)
    kj = lax.broadcasted_iota(jnp.int32, (w, 2 * w), 1)
    rel = w + qi - kj
    band = (rel >= 0) & (rel < w)
    lane_kv = lax.broadcasted_iota(jnp.int32, (2 * w, lanes), 1)
    lane_o = lax.broadcasted_iota(jnp.int32, (w, lanes), 1)

    def both_halves(prev, cur, hk):
        x = jnp.concatenate([prev, cur], axis=0).astype(F32)
        upper = hk % 2 == 1
        own = jnp.where((lane_kv >= hd) if upper else (lane_kv < hd), x, 0.0)
        other = pltpu.roll(own, hd, 1)
        lo, hi = (other, own) if upper else (own, other)
        return jnp.concatenate([lo, hi], axis=0).astype(BF16)

    for blk in range(blocks):
        rows = slice(blk * w, (blk + 1) * w)
        valid = band & ((n > 0) | (kj >= w)) if blk == 0 else band
        bias = jnp.where(valid, 0.0, MASK_VALUE)
        for hk in range(SWA_KV_HEADS):
            slab = slice((hk // 2) * lanes, (hk // 2 + 1) * lanes)
            prev_rows = slice((blk - 1) * w, blk * w)
            k_prev = kp_ref[:, slab] if blk == 0 else kc_ref[prev_rows, slab]
            v_prev = vp_ref[:, slab] if blk == 0 else vc_ref[prev_rows, slab]
            kk = both_halves(k_prev, kc_ref[rows, slab], hk)
            vv = both_halves(v_prev, vc_ref[rows, slab], hk)
            slabs = [slice((hk * pairs + r) * lanes, (hk * pairs + r + 1) * lanes) for r in range(pairs)]
            qs = jnp.concatenate([q_ref[rows, sl] for sl in slabs], axis=0)
            s_all = _dot_nt(qs, kk)
            probs, inv_den = [], []
            for r in range(pairs):
                halves, inv = [], []
                for half in range(2):
                    sink = sink_ref[hk * SWA_GROUPS + 2 * r + half] * LOG2E
                    s = s_all[r * w:(r + 1) * w, half * 2 * w:(half + 1) * 2 * w] + bias
                    m = jnp.maximum(jnp.max(s, axis=1, keepdims=True), sink)
                    p = jnp.exp2(s - m)
                    inv.append(1.0 / (jnp.sum(p, axis=1, keepdims=True) + jnp.exp2(sink - m)))
                    halves.append(p.astype(BF16))
                probs.append(jnp.concatenate(halves, axis=1))
                inv_den.append(jnp.where(lane_o < hd, inv[0], inv[1]))
            o_all = _dot(jnp.concatenate(probs, axis=0), vv)
            for r in range(pairs):
                o_ref[rows, slabs[r]] = (o_all[r * w:(r + 1) * w] * inv_den[r]).astype(o_ref.dtype)


def _sliding_window_attention(qkv, sinks, *, blocks=2):
    bsz, s_len, _ = qkv.shape
    w, hd = SWA_WINDOW, SWA_HEAD_DIM
    rows = blocks * w
    assert s_len % rows == 0, (s_len, rows)
    qw = SWA_HEADS * hd
    kw = SWA_KV_HEADS * hd
    k_blk = qw // kw
    v_blk = k_blk + 1
    prev = lambda n: jnp.maximum(n * blocks - 1, 0)
    return pl.pallas_call(
        functools.partial(_swa_kernel, blocks=blocks),
        grid=(bsz, s_len // rows),
        in_specs=[pl.BlockSpec(memory_space=pltpu.SMEM),
                  pl.BlockSpec((None, rows, qw), lambda b, n: (b, n, 0)),
                  pl.BlockSpec((None, w, kw), lambda b, n: (b, prev(n), k_blk)),
                  pl.BlockSpec((None, rows, kw), lambda b, n: (b, n, k_blk)),
                  pl.BlockSpec((None, w, kw), lambda b, n: (b, prev(n), v_blk)),
                  pl.BlockSpec((None, rows, kw), lambda b, n: (b, n, v_blk))],
        out_specs=pl.BlockSpec((None, rows, qw), lambda b, n: (b, n, 0)),
        out_shape=jax.ShapeDtypeStruct((bsz, s_len, qw), BF16),
        compiler_params=_params("parallel", "arbitrary"),
        name="sliding_window_attention",
    )(sinks.astype(F32), qkv, qkv, qkv, qkv, qkv)


def _even_mixer(x32, xa, bsz, s_len, layer, w_in, b_f, lam_re, lam_im, log_step, b_re, b_im, c_re, c_im,
                d_skip, w_glu, w_out, ln_g, ln_b):
    m = x32.shape[0]
    fw = FOX_WIDTH
    nh = FOX_HEADS
    w16 = _layer_to_bf16(w_in.astype(F32), layer, scale_cols=fw, scale=LOG2E / math.sqrt(FOX_HEAD_DIM))
    w_qkv = w16[:, :3 * fw]
    w_uf = jnp.concatenate([w16[:, 3 * fw + nh:],
                            jnp.pad(w16[:, 3 * fw:3 * fw + nh], ((0, 0), (0, V7X_LANES - nh)))], axis=1)
    bias_f = jnp.pad(b_f, (0, V7X_LANES - nh)).reshape(1, V7X_LANES).astype(F32)

    qkv = _matmul(xa, w_qkv, bm=1024, bn=3 * fw // 2, out_dtype=BF16)
    mats = _s5_prepare(lam_re, lam_im, log_step, b_re, b_im, c_re, c_im, d_skip)
    u_rows, f_logits = _u_projection_rows(xa, w_uf)
    c_col = _forget_gate_cumsum(f_logits, bias_f)
    fox = _forgetting_attention(qkv, c_col).reshape(m, fw)
    gy_rows = _s5_scan_gelu(u_rows, mats, bsz=bsz)
    ssm = _glu_from_rows(gy_rows, w_glu.astype(BF16), bsz=bsz).reshape(m, -1)
    w_out16 = w_out.astype(BF16)
    return _matmul_residual_layernorm([(fox, w_out16, 0), (ssm, w_out16, fw)], x32, ln_g, ln_b, bm=512)


def _odd_mixer(x32, x16, bsz, s_len, positions, w_in, sinks, w_out, ln_g, ln_b):
    m = x16.shape[0]
    rope_cols = (SWA_HEADS + SWA_KV_HEADS) * SWA_HEAD_DIM
    qkv = _rope_projection(x16, w_in.astype(BF16), positions.reshape(m, 1).astype(jnp.int32),
                           rope_cols=rope_cols, q_cols=SWA_HEADS * SWA_HEAD_DIM,
                           q_scale=LOG2E / math.sqrt(SWA_HEAD_DIM), bm=1024)
    o = _sliding_window_attention(qkv.reshape(bsz, s_len, -1), sinks).reshape(m, -1)
    return _matmul_residual_layernorm([(o, w_out.astype(BF16), 0)], x32, ln_g, ln_b, bm=512)


def kernel(x, positions, ev_w_in, ev_b_f, ev_lambda_re, ev_lambda_im, ev_log_step, ev_ssm_b_re,
           ev_ssm_b_im, ev_ssm_c_re, ev_ssm_c_im, ev_ssm_d, ev_w_glu, ev_w_out, od_w_in, od_sinks,
           od_w_out, ln_mix_g, ln_mix_b, ffn_w_up, ffn_conv_w, ffn_conv_b, ffn_w_down, ln_ffn_g,
           ln_ffn_b):
    bsz, s_len, d = x.shape
    x32 = x.reshape(bsz * s_len, d).astype(F32)
    x16 = None
    for i in range(DEPTH):
        j = i // 2
        if i % 2 == 0:
            xa = x.astype(F32) if x16 is None else x16.reshape(bsz, s_len, d)
            x32, x16 = _even_mixer(x32, xa, bsz, s_len, j, ev_w_in, ev_b_f[j], ev_lambda_re[j],
                                   ev_lambda_im[j], ev_log_step[j], ev_ssm_b_re[j], ev_ssm_b_im[j],
                                   ev_ssm_c_re[j], ev_ssm_c_im[j], ev_ssm_d[j], ev_w_glu[j],
                                   ev_w_out[j], ln_mix_g[i], ln_mix_b[i])
        else:
            x16 = x32.astype(BF16) if x16 is None else x16
            x32, x16 = _odd_mixer(x32, x16, bsz, s_len, positions, od_w_in[j], od_sinks[j],
                                  od_w_out[j], ln_mix_g[i], ln_mix_b[i])
        x32, x16 = _conv_ffn(x32, x16, i, ffn_w_up, ffn_conv_w, ffn_conv_b, ffn_w_down,
                             ln_ffn_g[i], ln_ffn_b[i], seq_len=s_len)
    return x32.reshape(bsz, s_len, d).astype(x.dtype)
```

```python
import functools
import math

import jax
import jax.numpy as jnp
from jax import lax
from jax.experimental import pallas as pl
from jax.experimental.pallas import tpu as pltpu

F32 = jnp.float32
BF16 = jnp.bfloat16

DEPTH = 2
FOX_HEADS = 8
FOX_HEAD_DIM = 128
FOX_WIDTH = FOX_HEADS * FOX_HEAD_DIM
SSM_GROUP = 16
SSM_STATE = 64
SSM_CHUNK = 16
SSM_ROW = SSM_CHUNK * SSM_GROUP
SWA_HEADS = 32
SWA_KV_HEADS = 4
SWA_HEAD_DIM = 64
SWA_GROUPS = SWA_HEADS // SWA_KV_HEADS
SWA_WINDOW = 128
ROPE_DIM = SWA_HEAD_DIM // 4
ROPE_THETA = 500000.0
CONV_WIDTH = 3
LN_EPS = 1e-5
DEEPNORM_ALPHA = (2.0 * DEPTH) ** 0.25

V7X_LANES = 128
V7X_VMEM_LIMIT_BYTES = 56 * 1024 * 1024
GROUPS_PER_SLAB = V7X_LANES // SSM_GROUP
STAGE_PITCH = 136
MASK_VALUE = -1e30
LOG2E = math.log2(math.e)

HIGHEST = lax.Precision.HIGHEST


def _params(*semantics):
    return pltpu.CompilerParams(dimension_semantics=semantics,
                                vmem_limit_bytes=V7X_VMEM_LIMIT_BYTES)


def _dot(a, b):
    return jnp.dot(a, b, preferred_element_type=F32)


def _dot_nt(a, b):
    return lax.dot_general(a, b, (((1,), (1,)), ((), ())), preferred_element_type=F32)


def _mm_kernel(a_ref, w_ref, o_ref):
    o_ref[...] = _dot(a_ref[...].astype(BF16), w_ref[...]).astype(o_ref.dtype)


def _matmul(a, w, *, bm, bn, out_dtype):
    bsz, s_len, k = a.shape
    n = w.shape[1]
    assert s_len % bm == 0 and n % bn == 0, (a.shape, w.shape, bm, bn)
    per_seq = s_len // bm
    rows = lambda i, j: (i // per_seq, i % per_seq)
    return pl.pallas_call(
        _mm_kernel,
        grid=(bsz * per_seq, n // bn),
        in_specs=[pl.BlockSpec((None, bm, k), lambda i, j: rows(i, j) + (0,)),
                  pl.BlockSpec((k, bn), lambda i, j: (0, j))],
        out_specs=pl.BlockSpec((None, bm, bn), lambda i, j: rows(i, j) + (j,)),
        out_shape=jax.ShapeDtypeStruct((bsz, s_len, n), out_dtype),
        compiler_params=_params("parallel", "arbitrary"),
        name="matmul",
    )(a, w)


def _mm_res_ln_kernel(*refs, n_pairs, sub):
    a_refs = refs[:n_pairs]
    w_refs = refs[n_pairs:2 * n_pairs]
    x_ref, g_ref, b_ref, o32_ref, o16_ref = refs[2 * n_pairs:]
    for r in range(x_ref.shape[0] // sub):
        rows = slice(r * sub, (r + 1) * sub)
        pre = DEEPNORM_ALPHA * x_ref[rows, :]
        for a_ref, w_ref in zip(a_refs, w_refs):
            pre = pre + _dot(a_ref[rows, :], w_ref[...])
        mu = jnp.mean(pre, axis=-1, keepdims=True)
        cen = pre - mu
        var = jnp.mean(cen * cen, axis=-1, keepdims=True)
        y = cen * lax.rsqrt(var + LN_EPS) * g_ref[...] + b_ref[...]
        o32_ref[rows, :] = y
        o16_ref[rows, :] = y.astype(BF16)


def _matmul_residual_layernorm(pairs, x32, gain, bias, *, bm, sub=256):
    m, d = x32.shape
    assert m % bm == 0 and bm % sub == 0, (m, bm, sub)
    n_pairs = len(pairs)
    a_list = [p[0] for p in pairs]
    w_list = [p[1] for p in pairs]
    row_spec = lambda width: pl.BlockSpec((bm, width), lambda i: (i, 0))
    const_spec = lambda rows: pl.BlockSpec((rows, d), lambda i: (0, 0), pipeline_mode=pl.Buffered(1))
    window_spec = lambda rows, row0: pl.BlockSpec((pl.Element(rows), pl.Element(d)), lambda i: (row0, 0),
                                                  pipeline_mode=pl.Buffered(1))
    in_specs = ([row_spec(a.shape[1]) for a in a_list]
                + [window_spec(a.shape[1], row0) for a, _, row0 in pairs]
                + [row_spec(d), const_spec(1), const_spec(1)])
    return pl.pallas_call(
        functools.partial(_mm_res_ln_kernel, n_pairs=n_pairs, sub=sub),
        grid=(m // bm,),
        in_specs=in_specs,
        out_specs=[row_spec(d), row_spec(d)],
        out_shape=[jax.ShapeDtypeStruct((m, d), F32), jax.ShapeDtypeStruct((m, d), BF16)],
        compiler_params=_params("parallel"),
        name="matmul_residual_layernorm",
    )(*a_list, *w_list, x32, gain.reshape(1, d).astype(F32), bias.reshape(1, d).astype(F32))


def _bf16_split3(v):
    hi = v.astype(BF16)
    r1 = v - hi.astype(F32)
    mid = r1.astype(BF16)
    lo = (r1 - mid.astype(F32)).astype(BF16)
    return hi, mid, lo


def _forget_gate_kernel(f_ref, bf_ref, c_ref, *, chunk):
    s_len = f_ref.shape[0]
    f = f_ref[...] + bf_ref[...]
    log_f = jnp.minimum(f, 0.0) - jnp.log(1.0 + jnp.exp(-jnp.abs(f)))
    row = lax.broadcasted_iota(jnp.int32, (chunk, chunk), 0)
    col = lax.broadcasted_iota(jnp.int32, (chunk, chunk), 1)
    tri = (row >= col).astype(BF16)
    carry = jnp.zeros((1, log_f.shape[1]), F32)
    for start in range(0, s_len, chunk):
        hi, mid, lo = _bf16_split3(log_f[start:start + chunk])
        part = _dot(tri, hi) + _dot(tri, mid) + _dot(tri, lo) + carry
        c_ref[start:start + chunk, :] = part
        carry = part[chunk - 1:chunk, :]


def _forget_gate_cumsum(f_logits, b_f, *, chunk=256):
    bsz, s_len, n = f_logits.shape
    return pl.pallas_call(
        functools.partial(_forget_gate_kernel, chunk=chunk),
        grid=(bsz,),
        in_specs=[pl.BlockSpec((None, s_len, n), lambda b: (b, 0, 0)),
                  pl.BlockSpec((1, n), lambda b: (0, 0))],
        out_specs=pl.BlockSpec((None, s_len, n), lambda b: (b, 0, 0)),
        out_shape=jax.ShapeDtypeStruct((bsz, s_len, n), F32),
        compiler_params=_params("parallel"),
        name="forget_gate_cumsum",
    )(f_logits, b_f)


def _bias_lanes(c, first):
    hi, mid, lo = (piece.astype(F32) for piece in _bf16_split3(c))
    lane = lax.broadcasted_iota(jnp.int32, (c.shape[0], V7X_LANES), 1)
    ones = jnp.where(lane < 6, 1.0, 0.0)
    return jnp.where(lane == first, hi, jnp.where(lane == first + 1, mid, jnp.where(lane == first + 2, lo, ones)))


def _fox_kernel(q_ref, k_ref, v_ref, c_ref, o_ref, kaug_ref, *, tile, heads):
    dh = FOX_HEAD_DIM
    aug = 2 * dh
    h0 = pl.program_id(1) * heads
    n_tiles = q_ref.shape[0] // tile

    def head_column(c_all, hh):
        lane = lax.broadcasted_iota(jnp.int32, c_all.shape, 1)
        return jnp.sum(jnp.where(lane == h0 + hh, c_all, 0.0), axis=1, keepdims=True) * LOG2E

    c_all = c_ref[...]
    for hh in range(heads):
        kaug_ref[:, hh * aug:hh * aug + dh] = k_ref[:, hh * dh:(hh + 1) * dh]
        kaug_ref[:, hh * aug + dh:(hh + 1) * aug] = _bias_lanes(-head_column(c_all, hh), 3).astype(BF16)

    def update(state, s, v):
        m, l, acc = state
        m_new = jnp.maximum(m, jnp.max(s, axis=1, keepdims=True))
        alpha = jnp.exp2(m - m_new)
        p = jnp.exp2(s - m_new)
        return (m_new, alpha * l + jnp.sum(p, axis=1, keepdims=True), alpha * acc + _dot(p.astype(BF16), v))

    for qt in range(n_tiles):
        rows = slice(qt * tile, (qt + 1) * tile)
        cq_all = c_ref[rows, :]
        for hh in range(heads):
            q_aug = jnp.concatenate([q_ref[rows, hh * dh:(hh + 1) * dh],
                                     _bias_lanes(head_column(cq_all, hh), 0).astype(BF16)], axis=1)
            state = (jnp.full((tile, 1), MASK_VALUE, F32), jnp.zeros((tile, 1), F32), jnp.zeros((tile, dh), F32))
            for keys, masked in ((slice(0, qt * tile), False), (rows, True)):
                if keys.stop == keys.start:
                    continue
                s = _dot_nt(q_aug, kaug_ref[keys, hh * aug:(hh + 1) * aug])
                if masked:
                    row = lax.broadcasted_iota(jnp.int32, s.shape, 0)
                    col = lax.broadcasted_iota(jnp.int32, s.shape, 1)
                    s = jnp.where(row >= col, s, MASK_VALUE)
                state = update(state, s, v_ref[keys, hh * dh:(hh + 1) * dh])
            _, l, acc = state
            o_ref[rows, hh * dh:(hh + 1) * dh] = (acc / l).astype(o_ref.dtype)


def _forgetting_attention(qkv, c_col, *, tile=512, heads=2):
    bsz, s_len, _ = qkv.shape
    assert s_len % tile == 0 and FOX_HEADS % heads == 0, (s_len, tile, heads)
    width = heads * FOX_HEAD_DIM
    nblk = FOX_WIDTH // width
    seq_spec = lambda off: pl.BlockSpec((None, s_len, width), lambda b, h: (b, 0, off + h))
    return pl.pallas_call(
        functools.partial(_fox_kernel, tile=tile, heads=heads),
        grid=(bsz, nblk),
        in_specs=[seq_spec(0), seq_spec(nblk), seq_spec(2 * nblk),
                  pl.BlockSpec((None, s_len, c_col.shape[2]), lambda b, h: (b, 0, 0))],
        out_specs=seq_spec(0),
        out_shape=jax.ShapeDtypeStruct((bsz, s_len, FOX_WIDTH), BF16),
        scratch_shapes=[pltpu.VMEM((s_len, 2 * width), BF16)],
        compiler_params=_params("parallel", "parallel"),
        name="forgetting_attention",
    )(qkv, qkv, qkv, c_col)


def _cmul(ar, ai, br, bi):
    return ar * br - ai * bi, ar * bi + ai * br


def _cpow(ar, ai, e, n_bits):
    rr = jnp.ones_like(ar)
    ri = jnp.zeros_like(ai)
    for bit in range(n_bits):
        nr, ni = _cmul(rr, ri, ar, ai)
        take = ((e >> bit) & 1) == 1
        rr = jnp.where(take, nr, rr)
        ri = jnp.where(take, ni, ri)
        if bit + 1 < n_bits:
            ar, ai = _cmul(ar, ai, ar, ai)
    return rr, ri


def _discretise(lr, li, dt):
    mag = jnp.exp(lr * dt)
    return mag * jnp.cos(li * dt), mag * jnp.sin(li * dt)


def _s5_prep_group(lam_row_ref, lam_col_ref, dt_ref, bt_re_ref, bt_im_ref, ct_re_ref, ct_im_ref,
                   d_ref, t0_ref, ws_re_ref, ws_im_ref, wc_re_ref, wc_im_ref, a_chunk_ref):
    chunk, grp, p = SSM_CHUNK, SSM_GROUP, SSM_STATE
    n_bits = chunk.bit_length()
    dt = jnp.exp(dt_ref[...])
    lr_r, li_r = lam_row_ref[0:1, :], lam_row_ref[1:2, :]
    ar_r, ai_r = _discretise(lr_r, li_r, dt)
    den = lr_r * lr_r + li_r * li_r
    xr, xi = ar_r - 1.0, ai_r
    g_re = (xr * lr_r + xi * li_r) / den
    g_im = (xi * lr_r - xr * li_r) / den
    bbt_re, bbt_im = _cmul(g_re, g_im, bt_re_ref[...], bt_im_ref[...])
    bx_re = jnp.concatenate([bbt_re] * chunk, axis=0)
    bx_im = jnp.concatenate([bbt_im] * chunk, axis=0)
    row_i = lax.broadcasted_iota(jnp.int32, (SSM_ROW, p), 0) // grp
    pr, pi = _cpow(jnp.broadcast_to(ar_r, (SSM_ROW, p)), jnp.broadcast_to(ai_r, (SSM_ROW, p)),
                   chunk - 1 - row_i, n_bits)
    ws_re, ws_im = _cmul(pr, pi, bx_re, bx_im)
    ws_re_ref[...] = ws_re.astype(BF16)
    ws_im_ref[...] = ws_im.astype(BF16)
    ac_r, ac_i = _cpow(ar_r, ai_r, jnp.full((1, p), chunk, jnp.int32), n_bits)
    a_chunk_ref[0:1, :] = ac_r
    a_chunk_ref[1:2, :] = ac_i
    lr_c, li_c = lam_col_ref[:, 0:1], lam_col_ref[:, 1:2]
    ar_c, ai_c = _discretise(lr_c, li_c, dt)
    rep = (lax.broadcasted_iota(jnp.int32, (grp, SSM_ROW), 0)
           == lax.broadcasted_iota(jnp.int32, (grp, SSM_ROW), 1) % grp).astype(F32)
    cx_re = jnp.dot(ct_re_ref[...], rep, precision=HIGHEST, preferred_element_type=F32)
    cx_im = jnp.dot(ct_im_ref[...], rep, precision=HIGHEST, preferred_element_type=F32)
    dx = jnp.dot(d_ref[...], rep, precision=HIGHEST, preferred_element_type=F32)
    col_j = lax.broadcasted_iota(jnp.int32, (p, SSM_ROW), 1) // grp
    ab_r = jnp.broadcast_to(ar_c, (p, SSM_ROW))
    ab_i = jnp.broadcast_to(ai_c, (p, SSM_ROW))
    qr, qi = _cpow(ab_r, ab_i, col_j, n_bits)
    f_re, f_im = _cmul(qr, qi, cx_re, cx_im)
    e_re, e_im = _cmul(f_re, f_im, ab_r, ab_i)
    wc_re_ref[...] = e_re.astype(BF16)
    wc_im_ref[...] = (-e_im).astype(BF16)
    r0 = (jnp.dot(bbt_re, f_re, precision=HIGHEST, preferred_element_type=F32)
          - jnp.dot(bbt_im, f_im, precision=HIGHEST, preferred_element_type=F32))
    lane = lax.broadcasted_iota(jnp.int32, (grp, SSM_ROW), 1)
    sub = lax.broadcasted_iota(jnp.int32, (grp, SSM_ROW), 0)
    for i in range(chunk):
        blk = r0 if i == 0 else pltpu.roll(r0, i * grp, 1)
        blk = jnp.where(lane >= i * grp, blk, 0.0)
        blk = blk + jnp.where(lane == i * grp + sub, dx, 0.0)
        t0_ref[i * grp:(i + 1) * grp, :] = blk.astype(BF16)


def _s5_prep_kernel(*refs, groups):
    for g in range(groups):
        _s5_prep_group(*(ref.at[g] for ref in refs))


def _s5_prepare(lam_re, lam_im, log_step, b_re, b_im, c_re, c_im, d_skip, *, groups_per_step=8):
    g, p = lam_re.shape
    grp = SSM_GROUP
    lam_row = jnp.stack([lam_re, lam_im], axis=1).astype(F32)
    lam_col = jnp.stack([lam_re, lam_im], axis=2).astype(F32)
    dt = log_step.reshape(g, 1, 1).astype(F32)
    bt_re = jnp.swapaxes(b_re, 1, 2).astype(F32)
    bt_im = jnp.swapaxes(b_im, 1, 2).astype(F32)
    ct_re = jnp.swapaxes(c_re, 1, 2).astype(F32)
    ct_im = jnp.swapaxes(c_im, 1, 2).astype(F32)
    d3 = d_skip.reshape(g, 1, grp).astype(F32)

    gs = groups_per_step

    def spec(*shape):
        return pl.BlockSpec((gs,) + shape, lambda i: (i,) + (0,) * len(shape))

    return pl.pallas_call(
        functools.partial(_s5_prep_kernel, groups=gs),
        grid=(g // gs,),
        in_specs=[spec(2, p), spec(p, 2), spec(1, 1), spec(grp, p), spec(grp, p),
                  spec(p, grp), spec(p, grp), spec(1, grp)],
        out_specs=[spec(SSM_ROW, SSM_ROW), spec(SSM_ROW, p), spec(SSM_ROW, p),
                   spec(p, SSM_ROW), spec(p, SSM_ROW), spec(2, p)],
        out_shape=[jax.ShapeDtypeStruct((g, SSM_ROW, SSM_ROW), BF16),
                   jax.ShapeDtypeStruct((g, SSM_ROW, p), BF16),
                   jax.ShapeDtypeStruct((g, SSM_ROW, p), BF16),
                   jax.ShapeDtypeStruct((g, p, SSM_ROW), BF16),
                   jax.ShapeDtypeStruct((g, p, SSM_ROW), BF16),
                   jax.ShapeDtypeStruct((g, 2, p), F32)],
        compiler_params=_params("parallel"),
        name="s5_prepare",
    )(lam_row, lam_col, dt, bt_re, bt_im, ct_re, ct_im, d3)


def _gelu_tanh(y):
    return 0.5 * y * (1.0 + jnp.tanh(math.sqrt(2.0 / math.pi) * (y + 0.044715 * (y * y * y))))


def _s5_scan_kernel(u_ref, t0_ref, ws_re_ref, ws_im_ref, wc_re_ref, wc_im_ref, a_ref, o_ref,
                    sre_ref, sim_ref, hre_ref, him_ref, *, groups, bsz, n_chunks):
    for g in range(groups):
        u = u_ref[g]
        sre_ref[g] = _dot(u, ws_re_ref[g])
        sim_ref[g] = _dot(u, ws_im_ref[g])
    ar = a_ref[:, 0:1, :]
    ai = a_ref[:, 1:2, :]

    def step(k, carry):
        hr, hi = carry
        rows = pl.ds(pl.multiple_of(k * bsz, bsz), bsz)
        hre_ref[:, rows, :] = hr
        him_ref[:, rows, :] = hi
        sr = sre_ref[:, rows, :]
        si = sim_ref[:, rows, :]
        return ar * hr - ai * hi + sr, ar * hi + ai * hr + si

    zero = jnp.zeros((groups, bsz, SSM_STATE), F32)
    lax.fori_loop(0, n_chunks, step, (zero, zero))
    for g in range(groups):
        y = (_dot(u_ref[g], t0_ref[g]) + _dot(hre_ref[g].astype(BF16), wc_re_ref[g])
             + _dot(him_ref[g].astype(BF16), wc_im_ref[g]))
        o_ref[g] = _gelu_tanh(y).astype(o_ref.dtype)


def _s5_scan_gelu(u_rows, mats, *, bsz, groups_per_step=8):
    t0, ws_re, ws_im, wc_re, wc_im, a_chunk = mats
    g, rows, _ = u_rows.shape
    gs = groups_per_step
    p = SSM_STATE
    assert g % gs == 0 and rows % bsz == 0 and bsz % 8 == 0, (u_rows.shape, bsz, gs)

    def spec(*shape):
        return pl.BlockSpec((gs,) + shape, lambda i: (i,) + (0,) * len(shape))

    return pl.pallas_call(
        functools.partial(_s5_scan_kernel, groups=gs, bsz=bsz, n_chunks=rows // bsz),
        grid=(g // gs,),
        in_specs=[spec(rows, SSM_ROW), spec(SSM_ROW, SSM_ROW), spec(SSM_ROW, p), spec(SSM_ROW, p),
                  spec(p, SSM_ROW), spec(p, SSM_ROW), spec(2, p)],
        out_specs=spec(rows, SSM_ROW),
        out_shape=jax.ShapeDtypeStruct((g, rows, SSM_ROW), BF16),
        scratch_shapes=[pltpu.VMEM((gs, rows, p), F32), pltpu.VMEM((gs, rows, p), F32),
                        pltpu.VMEM((gs, rows, p), F32), pltpu.VMEM((gs, rows, p), F32)],
        compiler_params=_params("parallel"),
        name="s5_scan_gelu",
    )(u_rows, t0, ws_re, ws_im, wc_re, wc_im, a_chunk)


def _sigmoid(x):
    return 1.0 / (1.0 + jnp.exp(-x))


def _block_transpose(xs):
    n = GROUPS_PER_SLAB
    width = xs[0].shape[1]
    blk = (lax.broadcasted_iota(jnp.int32, (1, width), 1) // SSM_GROUP) % n
    xs = list(xs)
    d = n // 2
    while d >= 1:
        high = (blk & d) != 0
        nxt = list(xs)
        for i in range(n):
            if i & d:
                continue
            lo_arr, hi_arr = xs[i], xs[i + d]
            nxt[i] = jnp.where(high, pltpu.roll(hi_arr, SSM_GROUP * d, 1), lo_arr)
            nxt[i + d] = jnp.where(high, hi_arr, pltpu.roll(lo_arr, width - SSM_GROUP * d, 1))
        xs = nxt
        d //= 2
    return xs


def _u_rows_kernel(x_ref, w_ref, o_ref, f_ref, stage_ref, *, bsz, chunks):
    tb = x_ref.shape[1]
    uf = _dot(x_ref[...].reshape(bsz * tb, x_ref.shape[2]).astype(BF16), w_ref[...])
    n_slabs = o_ref.shape[0] // GROUPS_PER_SLAB
    u = uf[:, :n_slabs * V7X_LANES]
    f_ref[...] = uf[:, n_slabs * V7X_LANES:].reshape(f_ref.shape)
    for v in range(n_slabs):
        for b in range(bsz):
            stage_ref[v, b * STAGE_PITCH:b * STAGE_PITCH + tb, :] = (
                u[b * tb:(b + 1) * tb, v * V7X_LANES:(v + 1) * V7X_LANES])
    n = GROUPS_PER_SLAB
    for jh in range(SSM_CHUNK // n):
        xs = []
        for jl in range(n):
            j = jh * n + jl
            xs.append(jnp.concatenate(
                [jnp.concatenate([stage_ref.at[v][pl.ds(SSM_CHUNK * c + j, bsz, stride=STAGE_PITCH), :]
                                  for v in range(n_slabs)], axis=1) for c in range(chunks)], axis=0))
        ys = _block_transpose(xs)
        for e in range(n):
            for v in range(n_slabs):
                o_ref[n * v + e, :, jh * V7X_LANES:(jh + 1) * V7X_LANES] = (
                    ys[e][:, v * V7X_LANES:(v + 1) * V7X_LANES].astype(o_ref.dtype))


def _u_projection_rows(x, w_uf, *, chunks=8):
    bsz, s_len, d = x.shape
    width = w_uf.shape[1] - V7X_LANES
    g = width // SSM_GROUP
    tb = chunks * SSM_CHUNK
    rows = chunks * bsz
    assert bsz % 8 == 0 and s_len % tb == 0 and tb <= STAGE_PITCH and g % GROUPS_PER_SLAB == 0, x.shape
    return pl.pallas_call(
        functools.partial(_u_rows_kernel, bsz=bsz, chunks=chunks),
        grid=(s_len // tb,),
        in_specs=[pl.BlockSpec((bsz, tb, d), lambda i: (0, i, 0)),
                  pl.BlockSpec((d, width + V7X_LANES), lambda i: (0, 0), pipeline_mode=pl.Buffered(1))],
        out_specs=[pl.BlockSpec((g, rows, SSM_ROW), lambda i: (0, i, 0)),
                   pl.BlockSpec((bsz, tb, V7X_LANES), lambda i: (0, i, 0))],
        out_shape=[jax.ShapeDtypeStruct((g, (s_len // SSM_CHUNK) * bsz, SSM_ROW), BF16),
                   jax.ShapeDtypeStruct((bsz, s_len, V7X_LANES), F32)],
        scratch_shapes=[pltpu.VMEM((width // V7X_LANES, bsz * STAGE_PITCH, V7X_LANES), F32)],
        compiler_params=_params("parallel"),
        name="u_projection_rows",
    )(x, w_uf)


def _glu_rows_kernel(gy_ref, w_ref, o_ref, stage_ref, *, bsz, chunks):
    tb = o_ref.shape[1]
    n = GROUPS_PER_SLAB
    n_slabs = gy_ref.shape[0] // n
    for jh in range(SSM_CHUNK // n):
        ys = [jnp.concatenate([gy_ref[n * v + e, :, jh * V7X_LANES:(jh + 1) * V7X_LANES].astype(F32)
                               for v in range(n_slabs)], axis=1) for e in range(n)]
        xs = _block_transpose(ys)
        for jl in range(n):
            j = jh * n + jl
            for c in range(chunks):
                for v in range(n_slabs):
                    stage_ref.at[v][pl.ds(SSM_CHUNK * c + j, bsz, stride=STAGE_PITCH), :] = (
                        xs[jl][c * bsz:(c + 1) * bsz, v * V7X_LANES:(v + 1) * V7X_LANES])
    a = jnp.concatenate(
        [jnp.concatenate([stage_ref[v, b * STAGE_PITCH:b * STAGE_PITCH + tb, :] for v in range(n_slabs)], axis=1)
         for b in range(bsz)], axis=0).astype(BF16)
    z = _dot(a, w_ref[...])
    half = z.shape[1] // 2
    o_ref[...] = (z[:, :half] * _sigmoid(z[:, half:])).astype(o_ref.dtype).reshape(o_ref.shape)


def _glu_from_rows(gy_rows, w_glu, *, bsz, chunks=8):
    g, total_rows, _ = gy_rows.shape
    width = g * SSM_GROUP
    n_out = w_glu.shape[1] // 2
    s_len = total_rows // bsz * SSM_CHUNK
    tb = chunks * SSM_CHUNK
    rows = chunks * bsz
    assert bsz % 8 == 0 and s_len % tb == 0 and tb <= STAGE_PITCH and g % GROUPS_PER_SLAB == 0, gy_rows.shape
    return pl.pallas_call(
        functools.partial(_glu_rows_kernel, bsz=bsz, chunks=chunks),
        grid=(s_len // tb,),
        in_specs=[pl.BlockSpec((g, rows, SSM_ROW), lambda i: (0, i, 0)),
                  pl.BlockSpec((width, 2 * n_out), lambda i: (0, 0), pipeline_mode=pl.Buffered(1))],
        out_specs=pl.BlockSpec((bsz, tb, n_out), lambda i: (0, i, 0)),
        out_shape=jax.ShapeDtypeStruct((bsz, s_len, n_out), BF16),
        scratch_shapes=[pltpu.VMEM((width // V7X_LANES, bsz * STAGE_PITCH, V7X_LANES), F32)],
        compiler_params=_params("parallel"),
        name="glu_from_rows",
    )(gy_rows, w_glu)


def _ffn_up_kernel(x_ref, wg_ref, wv_ref, cwg_ref, cbg_ref, cwv_ref, cbv_ref, o_ref,
                   wg16_ref, wv16_ref, hg_ref, hv_ref, *, blocks_per_seq):
    i = pl.program_id(1)
    bm, bn = o_ref.shape
    pad = hg_ref.shape[0] - bm

    @pl.when(i == 0)
    def _():
        wg16_ref[...] = wg_ref[0].astype(BF16)
        wv16_ref[...] = wv_ref[0].astype(BF16)

    @pl.when(i % blocks_per_seq == 0)
    def _():
        hg_ref[0:pad, :] = jnp.zeros((pad, bn), F32)
        hv_ref[0:pad, :] = jnp.zeros((pad, bn), F32)

    @pl.when(i % blocks_per_seq != 0)
    def _():
        hg_ref[0:pad, :] = hg_ref[bm:bm + pad, :]
        hv_ref[0:pad, :] = hv_ref[bm:bm + pad, :]

    x = x_ref[...]
    conv = []
    for w16, h_ref, cw_ref, cb_ref in ((wg16_ref, hg_ref, cwg_ref, cbg_ref), (wv16_ref, hv_ref, cwv_ref, cbv_ref)):
        h = _dot(x, w16[...])
        h_ref[pad:pad + bm, :] = h
        cw = cw_ref[0]
        conv.append(cb_ref[0] + cw[0:1, :] * h_ref[pad - 2:pad - 2 + bm, :]
                    + cw[1:2, :] * h_ref[pad - 1:pad - 1 + bm, :] + cw[2:3, :] * h)
    gate, val = conv
    o_ref[...] = (gate * _sigmoid(gate) * val).astype(o_ref.dtype)


def _ffn_up(x16, layer, w_up, conv_w, conv_b, *, col0, n_cols, seq_len, bm, bn):
    m, d = x16.shape
    f = w_up.shape[2] // 2
    assert m % bm == 0 and seq_len % bm == 0 and n_cols % bn == 0 and bn % V7X_LANES == 0, (m, bm, n_cols, bn)
    pad = 8
    cb = conv_b.reshape(conv_b.shape[0], 1, 2 * f)
    gate_col = lambda j: pl.multiple_of(col0 + j * bn, V7X_LANES)
    val_col = lambda j: pl.multiple_of(f + col0 + j * bn, V7X_LANES)
    window = lambda rows: (pl.Element(1), pl.Element(rows), pl.Element(bn))
    gate_spec = lambda rows: pl.BlockSpec(window(rows), lambda j, i: (layer, 0, gate_col(j)))
    val_spec = lambda rows: pl.BlockSpec(window(rows), lambda j, i: (layer, 0, val_col(j)))
    return pl.pallas_call(
        functools.partial(_ffn_up_kernel, blocks_per_seq=seq_len // bm),
        grid=(n_cols // bn, m // bm),
        in_specs=[pl.BlockSpec((bm, d), lambda j, i: (i, 0)),
                  gate_spec(d), val_spec(d), gate_spec(CONV_WIDTH), gate_spec(1),
                  val_spec(CONV_WIDTH), val_spec(1)],
        out_specs=pl.BlockSpec((bm, bn), lambda j, i: (i, j)),
        out_shape=jax.ShapeDtypeStruct((m, n_cols), BF16),
        scratch_shapes=[pltpu.VMEM((d, bn), BF16), pltpu.VMEM((d, bn), BF16),
                        pltpu.VMEM((pad + bm, bn), F32), pltpu.VMEM((pad + bm, bn), F32)],
        compiler_params=_params("arbitrary", "arbitrary"),
        name="ffn_up_conv_gate",
    )(x16, w_up, w_up, conv_w, cb, conv_w, cb)


def _cast_kernel(src_ref, dst_ref):
    dst_ref[...] = src_ref[...].astype(dst_ref.dtype)


def _layer_to_bf16(stacked, layer, *, steps=8):
    _, k, d = stacked.shape
    assert k % (16 * steps) == 0, (k, steps)
    rows = k // steps
    return pl.pallas_call(
        _cast_kernel,
        grid=(steps,),
        in_specs=[pl.BlockSpec((None, rows, d), lambda r: (layer, r, 0))],
        out_specs=pl.BlockSpec((rows, d), lambda r: (r, 0)),
        out_shape=jax.ShapeDtypeStruct((k, d), BF16),
        compiler_params=_params("parallel"),
        name="layer_to_bf16",
    )(stacked)


def _conv_ffn(x32, x16, layer, w_up, conv_w, conv_b, w_down, ln_g, ln_b, *, seq_len, bn=512):
    f = w_down.shape[1]
    w_up, conv_w, conv_b = w_up.astype(F32), conv_w.astype(F32), conv_b.astype(F32)
    w_down16 = _layer_to_bf16(w_down, layer)
    main = (f // bn) * bn
    pairs = []
    for col0, n_cols, width in ((0, main, bn), (main, f - main, f - main)):
        if n_cols:
            act = _ffn_up(x16, layer, w_up, conv_w, conv_b, col0=col0, n_cols=n_cols, seq_len=seq_len,
                          bm=1024, bn=width)
            pairs.append((act, w_down16, col0))
    return _matmul_residual_layernorm(pairs, x32, ln_g, ln_b, bm=256)


def _rope_proj_kernel(a_ref, w_ref, pos_ref, o_ref, *, rope_cols, q_cols, q_scale, sub):
    hd, half = SWA_HEAD_DIM, ROPE_DIM // 2
    bm, n = o_ref.shape
    lane1 = lax.broadcasted_iota(jnp.int32, (1, V7X_LANES), 1) % hd
    inv_freq = jnp.where(lane1 < ROPE_DIM,
                         jnp.exp((lane1 % half).astype(F32) * (-math.log(ROPE_THETA) / half)), 0.0)
    sign = jnp.where(lane1 < half, -1.0, 1.0)
    lane = lax.broadcasted_iota(jnp.int32, (1, n), 1)
    first = (lane & (hd - 1)) < half
    reps = n // V7X_LANES
    for r in range(bm // sub):
        rows = slice(r * sub, (r + 1) * sub)
        ang = pos_ref[rows, :].astype(F32) * inv_freq
        cos = jnp.concatenate([jnp.cos(ang)] * reps, axis=1)
        sin = jnp.concatenate([jnp.sin(ang) * sign] * reps, axis=1)
        acc = _dot(a_ref[rows, :], w_ref[...])
        partner = jnp.where(first, pltpu.roll(acc, n - half, 1), pltpu.roll(acc, half, 1))
        rot = acc * cos + partner * sin
        out = jnp.where(lane < rope_cols, rot, acc) * jnp.where(lane < q_cols, q_scale, 1.0)
        o_ref[rows, :] = out.astype(o_ref.dtype)


def _rope_projection(x16, w, pos_col, *, rope_cols, q_cols, q_scale, bm, sub=256):
    m, k = x16.shape
    n = w.shape[1]
    assert m % bm == 0 and bm % sub == 0 and n % V7X_LANES == 0, (m, n, bm, sub)
    return pl.pallas_call(
        functools.partial(_rope_proj_kernel, rope_cols=rope_cols, q_cols=q_cols, q_scale=q_scale, sub=sub),
        grid=(m // bm,),
        in_specs=[pl.BlockSpec((bm, k), lambda i: (i, 0)),
                  pl.BlockSpec((k, n), lambda i: (0, 0), pipeline_mode=pl.Buffered(1)),
                  pl.BlockSpec((bm, 1), lambda i: (i, 0))],
        out_specs=pl.BlockSpec((bm, n), lambda i: (i, 0)),
        out_shape=jax.ShapeDtypeStruct((m, n), BF16),
        compiler_params=_params("parallel"),
        name="rope_projection",
    )(x16, w, pos_col)


def _swa_kernel(sink_ref, q_ref, kp_ref, kc_ref, vp_ref, vc_ref, o_ref, *, blocks):
    n = pl.program_id(1)
    w, hd, lanes = SWA_WINDOW, SWA_HEAD_DIM, V7X_LANES
    pairs = SWA_GROUPS // 2
    qi = lax.broadcasted_iota(jnp.int32, (w, 2 * w), 0)
    kj = lax.broadcasted_iota(jnp.int32, (w, 2 * w), 1)
    rel = w + qi - kj
    band = (rel >= 0) & (rel < w)
    lane_kv = lax.broadcasted_iota(jnp.int32, (2 * w, lanes), 1)
    lane_o = lax.broadcasted_iota(jnp.int32, (w, lanes), 1)

    def both_halves(prev, cur, hk):
        x = jnp.concatenate([prev, cur], axis=0).astype(F32)
        upper = hk % 2 == 1
        own = jnp.where((lane_kv >= hd) if upper else (lane_kv < hd), x, 0.0)
        other = pltpu.roll(own, hd, 1)
        lo, hi = (other, own) if upper else (own, other)
        return jnp.concatenate([lo, hi], axis=0).astype(BF16)

    for blk in range(blocks):
        rows = slice(blk * w, (blk + 1) * w)
        valid = band & ((n > 0) | (kj >= w)) if blk == 0 else band
        bias = jnp.where(valid, 0.0, MASK_VALUE)
        for hk in range(SWA_KV_HEADS):
            slab = slice((hk // 2) * lanes, (hk // 2 + 1) * lanes)
            prev_rows = slice((blk - 1) * w, blk * w)
            k_prev = kp_ref[:, slab] if blk == 0 else kc_ref[prev_rows, slab]
            v_prev = vp_ref[:, slab] if blk == 0 else vc_ref[prev_rows, slab]
            kk = both_halves(k_prev, kc_ref[rows, slab], hk)
            vv = both_halves(v_prev, vc_ref[rows, slab], hk)
            slabs = [slice((hk * pairs + r) * lanes, (hk * pairs + r + 1) * lanes) for r in range(pairs)]
            qs = jnp.concatenate([q_ref[rows, sl] for sl in slabs], axis=0)
            s_all = _dot_nt(qs, kk)
            probs, inv_den = [], []
            for r in range(pairs):
                halves, inv = [], []
                for half in range(2):
                    sink = sink_ref[hk * SWA_GROUPS + 2 * r + half] * LOG2E
                    s = s_all[r * w:(r + 1) * w, half * 2 * w:(half + 1) * 2 * w] + bias
                    m = jnp.maximum(jnp.max(s, axis=1, keepdims=True), sink)
                    p = jnp.exp2(s - m)
                    inv.append(1.0 / (jnp.sum(p, axis=1, keepdims=True) + jnp.exp2(sink - m)))
                    halves.append(p.astype(BF16))
                probs.append(jnp.concatenate(halves, axis=1))
                inv_den.append(jnp.where(lane_o < hd, inv[0], inv[1]))
            o_all = _dot(jnp.concatenate(probs, axis=0), vv)
            for r in range(pairs):
                o_ref[rows, slabs[r]] = (o_all[r * w:(r + 1) * w] * inv_den[r]).astype(o_ref.dtype)


def _sliding_window_attention(qkv, sinks, *, blocks=2):
    bsz, s_len, _ = qkv.shape
    w, hd = SWA_WINDOW, SWA_HEAD_DIM
    rows = blocks * w
    assert s_len % rows == 0, (s_len, rows)
    qw = SWA_HEADS * hd
    kw = SWA_KV_HEADS * hd
    k_blk = qw // kw
    v_blk = k_blk + 1
    prev = lambda n: jnp.maximum(n * blocks - 1, 0)
    return pl.pallas_call(
        functools.partial(_swa_kernel, blocks=blocks),
        grid=(bsz, s_len // rows),
        in_specs=[pl.BlockSpec(memory_space=pltpu.SMEM),
                  pl.BlockSpec((None, rows, qw), lambda b, n: (b, n, 0)),
                  pl.BlockSpec((None, w, kw), lambda b, n: (b, prev(n), k_blk)),
                  pl.BlockSpec((None, rows, kw), lambda b, n: (b, n, k_blk)),
                  pl.BlockSpec((None, w, kw), lambda b, n: (b, prev(n), v_blk)),
                  pl.BlockSpec((None, rows, kw), lambda b, n: (b, n, v_blk))],
        out_specs=pl.BlockSpec((None, rows, qw), lambda b, n: (b, n, 0)),
        out_shape=jax.ShapeDtypeStruct((bsz, s_len, qw), BF16),
        compiler_params=_params("parallel", "arbitrary"),
        name="sliding_window_attention",
    )(sinks.astype(F32), qkv, qkv, qkv, qkv, qkv)


def _even_mixer(x32, xa, bsz, s_len, w_in, b_f, lam_re, lam_im, log_step, b_re, b_im, c_re, c_im,
                d_skip, w_glu, w_out, ln_g, ln_b):
    m = x32.shape[0]
    fw = FOX_WIDTH
    nh = FOX_HEADS
    q_factor = LOG2E / math.sqrt(FOX_HEAD_DIM)
    w_qkv = jnp.concatenate([w_in[:, :fw] * q_factor, w_in[:, fw:3 * fw]], axis=1).astype(BF16)
    w_uf = jnp.concatenate([w_in[:, 3 * fw + nh:],
                            jnp.pad(w_in[:, 3 * fw:3 * fw + nh], ((0, 0), (0, V7X_LANES - nh)))],
                           axis=1).astype(BF16)
    bias_f = jnp.pad(b_f, (0, V7X_LANES - nh)).reshape(1, V7X_LANES).astype(F32)

    qkv = _matmul(xa, w_qkv, bm=1024, bn=3 * fw // 2, out_dtype=BF16)
    mats = _s5_prepare(lam_re, lam_im, log_step, b_re, b_im, c_re, c_im, d_skip)
    u_rows, f_logits = _u_projection_rows(xa, w_uf)
    c_col = _forget_gate_cumsum(f_logits, bias_f)
    fox = _forgetting_attention(qkv, c_col).reshape(m, fw)
    gy_rows = _s5_scan_gelu(u_rows, mats, bsz=bsz)
    ssm = _glu_from_rows(gy_rows, w_glu.astype(BF16), bsz=bsz).reshape(m, -1)
    w_out16 = w_out.astype(BF16)
    return _matmul_residual_layernorm([(fox, w_out16, 0), (ssm, w_out16, fw)], x32, ln_g, ln_b, bm=512)


def _odd_mixer(x32, x16, bsz, s_len, positions, w_in, sinks, w_out, ln_g, ln_b):
    m = x16.shape[0]
    rope_cols = (SWA_HEADS + SWA_KV_HEADS) * SWA_HEAD_DIM
    qkv = _rope_projection(x16, w_in.astype(BF16), positions.reshape(m, 1).astype(jnp.int32),
                           rope_cols=rope_cols, q_cols=SWA_HEADS * SWA_HEAD_DIM,
                           q_scale=LOG2E / math.sqrt(SWA_HEAD_DIM), bm=1024)
    o = _sliding_window_attention(qkv.reshape(bsz, s_len, -1), sinks).reshape(m, -1)
    return _matmul_residual_layernorm([(o, w_out.astype(BF16), 0)], x32, ln_g, ln_b, bm=512)


def kernel(x, positions, ev_w_in, ev_b_f, ev_lambda_re, ev_lambda_im, ev_log_step, ev_ssm_b_re,
           ev_ssm_b_im, ev_ssm_c_re, ev_ssm_c_im, ev_ssm_d, ev_w_glu, ev_w_out, od_w_in, od_sinks,
           od_w_out, ln_mix_g, ln_mix_b, ffn_w_up, ffn_conv_w, ffn_conv_b, ffn_w_down, ln_ffn_g,
           ln_ffn_b):
    bsz, s_len, d = x.shape
    x32 = x.reshape(bsz * s_len, d).astype(F32)
    x16 = None
    for i in range(DEPTH):
        j = i // 2
        if i % 2 == 0:
            xa = x.astype(F32) if x16 is None else x16.reshape(bsz, s_len, d)
            x32, x16 = _even_mixer(x32, xa, bsz, s_len, ev_w_in[j], ev_b_f[j], ev_lambda_re[j],
                                   ev_lambda_im[j], ev_log_step[j], ev_ssm_b_re[j], ev_ssm_b_im[j],
                                   ev_ssm_c_re[j], ev_ssm_c_im[j], ev_ssm_d[j], ev_w_glu[j],
                                   ev_w_out[j], ln_mix_g[i], ln_mix_b[i])
        else:
            x16 = x32.astype(BF16) if x16 is None else x16
            x32, x16 = _odd_mixer(x32, x16, bsz, s_len, positions, od_w_in[j], od_sinks[j],
                                  od_w_out[j], ln_mix_g[i], ln_mix_b[i])
        x32, x16 = _conv_ffn(x32, x16, i, ffn_w_up, ffn_conv_w, ffn_conv_b, ffn_w_down,
                             ln_ffn_g[i], ln_ffn_b[i], seq_len=s_len)
    return x32.reshape(bsz, s_len, d).astype(x.dtype)
```
